```python
import math
import jax, jax.numpy as jnp
from jax import lax
import numpy as np

D_MODEL = 1024
BATCH = 4
SEQ = 4096
DEPTH = 2
DEC_BATCH = 16
DEC_SEQ = 64
PAST_LEN = 4096

CHUNK = 64
N_A_LAYERS = DEPTH // 2
N_B_LAYERS = DEPTH - N_A_LAYERS
A_CHUNK = 128
A_HALF = 2 * D_MODEL
A_GROUPS = 8
MLA_HEADS = 8
QK_NOPE = 128
QK_ROPE = 64
V_DIM = 128
KV_RANK = 256
Q_RANK = 512
ROPE_BASE = 10000.0
ATTN_SCALE = (QK_NOPE + QK_ROPE) ** -0.5
Q_BLOCK = 128
N_EXPERTS = 32
TOP_K = 4
D_EXPERT = D_MODEL
SWIGLU_LIMIT = 7.0
SWIGLU_ALPHA = 1.702
EXPERT_BLOCK = 128
DN_ALPHA = (2 * DEPTH) ** 0.25
DN_BETA = (8 * DEPTH) ** -0.25
LN_EPS = 1e-5
RMS_EPS = 1e-6

kernel_name = 'yoco_gmlp_mla_moe_stream_step'


def layer_norm(x, g, b):
    xf = x.astype(jnp.float32)
    xc = xf - jnp.mean(xf, -1, keepdims=True)
    var = jnp.mean(xc * xc, -1, keepdims=True)
    out = xc * lax.rsqrt(var + LN_EPS) * g.astype(jnp.float32) + b.astype(jnp.float32)
    return out.astype(x.dtype)


def rms_norm(x, g):
    xf = x.astype(jnp.float32)
    out = xf * lax.rsqrt(jnp.mean(xf * xf, -1, keepdims=True) + RMS_EPS) * g.astype(jnp.float32)
    return out.astype(x.dtype)


def modulate(x, shift, scale):
    return x * (1 + scale) + shift


def rope(x, pos):
    p = x.shape[-1]
    half = p // 2
    inv = 1.0 / (ROPE_BASE ** (jnp.arange(half, dtype=jnp.float32) * (2.0 / p)))
    ang = pos.astype(jnp.float32)[:, None] * inv[None, :]
    cos = jnp.cos(ang)[None, :, None, :]
    sin = jnp.sin(ang)[None, :, None, :]
    xf = x.astype(jnp.float32)
    x1, x2 = xf[..., :half], xf[..., half:]
    return jnp.concatenate([x1 * cos - x2 * sin, x1 * sin + x2 * cos], -1).astype(x.dtype)


def spatial_gate(v, w_s, b_s):
    bsz, s, c = v.shape
    L = min(s, A_CHUNK)
    n = s // L
    vg = v.reshape(bsz, n, L, A_GROUPS, c // A_GROUPS)
    idx = jnp.arange(L)
    mask = (idx[:, None] // CHUNK) >= (idx[None, :] // CHUNK)
    w = jnp.where(mask[None], w_s[:, :L, :L], 0)
    out = jnp.einsum('gij,bnjgc->bnigc', w, vg) + b_s[:, :L].T[None, None, :, :, None]
    return out.reshape(bsz, s, c)


def gmlp_mixer(h, w_in, b_in, vn_g, vn_b, w_s, b_s, w_out, b_out):
    z = jax.nn.gelu(h @ w_in + b_in)
    u, v = jnp.split(z, 2, axis=-1)
    v = layer_norm(v, vn_g, vn_b)
    y = (u * spatial_gate(v, w_s, b_s)) @ w_out + b_out
    return y, v


def shared_kv(x, pos, w_a, norm_g):
    kv = x @ w_a
    lat = rms_norm(kv[..., :KV_RANK], norm_g)
    kr = rope(kv[..., KV_RANK:][:, :, None, :], pos)[:, :, 0, :]
    return lat, kr


def latent_attention(q_lat, q_rope, lat, kr, q_pos, k_pos):
    bsz, sq, nh, r = q_lat.shape
    qb = min(sq, Q_BLOCK)
    nb = sq // qb
    ql = q_lat.reshape(bsz, nb, qb, nh, r).swapaxes(0, 1)
    qr = q_rope.reshape(bsz, nb, qb, nh, q_rope.shape[-1]).swapaxes(0, 1)
    qp = q_pos.reshape(nb, qb)
    k_chunk = k_pos // CHUNK

    def block(args):
        qlb, qrb, qpb = args
        s = (jnp.einsum('bqhr,btr->bhqt', qlb, lat) +
             jnp.einsum('bqhp,btp->bhqt', qrb, kr)).astype(jnp.float32) * ATTN_SCALE
        visible = k_chunk[None, :] <= (qpb // CHUNK)[:, None]
        s = jnp.where(visible[None, None], s, -jnp.inf)
        p = jax.nn.softmax(s, axis=-1).astype(lat.dtype)
        return jnp.einsum('bhqt,btr->bqhr', p, lat)

    o = lax.map(block, (ql, qr, qp))
    return o.swapaxes(0, 1).reshape(bsz, sq, nh, r)


def mla_mixer(h, q_pos, lat, kr, k_pos, w_dq, qn_g, w_qb, w_uk, w_uv, w_o):
    bsz, s, _ = h.shape
    cq = rms_norm(h @ w_dq, qn_g)
    q = jnp.einsum('bsr,rhd->bshd', cq, w_qb)
    q_nope = q[..., :QK_NOPE]
    q_rope = rope(q[..., QK_NOPE:], q_pos)
    q_lat = jnp.einsum('bshn,rhn->bshr', q_nope, w_uk)
    o_lat = latent_attention(q_lat, q_rope, lat, kr, q_pos, k_pos)
    o = jnp.einsum('bshr,rhv->bshv', o_lat, w_uv).reshape(bsz, s, MLA_HEADS * V_DIM)
    return o @ w_o


def moe_ffn(h, w_router, b_router, w_in, b_in, w_out, b_out):
    bsz, s, d = h.shape
    n = bsz * s
    m = n * TOP_K
    xt = h.reshape(n, d)
    logits = (xt @ w_router + b_router).astype(jnp.float32)
    top_v, top_e = lax.top_k(logits, TOP_K)
    gates = jax.nn.softmax(top_v, axis=-1)
    flat_e = top_e.reshape(m)
    order = jnp.argsort(flat_e)
    sorted_e = flat_e[order]
    tok = order // TOP_K
    g_sorted = gates.reshape(m)[order]
    counts = jnp.bincount(flat_e, length=N_EXPERTS)
    padded = (counts + EXPERT_BLOCK - 1) // EXPERT_BLOCK * EXPERT_BLOCK
    start = jnp.cumsum(counts) - counts
    pend = jnp.cumsum(padded)
    pstart = pend - padded
    dest = pstart[sorted_e] + (jnp.arange(m) - start[sorted_e])
    n_blocks = -(-m // EXPERT_BLOCK) + N_EXPERTS
    rows = n_blocks * EXPERT_BLOCK
    xpad = jnp.zeros((rows, d), h.dtype).at[dest].set(xt[tok])
    block_e = jnp.minimum(jnp.searchsorted(pend, jnp.arange(n_blocks) * EXPERT_BLOCK, side='right'),
                          N_EXPERTS - 1)

    def expert_block(args):
        xb, e = args
        z = xb @ w_in[e] + b_in[e]
        glu = jnp.minimum(z[:, 0::2], SWIGLU_LIMIT)
        lin = jnp.clip(z[:, 1::2], -SWIGLU_LIMIT, SWIGLU_LIMIT)
        a = glu * jax.nn.sigmoid(SWIGLU_ALPHA * glu) * (lin + 1)
        return a @ w_out[e] + b_out[e]

    ypad = lax.map(expert_block, (xpad.reshape(n_blocks, EXPERT_BLOCK, d), block_e)).reshape(rows, d)
    y = jnp.zeros((n, d), h.dtype).at[tok].add(ypad[dest] * g_sorted[:, None].astype(h.dtype))
    return y.reshape(bsz, s, d)


def setup_inputs(seed: int = 0) -> dict:
    key = jax.random.key(seed)
    ks = jax.random.split(key, 32)

    def nrm(k, shape, scale):
        return jax.random.normal(k, shape, jnp.float32) * scale

    D = D_MODEL
    gate_offset = jnp.zeros((6 * D,), jnp.float32).at[2 * D:3 * D].set(1.0).at[5 * D:].set(1.0)
    return {
        'x_prompt': nrm(ks[0], (BATCH, SEQ, D), 1.0),
        'x_sample': nrm(ks[1], (DEC_BATCH, DEC_SEQ, D), 1.0),
        'cache_kv_latent': nrm(ks[2], (DEC_BATCH, PAST_LEN, KV_RANK), 1.0),
        'cache_k_rope': nrm(ks[3], (DEC_BATCH, PAST_LEN, QK_ROPE), 1.0),
        'c_prompt': nrm(ks[4], (BATCH, D), 1.0),
        'c_sample': nrm(ks[5], (DEC_BATCH, D), 1.0),
        'ada_w': nrm(ks[6], (DEPTH, D, 6 * D), 0.1 * D ** -0.5),
        'ada_b': gate_offset + nrm(ks[7], (DEPTH, 6 * D), 0.02),
        'ln_g': 1.0 + nrm(ks[8], (DEPTH, 2, D), 0.02),
        'ln_b': nrm(ks[9], (DEPTH, 2, D), 0.02),
        'a_w_in': nrm(ks[10], (N_A_LAYERS, D, 2 * A_HALF), D ** -0.5),
        'a_b_in': nrm(ks[11], (N_A_LAYERS, 2 * A_HALF), 0.02),
        'a_vn_g': 1.0 + nrm(ks[12], (N_A_LAYERS, A_HALF), 0.02),
        'a_vn_b': nrm(ks[13], (N_A_LAYERS, A_HALF), 0.02),
        'a_w_s': nrm(ks[14], (N_A_LAYERS, A_GROUPS, A_CHUNK, A_CHUNK), 0.5 * A_CHUNK ** -0.5),
        'a_b_s': 1.0 + nrm(ks[15], (N_A_LAYERS, A_GROUPS, A_CHUNK), 0.02),
        'a_w_out': nrm(ks[16], (N_A_LAYERS, A_HALF, D), DN_BETA * A_HALF ** -0.5),
        'a_b_out': nrm(ks[17], (N_A_LAYERS, D), 0.02),
        'kv_w_a': nrm(ks[18], (D, KV_RANK + QK_ROPE), D ** -0.5),
        'kv_norm_g': 1.0 + nrm(ks[19], (KV_RANK,), 0.02),
        'kv_w_uk': nrm(ks[20], (KV_RANK, MLA_HEADS, QK_NOPE), KV_RANK ** -0.5),
        'kv_w_uv': nrm(ks[21], (KV_RANK, MLA_HEADS, V_DIM), DN_BETA * KV_RANK ** -0.5),
        'q_w_a': nrm(ks[22], (N_B_LAYERS, D, Q_RANK), D ** -0.5),
        'q_norm_g': 1.0 + nrm(ks[23], (N_B_LAYERS, Q_RANK), 0.02),
        'q_w_b': nrm(ks[24], (N_B_LAYERS, Q_RANK, MLA_HEADS, QK_NOPE + QK_ROPE), Q_RANK ** -0.5),
        'q_w_o': nrm(ks[25], (N_B_LAYERS, MLA_HEADS * V_DIM, D), DN_BETA * (MLA_HEADS * V_DIM) ** -0.5),
        'moe_w_router': nrm(ks[26], (DEPTH, D, N_EXPERTS), D ** -0.5),
        'moe_b_router': nrm(ks[27], (DEPTH, N_EXPERTS), 0.01),
        'moe_w_in': nrm(ks[28], (DEPTH, N_EXPERTS, D, 2 * D_EXPERT), D ** -0.5),
        'moe_b_in': nrm(ks[29], (DEPTH, N_EXPERTS, 2 * D_EXPERT), 0.01),
        'moe_w_out': nrm(ks[30], (DEPTH, N_EXPERTS, D_EXPERT, D), DN_BETA * D_EXPERT ** -0.5),
        'moe_b_out': nrm(ks[31], (DEPTH, N_EXPERTS, D), 0.01),
    }


def reference(x_prompt, x_sample, cache_kv_latent, cache_k_rope, c_prompt, c_sample,
              ada_w, ada_b, ln_g, ln_b,
              a_w_in, a_b_in, a_vn_g, a_vn_b, a_w_s, a_b_s, a_w_out, a_b_out,
              kv_w_a, kv_norm_g, kv_w_uk, kv_w_uv,
              q_w_a, q_norm_g, q_w_b, q_w_o,
              moe_w_router, moe_b_router, moe_w_in, moe_b_in, moe_w_out, moe_b_out):

    def run_group(x, c, pos, past_lat, past_kr):
        a_v = []
        lat_new = kr_new = None
        lat_all = kr_all = k_pos = None
        for l in range(DEPTH):
            mod = jax.nn.silu(c) @ ada_w[l] + ada_b[l]
            sh1, sc1, g1, sh2, sc2, g2 = jnp.split(mod[:, None, :], 6, axis=-1)
            h = modulate(x, sh1, sc1)
            if l < N_A_LAYERS:
                y, v = gmlp_mixer(h, a_w_in[l], a_b_in[l], a_vn_g[l], a_vn_b[l],
                                  a_w_s[l], a_b_s[l], a_w_out[l], a_b_out[l])
                a_v.append(v)
            else:
                j = l - N_A_LAYERS
                y = mla_mixer(h, pos, lat_all, kr_all, k_pos, q_w_a[j], q_norm_g[j], q_w_b[j],
                              kv_w_uk, kv_w_uv, q_w_o[j])
            x = layer_norm(DN_ALPHA * x + g1 * y, ln_g[l, 0], ln_b[l, 0])
            h = modulate(x, sh2, sc2)
            f = moe_ffn(h, moe_w_router[l], moe_b_router[l], moe_w_in[l], moe_b_in[l],
                        moe_w_out[l], moe_b_out[l])
            x = layer_norm(DN_ALPHA * x + g2 * f, ln_g[l, 1], ln_b[l, 1])
            if l == N_A_LAYERS - 1:
                lat_new, kr_new = shared_kv(x, pos, kv_w_a, kv_norm_g)
                if past_lat is None:
                    lat_all, kr_all, k_pos = lat_new, kr_new, pos
                else:
                    lat_all = jnp.concatenate([past_lat, lat_new], axis=1)
                    kr_all = jnp.concatenate([past_kr, kr_new], axis=1)
                    k_pos = jnp.arange(lat_all.shape[1], dtype=jnp.int32)
        return x, lat_new, kr_new, jnp.stack(a_v)

    pos_p = jnp.arange(x_prompt.shape[1], dtype=jnp.int32)
    y_p, lat_p, kr_p, _ = run_group(x_prompt, c_prompt, pos_p, None, None)
    past = cache_kv_latent.shape[1]
    pos_s = past + jnp.arange(x_sample.shape[1], dtype=jnp.int32)
    y_s, lat_s, kr_s, av_s = run_group(x_sample, c_sample, pos_s, cache_kv_latent, cache_k_rope)
    return (y_p, y_s, lat_p, kr_p, lat_s, kr_s, av_s)
```

```python
import functools

import jax
import jax.numpy as jnp
from jax import lax
from jax.experimental import pallas as pl
from jax.experimental.pallas import tpu as pltpu

D_MODEL = 1024
DEPTH = 2
CHUNK = 64
A_CHUNK = 128
A_HALF = 2 * D_MODEL
A_GROUPS = 8
A_GROUP_W = A_HALF // A_GROUPS
MLA_HEADS = 8
QK_NOPE = 128
QK_ROPE = 64
ROPE_HALF = QK_ROPE // 2
V_DIM = 128
KV_RANK = 256
Q_RANK = 512
QK_LAT = KV_RANK + QK_ROPE
ROPE_BASE = 10000.0
ATTN_SCALE = (QK_NOPE + QK_ROPE) ** -0.5
N_EXPERTS = 32
TOP_K = 4
D_EXPERT = D_MODEL
SWIGLU_LIMIT = 7.0
SWIGLU_ALPHA = 1.702
DN_ALPHA = (2 * DEPTH) ** 0.25
LN_EPS = 1e-5
RMS_EPS = 1e-6

BF16 = jnp.bfloat16
F32 = jnp.float32

VMEM_LIMIT = 56 * 1024 * 1024
EXPERT_ROWS = 256
ROUTER_ROWS = 512
NEG_BIG = -1e30


def _cparams(sem):
    return pltpu.CompilerParams(dimension_semantics=sem, vmem_limit_bytes=VMEM_LIMIT)


def _const_spec(shape):
    nd = len(shape)
    return pl.BlockSpec(shape, lambda *_: (0,) * nd, pipeline_mode=pl.Buffered(1))


def _layer_norm(r, g, b):
    rc = r - jnp.mean(r, axis=-1, keepdims=True)
    var = jnp.mean(rc * rc, axis=-1, keepdims=True)
    return rc * lax.rsqrt(var + LN_EPS) * g + b


def _gelu_tanh(x):
    c = 0.7978845608028654
    return 0.5 * x * (1.0 + jnp.tanh(c * (x + 0.044715 * (x * x * x))))


def _bdot(a, b):
    return jnp.dot(a, b, preferred_element_type=F32)


def _ada_kernel(c_ref, w_ref, b_ref, o_ref):
    c = c_ref[...]
    s = c * jax.nn.sigmoid(c)
    o_ref[...] = jnp.dot(s, w_ref[...], preferred_element_type=F32,
                         precision=lax.Precision.HIGHEST) + b_ref[...]


def _ada_mod(c_all, ada_w, ada_b):
    nb = c_all.shape[0]
    six_d = ada_w.shape[-1]
    tn = D_MODEL
    return pl.pallas_call(
        _ada_kernel,
        grid=(DEPTH, six_d // tn),
        in_specs=[
            pl.BlockSpec((nb, D_MODEL), lambda l, j: (0, 0)),
            pl.BlockSpec((None, D_MODEL, tn), lambda l, j: (l, 0, j)),
            pl.BlockSpec((None, 1, tn), lambda l, j: (l, 0, j)),
        ],
        out_specs=pl.BlockSpec((None, nb, tn), lambda l, j: (l, 0, j)),
        out_shape=jax.ShapeDtypeStruct((DEPTH, nb, six_d), F32),
        compiler_params=_cparams(("arbitrary", "arbitrary")),
        name="ada_mod",
    )(c_all, ada_w, ada_b.reshape(DEPTH, 1, six_d))


def _gmlp_kernel(x_ref, mod_ref, w_in_ref, b_in_ref, vng_ref, vnb_ref, ws_ref, bs_ref,
                 w_out_ref, b_out_ref, lng_ref, lnb_ref, *rest, write_v):
    if write_v:
        x1_ref, h2_ref, v_ref, u_s, v_s, p_s = rest
    else:
        x1_ref, h2_ref, u_s, v_s, p_s = rest
    x = x_ref[...]
    sh1, sc1, g1 = mod_ref[0:1, :], mod_ref[1:2, :], mod_ref[2:3, :]
    sh2, sc2 = mod_ref[3:4, :], mod_ref[4:5, :]
    h = (x * (1.0 + sc1) + sh1).astype(BF16)
    tm = x.shape[0]
    ch = 512
    n_ch = A_HALF // ch
    for j in range(n_ch):
        sl = slice(j * ch, (j + 1) * ch)
        u_s[:, sl] = _gelu_tanh(_bdot(h, w_in_ref[:, sl]) + b_in_ref[:, sl])
    tot = jnp.zeros((tm, 1), F32)
    for j in range(n_ch):
        sl = slice(j * ch, (j + 1) * ch)
        slw = slice(A_HALF + j * ch, A_HALF + (j + 1) * ch)
        g = _gelu_tanh(_bdot(h, w_in_ref[:, slw]) + b_in_ref[:, slw])
        v_s[:, sl] = g
        tot = tot + jnp.sum(g, axis=-1, keepdims=True)
    mean = tot * (1.0 / A_HALF)
    sq = jnp.zeros((tm, 1), F32)
    for j in range(n_ch):
        sl = slice(j * ch, (j + 1) * ch)
        c = v_s[:, sl] - mean
        sq = sq + jnp.sum(c * c, axis=-1, keepdims=True)
    rstd = lax.rsqrt(sq * (1.0 / A_HALF) + LN_EPS)
    for g in range(A_GROUPS):
        sl = slice(g * A_GROUP_W, (g + 1) * A_GROUP_W)
        vn = (v_s[:, sl] - mean) * rstd * vng_ref[:, sl] + vnb_ref[:, sl]
        if write_v:
            v_ref[:, sl] = vn
        sg = _bdot(ws_ref[g], vn.astype(BF16)) + bs_ref[:, g:g + 1]
        p_s[:, sl] = (u_s[:, sl] * sg).astype(BF16)
    y = _bdot(p_s[...], w_out_ref[...]) + b_out_ref[...]
    x1 = _layer_norm(DN_ALPHA * x + g1 * y, lng_ref[...], lnb_ref[...])
    x1_ref[...] = x1
    h2_ref[...] = x1 * (1.0 + sc2) + sh2


def _gmlp_layer(x, mod, w_in, b_in, vn_g, vn_b, w_s, b_s, w_out, b_out, ln_g, ln_b, *, tm, write_v):
    bsz, s, d = x.shape
    seg = min(s, A_CHUNK)
    idx = jnp.arange(seg)
    mask = (idx[:, None] // CHUNK) >= (idx[None, :] // CHUNK)
    wm = jnp.where(mask[None], w_s[:, :seg, :seg], 0.0)
    reps = tm // seg
    eye = jnp.eye(reps, dtype=F32)
    ws_big = jnp.einsum("ab,gij->gaibj", eye, wm).reshape(A_GROUPS, tm, tm).astype(BF16)
    bs_big = jnp.tile(b_s[:, :seg].T, (reps, 1))
    out_shape = [jax.ShapeDtypeStruct((bsz, s, d), F32), jax.ShapeDtypeStruct((bsz, s, d), F32)]
    tile = lambda w: pl.BlockSpec((None, tm, w), lambda b, i: (b, i, 0))
    out_specs = [tile(d), tile(d)]
    if write_v:
        out_shape.append(jax.ShapeDtypeStruct((bsz, s, A_HALF), F32))
        out_specs.append(tile(A_HALF))
    return pl.pallas_call(
        functools.partial(_gmlp_kernel, write_v=write_v),
        grid=(bsz, s // tm),
        in_specs=[
            tile(d),
            pl.BlockSpec((None, 6, d), lambda b, i: (b, 0, 0)),
            _const_spec((d, 2 * A_HALF)),
            _const_spec((1, 2 * A_HALF)),
            _const_spec((1, A_HALF)),
            _const_spec((1, A_HALF)),
            _const_spec((A_GROUPS, tm, tm)),
            _const_spec((tm, A_GROUPS)),
            _const_spec((A_HALF, d)),
            _const_spec((1, d)),
            _const_spec((1, d)),
            _const_spec((1, d)),
        ],
        out_specs=out_specs,
        out_shape=out_shape,
        scratch_shapes=[pltpu.VMEM((tm, A_HALF), F32), pltpu.VMEM((tm, A_HALF), F32),
                        pltpu.VMEM((tm, A_HALF), BF16)],
        compiler_params=_cparams(("arbitrary", "arbitrary")),
        name="gmlp_layer",
    )(x, mod, w_in.astype(BF16), b_in.reshape(1, -1), vn_g.reshape(1, -1), vn_b.reshape(1, -1),
      ws_big, bs_big, w_out.astype(BF16), b_out.reshape(1, -1), ln_g.reshape(1, -1), ln_b.reshape(1, -1))


def _router_kernel(h_ref, wr_ref, br_ref, tri_ref, e_ref, g_ref, r_ref, cnt_ref, base_s):
    i = pl.program_id(0)

    @pl.when(i == 0)
    def _():
        base_s[...] = jnp.zeros_like(base_s)

    h = h_ref[...]
    logits = lax.dot_general(wr_ref[...], h, (((1,), (1,)), ((), ())),
                             preferred_element_type=F32,
                             precision=lax.Precision.HIGHEST) + br_ref[...]
    rows = h.shape[0]
    iota = lax.broadcasted_iota(jnp.int32, (N_EXPERTS, rows), 0)
    l = logits
    vals, hots = [], []
    for k in range(TOP_K):
        m = jnp.max(l, axis=0, keepdims=True)
        idx = jnp.min(jnp.where(l == m, iota, N_EXPERTS), axis=0, keepdims=True)
        hot = iota == idx
        vals.append(m)
        hots.append(hot)
        e_ref[k:k + 1, :] = idx
        l = jnp.where(hot, -jnp.inf, l)
    exps = [jnp.exp(v - vals[0]) for v in vals]
    den = exps[0] + exps[1] + exps[2] + exps[3]
    for k in range(TOP_K):
        g_ref[k:k + 1, :] = exps[k] / den
    hot_all = jnp.where(hots[0] | hots[1] | hots[2] | hots[3], 1.0, 0.0)
    before = _bdot(hot_all.astype(BF16), tri_ref[...]) + base_s[:, 0:1]
    for k in range(TOP_K):
        r_ref[k:k + 1, :] = jnp.sum(jnp.where(hots[k], before, 0.0), axis=0, keepdims=True).astype(jnp.int32)
    base_s[...] = base_s[...] + jnp.sum(hot_all, axis=1, keepdims=True)
    cnt_ref[...] = base_s[...].astype(jnp.int32)


def _router(h_all, w_router, b_router):
    n = h_all.shape[0]
    tr = ROUTER_ROWS
    tri = (jnp.arange(tr)[:, None] < jnp.arange(tr)[None, :]).astype(BF16)
    sel = pl.BlockSpec((TOP_K, tr), lambda i: (0, i))
    top_e, gates, rank, cnt = pl.pallas_call(
        _router_kernel,
        grid=(n // tr,),
        in_specs=[
            pl.BlockSpec((tr, D_MODEL), lambda i: (i, 0)),
            _const_spec((N_EXPERTS, D_MODEL)),
            _const_spec((N_EXPERTS, 1)),
            _const_spec((tr, tr)),
        ],
        out_specs=[sel, sel, sel, pl.BlockSpec((N_EXPERTS, 128), lambda i: (0, 0))],
        out_shape=[jax.ShapeDtypeStruct((TOP_K, n), jnp.int32), jax.ShapeDtypeStruct((TOP_K, n), F32),
                   jax.ShapeDtypeStruct((TOP_K, n), jnp.int32),
                   jax.ShapeDtypeStruct((N_EXPERTS, 128), jnp.int32)],
        scratch_shapes=[pltpu.VMEM((N_EXPERTS, 128), F32)],
        compiler_params=_cparams(("arbitrary",)),
        name="moe_router",
    )(h_all, w_router.T, b_router.reshape(N_EXPERTS, 1), tri)
    return top_e, gates, rank, cnt[:, 0]


def _expert_kernel(be_ref, nact_ref, x_ref, wg_ref, wl_ref, bg_ref, bl_ref, wo_ref, bo_ref, y_ref):
    @pl.when(pl.program_id(0) < nact_ref[0])
    def _():
        x = x_ref[...].astype(BF16)
        zg = _bdot(x, wg_ref[...]) + bg_ref[...]
        zl = _bdot(x, wl_ref[...]) + bl_ref[...]
        glu = jnp.minimum(zg, SWIGLU_LIMIT)
        lin = jnp.clip(zl, -SWIGLU_LIMIT, SWIGLU_LIMIT)
        a = glu * jax.nn.sigmoid(SWIGLU_ALPHA * glu) * (lin + 1.0)
        y_ref[...] = _bdot(a.astype(BF16), wo_ref[...]) + bo_ref[...]


def _experts(xpad, block_e, nact, w_glu, w_lin, b_glu, b_lin, w_out, b_out):
    rows = xpad.shape[0]
    tmb = EXPERT_ROWS
    n_blocks = rows // tmb
    row_blk = lambda b, be, na: (jnp.minimum(b, na[0] - 1), 0)
    exp3 = lambda b, be, na: (be[b], 0, 0)
    grid_spec = pltpu.PrefetchScalarGridSpec(
        num_scalar_prefetch=2,
        grid=(n_blocks,),
        in_specs=[
            pl.BlockSpec((tmb, D_MODEL), row_blk),
            pl.BlockSpec((None, D_MODEL, D_EXPERT), exp3),
            pl.BlockSpec((None, D_MODEL, D_EXPERT), exp3),
            pl.BlockSpec((None, 1, D_EXPERT), exp3),
            pl.BlockSpec((None, 1, D_EXPERT), exp3),
            pl.BlockSpec((None, D_EXPERT, D_MODEL), exp3),
            pl.BlockSpec((None, 1, D_MODEL), exp3),
        ],
        out_specs=pl.BlockSpec((tmb, D_MODEL), row_blk),
    )
    return pl.pallas_call(
        _expert_kernel,
        grid_spec=grid_spec,
        out_shape=jax.ShapeDtypeStruct((rows, D_MODEL), F32),
        compiler_params=_cparams(("arbitrary",)),
        name="moe_experts",
    )(block_e, nact, xpad, w_glu, w_lin, b_glu, b_lin, w_out, b_out)


def _moe(h_all, w_router, b_router, w_in, b_in, w_out, b_out):
    n = h_all.shape[0]
    m = n * TOP_K
    tmb = EXPERT_ROWS
    top_e, gates, rank, counts = _router(h_all, w_router, b_router)
    padded = (counts + tmb - 1) // tmb * tmb
    pend = jnp.cumsum(padded)
    pstart = pend - padded
    hot = top_e[..., None] == jnp.arange(N_EXPERTS)[None, None, :]
    dest = jnp.sum(jnp.where(hot, pstart[None, None, :], 0), axis=-1) + rank
    n_blocks = -(-m // tmb) + N_EXPERTS
    rows = n_blocks * tmb
    block_e = jnp.minimum(jnp.searchsorted(pend, jnp.arange(n_blocks) * tmb, side="right"),
                          N_EXPERTS - 1).astype(jnp.int32)
    nact = (pend[-1] // tmb).astype(jnp.int32).reshape(1)
    tok = jnp.broadcast_to(jnp.arange(n, dtype=jnp.int32)[None, :], (TOP_K, n))
    row_tok = jnp.zeros((rows,), jnp.int32).at[dest.reshape(m)].set(tok.reshape(m))
    xpad = h_all[row_tok]
    w_glu = w_in[:, :, 0::2].astype(BF16)
    w_lin = w_in[:, :, 1::2].astype(BF16)
    b_glu = b_in[:, 0::2].reshape(N_EXPERTS, 1, D_EXPERT)
    b_lin = b_in[:, 1::2].reshape(N_EXPERTS, 1, D_EXPERT)
    ypad = _experts(xpad, block_e, nact, w_glu, w_lin, b_glu, b_lin,
                    w_out.astype(BF16), b_out.reshape(N_EXPERTS, 1, D_MODEL))
    f = jnp.zeros((n, D_MODEL), F32)
    for k in range(TOP_K):
        f = f + ypad[dest[k]] * gates[k][:, None]
    return f


def _post_kernel(x1_ref, f_ref, mod_ref, lng_ref, lnb_ref, *rest, with_kv):
    g2 = mod_ref[5:6, :]
    x2 = _layer_norm(DN_ALPHA * x1_ref[...] + g2 * f_ref[...], lng_ref[...], lnb_ref[...])
    if not with_kv:
        (x2_ref,) = rest
        x2_ref[...] = x2
        return
    wa_ref, kng_ref, cos_ref, sin_ref, x2_ref, lat_ref, kr_ref = rest
    x2_ref[...] = x2
    kv = _bdot(x2.astype(BF16), wa_ref[...])
    c = kv[:, :KV_RANK]
    lat_ref[...] = c * lax.rsqrt(jnp.mean(c * c, axis=-1, keepdims=True) + RMS_EPS) * kng_ref[...]
    k = kv[:, KV_RANK:KV_RANK + QK_ROPE]
    k_swapped = kv[:, KV_RANK + QK_ROPE:]
    kr_ref[...] = k * cos_ref[...] + k_swapped * sin_ref[...]


def _swap_halves_cols(w, width):
    shp = w.shape
    w4 = w.reshape(shp[:-1] + (shp[-1] // width, 2, width // 2))
    return w4[..., ::-1, :].reshape(shp)


def _rope_tables(pos, reps):
    inv = 1.0 / (ROPE_BASE ** (jnp.arange(ROPE_HALF, dtype=F32) * (2.0 / QK_ROPE)))
    ang = pos.astype(F32)[:, None] * inv[None, :]
    cos, sin = jnp.cos(ang), jnp.sin(ang)
    cos_t = jnp.tile(jnp.concatenate([cos, cos], -1), (1, reps))
    sin_t = jnp.tile(jnp.concatenate([-sin, sin], -1), (1, reps))
    return cos_t, sin_t


def _post_layer(x1, f_all, row0, mod, ln_g, ln_b, *, tm, kv=None):
    bsz, s, d = x1.shape
    blk0 = row0 // tm
    per_b = s // tm
    tile = lambda w: pl.BlockSpec((None, tm, w), lambda b, i: (b, i, 0))
    in_specs = [
        tile(d),
        pl.BlockSpec((tm, d), lambda b, i: (blk0 + b * per_b + i, 0)),
        pl.BlockSpec((None, 6, d), lambda b, i: (b, 0, 0)),
        _const_spec((1, d)),
        _const_spec((1, d)),
    ]
    args = [x1, f_all, mod, ln_g.reshape(1, -1), ln_b.reshape(1, -1)]
    out_shape = [jax.ShapeDtypeStruct((bsz, s, d), F32)]
    out_specs = [tile(d)]
    if kv is not None:
        w_a, kn_g, cos_t, sin_t = kv
        w_ext = jnp.concatenate([w_a, _swap_halves_cols(w_a[:, KV_RANK:], QK_ROPE)], axis=1).astype(BF16)
        in_specs += [_const_spec(w_ext.shape), _const_spec((1, KV_RANK)),
                     pl.BlockSpec((tm, QK_ROPE), lambda b, i: (i, 0)),
                     pl.BlockSpec((tm, QK_ROPE), lambda b, i: (i, 0))]
        args += [w_ext, kn_g.reshape(1, -1), cos_t, sin_t]
        out_shape += [jax.ShapeDtypeStruct((bsz, s, KV_RANK), F32), jax.ShapeDtypeStruct((bsz, s, QK_ROPE), F32)]
        out_specs += [tile(KV_RANK), tile(QK_ROPE)]
    return pl.pallas_call(
        functools.partial(_post_kernel, with_kv=kv is not None),
        grid=(bsz, per_b),
        in_specs=in_specs,
        out_specs=out_specs,
        out_shape=out_shape,
        compiler_params=_cparams(("arbitrary", "arbitrary")),
        name="post_moe",
    )(*args)


def _qproj_kernel(x_ref, mod_ref, wdq_ref, qng_ref, wqn_ref, wqr_ref, wqrs_ref, wuk_ref, cos_ref, sin_ref, q_ref):
    x = x_ref[...]
    sh1, sc1 = mod_ref[0:1, :], mod_ref[1:2, :]
    h = (x * (1.0 + sc1) + sh1).astype(BF16)
    cq = _bdot(h, wdq_ref[...])
    cq = (cq * lax.rsqrt(jnp.mean(cq * cq, axis=-1, keepdims=True) + RMS_EPS) * qng_ref[...]).astype(BF16)
    qn = _bdot(cq, wqn_ref[...])
    qr = _bdot(cq, wqr_ref[...])
    qrs = _bdot(cq, wqrs_ref[...])
    rope = qr * cos_ref[...] + qrs * sin_ref[...]
    for hd in range(MLA_HEADS):
        ql = _bdot(qn[:, hd * QK_NOPE:(hd + 1) * QK_NOPE].astype(BF16), wuk_ref[hd])
        q_ref[hd, :, 0:KV_RANK] = ql.astype(BF16)
        q_ref[hd, :, KV_RANK:QK_LAT] = rope[:, hd * QK_ROPE:(hd + 1) * QK_ROPE].astype(BF16)


def _qproj_layer(x, mod, w_dq, qn_g, w_qb, w_uk, cos_t, sin_t, *, tm):
    bsz, s, d = x.shape
    w_qn = w_qb[:, :, :QK_NOPE].reshape(Q_RANK, MLA_HEADS * QK_NOPE).astype(BF16)
    w_qr32 = w_qb[:, :, QK_NOPE:].reshape(Q_RANK, MLA_HEADS * QK_ROPE)
    w_qr = w_qr32.astype(BF16)
    w_qrs = _swap_halves_cols(w_qr32, QK_ROPE).astype(BF16)
    wuk_t = jnp.transpose(w_uk, (1, 2, 0)).astype(BF16)
    hr = MLA_HEADS * QK_ROPE
    return pl.pallas_call(
        _qproj_kernel,
        grid=(bsz, s // tm),
        in_specs=[
            pl.BlockSpec((None, tm, d), lambda b, i: (b, i, 0)),
            pl.BlockSpec((None, 6, d), lambda b, i: (b, 0, 0)),
            _const_spec((d, Q_RANK)),
            _const_spec((1, Q_RANK)),
            _const_spec(w_qn.shape),
            _const_spec(w_qr.shape),
            _const_spec(w_qrs.shape),
            _const_spec(wuk_t.shape),
            pl.BlockSpec((tm, hr), lambda b, i: (i, 0)),
            pl.BlockSpec((tm, hr), lambda b, i: (i, 0)),
        ],
        out_specs=pl.BlockSpec((None, MLA_HEADS, tm, QK_LAT), lambda b, i: (b, 0, i, 0)),
        out_shape=jax.ShapeDtypeStruct((bsz, MLA_HEADS, s, QK_LAT), BF16),
        compiler_params=_cparams(("arbitrary", "arbitrary")),
        name="mla_qproj",
    )(x, mod, w_dq.astype(BF16), qn_g.reshape(1, -1), w_qn, w_qr, w_qrs, wuk_t, cos_t, sin_t)


def _attn_kernel(q_ref, k_ref, x_ref, mod_ref, wuv_ref, wo_ref, lng_ref, lnb_ref, x1_ref, h2_ref,
                 m_s, l_s, acc_s, *, tq, big, small, causal, t_total):
    i = pl.program_id(1)
    rows = MLA_HEADS * tq
    q = q_ref[...].reshape(rows, QK_LAT)
    m_s[...] = jnp.full_like(m_s, NEG_BIG)
    l_s[...] = jnp.zeros_like(l_s)
    acc_s[...] = jnp.zeros_like(acc_s)

    def step(start, size, masked):
        k = k_ref[pl.ds(start, size), :]
        s = lax.dot_general(q, k, (((1,), (1,)), ((), ())), preferred_element_type=F32) * ATTN_SCALE
        if masked:
            q_chunk = (i * tq + lax.broadcasted_iota(jnp.int32, (rows, size), 0) % tq) // CHUNK
            k_chunk = (start + lax.broadcasted_iota(jnp.int32, (rows, size), 1)) // CHUNK
            s = jnp.where(k_chunk <= q_chunk, s, -jnp.inf)
        m_old = m_s[...]
        m_new = jnp.maximum(m_old, jnp.max(s, axis=-1, keepdims=True))
        alpha = jnp.exp(m_old - m_new)
        p = jnp.exp(s - m_new)
        l_s[...] = alpha * l_s[...] + jnp.sum(p, axis=-1, keepdims=True)
        acc_s[...] = alpha * acc_s[...] + _bdot(p.astype(BF16), k[:, :KV_RANK])
        m_s[...] = m_new

    if causal:
        per_big = big // tq
        n_big = i // per_big

        def big_body(j, c):
            step(pl.multiple_of(j * big, big), big, False)
            return c

        lax.fori_loop(0, n_big, big_body, 0)

        def small_body(j, c):
            step(pl.multiple_of(j * tq, tq), tq, False)
            return c

        lax.fori_loop(n_big * per_big, i, small_body, 0)
        step(pl.multiple_of(i * tq, tq), tq, True)
    else:
        n_big = t_total // big

        def big_body(j, c):
            step(pl.multiple_of(j * big, big), big, False)
            return c

        lax.fori_loop(0, n_big, big_body, 0)
        if t_total % big:
            step(n_big * big, t_total % big, False)

    o = acc_s[...] / l_s[...]
    heads = [_bdot(o[hd * tq:(hd + 1) * tq].astype(BF16), wuv_ref[hd]) for hd in range(MLA_HEADS)]
    oc = jnp.concatenate(heads, axis=-1).astype(BF16)
    y = _bdot(oc, wo_ref[...])
    x = x_ref[...]
    g1, sh2, sc2 = mod_ref[2:3, :], mod_ref[3:4, :], mod_ref[4:5, :]
    x1 = _layer_norm(DN_ALPHA * x + g1 * y, lng_ref[...], lnb_ref[...])
    x1_ref[...] = x1
    h2_ref[...] = x1 * (1.0 + sc2) + sh2


def _attn_layer(q, kcat, x, mod, w_uv, w_o, ln_g, ln_b, *, tq, causal):
    bsz, s, d = x.shape
    t_total = kcat.shape[1]
    rows = MLA_HEADS * tq
    wuv_t = jnp.transpose(w_uv, (1, 0, 2)).astype(BF16)
    tile = pl.BlockSpec((None, tq, d), lambda b, i: (b, i, 0))
    return pl.pallas_call(
        functools.partial(_attn_kernel, tq=tq, big=512, small=tq, causal=causal, t_total=t_total),
        grid=(bsz, s // tq),
        in_specs=[
            pl.BlockSpec((None, MLA_HEADS, tq, QK_LAT), lambda b, i: (b, 0, i, 0)),
            pl.BlockSpec((None, t_total, QK_LAT), lambda b, i: (b, 0, 0)),
            tile,
            pl.BlockSpec((None, 6, d), lambda b, i: (b, 0, 0)),
            _const_spec(wuv_t.shape),
            _const_spec((MLA_HEADS * V_DIM, d)),
            _const_spec((1, d)),
            _const_spec((1, d)),
        ],
        out_specs=[tile, tile],
        out_shape=[jax.ShapeDtypeStruct((bsz, s, d), F32), jax.ShapeDtypeStruct((bsz, s, d), F32)],
        scratch_shapes=[pltpu.VMEM((rows, 1), F32), pltpu.VMEM((rows, 1), F32), pltpu.VMEM((rows, KV_RANK), F32)],
        compiler_params=_cparams(("arbitrary", "arbitrary")),
        name="mla_attention",
    )(q, kcat, x, mod, wuv_t, w_o.astype(BF16), ln_g.reshape(1, -1), ln_b.reshape(1, -1))


def kernel(x_prompt, x_sample, cache_kv_latent, cache_k_rope, c_prompt, c_sample, ada_w, ada_b, ln_g, ln_b,
           a_w_in, a_b_in, a_vn_g, a_vn_b, a_w_s, a_b_s, a_w_out, a_b_out, kv_w_a, kv_norm_g, kv_w_uk, kv_w_uv,
           q_w_a, q_norm_g, q_w_b, q_w_o, moe_w_router, moe_b_router, moe_w_in, moe_b_in, moe_w_out, moe_b_out):
    bp, sp, d = x_prompt.shape
    bs, ss, _ = x_sample.shape
    past = cache_kv_latent.shape[1]
    n_p = bp * sp
    n_s = bs * ss

    mod = _ada_mod(jnp.concatenate([c_prompt, c_sample], axis=0), ada_w, ada_b)
    mod = mod.reshape(DEPTH, bp + bs, 6, d)
    mod_p, mod_s = mod[:, :bp], mod[:, bp:]

    pos_p = jnp.arange(sp, dtype=jnp.int32)
    pos_s = past + jnp.arange(ss, dtype=jnp.int32)

    def moe(l, h2_p, h2_s):
        h_all = jnp.concatenate([h2_p.reshape(n_p, d), h2_s.reshape(n_s, d)], axis=0)
        return _moe(h_all, moe_w_router[l], moe_b_router[l], moe_w_in[l], moe_b_in[l], moe_w_out[l], moe_b_out[l])

    gm = functools.partial(_gmlp_layer, w_in=a_w_in[0], b_in=a_b_in[0], vn_g=a_vn_g[0], vn_b=a_vn_b[0],
                           w_s=a_w_s[0], b_s=a_b_s[0], w_out=a_w_out[0], b_out=a_b_out[0],
                           ln_g=ln_g[0, 0], ln_b=ln_b[0, 0])
    x1_p, h2_p = gm(x_prompt, mod_p[0], tm=256, write_v=False)
    x1_s, h2_s, v_s = gm(x_sample, mod_s[0], tm=ss, write_v=True)
    f_all = moe(0, h2_p, h2_s)
    x2_p, lat_p, kr_p = _post_layer(x1_p, f_all, 0, mod_p[0], ln_g[0, 1], ln_b[0, 1], tm=512,
                                    kv=(kv_w_a, kv_norm_g) + _rope_tables(pos_p, 1))
    x2_s, lat_s, kr_s = _post_layer(x1_s, f_all, n_p, mod_s[0], ln_g[0, 1], ln_b[0, 1], tm=ss,
                                    kv=(kv_w_a, kv_norm_g) + _rope_tables(pos_s, 1))

    k_p = jnp.concatenate([lat_p, kr_p], axis=-1).astype(BF16)
    k_s = jnp.concatenate([jnp.concatenate([cache_kv_latent, lat_s], axis=1),
                           jnp.concatenate([cache_k_rope, kr_s], axis=1)], axis=-1).astype(BF16)
    qp = functools.partial(_qproj_layer, w_dq=q_w_a[0], qn_g=q_norm_g[0], w_qb=q_w_b[0], w_uk=kv_w_uk)
    at = functools.partial(_attn_layer, w_uv=kv_w_uv, w_o=q_w_o[0], ln_g=ln_g[1, 0], ln_b=ln_b[1, 0])
    cos_p, sin_p = _rope_tables(pos_p, MLA_HEADS)
    cos_s, sin_s = _rope_tables(pos_s, MLA_HEADS)
    q_p = qp(x2_p, mod_p[1], cos_t=cos_p, sin_t=sin_p, tm=256)
    q_s = qp(x2_s, mod_s[1], cos_t=cos_s, sin_t=sin_s, tm=ss)
    x3_p, h4_p = at(q_p, k_p, x2_p, mod_p[1], tq=128, causal=True)
    x3_s, h4_s = at(q_s, k_s, x2_s, mod_s[1], tq=ss, causal=False)
    f_all = moe(1, h4_p, h4_s)
    (y_p,) = _post_layer(x3_p, f_all, 0, mod_p[1], ln_g[1, 1], ln_b[1, 1], tm=512)
    (y_s,) = _post_layer(x3_s, f_all, n_p, mod_s[1], ln_g[1, 1], ln_b[1, 1], tm=ss)

    return (y_p, y_s, lat_p, kr_p, lat_s, kr_s, v_s[None])
```

```python
import functools

import jax
import jax.numpy as jnp
from jax import lax
from jax.experimental import pallas as pl
from jax.experimental.pallas import tpu as pltpu

D_MODEL = 1024
DEPTH = 2
CHUNK = 64
A_CHUNK = 128
A_HALF = 2 * D_MODEL
A_GROUPS = 8
A_GROUP_W = A_HALF // A_GROUPS
MLA_HEADS = 8
QK_NOPE = 128
QK_ROPE = 64
ROPE_HALF = QK_ROPE // 2
V_DIM = 128
KV_RANK = 256
Q_RANK = 512
QK_LAT = KV_RANK + QK_ROPE
ROPE_BASE = 10000.0
ATTN_SCALE = (QK_NOPE + QK_ROPE) ** -0.5
N_EXPERTS = 32
TOP_K = 4
D_EXPERT = D_MODEL
SWIGLU_LIMIT = 7.0
SWIGLU_ALPHA = 1.702
DN_ALPHA = (2 * DEPTH) ** 0.25
LN_EPS = 1e-5
RMS_EPS = 1e-6

BF16 = jnp.bfloat16
F32 = jnp.float32

VMEM_LIMIT = 56 * 1024 * 1024
EXPERT_ROWS = 256
ROUTER_ROWS = 512
NEG_BIG = -1e30


def _cparams(sem):
    return pltpu.CompilerParams(dimension_semantics=sem, vmem_limit_bytes=VMEM_LIMIT)


def _const_spec(shape):
    nd = len(shape)
    return pl.BlockSpec(shape, lambda *_: (0,) * nd, pipeline_mode=pl.Buffered(1))


def _layer_norm(r, g, b):
    rc = r - jnp.mean(r, axis=-1, keepdims=True)
    var = jnp.mean(rc * rc, axis=-1, keepdims=True)
    return rc * lax.rsqrt(var + LN_EPS) * g + b


def _gelu_tanh(x):
    c = 0.7978845608028654
    return 0.5 * x * (1.0 + jnp.tanh(c * (x + 0.044715 * (x * x * x))))


def _bdot(a, b):
    return jnp.dot(a, b, preferred_element_type=F32)


def _ada_kernel(c_ref, w_ref, b_ref, o_ref):
    c = c_ref[...]
    s = c * jax.nn.sigmoid(c)
    o_ref[...] = jnp.dot(s, w_ref[...], preferred_element_type=F32,
                         precision=lax.Precision.HIGHEST) + b_ref[...]


def _ada_mod(c_all, ada_w, ada_b):
    nb = c_all.shape[0]
    six_d = ada_w.shape[-1]
    tn = D_MODEL
    return pl.pallas_call(
        _ada_kernel,
        grid=(DEPTH, six_d // tn),
        in_specs=[
            pl.BlockSpec((nb, D_MODEL), lambda l, j: (0, 0)),
            pl.BlockSpec((None, D_MODEL, tn), lambda l, j: (l, 0, j)),
            pl.BlockSpec((None, 1, tn), lambda l, j: (l, 0, j)),
        ],
        out_specs=pl.BlockSpec((None, nb, tn), lambda l, j: (l, 0, j)),
        out_shape=jax.ShapeDtypeStruct((DEPTH, nb, six_d), F32),
        compiler_params=_cparams(("arbitrary", "arbitrary")),
        name="ada_mod",
    )(c_all, ada_w, ada_b.reshape(DEPTH, 1, six_d))


def _gmlp_kernel(x_ref, mod_ref, w_in_ref, b_in_ref, vng_ref, vnb_ref, ws_ref, bs_ref,
                 w_out_ref, b_out_ref, lng_ref, lnb_ref, *rest, write_v):
    if write_v:
        x1_ref, h2_ref, v_ref, u_s, v_s, p_s = rest
    else:
        x1_ref, h2_ref, u_s, v_s, p_s = rest
    x = x_ref[...]
    sh1, sc1, g1 = mod_ref[0:1, :], mod_ref[1:2, :], mod_ref[2:3, :]
    sh2, sc2 = mod_ref[3:4, :], mod_ref[4:5, :]
    h = (x * (1.0 + sc1) + sh1).astype(BF16)
    tm = x.shape[0]
    ch = 512
    n_ch = A_HALF // ch
    for j in range(n_ch):
        sl = slice(j * ch, (j + 1) * ch)
        u_s[:, sl] = _gelu_tanh(_bdot(h, w_in_ref[:, sl]) + b_in_ref[:, sl])
    tot = jnp.zeros((tm, 1), F32)
    for j in range(n_ch):
        sl = slice(j * ch, (j + 1) * ch)
        slw = slice(A_HALF + j * ch, A_HALF + (j + 1) * ch)
        g = _gelu_tanh(_bdot(h, w_in_ref[:, slw]) + b_in_ref[:, slw])
        v_s[:, sl] = g
        tot = tot + jnp.sum(g, axis=-1, keepdims=True)
    mean = tot * (1.0 / A_HALF)
    sq = jnp.zeros((tm, 1), F32)
    for j in range(n_ch):
        sl = slice(j * ch, (j + 1) * ch)
        c = v_s[:, sl] - mean
        sq = sq + jnp.sum(c * c, axis=-1, keepdims=True)
    rstd = lax.rsqrt(sq * (1.0 / A_HALF) + LN_EPS)
    for g in range(A_GROUPS):
        sl = slice(g * A_GROUP_W, (g + 1) * A_GROUP_W)
        vn = (v_s[:, sl] - mean) * rstd * vng_ref[:, sl] + vnb_ref[:, sl]
        if write_v:
            v_ref[:, sl] = vn
        sg = _bdot(ws_ref[g], vn.astype(BF16)) + bs_ref[:, g:g + 1]
        p_s[:, sl] = (u_s[:, sl] * sg).astype(BF16)
    y = _bdot(p_s[...], w_out_ref[...]) + b_out_ref[...]
    x1 = _layer_norm(DN_ALPHA * x + g1 * y, lng_ref[...], lnb_ref[...])
    x1_ref[...] = x1
    h2_ref[...] = x1 * (1.0 + sc2) + sh2


def _gmlp_layer(x, mod, w_in, b_in, vn_g, vn_b, w_s, b_s, w_out, b_out, ln_g, ln_b, *, tm, write_v):
    bsz, s, d = x.shape
    seg = min(s, A_CHUNK)
    idx = jnp.arange(seg)
    mask = (idx[:, None] // CHUNK) >= (idx[None, :] // CHUNK)
    wm = jnp.where(mask[None], w_s[:, :seg, :seg], 0.0)
    reps = tm // seg
    eye = jnp.eye(reps, dtype=F32)
    ws_big = jnp.einsum("ab,gij->gaibj", eye, wm).reshape(A_GROUPS, tm, tm).astype(BF16)
    bs_big = jnp.tile(b_s[:, :seg].T, (reps, 1))
    out_shape = [jax.ShapeDtypeStruct((bsz, s, d), F32), jax.ShapeDtypeStruct((bsz, s, d), F32)]
    tile = lambda w: pl.BlockSpec((None, tm, w), lambda b, i: (b, i, 0))
    out_specs = [tile(d), tile(d)]
    if write_v:
        out_shape.append(jax.ShapeDtypeStruct((bsz, s, A_HALF), F32))
        out_specs.append(tile(A_HALF))
    return pl.pallas_call(
        functools.partial(_gmlp_kernel, write_v=write_v),
        grid=(bsz, s // tm),
        in_specs=[
            tile(d),
            pl.BlockSpec((None, 6, d), lambda b, i: (b, 0, 0)),
            _const_spec((d, 2 * A_HALF)),
            _const_spec((1, 2 * A_HALF)),
            _const_spec((1, A_HALF)),
            _const_spec((1, A_HALF)),
            _const_spec((A_GROUPS, tm, tm)),
            _const_spec((tm, A_GROUPS)),
            _const_spec((A_HALF, d)),
            _const_spec((1, d)),
            _const_spec((1, d)),
            _const_spec((1, d)),
        ],
        out_specs=out_specs,
        out_shape=out_shape,
        scratch_shapes=[pltpu.VMEM((tm, A_HALF), F32), pltpu.VMEM((tm, A_HALF), F32),
                        pltpu.VMEM((tm, A_HALF), BF16)],
        compiler_params=_cparams(("arbitrary", "arbitrary")),
        name="gmlp_layer",
    )(x, mod, w_in.astype(BF16), b_in.reshape(1, -1), vn_g.reshape(1, -1), vn_b.reshape(1, -1),
      ws_big, bs_big, w_out.astype(BF16), b_out.reshape(1, -1), ln_g.reshape(1, -1), ln_b.reshape(1, -1))


def _router_kernel(h_ref, wr_ref, br_ref, tri_ref, e_ref, g_ref, r_ref, cnt_ref, base_s):
    i = pl.program_id(0)

    @pl.when(i == 0)
    def _():
        base_s[...] = jnp.zeros_like(base_s)

    h = h_ref[...]
    logits = lax.dot_general(wr_ref[...], h, (((1,), (1,)), ((), ())),
                             preferred_element_type=F32,
                             precision=lax.Precision.HIGHEST) + br_ref[...]
    rows = h.shape[0]
    iota = lax.broadcasted_iota(jnp.int32, (N_EXPERTS, rows), 0)
    l = logits
    vals, hots = [], []
    for k in range(TOP_K):
        m = jnp.max(l, axis=0, keepdims=True)
        idx = jnp.min(jnp.where(l == m, iota, N_EXPERTS), axis=0, keepdims=True)
        hot = iota == idx
        vals.append(m)
        hots.append(hot)
        e_ref[k:k + 1, :] = idx
        l = jnp.where(hot, -jnp.inf, l)
    exps = [jnp.exp(v - vals[0]) for v in vals]
    den = exps[0] + exps[1] + exps[2] + exps[3]
    for k in range(TOP_K):
        g_ref[k:k + 1, :] = exps[k] / den
    hot_all = jnp.where(hots[0] | hots[1] | hots[2] | hots[3], 1.0, 0.0)
    before = _bdot(hot_all.astype(BF16), tri_ref[...]) + base_s[:, 0:1]
    for k in range(TOP_K):
        r_ref[k:k + 1, :] = jnp.sum(jnp.where(hots[k], before, 0.0), axis=0, keepdims=True).astype(jnp.int32)
    base_s[...] = base_s[...] + jnp.sum(hot_all, axis=1, keepdims=True)
    cnt_ref[...] = base_s[...].astype(jnp.int32)


def _router(h_all, w_router, b_router):
    n = h_all.shape[0]
    tr = ROUTER_ROWS
    tri = (jnp.arange(tr)[:, None] < jnp.arange(tr)[None, :]).astype(BF16)
    sel = pl.BlockSpec((TOP_K, tr), lambda i: (0, i))
    top_e, gates, rank, cnt = pl.pallas_call(
        _router_kernel,
        grid=(n // tr,),
        in_specs=[
            pl.BlockSpec((tr, D_MODEL), lambda i: (i, 0)),
            _const_spec((N_EXPERTS, D_MODEL)),
            _const_spec((N_EXPERTS, 1)),
            _const_spec((tr, tr)),
        ],
        out_specs=[sel, sel, sel, pl.BlockSpec((N_EXPERTS, 128), lambda i: (0, 0))],
        out_shape=[jax.ShapeDtypeStruct((TOP_K, n), jnp.int32), jax.ShapeDtypeStruct((TOP_K, n), F32),
                   jax.ShapeDtypeStruct((TOP_K, n), jnp.int32),
                   jax.ShapeDtypeStruct((N_EXPERTS, 128), jnp.int32)],
        scratch_shapes=[pltpu.VMEM((N_EXPERTS, 128), F32)],
        compiler_params=_cparams(("arbitrary",)),
        name="moe_router",
    )(h_all, w_router.T, b_router.reshape(N_EXPERTS, 1), tri)
    return top_e, gates, rank, cnt[:, 0]


LANES = 128
PREP_ROWS = 64


def _pair_perm(v):
    shp = v.shape
    v4 = v.reshape(shp[:-1] + (shp[-1] // LANES, 2, LANES // 2))
    return jnp.swapaxes(v4, -1, -2).reshape(shp)


def _expert_kernel(be_ref, nact_ref, x_ref, win_ref, bg_ref, bl_ref, wout_ref, bo_ref, y_ref,
                   wg_s, wl_s, wo_t, wo_s):
    b = pl.program_id(0)
    active = b < nact_ref[0]
    fresh = jnp.logical_or(b == 0, be_ref[b] != be_ref[jnp.maximum(b - 1, 0)])

    @pl.when(jnp.logical_and(active, fresh))
    def _():
        even = lax.broadcasted_iota(jnp.int32, (PREP_ROWS, LANES), 1) % 2 == 0

        def split_rows(r, carry):
            rows = pl.ds(pl.multiple_of(r * PREP_ROWS, PREP_ROWS), PREP_ROWS)
            for c in range(D_EXPERT // LANES):
                blk_a = win_ref[rows, 2 * c * LANES:(2 * c + 1) * LANES]
                blk_b = win_ref[rows, (2 * c + 1) * LANES:(2 * c + 2) * LANES]
                wg_s[rows, c * LANES:(c + 1) * LANES] = jnp.where(
                    even, blk_a, pltpu.roll(blk_b, 1, 1)).astype(BF16)
                wl_s[rows, c * LANES:(c + 1) * LANES] = jnp.where(
                    even, pltpu.roll(blk_a, LANES - 1, 1), blk_b).astype(BF16)
            return carry

        lax.fori_loop(0, D_MODEL // PREP_ROWS, split_rows, 0)
        half = LANES // 2
        for c in range(D_EXPERT // LANES):
            for p in range(2):
                for cb in range(D_MODEL // LANES):
                    wo_t[cb, pl.ds(c * LANES + p, half, stride=2), :] = (
                        wout_ref[pl.ds(c * LANES + half * p, half), cb * LANES:(cb + 1) * LANES])

        def cast_rows(r, carry):
            rows = pl.ds(pl.multiple_of(r * PREP_ROWS, PREP_ROWS), PREP_ROWS)
            for cb in range(D_MODEL // LANES):
                wo_s[rows, cb * LANES:(cb + 1) * LANES] = wo_t[cb, rows, :].astype(BF16)
            return carry

        lax.fori_loop(0, D_EXPERT // PREP_ROWS, cast_rows, 0)

    @pl.when(active)
    def _():
        x = x_ref[...].astype(BF16)
        zg = _bdot(x, wg_s[...]) + bg_ref[...]
        zl = _bdot(x, wl_s[...]) + bl_ref[...]
        glu = jnp.minimum(zg, SWIGLU_LIMIT)
        lin = jnp.clip(zl, -SWIGLU_LIMIT, SWIGLU_LIMIT)
        a = glu * jax.nn.sigmoid(SWIGLU_ALPHA * glu) * (lin + 1.0)
        y_ref[...] = _bdot(a.astype(BF16), wo_s[...]) + bo_ref[...]


def _experts(xpad, block_e, nact, w_in, b_glu, b_lin, w_out, b_out):
    rows = xpad.shape[0]
    tmb = EXPERT_ROWS
    n_blocks = rows // tmb
    row_blk = lambda b, be, na: (jnp.minimum(b, na[0] - 1), 0)
    exp3 = lambda b, be, na: (be[b], 0, 0)
    grid_spec = pltpu.PrefetchScalarGridSpec(
        num_scalar_prefetch=2,
        grid=(n_blocks,),
        in_specs=[
            pl.BlockSpec((tmb, D_MODEL), row_blk),
            pl.BlockSpec((None, D_MODEL, 2 * D_EXPERT), exp3),
            pl.BlockSpec((None, 1, D_EXPERT), exp3),
            pl.BlockSpec((None, 1, D_EXPERT), exp3),
            pl.BlockSpec((None, D_EXPERT, D_MODEL), exp3),
            pl.BlockSpec((None, 1, D_MODEL), exp3),
        ],
        out_specs=pl.BlockSpec((tmb, D_MODEL), row_blk),
        scratch_shapes=[pltpu.VMEM((D_MODEL, D_EXPERT), BF16), pltpu.VMEM((D_MODEL, D_EXPERT), BF16),
                        pltpu.VMEM((D_MODEL // LANES, D_EXPERT, LANES), F32),
                        pltpu.VMEM((D_EXPERT, D_MODEL), BF16)],
    )
    return pl.pallas_call(
        _expert_kernel,
        grid_spec=grid_spec,
        out_shape=jax.ShapeDtypeStruct((rows, D_MODEL), F32),
        compiler_params=_cparams(("arbitrary",)),
        name="moe_experts",
    )(block_e, nact, xpad, w_in, b_glu, b_lin, w_out, b_out)


def _moe(h_all, w_router, b_router, w_in, b_in, w_out, b_out):
    n = h_all.shape[0]
    m = n * TOP_K
    tmb = EXPERT_ROWS
    top_e, gates, rank, counts = _router(h_all, w_router, b_router)
    padded = (counts + tmb - 1) // tmb * tmb
    pend = jnp.cumsum(padded)
    pstart = pend - padded
    hot = top_e[..., None] == jnp.arange(N_EXPERTS)[None, None, :]
    dest = jnp.sum(jnp.where(hot, pstart[None, None, :], 0), axis=-1) + rank
    n_blocks = -(-m // tmb) + N_EXPERTS
    rows = n_blocks * tmb
    blk_row0 = jnp.arange(n_blocks, dtype=jnp.int32) * tmb
    block_e = jnp.minimum(jnp.sum(pend[None, :] <= blk_row0[:, None], axis=1), N_EXPERTS - 1).astype(jnp.int32)
    nact = (pend[-1] // tmb).astype(jnp.int32).reshape(1)
    tok = jnp.broadcast_to(jnp.arange(n, dtype=jnp.int32)[None, :], (TOP_K, n))
    row_tok = jnp.zeros((rows,), jnp.int32).at[dest.reshape(m)].set(tok.reshape(m))
    xpad = h_all[row_tok]
    b_glu = _pair_perm(b_in[:, 0::2]).reshape(N_EXPERTS, 1, D_EXPERT)
    b_lin = _pair_perm(b_in[:, 1::2]).reshape(N_EXPERTS, 1, D_EXPERT)
    ypad = _experts(xpad, block_e, nact, w_in, b_glu, b_lin, w_out, b_out.reshape(N_EXPERTS, 1, D_MODEL))
    f = jnp.zeros((n, D_MODEL), F32)
    for k in range(TOP_K):
        f = f + ypad[dest[k]] * gates[k][:, None]
    return f


def _post_kernel(x1_ref, f_ref, mod_ref, lng_ref, lnb_ref, *rest, with_kv):
    g2 = mod_ref[5:6, :]
    x2 = _layer_norm(DN_ALPHA * x1_ref[...] + g2 * f_ref[...], lng_ref[...], lnb_ref[...])
    if not with_kv:
        (x2_ref,) = rest
        x2_ref[...] = x2
        return
    wa_ref, kng_ref, cos_ref, sin_ref, x2_ref, lat_ref, kr_ref = rest
    x2_ref[...] = x2
    kv = _bdot(x2.astype(BF16), wa_ref[...])
    c = kv[:, :KV_RANK]
    lat_ref[...] = c * lax.rsqrt(jnp.mean(c * c, axis=-1, keepdims=True) + RMS_EPS) * kng_ref[...]
    k = kv[:, KV_RANK:KV_RANK + QK_ROPE]
    k_swapped = kv[:, KV_RANK + QK_ROPE:]
    kr_ref[...] = k * cos_ref[...] + k_swapped * sin_ref[...]


def _swap_halves_cols(w, width):
    shp = w.shape
    w4 = w.reshape(shp[:-1] + (shp[-1] // width, 2, width // 2))
    return w4[..., ::-1, :].reshape(shp)


def _rope_tables(pos, reps):
    inv = 1.0 / (ROPE_BASE ** (jnp.arange(ROPE_HALF, dtype=F32) * (2.0 / QK_ROPE)))
    ang = pos.astype(F32)[:, None] * inv[None, :]
    cos, sin = jnp.cos(ang), jnp.sin(ang)
    cos_t = jnp.tile(jnp.concatenate([cos, cos], -1), (1, reps))
    sin_t = jnp.tile(jnp.concatenate([-sin, sin], -1), (1, reps))
    return cos_t, sin_t


def _post_layer(x1, f_all, row0, mod, ln_g, ln_b, *, tm, kv=None):
    bsz, s, d = x1.shape
    blk0 = row0 // tm
    per_b = s // tm
    tile = lambda w: pl.BlockSpec((None, tm, w), lambda b, i: (b, i, 0))
    in_specs = [
        tile(d),
        pl.BlockSpec((tm, d), lambda b, i: (blk0 + b * per_b + i, 0)),
        pl.BlockSpec((None, 6, d), lambda b, i: (b, 0, 0)),
        _const_spec((1, d)),
        _const_spec((1, d)),
    ]
    args = [x1, f_all, mod, ln_g.reshape(1, -1), ln_b.reshape(1, -1)]
    out_shape = [jax.ShapeDtypeStruct((bsz, s, d), F32)]
    out_specs = [tile(d)]
    if kv is not None:
        w_a, kn_g, cos_t, sin_t = kv
        w_ext = jnp.concatenate([w_a, _swap_halves_cols(w_a[:, KV_RANK:], QK_ROPE)], axis=1).astype(BF16)
        in_specs += [_const_spec(w_ext.shape), _const_spec((1, KV_RANK)),
                     pl.BlockSpec((tm, QK_ROPE), lambda b, i: (i, 0)),
                     pl.BlockSpec((tm, QK_ROPE), lambda b, i: (i, 0))]
        args += [w_ext, kn_g.reshape(1, -1), cos_t, sin_t]
        out_shape += [jax.ShapeDtypeStruct((bsz, s, KV_RANK), F32), jax.ShapeDtypeStruct((bsz, s, QK_ROPE), F32)]
        out_specs += [tile(KV_RANK), tile(QK_ROPE)]
    return pl.pallas_call(
        functools.partial(_post_kernel, with_kv=kv is not None),
        grid=(bsz, per_b),
        in_specs=in_specs,
        out_specs=out_specs,
        out_shape=out_shape,
        compiler_params=_cparams(("arbitrary", "arbitrary")),
        name="post_moe",
    )(*args)


def _qproj_kernel(x_ref, mod_ref, wdq_ref, qng_ref, wqn_ref, wqr_ref, wqrs_ref, wuk_ref, cos_ref, sin_ref, q_ref):
    x = x_ref[...]
    sh1, sc1 = mod_ref[0:1, :], mod_ref[1:2, :]
    h = (x * (1.0 + sc1) + sh1).astype(BF16)
    cq = _bdot(h, wdq_ref[...])
    cq = (cq * lax.rsqrt(jnp.mean(cq * cq, axis=-1, keepdims=True) + RMS_EPS) * qng_ref[...]).astype(BF16)
    qn = _bdot(cq, wqn_ref[...])
    qr = _bdot(cq, wqr_ref[...])
    qrs = _bdot(cq, wqrs_ref[...])
    rope = qr * cos_ref[...] + qrs * sin_ref[...]
    for hd in range(MLA_HEADS):
        ql = _bdot(qn[:, hd * QK_NOPE:(hd + 1) * QK_NOPE].astype(BF16), wuk_ref[hd])
        q_ref[hd, :, 0:KV_RANK] = ql.astype(BF16)
        q_ref[hd, :, KV_RANK:QK_LAT] = rope[:, hd * QK_ROPE:(hd + 1) * QK_ROPE].astype(BF16)


def _qproj_layer(x, mod, w_dq, qn_g, w_qb, w_uk, cos_t, sin_t, *, tm):
    bsz, s, d = x.shape
    w_qn = w_qb[:, :, :QK_NOPE].reshape(Q_RANK, MLA_HEADS * QK_NOPE).astype(BF16)
    w_qr32 = w_qb[:, :, QK_NOPE:].reshape(Q_RANK, MLA_HEADS * QK_ROPE)
    w_qr = w_qr32.astype(BF16)
    w_qrs = _swap_halves_cols(w_qr32, QK_ROPE).astype(BF16)
    wuk_t = jnp.transpose(w_uk, (1, 2, 0)).astype(BF16)
    hr = MLA_HEADS * QK_ROPE
    return pl.pallas_call(
        _qproj_kernel,
        grid=(bsz, s // tm),
        in_specs=[
            pl.BlockSpec((None, tm, d), lambda b, i: (b, i, 0)),
            pl.BlockSpec((None, 6, d), lambda b, i: (b, 0, 0)),
            _const_spec((d, Q_RANK)),
            _const_spec((1, Q_RANK)),
            _const_spec(w_qn.shape),
            _const_spec(w_qr.shape),
            _const_spec(w_qrs.shape),
            _const_spec(wuk_t.shape),
            pl.BlockSpec((tm, hr), lambda b, i: (i, 0)),
            pl.BlockSpec((tm, hr), lambda b, i: (i, 0)),
        ],
        out_specs=pl.BlockSpec((None, MLA_HEADS, tm, QK_LAT), lambda b, i: (b, 0, i, 0)),
        out_shape=jax.ShapeDtypeStruct((bsz, MLA_HEADS, s, QK_LAT), BF16),
        compiler_params=_cparams(("arbitrary", "arbitrary")),
        name="mla_qproj",
    )(x, mod, w_dq.astype(BF16), qn_g.reshape(1, -1), w_qn, w_qr, w_qrs, wuk_t, cos_t, sin_t)


def _attn_kernel(q_ref, k_ref, x_ref, mod_ref, wuv_ref, wo_ref, lng_ref, lnb_ref, x1_ref, h2_ref,
                 m_s, l_s, acc_s, *, tq, big, small, causal, t_total):
    i = pl.program_id(1)
    rows = MLA_HEADS * tq
    q = q_ref[...].reshape(rows, QK_LAT)
    m_s[...] = jnp.full_like(m_s, NEG_BIG)
    l_s[...] = jnp.zeros_like(l_s)
    acc_s[...] = jnp.zeros_like(acc_s)

    def step(start, size, masked):
        k = k_ref[pl.ds(start, size), :]
        s = lax.dot_general(q, k, (((1,), (1,)), ((), ())), preferred_element_type=F32) * ATTN_SCALE
        if masked:
            q_chunk = (i * tq + lax.broadcasted_iota(jnp.int32, (rows, size), 0) % tq) // CHUNK
            k_chunk = (start + lax.broadcasted_iota(jnp.int32, (rows, size), 1)) // CHUNK
            s = jnp.where(k_chunk <= q_chunk, s, -jnp.inf)
        m_old = m_s[...]
        m_new = jnp.maximum(m_old, jnp.max(s, axis=-1, keepdims=True))
        alpha = jnp.exp(m_old - m_new)
        p = jnp.exp(s - m_new)
        l_s[...] = alpha * l_s[...] + jnp.sum(p, axis=-1, keepdims=True)
        acc_s[...] = alpha * acc_s[...] + _bdot(p.astype(BF16), k[:, :KV_RANK])
        m_s[...] = m_new

    if causal:
        per_big = big // tq
        n_big = i // per_big

        def big_body(j, c):
            step(pl.multiple_of(j * big, big), big, False)
            return c

        lax.fori_loop(0, n_big, big_body, 0)

        def small_body(j, c):
            step(pl.multiple_of(j * tq, tq), tq, False)
            return c

        lax.fori_loop(n_big * per_big, i, small_body, 0)
        step(pl.multiple_of(i * tq, tq), tq, True)
    else:
        n_big = t_total // big

        def big_body(j, c):
            step(pl.multiple_of(j * big, big), big, False)
            return c

        lax.fori_loop(0, n_big, big_body, 0)
        if t_total % big:
            step(n_big * big, t_total % big, False)

    o = acc_s[...] / l_s[...]
    heads = [_bdot(o[hd * tq:(hd + 1) * tq].astype(BF16), wuv_ref[hd]) for hd in range(MLA_HEADS)]
    oc = jnp.concatenate(heads, axis=-1).astype(BF16)
    y = _bdot(oc, wo_ref[...])
    x = x_ref[...]
    g1, sh2, sc2 = mod_ref[2:3, :], mod_ref[3:4, :], mod_ref[4:5, :]
    x1 = _layer_norm(DN_ALPHA * x + g1 * y, lng_ref[...], lnb_ref[...])
    x1_ref[...] = x1
    h2_ref[...] = x1 * (1.0 + sc2) + sh2


def _attn_layer(q, kcat, x, mod, w_uv, w_o, ln_g, ln_b, *, tq, causal):
    bsz, s, d = x.shape
    t_total = kcat.shape[1]
    rows = MLA_HEADS * tq
    wuv_t = jnp.transpose(w_uv, (1, 0, 2)).astype(BF16)
    tile = pl.BlockSpec((None, tq, d), lambda b, i: (b, i, 0))
    return pl.pallas_call(
        functools.partial(_attn_kernel, tq=tq, big=512, small=tq, causal=causal, t_total=t_total),
        grid=(bsz, s // tq),
        in_specs=[
            pl.BlockSpec((None, MLA_HEADS, tq, QK_LAT), lambda b, i: (b, 0, i, 0)),
            pl.BlockSpec((None, t_total, QK_LAT), lambda b, i: (b, 0, 0)),
            tile,
            pl.BlockSpec((None, 6, d), lambda b, i: (b, 0, 0)),
            _const_spec(wuv_t.shape),
            _const_spec((MLA_HEADS * V_DIM, d)),
            _const_spec((1, d)),
            _const_spec((1, d)),
        ],
        out_specs=[tile, tile],
        out_shape=[jax.ShapeDtypeStruct((bsz, s, d), F32), jax.ShapeDtypeStruct((bsz, s, d), F32)],
        scratch_shapes=[pltpu.VMEM((rows, 1), F32), pltpu.VMEM((rows, 1), F32), pltpu.VMEM((rows, KV_RANK), F32)],
        compiler_params=_cparams(("arbitrary", "arbitrary")),
        name="mla_attention",
    )(q, kcat, x, mod, wuv_t, w_o.astype(BF16), ln_g.reshape(1, -1), ln_b.reshape(1, -1))


def kernel(x_prompt, x_sample, cache_kv_latent, cache_k_rope, c_prompt, c_sample, ada_w, ada_b, ln_g, ln_b,
           a_w_in, a_b_in, a_vn_g, a_vn_b, a_w_s, a_b_s, a_w_out, a_b_out, kv_w_a, kv_norm_g, kv_w_uk, kv_w_uv,
           q_w_a, q_norm_g, q_w_b, q_w_o, moe_w_router, moe_b_router, moe_w_in, moe_b_in, moe_w_out, moe_b_out):
    bp, sp, d = x_prompt.shape
    bs, ss, _ = x_sample.shape
    past = cache_kv_latent.shape[1]
    n_p = bp * sp
    n_s = bs * ss

    mod = _ada_mod(jnp.concatenate([c_prompt, c_sample], axis=0), ada_w, ada_b)
    mod = mod.reshape(DEPTH, bp + bs, 6, d)
    mod_p, mod_s = mod[:, :bp], mod[:, bp:]

    pos_p = jnp.arange(sp, dtype=jnp.int32)
    pos_s = past + jnp.arange(ss, dtype=jnp.int32)

    def moe(l, h2_p, h2_s):
        h_all = jnp.concatenate([h2_p.reshape(n_p, d), h2_s.reshape(n_s, d)], axis=0)
        return _moe(h_all, moe_w_router[l], moe_b_router[l], moe_w_in[l], moe_b_in[l], moe_w_out[l], moe_b_out[l])

    gm = functools.partial(_gmlp_layer, w_in=a_w_in[0], b_in=a_b_in[0], vn_g=a_vn_g[0], vn_b=a_vn_b[0],
                           w_s=a_w_s[0], b_s=a_b_s[0], w_out=a_w_out[0], b_out=a_b_out[0],
                           ln_g=ln_g[0, 0], ln_b=ln_b[0, 0])
    x1_p, h2_p = gm(x_prompt, mod_p[0], tm=256, write_v=False)
    x1_s, h2_s, v_s = gm(x_sample, mod_s[0], tm=ss, write_v=True)
    f_all = moe(0, h2_p, h2_s)
    x2_p, lat_p, kr_p = _post_layer(x1_p, f_all, 0, mod_p[0], ln_g[0, 1], ln_b[0, 1], tm=512,
                                    kv=(kv_w_a, kv_norm_g) + _rope_tables(pos_p, 1))
    x2_s, lat_s, kr_s = _post_layer(x1_s, f_all, n_p, mod_s[0], ln_g[0, 1], ln_b[0, 1], tm=ss,
                                    kv=(kv_w_a, kv_norm_g) + _rope_tables(pos_s, 1))

    k_p = jnp.concatenate([lat_p, kr_p], axis=-1).astype(BF16)
    k_s = jnp.concatenate([jnp.concatenate([cache_kv_latent, lat_s], axis=1),
                           jnp.concatenate([cache_k_rope, kr_s], axis=1)], axis=-1).astype(BF16)
    qp = functools.partial(_qproj_layer, w_dq=q_w_a[0], qn_g=q_norm_g[0], w_qb=q_w_b[0], w_uk=kv_w_uk)
    at = functools.partial(_attn_layer, w_uv=kv_w_uv, w_o=q_w_o[0], ln_g=ln_g[1, 0], ln_b=ln_b[1, 0])
    cos_p, sin_p = _rope_tables(pos_p, MLA_HEADS)
    cos_s, sin_s = _rope_tables(pos_s, MLA_HEADS)
    q_p = qp(x2_p, mod_p[1], cos_t=cos_p, sin_t=sin_p, tm=256)
    q_s = qp(x2_s, mod_s[1], cos_t=cos_s, sin_t=sin_s, tm=ss)
    x3_p, h4_p = at(q_p, k_p, x2_p, mod_p[1], tq=128, causal=True)
    x3_s, h4_s = at(q_s, k_s, x2_s, mod_s[1], tq=ss, causal=False)
    f_all = moe(1, h4_p, h4_s)
    (y_p,) = _post_layer(x3_p, f_all, 0, mod_p[1], ln_g[1, 1], ln_b[1, 1], tm=512)
    (y_s,) = _post_layer(x3_s, f_all, n_p, mod_s[1], ln_g[1, 1], ln_b[1, 1], tm=ss)

    return (y_p, y_s, lat_p, kr_p, lat_s, kr_s, v_s[None])
```

```python
import functools

import jax
import jax.numpy as jnp
from jax import lax
from jax.experimental import pallas as pl
from jax.experimental.pallas import tpu as pltpu

D_MODEL = 1024
DEPTH = 2
CHUNK = 64
A_CHUNK = 128
A_HALF = 2 * D_MODEL
A_GROUPS = 8
A_GROUP_W = A_HALF // A_GROUPS
MLA_HEADS = 8
QK_NOPE = 128
QK_ROPE = 64
ROPE_HALF = QK_ROPE // 2
V_DIM = 128
KV_RANK = 256
Q_RANK = 512
QK_LAT = KV_RANK + QK_ROPE
ROPE_BASE = 10000.0
ATTN_SCALE = (QK_NOPE + QK_ROPE) ** -0.5
N_EXPERTS = 32
TOP_K = 4
D_EXPERT = D_MODEL
SWIGLU_LIMIT = 7.0
SWIGLU_ALPHA = 1.702
DN_ALPHA = (2 * DEPTH) ** 0.25
LN_EPS = 1e-5
RMS_EPS = 1e-6

BF16 = jnp.bfloat16
F32 = jnp.float32

VMEM_LIMIT = 56 * 1024 * 1024
EXPERT_ROWS = 256
ROUTER_ROWS = 512
NEG_BIG = -1e30


def _cparams(sem):
    return pltpu.CompilerParams(dimension_semantics=sem, vmem_limit_bytes=VMEM_LIMIT)


def _const_spec(shape):
    nd = len(shape)
    return pl.BlockSpec(shape, lambda *_: (0,) * nd, pipeline_mode=pl.Buffered(1))


def _layer_norm(r, g, b):
    rc = r - jnp.mean(r, axis=-1, keepdims=True)
    var = jnp.mean(rc * rc, axis=-1, keepdims=True)
    return rc * lax.rsqrt(var + LN_EPS) * g + b


def _gelu_tanh(x):
    c = 0.7978845608028654
    return 0.5 * x * (1.0 + jnp.tanh(c * (x + 0.044715 * (x * x * x))))


def _bdot(a, b):
    return jnp.dot(a, b, preferred_element_type=F32)


def _ada_kernel(c_ref, w_ref, b_ref, o_ref):
    c = c_ref[...]
    s = c * jax.nn.sigmoid(c)
    o_ref[...] = jnp.dot(s, w_ref[...], preferred_element_type=F32,
                         precision=lax.Precision.HIGHEST) + b_ref[...]


def _ada_mod(c_all, ada_w, ada_b):
    nb = c_all.shape[0]
    six_d = ada_w.shape[-1]
    tn = D_MODEL
    return pl.pallas_call(
        _ada_kernel,
        grid=(DEPTH, six_d // tn),
        in_specs=[
            pl.BlockSpec((nb, D_MODEL), lambda l, j: (0, 0)),
            pl.BlockSpec((None, D_MODEL, tn), lambda l, j: (l, 0, j)),
            pl.BlockSpec((None, 1, tn), lambda l, j: (l, 0, j)),
        ],
        out_specs=pl.BlockSpec((None, nb, tn), lambda l, j: (l, 0, j)),
        out_shape=jax.ShapeDtypeStruct((DEPTH, nb, six_d), F32),
        compiler_params=_cparams(("arbitrary", "arbitrary")),
        name="ada_mod",
    )(c_all, ada_w, ada_b.reshape(DEPTH, 1, six_d))


def _gmlp_kernel(x_ref, mod_ref, w_in_ref, b_in_ref, vng_ref, vnb_ref, ws_ref, bs_ref,
                 w_out_ref, b_out_ref, lng_ref, lnb_ref, *rest, write_v):
    if write_v:
        x1_ref, h2_ref, v_ref, u_s, v_s, p_s = rest
    else:
        x1_ref, h2_ref, u_s, v_s, p_s = rest
    x = x_ref[...]
    sh1, sc1, g1 = mod_ref[0:1, :], mod_ref[1:2, :], mod_ref[2:3, :]
    sh2, sc2 = mod_ref[3:4, :], mod_ref[4:5, :]
    h = (x * (1.0 + sc1) + sh1).astype(BF16)
    tm = x.shape[0]
    ch = 512
    n_ch = A_HALF // ch
    for j in range(n_ch):
        sl = slice(j * ch, (j + 1) * ch)
        u_s[:, sl] = _gelu_tanh(_bdot(h, w_in_ref[:, sl]) + b_in_ref[:, sl])
    tot = jnp.zeros((tm, 1), F32)
    for j in range(n_ch):
        sl = slice(j * ch, (j + 1) * ch)
        slw = slice(A_HALF + j * ch, A_HALF + (j + 1) * ch)
        g = _gelu_tanh(_bdot(h, w_in_ref[:, slw]) + b_in_ref[:, slw])
        v_s[:, sl] = g
        tot = tot + jnp.sum(g, axis=-1, keepdims=True)
    mean = tot * (1.0 / A_HALF)
    sq = jnp.zeros((tm, 1), F32)
    for j in range(n_ch):
        sl = slice(j * ch, (j + 1) * ch)
        c = v_s[:, sl] - mean
        sq = sq + jnp.sum(c * c, axis=-1, keepdims=True)
    rstd = lax.rsqrt(sq * (1.0 / A_HALF) + LN_EPS)
    for g in range(A_GROUPS):
        sl = slice(g * A_GROUP_W, (g + 1) * A_GROUP_W)
        vn = (v_s[:, sl] - mean) * rstd * vng_ref[:, sl] + vnb_ref[:, sl]
        if write_v:
            v_ref[:, sl] = vn
        sg = _bdot(ws_ref[g], vn.astype(BF16)) + bs_ref[:, g:g + 1]
        p_s[:, sl] = (u_s[:, sl] * sg).astype(BF16)
    y = _bdot(p_s[...], w_out_ref[...]) + b_out_ref[...]
    x1 = _layer_norm(DN_ALPHA * x + g1 * y, lng_ref[...], lnb_ref[...])
    x1_ref[...] = x1
    h2_ref[...] = x1 * (1.0 + sc2) + sh2


def _gmlp_layer(x, mod, w_in, b_in, vn_g, vn_b, w_s, b_s, w_out, b_out, ln_g, ln_b, *, tm, write_v):
    bsz, s, d = x.shape
    seg = min(s, A_CHUNK)
    idx = jnp.arange(seg)
    mask = (idx[:, None] // CHUNK) >= (idx[None, :] // CHUNK)
    wm = jnp.where(mask[None], w_s[:, :seg, :seg], 0.0)
    reps = tm // seg
    eye = jnp.eye(reps, dtype=F32)
    ws_big = jnp.einsum("ab,gij->gaibj", eye, wm).reshape(A_GROUPS, tm, tm).astype(BF16)
    bs_big = jnp.tile(b_s[:, :seg].T, (reps, 1))
    out_shape = [jax.ShapeDtypeStruct((bsz, s, d), F32), jax.ShapeDtypeStruct((bsz, s, d), F32)]
    tile = lambda w: pl.BlockSpec((None, tm, w), lambda b, i: (b, i, 0))
    out_specs = [tile(d), tile(d)]
    if write_v:
        out_shape.append(jax.ShapeDtypeStruct((bsz, s, A_HALF), F32))
        out_specs.append(tile(A_HALF))
    return pl.pallas_call(
        functools.partial(_gmlp_kernel, write_v=write_v),
        grid=(bsz, s // tm),
        in_specs=[
            tile(d),
            pl.BlockSpec((None, 6, d), lambda b, i: (b, 0, 0)),
            _const_spec((d, 2 * A_HALF)),
            _const_spec((1, 2 * A_HALF)),
            _const_spec((1, A_HALF)),
            _const_spec((1, A_HALF)),
            _const_spec((A_GROUPS, tm, tm)),
            _const_spec((tm, A_GROUPS)),
            _const_spec((A_HALF, d)),
            _const_spec((1, d)),
            _const_spec((1, d)),
            _const_spec((1, d)),
        ],
        out_specs=out_specs,
        out_shape=out_shape,
        scratch_shapes=[pltpu.VMEM((tm, A_HALF), F32), pltpu.VMEM((tm, A_HALF), F32),
                        pltpu.VMEM((tm, A_HALF), BF16)],
        compiler_params=_cparams(("arbitrary", "arbitrary")),
        name="gmlp_layer",
    )(x, mod, w_in.astype(BF16), b_in.reshape(1, -1), vn_g.reshape(1, -1), vn_b.reshape(1, -1),
      ws_big, bs_big, w_out.astype(BF16), b_out.reshape(1, -1), ln_g.reshape(1, -1), ln_b.reshape(1, -1))


def _router_kernel(h_ref, wr_ref, br_ref, tri_ref, e_ref, g_ref, r_ref, cnt_ref, base_s):
    i = pl.program_id(0)

    @pl.when(i == 0)
    def _():
        base_s[...] = jnp.zeros_like(base_s)

    h = h_ref[...]
    logits = lax.dot_general(wr_ref[...], h, (((1,), (1,)), ((), ())),
                             preferred_element_type=F32,
                             precision=lax.Precision.HIGHEST) + br_ref[...]
    rows = h.shape[0]
    iota = lax.broadcasted_iota(jnp.int32, (N_EXPERTS, rows), 0)
    l = logits
    vals, hots = [], []
    for k in range(TOP_K):
        m = jnp.max(l, axis=0, keepdims=True)
        idx = jnp.min(jnp.where(l == m, iota, N_EXPERTS), axis=0, keepdims=True)
        hot = iota == idx
        vals.append(m)
        hots.append(hot)
        e_ref[k:k + 1, :] = idx
        l = jnp.where(hot, -jnp.inf, l)
    exps = [jnp.exp(v - vals[0]) for v in vals]
    den = exps[0] + exps[1] + exps[2] + exps[3]
    for k in range(TOP_K):
        g_ref[k:k + 1, :] = exps[k] / den
    hot_all = jnp.where(hots[0] | hots[1] | hots[2] | hots[3], 1.0, 0.0)
    before = _bdot(hot_all.astype(BF16), tri_ref[...]) + base_s[:, 0:1]
    for k in range(TOP_K):
        r_ref[k:k + 1, :] = jnp.sum(jnp.where(hots[k], before, 0.0), axis=0, keepdims=True).astype(jnp.int32)
    base_s[...] = base_s[...] + jnp.sum(hot_all, axis=1, keepdims=True)
    cnt_ref[...] = base_s[...].astype(jnp.int32)


def _router(h_all, w_router, b_router):
    n = h_all.shape[0]
    tr = ROUTER_ROWS
    tri = (jnp.arange(tr)[:, None] < jnp.arange(tr)[None, :]).astype(BF16)
    sel = pl.BlockSpec((TOP_K, tr), lambda i: (0, i))
    top_e, gates, rank, cnt = pl.pallas_call(
        _router_kernel,
        grid=(n // tr,),
        in_specs=[
            pl.BlockSpec((tr, D_MODEL), lambda i: (i, 0)),
            _const_spec((N_EXPERTS, D_MODEL)),
            _const_spec((N_EXPERTS, 1)),
            _const_spec((tr, tr)),
        ],
        out_specs=[sel, sel, sel, pl.BlockSpec((N_EXPERTS, 128), lambda i: (0, 0))],
        out_shape=[jax.ShapeDtypeStruct((TOP_K, n), jnp.int32), jax.ShapeDtypeStruct((TOP_K, n), F32),
                   jax.ShapeDtypeStruct((TOP_K, n), jnp.int32),
                   jax.ShapeDtypeStruct((N_EXPERTS, 128), jnp.int32)],
        scratch_shapes=[pltpu.VMEM((N_EXPERTS, 128), F32)],
        compiler_params=_cparams(("arbitrary",)),
        name="moe_router",
    )(h_all, w_router.T, b_router.reshape(N_EXPERTS, 1), tri)
    return top_e, gates, rank, cnt[:, 0]


LANES = 128
PREP_ROWS = 64


def _pair_perm(v):
    shp = v.shape
    v4 = v.reshape(shp[:-1] + (shp[-1] // LANES, 2, LANES // 2))
    return jnp.swapaxes(v4, -1, -2).reshape(shp)


def _expert_kernel(be_ref, nact_ref, x_ref, win_ref, bg_ref, bl_ref, wout_ref, bo_ref, y_ref,
                   wg_s, wl_s, wo_t, wo_s):
    b = pl.program_id(0)
    active = b < nact_ref[0]
    fresh = jnp.logical_or(b == 0, be_ref[b] != be_ref[jnp.maximum(b - 1, 0)])

    @pl.when(jnp.logical_and(active, fresh))
    def _():
        even = lax.broadcasted_iota(jnp.int32, (PREP_ROWS, LANES), 1) % 2 == 0

        def split_rows(r, carry):
            rows = pl.ds(pl.multiple_of(r * PREP_ROWS, PREP_ROWS), PREP_ROWS)
            for c in range(D_EXPERT // LANES):
                blk_a = win_ref[rows, 2 * c * LANES:(2 * c + 1) * LANES]
                blk_b = win_ref[rows, (2 * c + 1) * LANES:(2 * c + 2) * LANES]
                wg_s[rows, c * LANES:(c + 1) * LANES] = jnp.where(
                    even, blk_a, pltpu.roll(blk_b, 1, 1)).astype(BF16)
                wl_s[rows, c * LANES:(c + 1) * LANES] = jnp.where(
                    even, pltpu.roll(blk_a, LANES - 1, 1), blk_b).astype(BF16)
            return carry

        lax.fori_loop(0, D_MODEL // PREP_ROWS, split_rows, 0)
        half = LANES // 2
        for c in range(D_EXPERT // LANES):
            for p in range(2):
                for cb in range(D_MODEL // LANES):
                    wo_t[cb, pl.ds(c * LANES + p, half, stride=2), :] = (
                        wout_ref[pl.ds(c * LANES + half * p, half), cb * LANES:(cb + 1) * LANES])

        def cast_rows(r, carry):
            rows = pl.ds(pl.multiple_of(r * PREP_ROWS, PREP_ROWS), PREP_ROWS)
            for cb in range(D_MODEL // LANES):
                wo_s[rows, cb * LANES:(cb + 1) * LANES] = wo_t[cb, rows, :].astype(BF16)
            return carry

        lax.fori_loop(0, D_EXPERT // PREP_ROWS, cast_rows, 0)

    @pl.when(active)
    def _():
        x = x_ref[...].astype(BF16)
        zg = _bdot(x, wg_s[...]) + bg_ref[...]
        zl = _bdot(x, wl_s[...]) + bl_ref[...]
        glu = jnp.minimum(zg, SWIGLU_LIMIT)
        lin = jnp.clip(zl, -SWIGLU_LIMIT, SWIGLU_LIMIT)
        a = glu * jax.nn.sigmoid(SWIGLU_ALPHA * glu) * (lin + 1.0)
        y_ref[...] = _bdot(a.astype(BF16), wo_s[...]) + bo_ref[...]


def _experts(xpad, block_e, nact, w_in, b_glu, b_lin, w_out, b_out):
    rows = xpad.shape[0]
    tmb = EXPERT_ROWS
    n_blocks = rows // tmb
    row_blk = lambda b, be, na: (jnp.maximum(jnp.minimum(b, na[0] - 1), 0), 0)
    exp3 = lambda b, be, na: (be[b], 0, 0)
    grid_spec = pltpu.PrefetchScalarGridSpec(
        num_scalar_prefetch=2,
        grid=(n_blocks,),
        in_specs=[
            pl.BlockSpec((tmb, D_MODEL), row_blk),
            pl.BlockSpec((None, D_MODEL, 2 * D_EXPERT), exp3),
            pl.BlockSpec((None, 1, D_EXPERT), exp3),
            pl.BlockSpec((None, 1, D_EXPERT), exp3),
            pl.BlockSpec((None, D_EXPERT, D_MODEL), exp3),
            pl.BlockSpec((None, 1, D_MODEL), exp3),
        ],
        out_specs=pl.BlockSpec((tmb, D_MODEL), row_blk),
        scratch_shapes=[pltpu.VMEM((D_MODEL, D_EXPERT), BF16), pltpu.VMEM((D_MODEL, D_EXPERT), BF16),
                        pltpu.VMEM((D_MODEL // LANES, D_EXPERT, LANES), F32),
                        pltpu.VMEM((D_EXPERT, D_MODEL), BF16)],
    )
    return pl.pallas_call(
        _expert_kernel,
        grid_spec=grid_spec,
        out_shape=jax.ShapeDtypeStruct((rows, D_MODEL), F32),
        compiler_params=_cparams(("arbitrary",)),
        name="moe_experts",
    )(block_e, nact, xpad, w_in, b_glu, b_lin, w_out, b_out)


def _moe(h_all, w_router, b_router, w_in, b_in, w_out, b_out):
    n = h_all.shape[0]
    m = n * TOP_K
    tmb = EXPERT_ROWS
    top_e, gates, rank, counts = _router(h_all, w_router, b_router)
    padded = (counts + tmb - 1) // tmb * tmb
    pend = jnp.cumsum(padded)
    pstart = pend - padded
    hot = top_e[..., None] == jnp.arange(N_EXPERTS)[None, None, :]
    dest = jnp.sum(jnp.where(hot, pstart[None, None, :], 0), axis=-1) + rank
    n_blocks = -(-m // tmb) + N_EXPERTS
    rows = n_blocks * tmb
    blk_row0 = jnp.arange(n_blocks, dtype=jnp.int32) * tmb
    block_e = jnp.minimum(jnp.sum(pend[None, :] <= blk_row0[:, None], axis=1), N_EXPERTS - 1).astype(jnp.int32)
    nact = (pend[-1] // tmb).astype(jnp.int32).reshape(1)
    tok = jnp.broadcast_to(jnp.arange(n, dtype=jnp.int32)[None, :], (TOP_K, n))
    row_tok = jnp.zeros((rows,), jnp.int32).at[dest.reshape(m)].set(tok.reshape(m))
    xpad = h_all[row_tok]
    b_glu = _pair_perm(b_in[:, 0::2]).reshape(N_EXPERTS, 1, D_EXPERT)
    b_lin = _pair_perm(b_in[:, 1::2]).reshape(N_EXPERTS, 1, D_EXPERT)
    ypad = _experts(xpad, block_e, nact, w_in, b_glu, b_lin, w_out, b_out.reshape(N_EXPERTS, 1, D_MODEL))
    f = jnp.zeros((n, D_MODEL), F32)
    for k in range(TOP_K):
        f = f + ypad[dest[k]] * gates[k][:, None]
    return f


KEY_TILE = 512


def _post_kernel(x1_ref, f_ref, mod_ref, lng_ref, lnb_ref, *rest, with_kv, emit_lat_t):
    g2 = mod_ref[5:6, :]
    x2 = _layer_norm(DN_ALPHA * x1_ref[...] + g2 * f_ref[...], lng_ref[...], lnb_ref[...])
    if not with_kv:
        (x2_ref,) = rest
        x2_ref[...] = x2
        return
    wa_ref, kng_ref, cos_ref, sin_ref, x2_ref, lat_ref, kr_ref, kcat_ref = rest[:8]
    x2_ref[...] = x2
    kv = _bdot(x2.astype(BF16), wa_ref[...])
    c = kv[:, :KV_RANK]
    lat = c * lax.rsqrt(jnp.mean(c * c, axis=-1, keepdims=True) + RMS_EPS) * kng_ref[...]
    k = kv[:, KV_RANK:KV_RANK + QK_ROPE]
    k_swapped = kv[:, KV_RANK + QK_ROPE:]
    kr = k * cos_ref[...] + k_swapped * sin_ref[...]
    lat_ref[...] = lat
    kr_ref[...] = kr
    kcat_ref[:, :KV_RANK] = lat.astype(BF16)
    kcat_ref[:, KV_RANK:] = kr.astype(BF16)
    if emit_lat_t:
        lat_t_ref = rest[8]
        for t in range(lat.shape[0] // KEY_TILE):
            lat_t_ref[t] = lat[t * KEY_TILE:(t + 1) * KEY_TILE, :].T.astype(BF16)


def _swap_halves_cols(w, width):
    shp = w.shape
    w4 = w.reshape(shp[:-1] + (shp[-1] // width, 2, width // 2))
    return w4[..., ::-1, :].reshape(shp)


def _rope_tables(pos, reps):
    inv = 1.0 / (ROPE_BASE ** (jnp.arange(ROPE_HALF, dtype=F32) * (2.0 / QK_ROPE)))
    ang = pos.astype(F32)[:, None] * inv[None, :]
    cos, sin = jnp.cos(ang), jnp.sin(ang)
    cos_t = jnp.tile(jnp.concatenate([cos, cos], -1), (1, reps))
    sin_t = jnp.tile(jnp.concatenate([-sin, sin], -1), (1, reps))
    return cos_t, sin_t


def _post_layer(x1, f_all, row0, mod, ln_g, ln_b, *, tm, kv=None):
    bsz, s, d = x1.shape
    blk0 = row0 // tm
    per_b = s // tm
    tile = lambda w: pl.BlockSpec((None, tm, w), lambda b, i: (b, i, 0))
    in_specs = [
        tile(d),
        pl.BlockSpec((tm, d), lambda b, i: (blk0 + b * per_b + i, 0)),
        pl.BlockSpec((None, 6, d), lambda b, i: (b, 0, 0)),
        _const_spec((1, d)),
        _const_spec((1, d)),
    ]
    args = [x1, f_all, mod, ln_g.reshape(1, -1), ln_b.reshape(1, -1)]
    out_shape = [jax.ShapeDtypeStruct((bsz, s, d), F32)]
    out_specs = [tile(d)]
    emit_lat_t = kv is not None and tm % KEY_TILE == 0
    if kv is not None:
        w_a, kn_g, cos_t, sin_t = kv
        w_ext = jnp.concatenate([w_a, _swap_halves_cols(w_a[:, KV_RANK:], QK_ROPE)], axis=1).astype(BF16)
        in_specs += [_const_spec(w_ext.shape), _const_spec((1, KV_RANK)),
                     pl.BlockSpec((tm, QK_ROPE), lambda b, i: (i, 0)),
                     pl.BlockSpec((tm, QK_ROPE), lambda b, i: (i, 0))]
        args += [w_ext, kn_g.reshape(1, -1), cos_t, sin_t]
        out_shape += [jax.ShapeDtypeStruct((bsz, s, KV_RANK), F32), jax.ShapeDtypeStruct((bsz, s, QK_ROPE), F32),
                      jax.ShapeDtypeStruct((bsz, s, QK_LAT), BF16)]
        out_specs += [tile(KV_RANK), tile(QK_ROPE), tile(QK_LAT)]
        if emit_lat_t:
            nt = tm // KEY_TILE
            out_shape.append(jax.ShapeDtypeStruct((bsz, s // KEY_TILE, KV_RANK, KEY_TILE), BF16))
            out_specs.append(pl.BlockSpec((None, nt, KV_RANK, KEY_TILE), lambda b, i: (b, i, 0, 0)))
    return pl.pallas_call(
        functools.partial(_post_kernel, with_kv=kv is not None, emit_lat_t=emit_lat_t),
        grid=(bsz, per_b),
        in_specs=in_specs,
        out_specs=out_specs,
        out_shape=out_shape,
        compiler_params=_cparams(("arbitrary", "arbitrary")),
        name="post_moe",
    )(*args)


def _cache_prep_kernel(lat_ref, kr_ref, kcat_ref, lat_t_ref):
    lat = lat_ref[...]
    kcat_ref[:, :KV_RANK] = lat.astype(BF16)
    kcat_ref[:, KV_RANK:] = kr_ref[...].astype(BF16)
    lat_t_ref[...] = lat.T.astype(BF16)


def _cache_prep(cache_lat, cache_kr):
    bsz, t, _ = cache_lat.shape
    return pl.pallas_call(
        _cache_prep_kernel,
        grid=(bsz, t // KEY_TILE),
        in_specs=[pl.BlockSpec((None, KEY_TILE, KV_RANK), lambda b, i: (b, i, 0)),
                  pl.BlockSpec((None, KEY_TILE, QK_ROPE), lambda b, i: (b, i, 0))],
        out_specs=[pl.BlockSpec((None, KEY_TILE, QK_LAT), lambda b, i: (b, i, 0)),
                   pl.BlockSpec((None, None, KV_RANK, KEY_TILE), lambda b, i: (b, i, 0, 0))],
        out_shape=[jax.ShapeDtypeStruct((bsz, t, QK_LAT), BF16),
                   jax.ShapeDtypeStruct((bsz, t // KEY_TILE, KV_RANK, KEY_TILE), BF16)],
        compiler_params=_cparams(("arbitrary", "arbitrary")),
        name="cache_prep",
    )(cache_lat, cache_kr)


Q_GROUP = 512


def _qproj_kernel(x_ref, mod_ref, wdq_ref, qng_ref, wqn_ref, wqr_ref, wqrs_ref, wuk_ref, cos_ref, sin_ref, q_ref,
                  *, tq):
    x = x_ref[...]
    sh1, sc1 = mod_ref[0:1, :], mod_ref[1:2, :]
    h = (x * (1.0 + sc1) + sh1).astype(BF16)
    tm = x.shape[0]
    cq = lax.dot_general(wdq_ref[...], h, (((1,), (1,)), ((), ())), preferred_element_type=F32)
    cq = (cq * lax.rsqrt(jnp.mean(cq * cq, axis=0, keepdims=True) + RMS_EPS) * qng_ref[...]).astype(BF16)
    qn = _bdot(wqn_ref[...], cq)
    qr = _bdot(wqr_ref[...], cq)
    qrs = _bdot(wqrs_ref[...], cq)
    rope = (qr * cos_ref[...] + qrs * sin_ref[...]).astype(BF16)
    per_group = Q_GROUP // tq
    for hd in range(MLA_HEADS):
        ql = _bdot(wuk_ref[hd], qn[hd * QK_NOPE:(hd + 1) * QK_NOPE, :].astype(BF16)).astype(BF16)
        g, off = hd // per_group, (hd % per_group) * tq
        for t in range(tm // tq):
            q_ref[t, g, 0:KV_RANK, off:off + tq] = ql[:, t * tq:(t + 1) * tq]
            q_ref[t, g, KV_RANK:QK_LAT, off:off + tq] = rope[hd * QK_ROPE:(hd + 1) * QK_ROPE, t * tq:(t + 1) * tq]


def _qproj_layer(x, mod, w_dq, qn_g, w_qb, w_uk, cos_t, sin_t, *, tm, tq):
    bsz, s, d = x.shape
    w_qn = w_qb[:, :, :QK_NOPE].reshape(Q_RANK, MLA_HEADS * QK_NOPE)
    w_qr = w_qb[:, :, QK_NOPE:].reshape(Q_RANK, MLA_HEADS * QK_ROPE)
    w_qrs = _swap_halves_cols(w_qr, QK_ROPE)
    wuk_h = jnp.transpose(w_uk, (1, 0, 2)).astype(BF16)
    hr = MLA_HEADS * QK_ROPE
    n_groups = MLA_HEADS * tq // Q_GROUP
    nt = tm // tq
    return pl.pallas_call(
        functools.partial(_qproj_kernel, tq=tq),
        grid=(bsz, s // tm),
        in_specs=[
            pl.BlockSpec((None, tm, d), lambda b, i: (b, i, 0)),
            pl.BlockSpec((None, 6, d), lambda b, i: (b, 0, 0)),
            _const_spec((Q_RANK, d)),
            _const_spec((Q_RANK, 1)),
            _const_spec((MLA_HEADS * QK_NOPE, Q_RANK)),
            _const_spec((hr, Q_RANK)),
            _const_spec((hr, Q_RANK)),
            _const_spec(wuk_h.shape),
            pl.BlockSpec((hr, tm), lambda b, i: (0, i)),
            pl.BlockSpec((hr, tm), lambda b, i: (0, i)),
        ],
        out_specs=pl.BlockSpec((None, nt, n_groups, QK_LAT, Q_GROUP), lambda b, i: (b, i, 0, 0, 0)),
        out_shape=jax.ShapeDtypeStruct((bsz, s // tq, n_groups, QK_LAT, Q_GROUP), BF16),
        compiler_params=_cparams(("arbitrary", "arbitrary")),
        name="mla_qproj",
    )(x, mod, w_dq.T.astype(BF16), qn_g.reshape(-1, 1), w_qn.T.astype(BF16), w_qr.T.astype(BF16),
      w_qrs.T.astype(BF16), wuk_h, cos_t.T, sin_t.T)


EXP2_SCALE = ATTN_SCALE * 1.4426950408889634


def _attn_kernel(q_ref, k_ref, vt_ref, *rest, tq, causal, n_main, tail):
    if tail:
        kt_ref, vtt_ref = rest[:2]
        rest = rest[2:]
    x_ref, mod_ref, wuv_ref, wo_ref, lng_ref, lnb_ref, x1_ref, h2_ref, m_s, l_s, acc_s, ta_s, tb_s = rest
    i = pl.program_id(1)
    n_groups = q_ref.shape[0]
    per_group = Q_GROUP // tq
    m_s[...] = jnp.full_like(m_s, NEG_BIG)
    l_s[...] = jnp.zeros_like(l_s)
    acc_s[...] = jnp.zeros_like(acc_s)

    def key_rows(j):
        return k_ref[pl.ds(pl.multiple_of(j * KEY_TILE, KEY_TILE), KEY_TILE), :]

    def softmax_step(g, t, vt, key0, masked):
        t = t * EXP2_SCALE
        if masked:
            n = t.shape[0]
            k_chunk = (key0 + lax.broadcasted_iota(jnp.int32, (n, Q_GROUP), 0)) // CHUNK
            q_chunk = (i * tq + lax.broadcasted_iota(jnp.int32, (n, Q_GROUP), 1) % tq) // CHUNK
            t = jnp.where(k_chunk <= q_chunk, t, -jnp.inf)
        m_old = m_s[g]
        m_new = jnp.maximum(m_old, jnp.max(t, axis=0, keepdims=True))
        alpha = jnp.exp2(m_old - m_new)
        p = jnp.exp2(t - m_new)
        l_s[g] = alpha * l_s[g] + jnp.sum(p, axis=0, keepdims=True)
        acc_s[g] = alpha * acc_s[g] + _bdot(vt, p.astype(BF16))
        m_s[g] = m_new

    def stage(cur, nxt, j, masked, make_next):
        vt = vt_ref[j]
        k_next = key_rows(j + 1) if make_next else None
        for g in range(n_groups):
            if make_next:
                nxt[g] = _bdot(k_next, q_ref[g])
            softmax_step(g, cur[g], vt, j * KEY_TILE, masked)

    n_open = (i * tq) // KEY_TILE if causal else n_main - 1
    k0 = key_rows(0)
    for g in range(n_groups):
        ta_s[g] = _bdot(k0, q_ref[g])

    def pair_body(p, c):
        stage(ta_s, tb_s, 2 * p, False, True)
        stage(tb_s, ta_s, 2 * p + 1, False, True)
        return c

    lax.fori_loop(0, n_open // 2, pair_body, 0)
    if causal:
        @pl.when(n_open % 2 == 0)
        def _():
            stage(ta_s, tb_s, n_open, True, False)

        @pl.when(n_open % 2 == 1)
        def _():
            stage(ta_s, tb_s, n_open - 1, False, True)
            stage(tb_s, ta_s, n_open, True, False)
    elif n_open % 2 == 0:
        stage(ta_s, tb_s, n_open, False, False)
    else:
        stage(ta_s, tb_s, n_open - 1, False, True)
        stage(tb_s, ta_s, n_open, False, False)
    if tail:
        kt = kt_ref[...]
        for g in range(n_groups):
            softmax_step(g, _bdot(kt, q_ref[g]), vtt_ref[...], n_main * KEY_TILE, False)

    heads = []
    for g in range(n_groups):
        o = (acc_s[g] / l_s[g]).T
        for hh in range(per_group):
            hd = g * per_group + hh
            heads.append(_bdot(o[hh * tq:(hh + 1) * tq].astype(BF16), wuv_ref[hd]))
    oc = jnp.concatenate(heads, axis=-1).astype(BF16)
    y = _bdot(oc, wo_ref[...])
    x = x_ref[...]
    g1, sh2, sc2 = mod_ref[2:3, :], mod_ref[3:4, :], mod_ref[4:5, :]
    x1 = _layer_norm(DN_ALPHA * x + g1 * y, lng_ref[...], lnb_ref[...])
    x1_ref[...] = x1
    h2_ref[...] = x1 * (1.0 + sc2) + sh2


def _attn_layer(q, k_main, vt_main, x, mod, w_uv, w_o, ln_g, ln_b, *, tq, causal, k_tail=None, vt_tail=None):
    bsz, s, d = x.shape
    t_main = k_main.shape[1]
    n_main = t_main // KEY_TILE
    n_groups = q.shape[2]
    tail = k_tail is not None
    wuv_t = jnp.transpose(w_uv, (1, 0, 2)).astype(BF16)
    tile = pl.BlockSpec((None, tq, d), lambda b, i: (b, i, 0))
    in_specs = [
        pl.BlockSpec((None, None, n_groups, QK_LAT, Q_GROUP), lambda b, i: (b, i, 0, 0, 0)),
        pl.BlockSpec((None, t_main, QK_LAT), lambda b, i: (b, 0, 0)),
        pl.BlockSpec((None, n_main, KV_RANK, KEY_TILE), lambda b, i: (b, 0, 0, 0)),
    ]
    args = [q, k_main, vt_main]
    if tail:
        n_tail = k_tail.shape[1]
        in_specs += [pl.BlockSpec((None, n_tail, QK_LAT), lambda b, i: (b, 0, 0)),
                     pl.BlockSpec((None, KV_RANK, n_tail), lambda b, i: (b, 0, 0))]
        args += [k_tail, vt_tail]
    in_specs += [
        tile,
        pl.BlockSpec((None, 6, d), lambda b, i: (b, 0, 0)),
        _const_spec(wuv_t.shape),
        _const_spec((MLA_HEADS * V_DIM, d)),
        _const_spec((1, d)),
        _const_spec((1, d)),
    ]
    args += [x, mod, wuv_t, w_o.astype(BF16), ln_g.reshape(1, -1), ln_b.reshape(1, -1)]
    return pl.pallas_call(
        functools.partial(_attn_kernel, tq=tq, causal=causal, n_main=n_main, tail=tail),
        grid=(bsz, s // tq),
        in_specs=in_specs,
        out_specs=[tile, tile],
        out_shape=[jax.ShapeDtypeStruct((bsz, s, d), F32), jax.ShapeDtypeStruct((bsz, s, d), F32)],
        scratch_shapes=[pltpu.VMEM((n_groups, 1, Q_GROUP), F32), pltpu.VMEM((n_groups, 1, Q_GROUP), F32),
                        pltpu.VMEM((n_groups, KV_RANK, Q_GROUP), F32),
                        pltpu.VMEM((n_groups, KEY_TILE, Q_GROUP), F32),
                        pltpu.VMEM((n_groups, KEY_TILE, Q_GROUP), F32)],
        compiler_params=_cparams(("arbitrary", "arbitrary")),
        name="mla_attention",
    )(*args)


def kernel(x_prompt, x_sample, cache_kv_latent, cache_k_rope, c_prompt, c_sample, ada_w, ada_b, ln_g, ln_b,
           a_w_in, a_b_in, a_vn_g, a_vn_b, a_w_s, a_b_s, a_w_out, a_b_out, kv_w_a, kv_norm_g, kv_w_uk, kv_w_uv,
           q_w_a, q_norm_g, q_w_b, q_w_o, moe_w_router, moe_b_router, moe_w_in, moe_b_in, moe_w_out, moe_b_out):
    bp, sp, d = x_prompt.shape
    bs, ss, _ = x_sample.shape
    past = cache_kv_latent.shape[1]
    n_p = bp * sp
    n_s = bs * ss

    mod = _ada_mod(jnp.concatenate([c_prompt, c_sample], axis=0), ada_w, ada_b)
    mod = mod.reshape(DEPTH, bp + bs, 6, d)
    mod_p, mod_s = mod[:, :bp], mod[:, bp:]

    pos_p = jnp.arange(sp, dtype=jnp.int32)
    pos_s = past + jnp.arange(ss, dtype=jnp.int32)

    def moe(l, h2_p, h2_s):
        h_all = jnp.concatenate([h2_p.reshape(n_p, d), h2_s.reshape(n_s, d)], axis=0)
        return _moe(h_all, moe_w_router[l], moe_b_router[l], moe_w_in[l], moe_b_in[l], moe_w_out[l], moe_b_out[l])

    gm = functools.partial(_gmlp_layer, w_in=a_w_in[0], b_in=a_b_in[0], vn_g=a_vn_g[0], vn_b=a_vn_b[0],
                           w_s=a_w_s[0], b_s=a_b_s[0], w_out=a_w_out[0], b_out=a_b_out[0],
                           ln_g=ln_g[0, 0], ln_b=ln_b[0, 0])
    x1_p, h2_p = gm(x_prompt, mod_p[0], tm=256, write_v=False)
    x1_s, h2_s, v_s = gm(x_sample, mod_s[0], tm=ss, write_v=True)
    f_all = moe(0, h2_p, h2_s)
    x2_p, lat_p, kr_p, kcat_p, lat_t_p = _post_layer(x1_p, f_all, 0, mod_p[0], ln_g[0, 1], ln_b[0, 1], tm=512,
                                                     kv=(kv_w_a, kv_norm_g) + _rope_tables(pos_p, 1))
    x2_s, lat_s, kr_s, kcat_s = _post_layer(x1_s, f_all, n_p, mod_s[0], ln_g[0, 1], ln_b[0, 1], tm=ss,
                                            kv=(kv_w_a, kv_norm_g) + _rope_tables(pos_s, 1))

    kcat_c, lat_t_c = _cache_prep(cache_kv_latent, cache_k_rope)
    lat_t_s = jnp.swapaxes(lat_s, 1, 2).astype(BF16)
    qp = functools.partial(_qproj_layer, w_dq=q_w_a[0], qn_g=q_norm_g[0], w_qb=q_w_b[0], w_uk=kv_w_uk)
    at = functools.partial(_attn_layer, w_uv=kv_w_uv, w_o=q_w_o[0], ln_g=ln_g[1, 0], ln_b=ln_b[1, 0])
    cos_p, sin_p = _rope_tables(pos_p, MLA_HEADS)
    cos_s, sin_s = _rope_tables(pos_s, MLA_HEADS)
    q_p = qp(x2_p, mod_p[1], cos_t=cos_p, sin_t=sin_p, tm=256, tq=128)
    q_s = qp(x2_s, mod_s[1], cos_t=cos_s, sin_t=sin_s, tm=ss, tq=ss)
    x3_p, h4_p = at(q_p, kcat_p, lat_t_p, x2_p, mod_p[1], tq=128, causal=True)
    x3_s, h4_s = at(q_s, kcat_c, lat_t_c, x2_s, mod_s[1], tq=ss, causal=False, k_tail=kcat_s, vt_tail=lat_t_s)
    f_all = moe(1, h4_p, h4_s)
    (y_p,) = _post_layer(x3_p, f_all, 0, mod_p[1], ln_g[1, 1], ln_b[1, 1], tm=512)
    (y_s,) = _post_layer(x3_s, f_all, n_p, mod_s[1], ln_g[1, 1], ln_b[1, 1], tm=ss)

    return (y_p, y_s, lat_p, kr_p, lat_s, kr_s, v_s[None])
```

```python
import functools

import jax
import jax.numpy as jnp
from jax import lax
from jax.experimental import pallas as pl
from jax.experimental.pallas import tpu as pltpu

D_MODEL = 1024
DEPTH = 2
CHUNK = 64
A_CHUNK = 128
A_HALF = 2 * D_MODEL
A_GROUPS = 8
A_GROUP_W = A_HALF // A_GROUPS
MLA_HEADS = 8
QK_NOPE = 128
QK_ROPE = 64
ROPE_HALF = QK_ROPE // 2
V_DIM = 128
KV_RANK = 256
Q_RANK = 512
QK_LAT = KV_RANK + QK_ROPE
ROPE_BASE = 10000.0
ATTN_SCALE = (QK_NOPE + QK_ROPE) ** -0.5
N_EXPERTS = 32
TOP_K = 4
D_EXPERT = D_MODEL
SWIGLU_LIMIT = 7.0
SWIGLU_ALPHA = 1.702
DN_ALPHA = (2 * DEPTH) ** 0.25
LN_EPS = 1e-5
RMS_EPS = 1e-6

BF16 = jnp.bfloat16
F32 = jnp.float32

VMEM_LIMIT = 56 * 1024 * 1024
EXPERT_ROWS = 256
ROUTER_ROWS = 512
NEG_BIG = -1e30


def _cparams(sem):
    return pltpu.CompilerParams(dimension_semantics=sem, vmem_limit_bytes=VMEM_LIMIT)


def _const_spec(shape):
    nd = len(shape)
    return pl.BlockSpec(shape, lambda *_: (0,) * nd, pipeline_mode=pl.Buffered(1))


def _layer_norm(r, g, b):
    rc = r - jnp.mean(r, axis=-1, keepdims=True)
    var = jnp.mean(rc * rc, axis=-1, keepdims=True)
    return rc * lax.rsqrt(var + LN_EPS) * g + b


def _gelu_tanh(x):
    c = 0.7978845608028654
    return 0.5 * x * (1.0 + jnp.tanh(c * (x + 0.044715 * (x * x * x))))


def _bdot(a, b):
    return jnp.dot(a, b, preferred_element_type=F32)


def _ada_kernel(c_ref, w_ref, b_ref, o_ref):
    c = c_ref[...]
    s = c * jax.nn.sigmoid(c)
    o_ref[...] = jnp.dot(s, w_ref[...], preferred_element_type=F32,
                         precision=lax.Precision.HIGHEST) + b_ref[...]


def _ada_mod(c_all, ada_w, ada_b):
    nb = c_all.shape[0]
    six_d = ada_w.shape[-1]
    tn = D_MODEL
    return pl.pallas_call(
        _ada_kernel,
        grid=(DEPTH, six_d // tn),
        in_specs=[
            pl.BlockSpec((nb, D_MODEL), lambda l, j: (0, 0)),
            pl.BlockSpec((None, D_MODEL, tn), lambda l, j: (l, 0, j)),
            pl.BlockSpec((None, 1, tn), lambda l, j: (l, 0, j)),
        ],
        out_specs=pl.BlockSpec((None, nb, tn), lambda l, j: (l, 0, j)),
        out_shape=jax.ShapeDtypeStruct((DEPTH, nb, six_d), F32),
        compiler_params=_cparams(("arbitrary", "arbitrary")),
        name="ada_mod",
    )(c_all, ada_w, ada_b.reshape(DEPTH, 1, six_d))


def _gmlp_kernel(x_ref, mod_ref, w_in_ref, b_in_ref, vng_ref, vnb_ref, ws_ref, bs_ref,
                 w_out_ref, b_out_ref, lng_ref, lnb_ref, *rest, write_v):
    if write_v:
        x1_ref, h2_ref, v_ref, u_s, v_s, p_s = rest
    else:
        x1_ref, h2_ref, u_s, v_s, p_s = rest
    x = x_ref[...]
    sh1, sc1, g1 = mod_ref[0:1, :], mod_ref[1:2, :], mod_ref[2:3, :]
    sh2, sc2 = mod_ref[3:4, :], mod_ref[4:5, :]
    h = (x * (1.0 + sc1) + sh1).astype(BF16)
    tm = x.shape[0]
    ch = 512
    n_ch = A_HALF // ch
    for j in range(n_ch):
        sl = slice(j * ch, (j + 1) * ch)
        u_s[:, sl] = _gelu_tanh(_bdot(h, w_in_ref[:, sl]) + b_in_ref[:, sl])
    tot = jnp.zeros((tm, 1), F32)
    for j in range(n_ch):
        sl = slice(j * ch, (j + 1) * ch)
        slw = slice(A_HALF + j * ch, A_HALF + (j + 1) * ch)
        g = _gelu_tanh(_bdot(h, w_in_ref[:, slw]) + b_in_ref[:, slw])
        v_s[:, sl] = g
        tot = tot + jnp.sum(g, axis=-1, keepdims=True)
    mean = tot * (1.0 / A_HALF)
    sq = jnp.zeros((tm, 1), F32)
    for j in range(n_ch):
        sl = slice(j * ch, (j + 1) * ch)
        c = v_s[:, sl] - mean
        sq = sq + jnp.sum(c * c, axis=-1, keepdims=True)
    rstd = lax.rsqrt(sq * (1.0 / A_HALF) + LN_EPS)
    for g in range(A_GROUPS):
        sl = slice(g * A_GROUP_W, (g + 1) * A_GROUP_W)
        vn = (v_s[:, sl] - mean) * rstd * vng_ref[:, sl] + vnb_ref[:, sl]
        if write_v:
            v_ref[:, sl] = vn
        sg = _bdot(ws_ref[g], vn.astype(BF16)) + bs_ref[:, g:g + 1]
        p_s[:, sl] = (u_s[:, sl] * sg).astype(BF16)
    y = _bdot(p_s[...], w_out_ref[...]) + b_out_ref[...]
    x1 = _layer_norm(DN_ALPHA * x + g1 * y, lng_ref[...], lnb_ref[...])
    x1_ref[...] = x1
    h2_ref[...] = x1 * (1.0 + sc2) + sh2


def _gmlp_layer(x, mod, w_in, b_in, vn_g, vn_b, w_s, b_s, w_out, b_out, ln_g, ln_b, *, tm, write_v):
    bsz, s, d = x.shape
    seg = min(s, A_CHUNK)
    idx = jnp.arange(seg)
    mask = (idx[:, None] // CHUNK) >= (idx[None, :] // CHUNK)
    wm = jnp.where(mask[None], w_s[:, :seg, :seg], 0.0)
    reps = tm // seg
    eye = jnp.eye(reps, dtype=F32)
    ws_big = jnp.einsum("ab,gij->gaibj", eye, wm).reshape(A_GROUPS, tm, tm).astype(BF16)
    bs_big = jnp.tile(b_s[:, :seg].T, (reps, 1))
    out_shape = [jax.ShapeDtypeStruct((bsz, s, d), F32), jax.ShapeDtypeStruct((bsz, s, d), F32)]
    tile = lambda w: pl.BlockSpec((None, tm, w), lambda b, i: (b, i, 0))
    out_specs = [tile(d), tile(d)]
    if write_v:
        out_shape.append(jax.ShapeDtypeStruct((bsz, s, A_HALF), F32))
        out_specs.append(tile(A_HALF))
    return pl.pallas_call(
        functools.partial(_gmlp_kernel, write_v=write_v),
        grid=(bsz, s // tm),
        in_specs=[
            tile(d),
            pl.BlockSpec((None, 6, d), lambda b, i: (b, 0, 0)),
            _const_spec((d, 2 * A_HALF)),
            _const_spec((1, 2 * A_HALF)),
            _const_spec((1, A_HALF)),
            _const_spec((1, A_HALF)),
            _const_spec((A_GROUPS, tm, tm)),
            _const_spec((tm, A_GROUPS)),
            _const_spec((A_HALF, d)),
            _const_spec((1, d)),
            _const_spec((1, d)),
            _const_spec((1, d)),
        ],
        out_specs=out_specs,
        out_shape=out_shape,
        scratch_shapes=[pltpu.VMEM((tm, A_HALF), F32), pltpu.VMEM((tm, A_HALF), F32),
                        pltpu.VMEM((tm, A_HALF), BF16)],
        compiler_params=_cparams(("arbitrary", "arbitrary")),
        name="gmlp_layer",
    )(x, mod, w_in.astype(BF16), b_in.reshape(1, -1), vn_g.reshape(1, -1), vn_b.reshape(1, -1),
      ws_big, bs_big, w_out.astype(BF16), b_out.reshape(1, -1), ln_g.reshape(1, -1), ln_b.reshape(1, -1))


def _router_kernel(h_ref, wr_ref, br_ref, tri_ref, e_ref, g_ref, r_ref, cnt_ref, base_s):
    i = pl.program_id(0)

    @pl.when(i == 0)
    def _():
        base_s[...] = jnp.zeros_like(base_s)

    h = h_ref[...]
    logits = lax.dot_general(wr_ref[...], h, (((1,), (1,)), ((), ())),
                             preferred_element_type=F32,
                             precision=lax.Precision.HIGHEST) + br_ref[...]
    rows = h.shape[0]
    iota = lax.broadcasted_iota(jnp.int32, (N_EXPERTS, rows), 0)
    l = logits
    vals, hots = [], []
    for k in range(TOP_K):
        m = jnp.max(l, axis=0, keepdims=True)
        idx = jnp.min(jnp.where(l == m, iota, N_EXPERTS), axis=0, keepdims=True)
        hot = iota == idx
        vals.append(m)
        hots.append(hot)
        e_ref[k:k + 1, :] = idx
        l = jnp.where(hot, -jnp.inf, l)
    exps = [jnp.exp(v - vals[0]) for v in vals]
    den = exps[0] + exps[1] + exps[2] + exps[3]
    for k in range(TOP_K):
        g_ref[k:k + 1, :] = exps[k] / den
    hot_all = jnp.where(hots[0] | hots[1] | hots[2] | hots[3], 1.0, 0.0)
    before = _bdot(hot_all.astype(BF16), tri_ref[...]) + base_s[:, 0:1]
    for k in range(TOP_K):
        r_ref[k:k + 1, :] = jnp.sum(jnp.where(hots[k], before, 0.0), axis=0, keepdims=True).astype(jnp.int32)
    base_s[...] = base_s[...] + jnp.sum(hot_all, axis=1, keepdims=True)
    cnt_ref[...] = base_s[...].astype(jnp.int32)


def _router(h_all, w_router, b_router):
    n = h_all.shape[0]
    tr = ROUTER_ROWS
    tri = (jnp.arange(tr)[:, None] < jnp.arange(tr)[None, :]).astype(BF16)
    sel = pl.BlockSpec((TOP_K, tr), lambda i: (0, i))
    top_e, gates, rank, cnt = pl.pallas_call(
        _router_kernel,
        grid=(n // tr,),
        in_specs=[
            pl.BlockSpec((tr, D_MODEL), lambda i: (i, 0)),
            _const_spec((N_EXPERTS, D_MODEL)),
            _const_spec((N_EXPERTS, 1)),
            _const_spec((tr, tr)),
        ],
        out_specs=[sel, sel, sel, pl.BlockSpec((N_EXPERTS, 128), lambda i: (0, 0))],
        out_shape=[jax.ShapeDtypeStruct((TOP_K, n), jnp.int32), jax.ShapeDtypeStruct((TOP_K, n), F32),
                   jax.ShapeDtypeStruct((TOP_K, n), jnp.int32),
                   jax.ShapeDtypeStruct((N_EXPERTS, 128), jnp.int32)],
        scratch_shapes=[pltpu.VMEM((N_EXPERTS, 128), F32)],
        compiler_params=_cparams(("arbitrary",)),
        name="moe_router",
    )(h_all, w_router.T, b_router.reshape(N_EXPERTS, 1), tri)
    return top_e, gates, rank, cnt[:, 0]


LANES = 128
PREP_ROWS = 64


def _pair_perm(v):
    shp = v.shape
    v4 = v.reshape(shp[:-1] + (shp[-1] // LANES, 2, LANES // 2))
    return jnp.swapaxes(v4, -1, -2).reshape(shp)


def _expert_kernel(be_ref, nact_ref, x_ref, win_ref, bg_ref, bl_ref, wout_ref, bo_ref, y_ref,
                   wg_s, wl_s, wo_t, wo_s):
    b = pl.program_id(0)
    active = b < nact_ref[0]
    fresh = jnp.logical_or(b == 0, be_ref[b] != be_ref[jnp.maximum(b - 1, 0)])

    @pl.when(jnp.logical_and(active, fresh))
    def _():
        even = lax.broadcasted_iota(jnp.int32, (PREP_ROWS, LANES), 1) % 2 == 0

        def split_rows(r, carry):
            rows = pl.ds(pl.multiple_of(r * PREP_ROWS, PREP_ROWS), PREP_ROWS)
            for c in range(D_EXPERT // LANES):
                blk_a = win_ref[rows, 2 * c * LANES:(2 * c + 1) * LANES]
                blk_b = win_ref[rows, (2 * c + 1) * LANES:(2 * c + 2) * LANES]
                wg_s[rows, c * LANES:(c + 1) * LANES] = jnp.where(
                    even, blk_a, pltpu.roll(blk_b, 1, 1)).astype(BF16)
                wl_s[rows, c * LANES:(c + 1) * LANES] = jnp.where(
                    even, pltpu.roll(blk_a, LANES - 1, 1), blk_b).astype(BF16)
            return carry

        lax.fori_loop(0, D_MODEL // PREP_ROWS, split_rows, 0)
        half = LANES // 2
        for c in range(D_EXPERT // LANES):
            for p in range(2):
                for cb in range(D_MODEL // LANES):
                    wo_t[cb, pl.ds(c * LANES + p, half, stride=2), :] = (
                        wout_ref[pl.ds(c * LANES + half * p, half), cb * LANES:(cb + 1) * LANES])

        def cast_rows(r, carry):
            rows = pl.ds(pl.multiple_of(r * PREP_ROWS, PREP_ROWS), PREP_ROWS)
            for cb in range(D_MODEL // LANES):
                wo_s[rows, cb * LANES:(cb + 1) * LANES] = wo_t[cb, rows, :].astype(BF16)
            return carry

        lax.fori_loop(0, D_EXPERT // PREP_ROWS, cast_rows, 0)

    @pl.when(active)
    def _():
        x = x_ref[...].astype(BF16)
        zg = _bdot(x, wg_s[...]) + bg_ref[...]
        zl = _bdot(x, wl_s[...]) + bl_ref[...]
        glu = jnp.minimum(zg, SWIGLU_LIMIT)
        lin = jnp.clip(zl, -SWIGLU_LIMIT, SWIGLU_LIMIT)
        a = glu * jax.nn.sigmoid(SWIGLU_ALPHA * glu) * (lin + 1.0)
        y_ref[...] = _bdot(a.astype(BF16), wo_s[...]) + bo_ref[...]


def _experts(xpad, block_e, nact, layer, w_in, b_glu, b_lin, w_out, b_out):
    rows = xpad.shape[0]
    tmb = EXPERT_ROWS
    n_blocks = rows // tmb
    row_blk = lambda b, be, na: (jnp.maximum(jnp.minimum(b, na[0] - 1), 0), 0)
    exp3 = lambda b, be, na: (be[b], 0, 0)
    exp4 = lambda b, be, na: (layer, be[b], 0, 0)
    grid_spec = pltpu.PrefetchScalarGridSpec(
        num_scalar_prefetch=2,
        grid=(n_blocks,),
        in_specs=[
            pl.BlockSpec((tmb, D_MODEL), row_blk),
            pl.BlockSpec((None, None, D_MODEL, 2 * D_EXPERT), exp4),
            pl.BlockSpec((None, 1, D_EXPERT), exp3),
            pl.BlockSpec((None, 1, D_EXPERT), exp3),
            pl.BlockSpec((None, None, D_EXPERT, D_MODEL), exp4),
            pl.BlockSpec((None, 1, D_MODEL), exp3),
        ],
        out_specs=pl.BlockSpec((tmb, D_MODEL), row_blk),
        scratch_shapes=[pltpu.VMEM((D_MODEL, D_EXPERT), BF16), pltpu.VMEM((D_MODEL, D_EXPERT), BF16),
                        pltpu.VMEM((D_MODEL // LANES, D_EXPERT, LANES), F32),
                        pltpu.VMEM((D_EXPERT, D_MODEL), BF16)],
    )
    return pl.pallas_call(
        _expert_kernel,
        grid_spec=grid_spec,
        out_shape=jax.ShapeDtypeStruct((rows, D_MODEL), F32),
        compiler_params=_cparams(("arbitrary",)),
        name="moe_experts",
    )(block_e, nact, xpad, w_in, b_glu, b_lin, w_out, b_out)


def _moe(h_all, layer, w_router, b_router, w_in, b_in, w_out, b_out):
    n = h_all.shape[0]
    m = n * TOP_K
    tmb = EXPERT_ROWS
    top_e, gates, rank, counts = _router(h_all, w_router, b_router)
    padded = (counts + tmb - 1) // tmb * tmb
    pend = jnp.cumsum(padded)
    pstart = pend - padded
    hot = top_e[..., None] == jnp.arange(N_EXPERTS)[None, None, :]
    dest = jnp.sum(jnp.where(hot, pstart[None, None, :], 0), axis=-1) + rank
    n_blocks = -(-m // tmb) + N_EXPERTS
    rows = n_blocks * tmb
    blk_row0 = jnp.arange(n_blocks, dtype=jnp.int32) * tmb
    block_e = jnp.minimum(jnp.sum(pend[None, :] <= blk_row0[:, None], axis=1), N_EXPERTS - 1).astype(jnp.int32)
    nact = (pend[-1] // tmb).astype(jnp.int32).reshape(1)
    tok = jnp.broadcast_to(jnp.arange(n, dtype=jnp.int32)[None, :], (TOP_K, n))
    row_tok = jnp.zeros((rows,), jnp.int32).at[dest.reshape(m)].set(tok.reshape(m))
    xpad = h_all[row_tok]
    b_glu = _pair_perm(b_in[:, 0::2]).reshape(N_EXPERTS, 1, D_EXPERT)
    b_lin = _pair_perm(b_in[:, 1::2]).reshape(N_EXPERTS, 1, D_EXPERT)
    ypad = _experts(xpad, block_e, nact, layer, w_in, b_glu, b_lin, w_out, b_out.reshape(N_EXPERTS, 1, D_MODEL))
    f = jnp.zeros((n, D_MODEL), F32)
    for k in range(TOP_K):
        f = f + ypad[dest[k]] * gates[k][:, None]
    return f


def _post_kernel(x1_ref, f_ref, mod_ref, lng_ref, lnb_ref, *rest, with_kv):
    g2 = mod_ref[5:6, :]
    x2 = _layer_norm(DN_ALPHA * x1_ref[...] + g2 * f_ref[...], lng_ref[...], lnb_ref[...])
    if not with_kv:
        (x2_ref,) = rest
        x2_ref[...] = x2
        return
    wa_ref, kng_ref, cos_ref, sin_ref, x2_ref, lat_ref, kr_ref, kcat_ref = rest
    x2_ref[...] = x2
    kv = _bdot(x2.astype(BF16), wa_ref[...])
    c = kv[:, :KV_RANK]
    lat = c * lax.rsqrt(jnp.mean(c * c, axis=-1, keepdims=True) + RMS_EPS) * kng_ref[...]
    k = kv[:, KV_RANK:KV_RANK + QK_ROPE]
    k_swapped = kv[:, KV_RANK + QK_ROPE:]
    kr = k * cos_ref[...] + k_swapped * sin_ref[...]
    lat_ref[...] = lat
    kr_ref[...] = kr
    kcat_ref[:, :KV_RANK] = lat.astype(BF16)
    kcat_ref[:, KV_RANK:] = kr.astype(BF16)


def _swap_halves_cols(w, width):
    shp = w.shape
    w4 = w.reshape(shp[:-1] + (shp[-1] // width, 2, width // 2))
    return w4[..., ::-1, :].reshape(shp)


def _rope_tables(pos, reps):
    inv = 1.0 / (ROPE_BASE ** (jnp.arange(ROPE_HALF, dtype=F32) * (2.0 / QK_ROPE)))
    ang = pos.astype(F32)[:, None] * inv[None, :]
    cos, sin = jnp.cos(ang), jnp.sin(ang)
    cos_t = jnp.tile(jnp.concatenate([cos, cos], -1), (1, reps))
    sin_t = jnp.tile(jnp.concatenate([-sin, sin], -1), (1, reps))
    return cos_t, sin_t


def _post_layer(x1, f_all, row0, mod, ln_g, ln_b, *, tm, kv=None):
    bsz, s, d = x1.shape
    blk0 = row0 // tm
    per_b = s // tm
    tile = lambda w: pl.BlockSpec((None, tm, w), lambda b, i: (b, i, 0))
    in_specs = [
        tile(d),
        pl.BlockSpec((tm, d), lambda b, i: (blk0 + b * per_b + i, 0)),
        pl.BlockSpec((None, 6, d), lambda b, i: (b, 0, 0)),
        _const_spec((1, d)),
        _const_spec((1, d)),
    ]
    args = [x1, f_all, mod, ln_g.reshape(1, -1), ln_b.reshape(1, -1)]
    out_shape = [jax.ShapeDtypeStruct((bsz, s, d), F32)]
    out_specs = [tile(d)]
    if kv is not None:
        w_a, kn_g, cos_t, sin_t = kv
        w_ext = jnp.concatenate([w_a, _swap_halves_cols(w_a[:, KV_RANK:], QK_ROPE)], axis=1).astype(BF16)
        in_specs += [_const_spec(w_ext.shape), _const_spec((1, KV_RANK)),
                     pl.BlockSpec((tm, QK_ROPE), lambda b, i: (i, 0)),
                     pl.BlockSpec((tm, QK_ROPE), lambda b, i: (i, 0))]
        args += [w_ext, kn_g.reshape(1, -1), cos_t, sin_t]
        out_shape += [jax.ShapeDtypeStruct((bsz, s, KV_RANK), F32), jax.ShapeDtypeStruct((bsz, s, QK_ROPE), F32),
                      jax.ShapeDtypeStruct((bsz, s, QK_LAT), BF16)]
        out_specs += [tile(KV_RANK), tile(QK_ROPE), tile(QK_LAT)]
    return pl.pallas_call(
        functools.partial(_post_kernel, with_kv=kv is not None),
        grid=(bsz, per_b),
        in_specs=in_specs,
        out_specs=out_specs,
        out_shape=out_shape,
        compiler_params=_cparams(("arbitrary", "arbitrary")),
        name="post_moe",
    )(*args)


Q_GROUP = 512
KEY_TILE = 512


def _qproj_kernel(x_ref, mod_ref, wdq_ref, qng_ref, wqn_ref, wqr_ref, wqrs_ref, wuk_ref, cos_ref, sin_ref, q_ref,
                  *, tq):
    x = x_ref[...]
    sh1, sc1 = mod_ref[0:1, :], mod_ref[1:2, :]
    h = (x * (1.0 + sc1) + sh1).astype(BF16)
    tm = x.shape[0]
    cq = lax.dot_general(wdq_ref[...], h, (((1,), (1,)), ((), ())), preferred_element_type=F32)
    cq = (cq * lax.rsqrt(jnp.mean(cq * cq, axis=0, keepdims=True) + RMS_EPS) * qng_ref[...]).astype(BF16)
    qn = _bdot(wqn_ref[...], cq)
    qr = _bdot(wqr_ref[...], cq)
    qrs = _bdot(wqrs_ref[...], cq)
    rope = (qr * cos_ref[...] + qrs * sin_ref[...]).astype(BF16)
    per_group = Q_GROUP // tq
    for hd in range(MLA_HEADS):
        ql = _bdot(wuk_ref[hd], qn[hd * QK_NOPE:(hd + 1) * QK_NOPE, :].astype(BF16)).astype(BF16)
        g, off = hd // per_group, (hd % per_group) * tq
        for t in range(tm // tq):
            q_ref[t, g, 0:KV_RANK, off:off + tq] = ql[:, t * tq:(t + 1) * tq]
            q_ref[t, g, KV_RANK:QK_LAT, off:off + tq] = rope[hd * QK_ROPE:(hd + 1) * QK_ROPE, t * tq:(t + 1) * tq]


def _qproj_layer(x, mod, w_dq, qn_g, w_qb, w_uk, cos_t, sin_t, *, tm, tq):
    bsz, s, d = x.shape
    w_qn = w_qb[:, :, :QK_NOPE].reshape(Q_RANK, MLA_HEADS * QK_NOPE)
    w_qr = w_qb[:, :, QK_NOPE:].reshape(Q_RANK, MLA_HEADS * QK_ROPE)
    w_qrs = _swap_halves_cols(w_qr, QK_ROPE)
    wuk_h = jnp.transpose(w_uk, (1, 0, 2)).astype(BF16)
    hr = MLA_HEADS * QK_ROPE
    n_groups = MLA_HEADS * tq // Q_GROUP
    nt = tm // tq
    return pl.pallas_call(
        functools.partial(_qproj_kernel, tq=tq),
        grid=(bsz, s // tm),
        in_specs=[
            pl.BlockSpec((None, tm, d), lambda b, i: (b, i, 0)),
            pl.BlockSpec((None, 6, d), lambda b, i: (b, 0, 0)),
            _const_spec((Q_RANK, d)),
            _const_spec((Q_RANK, 1)),
            _const_spec((MLA_HEADS * QK_NOPE, Q_RANK)),
            _const_spec((hr, Q_RANK)),
            _const_spec((hr, Q_RANK)),
            _const_spec(wuk_h.shape),
            pl.BlockSpec((hr, tm), lambda b, i: (0, i)),
            pl.BlockSpec((hr, tm), lambda b, i: (0, i)),
        ],
        out_specs=pl.BlockSpec((None, nt, n_groups, QK_LAT, Q_GROUP), lambda b, i: (b, i, 0, 0, 0)),
        out_shape=jax.ShapeDtypeStruct((bsz, s // tq, n_groups, QK_LAT, Q_GROUP), BF16),
        compiler_params=_cparams(("arbitrary", "arbitrary")),
        name="mla_qproj",
    )(x, mod, w_dq.T.astype(BF16), qn_g.reshape(-1, 1), w_qn.T.astype(BF16), w_qr.T.astype(BF16),
      w_qrs.T.astype(BF16), wuk_h, cos_t.T, sin_t.T)


EXP2_SCALE = ATTN_SCALE * 1.4426950408889634


def _attn_kernel(q_ref, *rest, tq, causal, n_main, tail, split_keys):
    if split_keys:
        klat_ref, kkr_ref = rest[:2]
        rest = rest[2:]
    else:
        k_ref = rest[0]
        rest = rest[1:]
    if tail:
        kt_ref = rest[0]
        rest = rest[1:]
    x_ref, mod_ref, wuv_ref, wo_ref, lng_ref, lnb_ref, x1_ref, h2_ref, m_s, l_s, acc_s, ta_s, tb_s = rest
    i = pl.program_id(1)
    n_groups = q_ref.shape[0]
    per_group = Q_GROUP // tq
    m_s[...] = jnp.full_like(m_s, NEG_BIG)
    l_s[...] = jnp.zeros_like(l_s)
    acc_s[...] = jnp.zeros_like(acc_s)

    def tile_rows(j):
        return pl.ds(pl.multiple_of(j * KEY_TILE, KEY_TILE), KEY_TILE)

    def key_rows(j):
        if split_keys:
            return jnp.concatenate([klat_ref[tile_rows(j), :].astype(BF16), kkr_ref[tile_rows(j), :].astype(BF16)],
                                   axis=1)
        return k_ref[tile_rows(j), :]

    def value_rows(j):
        if split_keys:
            return klat_ref[tile_rows(j), :].astype(BF16)
        return k_ref[tile_rows(j), :KV_RANK]

    def softmax_step(g, t, v, key0, masked):
        t = t * EXP2_SCALE
        if masked:
            n = t.shape[0]
            k_chunk = (key0 + lax.broadcasted_iota(jnp.int32, (n, Q_GROUP), 0)) // CHUNK
            q_chunk = (i * tq + lax.broadcasted_iota(jnp.int32, (n, Q_GROUP), 1) % tq) // CHUNK
            t = jnp.where(k_chunk <= q_chunk, t, -jnp.inf)
        m_old = m_s[g]
        m_new = jnp.maximum(m_old, jnp.max(t, axis=0, keepdims=True))
        alpha = jnp.exp2(m_old - m_new)
        p = jnp.exp2(t - m_new)
        l_s[g] = alpha * l_s[g] + jnp.sum(p, axis=0, keepdims=True)
        acc_s[g] = alpha * acc_s[g] + lax.dot_general(
            v, p.astype(BF16), (((0,), (0,)), ((), ())), preferred_element_type=F32)
        m_s[g] = m_new

    def stage(cur, nxt, j, masked, make_next):
        v = value_rows(j)
        k_next = key_rows(j + 1) if make_next else None
        for g in range(n_groups):
            if make_next:
                nxt[g] = _bdot(k_next, q_ref[g])
            softmax_step(g, cur[g], v, j * KEY_TILE, masked)

    n_open = (i * tq) // KEY_TILE if causal else n_main - 1
    k0 = key_rows(0)
    for g in range(n_groups):
        ta_s[g] = _bdot(k0, q_ref[g])

    def pair_body(p, c):
        stage(ta_s, tb_s, 2 * p, False, True)
        stage(tb_s, ta_s, 2 * p + 1, False, True)
        return c

    lax.fori_loop(0, n_open // 2, pair_body, 0)
    if causal:
        @pl.when(n_open % 2 == 0)
        def _():
            stage(ta_s, tb_s, n_open, True, False)

        @pl.when(n_open % 2 == 1)
        def _():
            stage(ta_s, tb_s, n_open - 1, False, True)
            stage(tb_s, ta_s, n_open, True, False)
    elif n_open % 2 == 0:
        stage(ta_s, tb_s, n_open, False, False)
    else:
        stage(ta_s, tb_s, n_open - 1, False, True)
        stage(tb_s, ta_s, n_open, False, False)
    if tail:
        kt = kt_ref[...]
        for g in range(n_groups):
            softmax_step(g, _bdot(kt, q_ref[g]), kt[:, :KV_RANK], n_main * KEY_TILE, False)

    heads = []
    for g in range(n_groups):
        o = (acc_s[g] / l_s[g]).T
        for hh in range(per_group):
            hd = g * per_group + hh
            heads.append(_bdot(o[hh * tq:(hh + 1) * tq].astype(BF16), wuv_ref[hd]))
    oc = jnp.concatenate(heads, axis=-1).astype(BF16)
    y = _bdot(oc, wo_ref[...])
    x = x_ref[...]
    g1, sh2, sc2 = mod_ref[2:3, :], mod_ref[3:4, :], mod_ref[4:5, :]
    x1 = _layer_norm(DN_ALPHA * x + g1 * y, lng_ref[...], lnb_ref[...])
    x1_ref[...] = x1
    h2_ref[...] = x1 * (1.0 + sc2) + sh2


def _attn_layer(q, keys, x, mod, w_uv, w_o, ln_g, ln_b, *, tq, causal, k_tail=None):
    bsz, s, d = x.shape
    split_keys = isinstance(keys, tuple)
    keys = keys if split_keys else (keys,)
    t_main = keys[0].shape[1]
    n_main = t_main // KEY_TILE
    n_groups = q.shape[2]
    tail = k_tail is not None
    wuv_t = jnp.transpose(w_uv, (1, 0, 2)).astype(BF16)
    tile = pl.BlockSpec((None, tq, d), lambda b, i: (b, i, 0))
    in_specs = [pl.BlockSpec((None, None, n_groups, QK_LAT, Q_GROUP), lambda b, i: (b, i, 0, 0, 0))]
    in_specs += [pl.BlockSpec((None, t_main, k.shape[2]), lambda b, i: (b, 0, 0)) for k in keys]
    args = [q, *keys]
    if tail:
        in_specs.append(pl.BlockSpec((None, k_tail.shape[1], QK_LAT), lambda b, i: (b, 0, 0)))
        args.append(k_tail)
    in_specs += [
        tile,
        pl.BlockSpec((None, 6, d), lambda b, i: (b, 0, 0)),
        _const_spec(wuv_t.shape),
        _const_spec((MLA_HEADS * V_DIM, d)),
        _const_spec((1, d)),
        _const_spec((1, d)),
    ]
    args += [x, mod, wuv_t, w_o.astype(BF16), ln_g.reshape(1, -1), ln_b.reshape(1, -1)]
    return pl.pallas_call(
        functools.partial(_attn_kernel, tq=tq, causal=causal, n_main=n_main, tail=tail, split_keys=split_keys),
        grid=(bsz, s // tq),
        in_specs=in_specs,
        out_specs=[tile, tile],
        out_shape=[jax.ShapeDtypeStruct((bsz, s, d), F32), jax.ShapeDtypeStruct((bsz, s, d), F32)],
        scratch_shapes=[pltpu.VMEM((n_groups, 1, Q_GROUP), F32), pltpu.VMEM((n_groups, 1, Q_GROUP), F32),
                        pltpu.VMEM((n_groups, KV_RANK, Q_GROUP), F32),
                        pltpu.VMEM((n_groups, KEY_TILE, Q_GROUP), F32),
                        pltpu.VMEM((n_groups, KEY_TILE, Q_GROUP), F32)],
        compiler_params=_cparams(("arbitrary", "arbitrary")),
        name="mla_attention",
    )(*args)


def kernel(x_prompt, x_sample, cache_kv_latent, cache_k_rope, c_prompt, c_sample, ada_w, ada_b, ln_g, ln_b,
           a_w_in, a_b_in, a_vn_g, a_vn_b, a_w_s, a_b_s, a_w_out, a_b_out, kv_w_a, kv_norm_g, kv_w_uk, kv_w_uv,
           q_w_a, q_norm_g, q_w_b, q_w_o, moe_w_router, moe_b_router, moe_w_in, moe_b_in, moe_w_out, moe_b_out):
    bp, sp, d = x_prompt.shape
    bs, ss, _ = x_sample.shape
    past = cache_kv_latent.shape[1]
    n_p = bp * sp
    n_s = bs * ss

    mod = _ada_mod(jnp.concatenate([c_prompt, c_sample], axis=0), ada_w, ada_b)
    mod = mod.reshape(DEPTH, bp + bs, 6, d)
    mod_p, mod_s = mod[:, :bp], mod[:, bp:]

    pos_p = jnp.arange(sp, dtype=jnp.int32)
    pos_s = past + jnp.arange(ss, dtype=jnp.int32)

    def moe(l, h2_p, h2_s):
        h_all = jnp.concatenate([h2_p.reshape(n_p, d), h2_s.reshape(n_s, d)], axis=0)
        return _moe(h_all, l, moe_w_router[l], moe_b_router[l], moe_w_in, moe_b_in[l], moe_w_out, moe_b_out[l])

    gm = functools.partial(_gmlp_layer, w_in=a_w_in[0], b_in=a_b_in[0], vn_g=a_vn_g[0], vn_b=a_vn_b[0],
                           w_s=a_w_s[0], b_s=a_b_s[0], w_out=a_w_out[0], b_out=a_b_out[0],
                           ln_g=ln_g[0, 0], ln_b=ln_b[0, 0])
    x1_p, h2_p = gm(x_prompt, mod_p[0], tm=256, write_v=False)
    x1_s, h2_s, v_s = gm(x_sample, mod_s[0], tm=ss, write_v=True)
    f_all = moe(0, h2_p, h2_s)
    x2_p, lat_p, kr_p, kcat_p = _post_layer(x1_p, f_all, 0, mod_p[0], ln_g[0, 1], ln_b[0, 1], tm=512,
                                            kv=(kv_w_a, kv_norm_g) + _rope_tables(pos_p, 1))
    x2_s, lat_s, kr_s, kcat_s = _post_layer(x1_s, f_all, n_p, mod_s[0], ln_g[0, 1], ln_b[0, 1], tm=ss,
                                            kv=(kv_w_a, kv_norm_g) + _rope_tables(pos_s, 1))

    qp = functools.partial(_qproj_layer, w_dq=q_w_a[0], qn_g=q_norm_g[0], w_qb=q_w_b[0], w_uk=kv_w_uk)
    at = functools.partial(_attn_layer, w_uv=kv_w_uv, w_o=q_w_o[0], ln_g=ln_g[1, 0], ln_b=ln_b[1, 0])
    cos_p, sin_p = _rope_tables(pos_p, MLA_HEADS)
    cos_s, sin_s = _rope_tables(pos_s, MLA_HEADS)
    q_p = qp(x2_p, mod_p[1], cos_t=cos_p, sin_t=sin_p, tm=256, tq=128)
    q_s = qp(x2_s, mod_s[1], cos_t=cos_s, sin_t=sin_s, tm=ss, tq=ss)
    x3_p, h4_p = at(q_p, kcat_p, x2_p, mod_p[1], tq=128, causal=True)
    x3_s, h4_s = at(q_s, (cache_kv_latent, cache_k_rope), x2_s, mod_s[1], tq=ss, causal=False, k_tail=kcat_s)
    f_all = moe(1, h4_p, h4_s)
    (y_p,) = _post_layer(x3_p, f_all, 0, mod_p[1], ln_g[1, 1], ln_b[1, 1], tm=512)
    (y_s,) = _post_layer(x3_s, f_all, n_p, mod_s[1], ln_g[1, 1], ln_b[1, 1], tm=ss)

    return (y_p, y_s, lat_p, kr_p, lat_s, kr_s, v_s[None])
```

```python
import functools

import jax
import jax.numpy as jnp
from jax import lax
from jax.experimental import pallas as pl
from jax.experimental.pallas import tpu as pltpu

D_MODEL = 1024
DEPTH = 2
CHUNK = 64
A_CHUNK = 128
A_HALF = 2 * D_MODEL
A_GROUPS = 8
A_GROUP_W = A_HALF // A_GROUPS
MLA_HEADS = 8
QK_NOPE = 128
QK_ROPE = 64
ROPE_HALF = QK_ROPE // 2
V_DIM = 128
KV_RANK = 256
Q_RANK = 512
QK_LAT = KV_RANK + QK_ROPE
ROPE_BASE = 10000.0
ATTN_SCALE = (QK_NOPE + QK_ROPE) ** -0.5
N_EXPERTS = 32
TOP_K = 4
D_EXPERT = D_MODEL
SWIGLU_LIMIT = 7.0
SWIGLU_ALPHA = 1.702
DN_ALPHA = (2 * DEPTH) ** 0.25
LN_EPS = 1e-5
RMS_EPS = 1e-6

BF16 = jnp.bfloat16
F32 = jnp.float32

VMEM_LIMIT = 56 * 1024 * 1024
EXPERT_ROWS = 256
ROUTER_ROWS = 512
NEG_BIG = -1e30


def _cparams(sem):
    return pltpu.CompilerParams(dimension_semantics=sem, vmem_limit_bytes=VMEM_LIMIT)


def _const_spec(shape):
    nd = len(shape)
    return pl.BlockSpec(shape, lambda *_: (0,) * nd, pipeline_mode=pl.Buffered(1))


def _layer_norm(r, g, b):
    rc = r - jnp.mean(r, axis=-1, keepdims=True)
    var = jnp.mean(rc * rc, axis=-1, keepdims=True)
    return rc * lax.rsqrt(var + LN_EPS) * g + b


def _gelu_tanh(x):
    c = 0.7978845608028654
    return 0.5 * x * (1.0 + jnp.tanh(c * (x + 0.044715 * (x * x * x))))


def _bdot(a, b):
    return jnp.dot(a, b, preferred_element_type=F32)


def _ada_kernel(c_ref, w_ref, b_ref, o_ref):
    c = c_ref[...]
    s = c * jax.nn.sigmoid(c)
    o_ref[...] = jnp.dot(s, w_ref[...], preferred_element_type=F32,
                         precision=lax.Precision.HIGHEST) + b_ref[...]


def _ada_mod(c_all, ada_w, ada_b):
    nb = c_all.shape[0]
    six_d = ada_w.shape[-1]
    tn = D_MODEL
    return pl.pallas_call(
        _ada_kernel,
        grid=(DEPTH, six_d // tn),
        in_specs=[
            pl.BlockSpec((nb, D_MODEL), lambda l, j: (0, 0)),
            pl.BlockSpec((None, D_MODEL, tn), lambda l, j: (l, 0, j)),
            pl.BlockSpec((None, 1, tn), lambda l, j: (l, 0, j)),
        ],
        out_specs=pl.BlockSpec((None, nb, tn), lambda l, j: (l, 0, j)),
        out_shape=jax.ShapeDtypeStruct((DEPTH, nb, six_d), F32),
        compiler_params=_cparams(("arbitrary", "arbitrary")),
        name="ada_mod",
    )(c_all, ada_w, ada_b.reshape(DEPTH, 1, six_d))


def _gmlp_kernel(x_ref, mod_ref, w_in_ref, b_in_ref, vng_ref, vnb_ref, ws_ref, bs_ref,
                 w_out_ref, b_out_ref, lng_ref, lnb_ref, *rest, write_v):
    if write_v:
        x1_ref, h2_ref, v_ref, u_s, v_s, p_s = rest
    else:
        x1_ref, h2_ref, u_s, v_s, p_s = rest
    x = x_ref[...]
    sh1, sc1, g1 = mod_ref[0:1, :], mod_ref[1:2, :], mod_ref[2:3, :]
    sh2, sc2 = mod_ref[3:4, :], mod_ref[4:5, :]
    h = (x * (1.0 + sc1) + sh1).astype(BF16)
    tm = x.shape[0]
    ch = 512
    n_ch = A_HALF // ch
    for j in range(n_ch):
        sl = slice(j * ch, (j + 1) * ch)
        u_s[:, sl] = _gelu_tanh(_bdot(h, w_in_ref[:, sl]) + b_in_ref[:, sl])
    tot = jnp.zeros((tm, 1), F32)
    for j in range(n_ch):
        sl = slice(j * ch, (j + 1) * ch)
        slw = slice(A_HALF + j * ch, A_HALF + (j + 1) * ch)
        g = _gelu_tanh(_bdot(h, w_in_ref[:, slw]) + b_in_ref[:, slw])
        v_s[:, sl] = g
        tot = tot + jnp.sum(g, axis=-1, keepdims=True)
    mean = tot * (1.0 / A_HALF)
    sq = jnp.zeros((tm, 1), F32)
    for j in range(n_ch):
        sl = slice(j * ch, (j + 1) * ch)
        c = v_s[:, sl] - mean
        sq = sq + jnp.sum(c * c, axis=-1, keepdims=True)
    rstd = lax.rsqrt(sq * (1.0 / A_HALF) + LN_EPS)
    for g in range(A_GROUPS):
        sl = slice(g * A_GROUP_W, (g + 1) * A_GROUP_W)
        vn = (v_s[:, sl] - mean) * rstd * vng_ref[:, sl] + vnb_ref[:, sl]
        if write_v:
            v_ref[:, sl] = vn
        sg = _bdot(ws_ref[g], vn.astype(BF16)) + bs_ref[:, g:g + 1]
        p_s[:, sl] = (u_s[:, sl] * sg).astype(BF16)
    y = _bdot(p_s[...], w_out_ref[...]) + b_out_ref[...]
    x1 = _layer_norm(DN_ALPHA * x + g1 * y, lng_ref[...], lnb_ref[...])
    x1_ref[...] = x1
    h2_ref[...] = x1 * (1.0 + sc2) + sh2


def _gmlp_layer(x, mod, w_in, b_in, vn_g, vn_b, w_s, b_s, w_out, b_out, ln_g, ln_b, *, tm, write_v):
    bsz, s, d = x.shape
    seg = min(s, A_CHUNK)
    idx = jnp.arange(seg)
    mask = (idx[:, None] // CHUNK) >= (idx[None, :] // CHUNK)
    wm = jnp.where(mask[None], w_s[:, :seg, :seg], 0.0)
    reps = tm // seg
    eye = jnp.eye(reps, dtype=F32)
    ws_big = jnp.einsum("ab,gij->gaibj", eye, wm).reshape(A_GROUPS, tm, tm).astype(BF16)
    bs_big = jnp.tile(b_s[:, :seg].T, (reps, 1))
    out_shape = [jax.ShapeDtypeStruct((bsz, s, d), F32), jax.ShapeDtypeStruct((bsz, s, d), F32)]
    tile = lambda w: pl.BlockSpec((None, tm, w), lambda b, i: (b, i, 0))
    out_specs = [tile(d), tile(d)]
    if write_v:
        out_shape.append(jax.ShapeDtypeStruct((bsz, s, A_HALF), F32))
        out_specs.append(tile(A_HALF))
    return pl.pallas_call(
        functools.partial(_gmlp_kernel, write_v=write_v),
        grid=(bsz, s // tm),
        in_specs=[
            tile(d),
            pl.BlockSpec((None, 6, d), lambda b, i: (b, 0, 0)),
            _const_spec((d, 2 * A_HALF)),
            _const_spec((1, 2 * A_HALF)),
            _const_spec((1, A_HALF)),
            _const_spec((1, A_HALF)),
            _const_spec((A_GROUPS, tm, tm)),
            _const_spec((tm, A_GROUPS)),
            _const_spec((A_HALF, d)),
            _const_spec((1, d)),
            _const_spec((1, d)),
            _const_spec((1, d)),
        ],
        out_specs=out_specs,
        out_shape=out_shape,
        scratch_shapes=[pltpu.VMEM((tm, A_HALF), F32), pltpu.VMEM((tm, A_HALF), F32),
                        pltpu.VMEM((tm, A_HALF), BF16)],
        compiler_params=_cparams(("arbitrary", "arbitrary")),
        name="gmlp_layer",
    )(x, mod, w_in.astype(BF16), b_in.reshape(1, -1), vn_g.reshape(1, -1), vn_b.reshape(1, -1),
      ws_big, bs_big, w_out.astype(BF16), b_out.reshape(1, -1), ln_g.reshape(1, -1), ln_b.reshape(1, -1))


ROW_CHUNKS = D_MODEL // 128


def _router_kernel(hp_ref, hs_ref, wr_ref, br_ref, tri_ref, e_ref, g_ref, r_ref, cnt_ref, h3_ref, base_s,
                   *, n_first):
    i = pl.program_id(0)

    @pl.when(i == 0)
    def _():
        base_s[...] = jnp.zeros_like(base_s)

    h = jnp.where(i < n_first, hp_ref[...], hs_ref[...])
    for c in range(ROW_CHUNKS):
        h3_ref[:, c, :] = h[:, c * 128:(c + 1) * 128]
    logits = lax.dot_general(wr_ref[...], h, (((1,), (1,)), ((), ())),
                             preferred_element_type=F32,
                             precision=lax.Precision.HIGHEST) + br_ref[...]
    rows = h.shape[0]
    iota = lax.broadcasted_iota(jnp.int32, (N_EXPERTS, rows), 0)
    l = logits
    vals, hots = [], []
    for k in range(TOP_K):
        m = jnp.max(l, axis=0, keepdims=True)
        idx = jnp.min(jnp.where(l == m, iota, N_EXPERTS), axis=0, keepdims=True)
        hot = iota == idx
        vals.append(m)
        hots.append(hot)
        e_ref[k:k + 1, :] = idx
        l = jnp.where(hot, -jnp.inf, l)
    exps = [jnp.exp(v - vals[0]) for v in vals]
    den = exps[0] + exps[1] + exps[2] + exps[3]
    for k in range(TOP_K):
        g_ref[k:k + 1, :] = exps[k] / den
    hot_all = jnp.where(hots[0] | hots[1] | hots[2] | hots[3], 1.0, 0.0)
    before = _bdot(hot_all.astype(BF16), tri_ref[...]) + base_s[:, 0:1]
    for k in range(TOP_K):
        r_ref[k:k + 1, :] = jnp.sum(jnp.where(hots[k], before, 0.0), axis=0, keepdims=True).astype(jnp.int32)
    base_s[...] = base_s[...] + jnp.sum(hot_all, axis=1, keepdims=True)
    cnt_ref[...] = base_s[...].astype(jnp.int32)


def _router(h_first, h_second, w_router, b_router):
    tr = ROUTER_ROWS
    n_first = h_first.shape[0] // tr
    n_second = h_second.shape[0] // tr
    n = h_first.shape[0] + h_second.shape[0]
    tri = (jnp.arange(tr)[:, None] < jnp.arange(tr)[None, :]).astype(BF16)
    sel = pl.BlockSpec((TOP_K, tr), lambda i: (0, i))
    top_e, gates, rank, cnt, h3 = pl.pallas_call(
        functools.partial(_router_kernel, n_first=n_first),
        grid=(n_first + n_second,),
        in_specs=[
            pl.BlockSpec((tr, D_MODEL), lambda i: (jnp.minimum(i, n_first - 1), 0)),
            pl.BlockSpec((tr, D_MODEL), lambda i: (jnp.maximum(i - n_first, 0), 0)),
            _const_spec((N_EXPERTS, D_MODEL)),
            _const_spec((N_EXPERTS, 1)),
            _const_spec((tr, tr)),
        ],
        out_specs=[sel, sel, sel, pl.BlockSpec((N_EXPERTS, 128), lambda i: (0, 0)),
                   pl.BlockSpec((tr, ROW_CHUNKS, 128), lambda i: (i, 0, 0))],
        out_shape=[jax.ShapeDtypeStruct((TOP_K, n), jnp.int32), jax.ShapeDtypeStruct((TOP_K, n), F32),
                   jax.ShapeDtypeStruct((TOP_K, n), jnp.int32),
                   jax.ShapeDtypeStruct((N_EXPERTS, 128), jnp.int32),
                   jax.ShapeDtypeStruct((n, ROW_CHUNKS, 128), F32)],
        scratch_shapes=[pltpu.VMEM((N_EXPERTS, 128), F32)],
        compiler_params=_cparams(("arbitrary",)),
        name="moe_router",
    )(h_first, h_second, w_router.T, b_router.reshape(N_EXPERTS, 1), tri)
    return top_e, gates, rank, cnt[:, 0], h3


LANES = 128
PREP_ROWS = 64


def _pair_perm(v):
    shp = v.shape
    v4 = v.reshape(shp[:-1] + (shp[-1] // LANES, 2, LANES // 2))
    return jnp.swapaxes(v4, -1, -2).reshape(shp)


def _expert_kernel(be_ref, nact_ref, src0_ref, src1_ref, dstp_ref, dstc_ref, h3_ref,
                   win_ref, bg_ref, bl_ref, wout_ref, bo_ref, ys_ref,
                   wg_s, wl_s, wo_t, wo_s, xbuf, ybuf, gsem, ssem):
    b = pl.program_id(0)
    nact = nact_ref[0]
    active = b < nact
    slot = b % 2
    other = 1 - slot
    tmb = xbuf.shape[1]
    fresh = jnp.logical_or(b == 0, be_ref[b] != be_ref[jnp.maximum(b - 1, 0)])

    def start_gather(src_ref, to_slot):
        for r in range(tmb):
            pltpu.make_async_copy(h3_ref.at[src_ref[0, 0, r]], xbuf.at[to_slot, r], gsem.at[to_slot]).start()

    def wait_gather(of_slot):
        pltpu.make_async_copy(h3_ref.at[pl.ds(0, tmb)], xbuf.at[of_slot], gsem.at[of_slot]).wait()

    def start_scatter(dst_ref, from_slot):
        for r in range(tmb):
            pltpu.make_async_copy(ybuf.at[from_slot, r], ys_ref.at[dst_ref[0, 0, r]], ssem.at[from_slot]).start()

    def wait_scatter(of_slot):
        pltpu.make_async_copy(ybuf.at[of_slot], ys_ref.at[pl.ds(0, tmb)], ssem.at[of_slot]).wait()

    @pl.when(jnp.logical_and(active, b == 0))
    def _():
        start_gather(src0_ref, 0)

    @pl.when(jnp.logical_and(active, fresh))
    def _():
        even = lax.broadcasted_iota(jnp.int32, (PREP_ROWS, LANES), 1) % 2 == 0

        def split_rows(r, carry):
            rows = pl.ds(pl.multiple_of(r * PREP_ROWS, PREP_ROWS), PREP_ROWS)
            for c in range(D_EXPERT // LANES):
                blk_a = win_ref[rows, 2 * c * LANES:(2 * c + 1) * LANES]
                blk_b = win_ref[rows, (2 * c + 1) * LANES:(2 * c + 2) * LANES]
                wg_s[rows, c * LANES:(c + 1) * LANES] = jnp.where(
                    even, blk_a, pltpu.roll(blk_b, 1, 1)).astype(BF16)
                wl_s[rows, c * LANES:(c + 1) * LANES] = jnp.where(
                    even, pltpu.roll(blk_a, LANES - 1, 1), blk_b).astype(BF16)
            return carry

        lax.fori_loop(0, D_MODEL // PREP_ROWS, split_rows, 0)
        half = LANES // 2
        for c in range(D_EXPERT // LANES):
            for p in range(2):
                for cb in range(D_MODEL // LANES):
                    wo_t[cb, pl.ds(c * LANES + p, half, stride=2), :] = (
                        wout_ref[pl.ds(c * LANES + half * p, half), cb * LANES:(cb + 1) * LANES])

        def cast_rows(r, carry):
            rows = pl.ds(pl.multiple_of(r * PREP_ROWS, PREP_ROWS), PREP_ROWS)
            for cb in range(D_MODEL // LANES):
                wo_s[rows, cb * LANES:(cb + 1) * LANES] = wo_t[cb, rows, :].astype(BF16)
            return carry

        lax.fori_loop(0, D_EXPERT // PREP_ROWS, cast_rows, 0)

    @pl.when(active)
    def _():
        wait_gather(slot)

    @pl.when(jnp.logical_and(active, b >= 2))
    def _():
        wait_scatter(slot)

    def compute():
        x = jnp.concatenate([xbuf[slot, :, c, :] for c in range(ROW_CHUNKS)], axis=1).astype(BF16)
        zg = _bdot(x, wg_s[...]) + bg_ref[...]
        zl = _bdot(x, wl_s[...]) + bl_ref[...]
        glu = jnp.minimum(zg, SWIGLU_LIMIT)
        lin = jnp.clip(zl, -SWIGLU_LIMIT, SWIGLU_LIMIT)
        a = glu * jax.nn.sigmoid(SWIGLU_ALPHA * glu) * (lin + 1.0)
        y = _bdot(a.astype(BF16), wo_s[...]) + bo_ref[...]
        for c in range(ROW_CHUNKS):
            ybuf[slot, :, c, :] = y[:, c * LANES:(c + 1) * LANES]

    @pl.when(jnp.logical_and(active, b == 0))
    def _():
        start_gather(src1_ref, other)
        compute()

    @pl.when(jnp.logical_and(active, b > 0))
    def _():
        start_gather(src1_ref, other)
        start_scatter(dstp_ref, other)
        compute()

    @pl.when(b == nact - 1)
    def _():
        start_scatter(dstc_ref, slot)
        wait_gather(other)
        wait_scatter(slot)

    @pl.when(jnp.logical_and(b == nact - 1, b > 0))
    def _():
        wait_scatter(other)

    @pl.when(b == nact - 1)
    def _():
        spare0 = ys_ref.shape[0] - 2 * tmb
        ybuf[...] = jnp.zeros_like(ybuf)
        fills = [pltpu.make_async_copy(ybuf.at[s], ys_ref.at[pl.ds(spare0 + s * tmb, tmb)], ssem.at[s])
                 for s in range(2)]
        for cp in fills:
            cp.start()
        for cp in fills:
            cp.wait()


def _experts(h3, src_rows, dst_rows, block_e, nact, n_slots, layer, w_in, b_glu, b_lin, w_out, b_out):
    tmb = EXPERT_ROWS
    n_blocks = src_rows.shape[0]
    exp3 = lambda b, be, na: (be[b], 0, 0)
    exp4 = lambda b, be, na: (layer, be[b], 0, 0)
    idx_spec = lambda f: pl.BlockSpec((1, 1, tmb), lambda b, be, na: (f(b), 0, 0), memory_space=pltpu.SMEM)
    grid_spec = pltpu.PrefetchScalarGridSpec(
        num_scalar_prefetch=2,
        grid=(n_blocks,),
        in_specs=[
            idx_spec(lambda b: b),
            idx_spec(lambda b: jnp.minimum(b + 1, n_blocks - 1)),
            idx_spec(lambda b: jnp.maximum(b - 1, 0)),
            idx_spec(lambda b: b),
            pl.BlockSpec(memory_space=pl.ANY),
            pl.BlockSpec((None, None, D_MODEL, 2 * D_EXPERT), exp4),
            pl.BlockSpec((None, 1, D_EXPERT), exp3),
            pl.BlockSpec((None, 1, D_EXPERT), exp3),
            pl.BlockSpec((None, None, D_EXPERT, D_MODEL), exp4),
            pl.BlockSpec((None, 1, D_MODEL), exp3),
        ],
        out_specs=pl.BlockSpec(memory_space=pl.ANY),
        scratch_shapes=[pltpu.VMEM((D_MODEL, D_EXPERT), BF16), pltpu.VMEM((D_MODEL, D_EXPERT), BF16),
                        pltpu.VMEM((D_MODEL // LANES, D_EXPERT, LANES), F32),
                        pltpu.VMEM((D_EXPERT, D_MODEL), BF16),
                        pltpu.VMEM((2, tmb, ROW_CHUNKS, LANES), F32),
                        pltpu.VMEM((2, tmb, ROW_CHUNKS, LANES), F32),
                        pltpu.SemaphoreType.DMA((2,)), pltpu.SemaphoreType.DMA((2,))],
    )
    return pl.pallas_call(
        _expert_kernel,
        grid_spec=grid_spec,
        out_shape=jax.ShapeDtypeStruct((n_slots, ROW_CHUNKS, LANES), F32),
        compiler_params=_cparams(("arbitrary",)),
        name="moe_experts",
    )(block_e, nact, src_rows, src_rows, dst_rows, dst_rows, h3, w_in, b_glu, b_lin, w_out, b_out)


def _moe(h_first, h_second, layer, w_router, b_router, w_in, b_in, w_out, b_out):
    tmb = EXPERT_ROWS
    top_e, gates, rank, counts, h3 = _router(h_first, h_second, w_router, b_router)
    n = h3.shape[0]
    m = n * TOP_K
    padded = (counts + tmb - 1) // tmb * tmb
    pend = jnp.cumsum(padded)
    pstart = pend - padded
    hot = top_e[..., None] == jnp.arange(N_EXPERTS)[None, None, :]
    dest = jnp.sum(jnp.where(hot, pstart[None, None, :], 0), axis=-1) + rank
    n_blocks = m // tmb + N_EXPERTS
    blk_row0 = jnp.arange(n_blocks, dtype=jnp.int32) * tmb
    block_e = jnp.minimum(jnp.sum(pend[None, :] <= blk_row0[:, None], axis=1), N_EXPERTS - 1).astype(jnp.int32)
    nact = (pend[-1] // tmb).astype(jnp.int32).reshape(1)
    tok = jnp.broadcast_to(jnp.arange(n, dtype=jnp.int32)[None, :], (TOP_K, n))
    slot = tok * TOP_K + jnp.arange(TOP_K, dtype=jnp.int32)[:, None]
    j = jnp.arange(N_EXPERTS * tmb, dtype=jnp.int32)
    off = j % tmb
    pad = padded - counts
    rep = lambda v: jnp.repeat(v, tmb)
    pad_key = jnp.where(off < rep(pad), rep(pstart + counts) + off, pend[-1] + j - rep(jnp.cumsum(pad)))
    pad_dst = m + ((pad_key // tmb) % 2) * tmb + pad_key % tmb
    keys = jnp.concatenate([dest.reshape(m), pad_key]).astype(jnp.int32)
    src = jnp.concatenate([tok.reshape(m), jnp.zeros_like(j)])
    dst = jnp.concatenate([slot.reshape(m), pad_dst]).astype(jnp.int32)
    _, src_rows, dst_rows = lax.sort((keys, src, dst), num_keys=1)
    b_glu = _pair_perm(b_in[:, 0::2]).reshape(N_EXPERTS, 1, D_EXPERT)
    b_lin = _pair_perm(b_in[:, 1::2]).reshape(N_EXPERTS, 1, D_EXPERT)
    ys = _experts(h3, src_rows.reshape(n_blocks, 1, tmb), dst_rows.reshape(n_blocks, 1, tmb), block_e, nact,
                  m + 2 * tmb, layer, w_in, b_glu, b_lin, w_out, b_out.reshape(N_EXPERTS, 1, D_MODEL))
    return ys.reshape((m + 2 * tmb) // TOP_K, TOP_K * ROW_CHUNKS, LANES), gates.T


def _post_kernel(x1_ref, ys_ref, gate_ref, mod_ref, lng_ref, lnb_ref, *rest, with_kv):
    g2 = mod_ref[5:6, :]
    f = None
    for k in range(TOP_K):
        yk = jnp.concatenate([ys_ref[:, k * ROW_CHUNKS + c, :] for c in range(ROW_CHUNKS)], axis=1)
        term = yk * gate_ref[:, k:k + 1]
        f = term if f is None else f + term
    x2 = _layer_norm(DN_ALPHA * x1_ref[...] + g2 * f, lng_ref[...], lnb_ref[...])
    if not with_kv:
        (x2_ref,) = rest
        x2_ref[...] = x2
        return
    wa_ref, kng_ref, cos_ref, sin_ref, x2_ref, lat_ref, kr_ref, kcat_ref = rest
    x2_ref[...] = x2
    kv = _bdot(x2.astype(BF16), wa_ref[...])
    c = kv[:, :KV_RANK]
    lat = c * lax.rsqrt(jnp.mean(c * c, axis=-1, keepdims=True) + RMS_EPS) * kng_ref[...]
    k = kv[:, KV_RANK:KV_RANK + QK_ROPE]
    k_swapped = kv[:, KV_RANK + QK_ROPE:]
    kr = k * cos_ref[...] + k_swapped * sin_ref[...]
    lat_ref[...] = lat
    kr_ref[...] = kr
    kcat_ref[:, :KV_RANK] = lat.astype(BF16)
    kcat_ref[:, KV_RANK:] = kr.astype(BF16)


def _swap_halves_cols(w, width):
    shp = w.shape
    w4 = w.reshape(shp[:-1] + (shp[-1] // width, 2, width // 2))
    return w4[..., ::-1, :].reshape(shp)


def _rope_tables(pos, reps):
    inv = 1.0 / (ROPE_BASE ** (jnp.arange(ROPE_HALF, dtype=F32) * (2.0 / QK_ROPE)))
    ang = pos.astype(F32)[:, None] * inv[None, :]
    cos, sin = jnp.cos(ang), jnp.sin(ang)
    cos_t = jnp.tile(jnp.concatenate([cos, cos], -1), (1, reps))
    sin_t = jnp.tile(jnp.concatenate([-sin, sin], -1), (1, reps))
    return cos_t, sin_t


def _post_layer(x1, ys, gates, row0, mod, ln_g, ln_b, *, tm, kv=None):
    bsz, s, d = x1.shape
    blk0 = row0 // tm
    per_b = s // tm
    tile = lambda w: pl.BlockSpec((None, tm, w), lambda b, i: (b, i, 0))
    in_specs = [
        tile(d),
        pl.BlockSpec((tm, TOP_K * ROW_CHUNKS, LANES), lambda b, i: (blk0 + b * per_b + i, 0, 0)),
        pl.BlockSpec((tm, TOP_K), lambda b, i: (blk0 + b * per_b + i, 0)),
        pl.BlockSpec((None, 6, d), lambda b, i: (b, 0, 0)),
        _const_spec((1, d)),
        _const_spec((1, d)),
    ]
    args = [x1, ys, gates, mod, ln_g.reshape(1, -1), ln_b.reshape(1, -1)]
    out_shape = [jax.ShapeDtypeStruct((bsz, s, d), F32)]
    out_specs = [tile(d)]
    if kv is not None:
        w_a, kn_g, cos_t, sin_t = kv
        w_ext = jnp.concatenate([w_a, _swap_halves_cols(w_a[:, KV_RANK:], QK_ROPE)], axis=1).astype(BF16)
        in_specs += [_const_spec(w_ext.shape), _const_spec((1, KV_RANK)),
                     pl.BlockSpec((tm, QK_ROPE), lambda b, i: (i, 0)),
                     pl.BlockSpec((tm, QK_ROPE), lambda b, i: (i, 0))]
        args += [w_ext, kn_g.reshape(1, -1), cos_t, sin_t]
        out_shape += [jax.ShapeDtypeStruct((bsz, s, KV_RANK), F32), jax.ShapeDtypeStruct((bsz, s, QK_ROPE), F32),
                      jax.ShapeDtypeStruct((bsz, s, QK_LAT), BF16)]
        out_specs += [tile(KV_RANK), tile(QK_ROPE), tile(QK_LAT)]
    return pl.pallas_call(
        functools.partial(_post_kernel, with_kv=kv is not None),
        grid=(bsz, per_b),
        in_specs=in_specs,
        out_specs=out_specs,
        out_shape=out_shape,
        compiler_params=_cparams(("arbitrary", "arbitrary")),
        name="post_moe",
    )(*args)


Q_GROUP = 512
KEY_TILE = 512


def _qproj_kernel(x_ref, mod_ref, wdq_ref, qng_ref, wqn_ref, wqr_ref, wqrs_ref, wuk_ref, cos_ref, sin_ref, q_ref,
                  *, tq):
    x = x_ref[...]
    sh1, sc1 = mod_ref[0:1, :], mod_ref[1:2, :]
    h = (x * (1.0 + sc1) + sh1).astype(BF16)
    tm = x.shape[0]
    cq = lax.dot_general(wdq_ref[...], h, (((1,), (1,)), ((), ())), preferred_element_type=F32)
    cq = (cq * lax.rsqrt(jnp.mean(cq * cq, axis=0, keepdims=True) + RMS_EPS) * qng_ref[...]).astype(BF16)
    qn = _bdot(wqn_ref[...], cq)
    qr = _bdot(wqr_ref[...], cq)
    qrs = _bdot(wqrs_ref[...], cq)
    rope = (qr * cos_ref[...] + qrs * sin_ref[...]).astype(BF16)
    per_group = Q_GROUP // tq
    for hd in range(MLA_HEADS):
        ql = _bdot(wuk_ref[hd], qn[hd * QK_NOPE:(hd + 1) * QK_NOPE, :].astype(BF16)).astype(BF16)
        g, off = hd // per_group, (hd % per_group) * tq
        for t in range(tm // tq):
            q_ref[t, g, 0:KV_RANK, off:off + tq] = ql[:, t * tq:(t + 1) * tq]
            q_ref[t, g, KV_RANK:QK_LAT, off:off + tq] = rope[hd * QK_ROPE:(hd + 1) * QK_ROPE, t * tq:(t + 1) * tq]


def _qproj_layer(x, mod, w_dq, qn_g, w_qb, w_uk, cos_t, sin_t, *, tm, tq):
    bsz, s, d = x.shape
    w_qn = w_qb[:, :, :QK_NOPE].reshape(Q_RANK, MLA_HEADS * QK_NOPE)
    w_qr = w_qb[:, :, QK_NOPE:].reshape(Q_RANK, MLA_HEADS * QK_ROPE)
    w_qrs = _swap_halves_cols(w_qr, QK_ROPE)
    wuk_h = jnp.transpose(w_uk, (1, 0, 2)).astype(BF16)
    hr = MLA_HEADS * QK_ROPE
    n_groups = MLA_HEADS * tq // Q_GROUP
    nt = tm // tq
    return pl.pallas_call(
        functools.partial(_qproj_kernel, tq=tq),
        grid=(bsz, s // tm),
        in_specs=[
            pl.BlockSpec((None, tm, d), lambda b, i: (b, i, 0)),
            pl.BlockSpec((None, 6, d), lambda b, i: (b, 0, 0)),
            _const_spec((Q_RANK, d)),
            _const_spec((Q_RANK, 1)),
            _const_spec((MLA_HEADS * QK_NOPE, Q_RANK)),
            _const_spec((hr, Q_RANK)),
            _const_spec((hr, Q_RANK)),
            _const_spec(wuk_h.shape),
            pl.BlockSpec((hr, tm), lambda b, i: (0, i)),
            pl.BlockSpec((hr, tm), lambda b, i: (0, i)),
        ],
        out_specs=pl.BlockSpec((None, nt, n_groups, QK_LAT, Q_GROUP), lambda b, i: (b, i, 0, 0, 0)),
        out_shape=jax.ShapeDtypeStruct((bsz, s // tq, n_groups, QK_LAT, Q_GROUP), BF16),
        compiler_params=_cparams(("arbitrary", "arbitrary")),
        name="mla_qproj",
    )(x, mod, w_dq.T.astype(BF16), qn_g.reshape(-1, 1), w_qn.T.astype(BF16), w_qr.T.astype(BF16),
      w_qrs.T.astype(BF16), wuk_h, cos_t.T, sin_t.T)


EXP2_SCALE = ATTN_SCALE * 1.4426950408889634


def _attn_kernel(q_ref, *rest, tq, causal, n_main, tail, split_keys):
    if split_keys:
        klat_ref, kkr_ref = rest[:2]
        rest = rest[2:]
    else:
        k_ref = rest[0]
        rest = rest[1:]
    if tail:
        kt_ref = rest[0]
        rest = rest[1:]
    x_ref, mod_ref, wuv_ref, wo_ref, lng_ref, lnb_ref, x1_ref, h2_ref, m_s, l_s, acc_s, ta_s, tb_s = rest
    i = pl.program_id(1)
    n_groups = q_ref.shape[0]
    per_group = Q_GROUP // tq
    m_s[...] = jnp.full_like(m_s, NEG_BIG)
    l_s[...] = jnp.zeros_like(l_s)
    acc_s[...] = jnp.zeros_like(acc_s)

    def tile_rows(j):
        return pl.ds(pl.multiple_of(j * KEY_TILE, KEY_TILE), KEY_TILE)

    def key_rows(j):
        if split_keys:
            return jnp.concatenate([klat_ref[tile_rows(j), :].astype(BF16), kkr_ref[tile_rows(j), :].astype(BF16)],
                                   axis=1)
        return k_ref[tile_rows(j), :]

    def value_rows(j):
        if split_keys:
            return klat_ref[tile_rows(j), :].astype(BF16)
        return k_ref[tile_rows(j), :KV_RANK]

    def softmax_step(g, t, v, key0, masked):
        t = t * EXP2_SCALE
        if masked:
            n = t.shape[0]
            k_chunk = (key0 + lax.broadcasted_iota(jnp.int32, (n, Q_GROUP), 0)) // CHUNK
            q_chunk = (i * tq + lax.broadcasted_iota(jnp.int32, (n, Q_GROUP), 1) % tq) // CHUNK
            t = jnp.where(k_chunk <= q_chunk, t, -jnp.inf)
        m_old = m_s[g]
        m_new = jnp.maximum(m_old, jnp.max(t, axis=0, keepdims=True))
        alpha = jnp.exp2(m_old - m_new)
        p = jnp.exp2(t - m_new)
        l_s[g] = alpha * l_s[g] + jnp.sum(p, axis=0, keepdims=True)
        acc_s[g] = alpha * acc_s[g] + lax.dot_general(
            v, p.astype(BF16), (((0,), (0,)), ((), ())), preferred_element_type=F32)
        m_s[g] = m_new

    def stage(cur, nxt, j, masked, make_next):
        v = value_rows(j)
        k_next = key_rows(j + 1) if make_next else None
        for g in range(n_groups):
            if make_next:
                nxt[g] = _bdot(k_next, q_ref[g])
            softmax_step(g, cur[g], v, j * KEY_TILE, masked)

    n_open = (i * tq) // KEY_TILE if causal else n_main - 1
    k0 = key_rows(0)
    for g in range(n_groups):
        ta_s[g] = _bdot(k0, q_ref[g])

    def pair_body(p, c):
        stage(ta_s, tb_s, 2 * p, False, True)
        stage(tb_s, ta_s, 2 * p + 1, False, True)
        return c

    lax.fori_loop(0, n_open // 2, pair_body, 0)
    if causal:
        @pl.when(n_open % 2 == 0)
        def _():
            stage(ta_s, tb_s, n_open, True, False)

        @pl.when(n_open % 2 == 1)
        def _():
            stage(ta_s, tb_s, n_open - 1, False, True)
            stage(tb_s, ta_s, n_open, True, False)
    elif n_open % 2 == 0:
        stage(ta_s, tb_s, n_open, False, False)
    else:
        stage(ta_s, tb_s, n_open - 1, False, True)
        stage(tb_s, ta_s, n_open, False, False)
    if tail:
        kt = kt_ref[...]
        for g in range(n_groups):
            softmax_step(g, _bdot(kt, q_ref[g]), kt[:, :KV_RANK], n_main * KEY_TILE, False)

    heads = []
    for g in range(n_groups):
        o = (acc_s[g] / l_s[g]).T
        for hh in range(per_group):
            hd = g * per_group + hh
            heads.append(_bdot(o[hh * tq:(hh + 1) * tq].astype(BF16), wuv_ref[hd]))
    oc = jnp.concatenate(heads, axis=-1).astype(BF16)
    y = _bdot(oc, wo_ref[...])
    x = x_ref[...]
    g1, sh2, sc2 = mod_ref[2:3, :], mod_ref[3:4, :], mod_ref[4:5, :]
    x1 = _layer_norm(DN_ALPHA * x + g1 * y, lng_ref[...], lnb_ref[...])
    x1_ref[...] = x1
    h2_ref[...] = x1 * (1.0 + sc2) + sh2


def _attn_layer(q, keys, x, mod, w_uv, w_o, ln_g, ln_b, *, tq, causal, k_tail=None):
    bsz, s, d = x.shape
    split_keys = isinstance(keys, tuple)
    keys = keys if split_keys else (keys,)
    t_main = keys[0].shape[1]
    n_main = t_main // KEY_TILE
    n_groups = q.shape[2]
    tail = k_tail is not None
    wuv_t = jnp.transpose(w_uv, (1, 0, 2)).astype(BF16)
    tile = pl.BlockSpec((None, tq, d), lambda b, i: (b, i, 0))
    in_specs = [pl.BlockSpec((None, None, n_groups, QK_LAT, Q_GROUP), lambda b, i: (b, i, 0, 0, 0))]
    in_specs += [pl.BlockSpec((None, t_main, k.shape[2]), lambda b, i: (b, 0, 0)) for k in keys]
    args = [q, *keys]
    if tail:
        in_specs.append(pl.BlockSpec((None, k_tail.shape[1], QK_LAT), lambda b, i: (b, 0, 0)))
        args.append(k_tail)
    in_specs += [
        tile,
        pl.BlockSpec((None, 6, d), lambda b, i: (b, 0, 0)),
        _const_spec(wuv_t.shape),
        _const_spec((MLA_HEADS * V_DIM, d)),
        _const_spec((1, d)),
        _const_spec((1, d)),
    ]
    args += [x, mod, wuv_t, w_o.astype(BF16), ln_g.reshape(1, -1), ln_b.reshape(1, -1)]
    return pl.pallas_call(
        functools.partial(_attn_kernel, tq=tq, causal=causal, n_main=n_main, tail=tail, split_keys=split_keys),
        grid=(bsz, s // tq),
        in_specs=in_specs,
        out_specs=[tile, tile],
        out_shape=[jax.ShapeDtypeStruct((bsz, s, d), F32), jax.ShapeDtypeStruct((bsz, s, d), F32)],
        scratch_shapes=[pltpu.VMEM((n_groups, 1, Q_GROUP), F32), pltpu.VMEM((n_groups, 1, Q_GROUP), F32),
                        pltpu.VMEM((n_groups, KV_RANK, Q_GROUP), F32),
                        pltpu.VMEM((n_groups, KEY_TILE, Q_GROUP), F32),
                        pltpu.VMEM((n_groups, KEY_TILE, Q_GROUP), F32)],
        compiler_params=_cparams(("arbitrary", "arbitrary")),
        name="mla_attention",
    )(*args)


def kernel(x_prompt, x_sample, cache_kv_latent, cache_k_rope, c_prompt, c_sample, ada_w, ada_b, ln_g, ln_b,
           a_w_in, a_b_in, a_vn_g, a_vn_b, a_w_s, a_b_s, a_w_out, a_b_out, kv_w_a, kv_norm_g, kv_w_uk, kv_w_uv,
           q_w_a, q_norm_g, q_w_b, q_w_o, moe_w_router, moe_b_router, moe_w_in, moe_b_in, moe_w_out, moe_b_out):
    bp, sp, d = x_prompt.shape
    bs, ss, _ = x_sample.shape
    past = cache_kv_latent.shape[1]
    n_p = bp * sp
    n_s = bs * ss

    mod = _ada_mod(jnp.concatenate([c_prompt, c_sample], axis=0), ada_w, ada_b)
    mod = mod.reshape(DEPTH, bp + bs, 6, d)
    mod_p, mod_s = mod[:, :bp], mod[:, bp:]

    pos_p = jnp.arange(sp, dtype=jnp.int32)
    pos_s = past + jnp.arange(ss, dtype=jnp.int32)

    def moe(l, h2_p, h2_s):
        return _moe(h2_p.reshape(n_p, d), h2_s.reshape(n_s, d), l, moe_w_router[l], moe_b_router[l],
                    moe_w_in, moe_b_in[l], moe_w_out, moe_b_out[l])

    gm = functools.partial(_gmlp_layer, w_in=a_w_in[0], b_in=a_b_in[0], vn_g=a_vn_g[0], vn_b=a_vn_b[0],
                           w_s=a_w_s[0], b_s=a_b_s[0], w_out=a_w_out[0], b_out=a_b_out[0],
                           ln_g=ln_g[0, 0], ln_b=ln_b[0, 0])
    x1_p, h2_p = gm(x_prompt, mod_p[0], tm=256, write_v=False)
    x1_s, h2_s, v_s = gm(x_sample, mod_s[0], tm=ss, write_v=True)
    ys, gates = moe(0, h2_p, h2_s)
    x2_p, lat_p, kr_p, kcat_p = _post_layer(x1_p, ys, gates, 0, mod_p[0], ln_g[0, 1], ln_b[0, 1], tm=512,
                                            kv=(kv_w_a, kv_norm_g) + _rope_tables(pos_p, 1))
    x2_s, lat_s, kr_s, kcat_s = _post_layer(x1_s, ys, gates, n_p, mod_s[0], ln_g[0, 1], ln_b[0, 1], tm=ss,
                                            kv=(kv_w_a, kv_norm_g) + _rope_tables(pos_s, 1))

    qp = functools.partial(_qproj_layer, w_dq=q_w_a[0], qn_g=q_norm_g[0], w_qb=q_w_b[0], w_uk=kv_w_uk)
    at = functools.partial(_attn_layer, w_uv=kv_w_uv, w_o=q_w_o[0], ln_g=ln_g[1, 0], ln_b=ln_b[1, 0])
    cos_p, sin_p = _rope_tables(pos_p, MLA_HEADS)
    cos_s, sin_s = _rope_tables(pos_s, MLA_HEADS)
    q_p = qp(x2_p, mod_p[1], cos_t=cos_p, sin_t=sin_p, tm=256, tq=128)
    q_s = qp(x2_s, mod_s[1], cos_t=cos_s, sin_t=sin_s, tm=ss, tq=ss)
    x3_p, h4_p = at(q_p, kcat_p, x2_p, mod_p[1], tq=128, causal=True)
    x3_s, h4_s = at(q_s, (cache_kv_latent, cache_k_rope), x2_s, mod_s[1], tq=ss, causal=False, k_tail=kcat_s)
    ys, gates = moe(1, h4_p, h4_s)
    (y_p,) = _post_layer(x3_p, ys, gates, 0, mod_p[1], ln_g[1, 1], ln_b[1, 1], tm=512)
    (y_s,) = _post_layer(x3_s, ys, gates, n_p, mod_s[1], ln_g[1, 1], ln_b[1, 1], tm=ss)

    return (y_p, y_s, lat_p, kr_p, lat_s, kr_s, v_s[None])
```

```python
import functools

import jax
import jax.numpy as jnp
from jax import lax
from jax.experimental import pallas as pl
from jax.experimental.pallas import tpu as pltpu

D_MODEL = 1024
DEPTH = 2
CHUNK = 64
A_CHUNK = 128
A_HALF = 2 * D_MODEL
A_GROUPS = 8
A_GROUP_W = A_HALF // A_GROUPS
MLA_HEADS = 8
QK_NOPE = 128
QK_ROPE = 64
ROPE_HALF = QK_ROPE // 2
V_DIM = 128
KV_RANK = 256
Q_RANK = 512
QK_LAT = KV_RANK + QK_ROPE
ROPE_BASE = 10000.0
ATTN_SCALE = (QK_NOPE + QK_ROPE) ** -0.5
N_EXPERTS = 32
TOP_K = 4
D_EXPERT = D_MODEL
SWIGLU_LIMIT = 7.0
SWIGLU_ALPHA = 1.702
DN_ALPHA = (2 * DEPTH) ** 0.25
LN_EPS = 1e-5
RMS_EPS = 1e-6

BF16 = jnp.bfloat16
F32 = jnp.float32

VMEM_LIMIT = 56 * 1024 * 1024
EXPERT_ROWS = 256
ROUTER_ROWS = 512
NEG_BIG = -1e30


def _cparams(sem):
    return pltpu.CompilerParams(dimension_semantics=sem, vmem_limit_bytes=VMEM_LIMIT)


def _const_spec(shape):
    nd = len(shape)
    return pl.BlockSpec(shape, lambda *_: (0,) * nd, pipeline_mode=pl.Buffered(1))


def _layer_norm(r, g, b):
    rc = r - jnp.mean(r, axis=-1, keepdims=True)
    var = jnp.mean(rc * rc, axis=-1, keepdims=True)
    return rc * lax.rsqrt(var + LN_EPS) * g + b


def _gelu_tanh(x):
    c = 0.7978845608028654
    return 0.5 * x * (1.0 + jnp.tanh(c * (x + 0.044715 * (x * x * x))))


def _bdot(a, b):
    return jnp.dot(a, b, preferred_element_type=F32)


def _ada_kernel(c_ref, w_ref, b_ref, o_ref):
    c = c_ref[...]
    s = c * jax.nn.sigmoid(c)
    o_ref[...] = jnp.dot(s, w_ref[...], preferred_element_type=F32,
                         precision=lax.Precision.HIGHEST) + b_ref[...]


def _ada_mod(c_all, ada_w, ada_b):
    nb = c_all.shape[0]
    six_d = ada_w.shape[-1]
    tn = D_MODEL
    return pl.pallas_call(
        _ada_kernel,
        grid=(DEPTH, six_d // tn),
        in_specs=[
            pl.BlockSpec((nb, D_MODEL), lambda l, j: (0, 0)),
            pl.BlockSpec((None, D_MODEL, tn), lambda l, j: (l, 0, j)),
            pl.BlockSpec((None, 1, tn), lambda l, j: (l, 0, j)),
        ],
        out_specs=pl.BlockSpec((None, nb, tn), lambda l, j: (l, 0, j)),
        out_shape=jax.ShapeDtypeStruct((DEPTH, nb, six_d), F32),
        compiler_params=_cparams(("arbitrary", "arbitrary")),
        name="ada_mod",
    )(c_all, ada_w, ada_b.reshape(DEPTH, 1, six_d))


def _gmlp_kernel(x_ref, mod_ref, w_in_ref, b_in_ref, vng_ref, vnb_ref, ws_ref, bs_ref,
                 w_out_ref, b_out_ref, lng_ref, lnb_ref, *rest, write_v):
    if write_v:
        x1_ref, h2_ref, v_ref, u_s, v_s, p_s = rest
    else:
        x1_ref, h2_ref, u_s, v_s, p_s = rest
    x = x_ref[...]
    sh1, sc1, g1 = mod_ref[0:1, :], mod_ref[1:2, :], mod_ref[2:3, :]
    sh2, sc2 = mod_ref[3:4, :], mod_ref[4:5, :]
    h = (x * (1.0 + sc1) + sh1).astype(BF16)
    tm = x.shape[0]
    ch = 512
    n_ch = A_HALF // ch
    for j in range(n_ch):
        sl = slice(j * ch, (j + 1) * ch)
        u_s[:, sl] = _gelu_tanh(_bdot(h, w_in_ref[:, sl]) + b_in_ref[:, sl])
    tot = jnp.zeros((tm, 1), F32)
    for j in range(n_ch):
        sl = slice(j * ch, (j + 1) * ch)
        slw = slice(A_HALF + j * ch, A_HALF + (j + 1) * ch)
        g = _gelu_tanh(_bdot(h, w_in_ref[:, slw]) + b_in_ref[:, slw])
        v_s[:, sl] = g
        tot = tot + jnp.sum(g, axis=-1, keepdims=True)
    mean = tot * (1.0 / A_HALF)
    sq = jnp.zeros((tm, 1), F32)
    for j in range(n_ch):
        sl = slice(j * ch, (j + 1) * ch)
        c = v_s[:, sl] - mean
        sq = sq + jnp.sum(c * c, axis=-1, keepdims=True)
    rstd = lax.rsqrt(sq * (1.0 / A_HALF) + LN_EPS)
    for g in range(A_GROUPS):
        sl = slice(g * A_GROUP_W, (g + 1) * A_GROUP_W)
        vn = (v_s[:, sl] - mean) * rstd * vng_ref[:, sl] + vnb_ref[:, sl]
        if write_v:
            v_ref[:, sl] = vn
        sg = _bdot(ws_ref[g], vn.astype(BF16)) + bs_ref[:, g:g + 1]
        p_s[:, sl] = (u_s[:, sl] * sg).astype(BF16)
    y = _bdot(p_s[...], w_out_ref[...]) + b_out_ref[...]
    x1 = _layer_norm(DN_ALPHA * x + g1 * y, lng_ref[...], lnb_ref[...])
    x1_ref[...] = x1
    h2_ref[...] = x1 * (1.0 + sc2) + sh2


def _gmlp_layer(x, mod, w_in, b_in, vn_g, vn_b, w_s, b_s, w_out, b_out, ln_g, ln_b, *, tm, write_v):
    bsz, s, d = x.shape
    seg = min(s, A_CHUNK)
    idx = jnp.arange(seg)
    mask = (idx[:, None] // CHUNK) >= (idx[None, :] // CHUNK)
    wm = jnp.where(mask[None], w_s[:, :seg, :seg], 0.0)
    reps = tm // seg
    eye = jnp.eye(reps, dtype=F32)
    ws_big = jnp.einsum("ab,gij->gaibj", eye, wm).reshape(A_GROUPS, tm, tm).astype(BF16)
    bs_big = jnp.tile(b_s[:, :seg].T, (reps, 1))
    out_shape = [jax.ShapeDtypeStruct((bsz, s, d), F32), jax.ShapeDtypeStruct((bsz, s, d), F32)]
    tile = lambda w: pl.BlockSpec((None, tm, w), lambda b, i: (b, i, 0))
    out_specs = [tile(d), tile(d)]
    if write_v:
        out_shape.append(jax.ShapeDtypeStruct((bsz, s, A_HALF), F32))
        out_specs.append(tile(A_HALF))
    return pl.pallas_call(
        functools.partial(_gmlp_kernel, write_v=write_v),
        grid=(bsz, s // tm),
        in_specs=[
            tile(d),
            pl.BlockSpec((None, 6, d), lambda b, i: (b, 0, 0)),
            _const_spec((d, 2 * A_HALF)),
            _const_spec((1, 2 * A_HALF)),
            _const_spec((1, A_HALF)),
            _const_spec((1, A_HALF)),
            _const_spec((A_GROUPS, tm, tm)),
            _const_spec((tm, A_GROUPS)),
            _const_spec((A_HALF, d)),
            _const_spec((1, d)),
            _const_spec((1, d)),
            _const_spec((1, d)),
        ],
        out_specs=out_specs,
        out_shape=out_shape,
        scratch_shapes=[pltpu.VMEM((tm, A_HALF), F32), pltpu.VMEM((tm, A_HALF), F32),
                        pltpu.VMEM((tm, A_HALF), BF16)],
        compiler_params=_cparams(("arbitrary", "arbitrary")),
        name="gmlp_layer",
    )(x, mod, w_in.astype(BF16), b_in.reshape(1, -1), vn_g.reshape(1, -1), vn_b.reshape(1, -1),
      ws_big, bs_big, w_out.astype(BF16), b_out.reshape(1, -1), ln_g.reshape(1, -1), ln_b.reshape(1, -1))


ROW_CHUNKS = D_MODEL // 128


def _router_kernel(hp_ref, hs_ref, wr_ref, br_ref, tri_ref, e_ref, g_ref, r_ref, cnt_ref, h3_ref, base_s,
                   *, n_first):
    i = pl.program_id(0)

    @pl.when(i == 0)
    def _():
        base_s[...] = jnp.zeros_like(base_s)

    h = jnp.where(i < n_first, hp_ref[...], hs_ref[...])
    rows = h.shape[0]
    for c in range(ROW_CHUNKS):
        h3_ref[pl.ds(c, rows, stride=ROW_CHUNKS), :] = h[:, c * 128:(c + 1) * 128]
    logits = lax.dot_general(wr_ref[...], h, (((1,), (1,)), ((), ())),
                             preferred_element_type=F32,
                             precision=lax.Precision.HIGHEST) + br_ref[...]
    rows = h.shape[0]
    iota = lax.broadcasted_iota(jnp.int32, (N_EXPERTS, rows), 0)
    l = logits
    vals, hots = [], []
    for k in range(TOP_K):
        m = jnp.max(l, axis=0, keepdims=True)
        idx = jnp.min(jnp.where(l == m, iota, N_EXPERTS), axis=0, keepdims=True)
        hot = iota == idx
        vals.append(m)
        hots.append(hot)
        e_ref[k:k + 1, :] = idx
        l = jnp.where(hot, -jnp.inf, l)
    exps = [jnp.exp(v - vals[0]) for v in vals]
    den = exps[0] + exps[1] + exps[2] + exps[3]
    for k in range(TOP_K):
        g_ref[k:k + 1, :] = exps[k] / den
    hot_all = jnp.where(hots[0] | hots[1] | hots[2] | hots[3], 1.0, 0.0)
    before = _bdot(hot_all.astype(BF16), tri_ref[...]) + base_s[:, 0:1]
    for k in range(TOP_K):
        r_ref[k:k + 1, :] = jnp.sum(jnp.where(hots[k], before, 0.0), axis=0, keepdims=True).astype(jnp.int32)
    base_s[...] = base_s[...] + jnp.sum(hot_all, axis=1, keepdims=True)
    cnt_ref[...] = base_s[...].astype(jnp.int32)


def _router(h_first, h_second, w_router, b_router):
    tr = ROUTER_ROWS
    n_first = h_first.shape[0] // tr
    n_second = h_second.shape[0] // tr
    n = h_first.shape[0] + h_second.shape[0]
    tri = (jnp.arange(tr)[:, None] < jnp.arange(tr)[None, :]).astype(BF16)
    sel = pl.BlockSpec((TOP_K, tr), lambda i: (0, i))
    top_e, gates, rank, cnt, h3 = pl.pallas_call(
        functools.partial(_router_kernel, n_first=n_first),
        grid=(n_first + n_second,),
        in_specs=[
            pl.BlockSpec((tr, D_MODEL), lambda i: (jnp.minimum(i, n_first - 1), 0)),
            pl.BlockSpec((tr, D_MODEL), lambda i: (jnp.maximum(i - n_first, 0), 0)),
            _const_spec((N_EXPERTS, D_MODEL)),
            _const_spec((N_EXPERTS, 1)),
            _const_spec((tr, tr)),
        ],
        out_specs=[sel, sel, sel, pl.BlockSpec((N_EXPERTS, 128), lambda i: (0, 0)),
                   pl.BlockSpec((tr * ROW_CHUNKS, 128), lambda i: (i, 0))],
        out_shape=[jax.ShapeDtypeStruct((TOP_K, n), jnp.int32), jax.ShapeDtypeStruct((TOP_K, n), F32),
                   jax.ShapeDtypeStruct((TOP_K, n), jnp.int32),
                   jax.ShapeDtypeStruct((N_EXPERTS, 128), jnp.int32),
                   jax.ShapeDtypeStruct((n * ROW_CHUNKS, 128), F32)],
        scratch_shapes=[pltpu.VMEM((N_EXPERTS, 128), F32)],
        compiler_params=_cparams(("arbitrary",)),
        name="moe_router",
    )(h_first, h_second, w_router.T, b_router.reshape(N_EXPERTS, 1), tri)
    return top_e, gates, rank, cnt[:, 0], h3


LANES = 128
PREP_ROWS = 64


def _pair_perm(v):
    shp = v.shape
    v4 = v.reshape(shp[:-1] + (shp[-1] // LANES, 2, LANES // 2))
    return jnp.swapaxes(v4, -1, -2).reshape(shp)


def _expert_kernel(be_ref, nact_ref, src0_ref, src1_ref, dstp_ref, dstc_ref, h3_ref,
                   win_ref, bg_ref, bl_ref, wout_ref, bo_ref, ys_ref,
                   wg_s, wl_s, wo_t, wo_s, xbuf0, xbuf1, ybuf0, ybuf1, gsem, ssem):
    b = pl.program_id(0)
    nact = nact_ref[0]
    active = b < nact
    last = b == nact - 1
    xbufs, ybufs = (xbuf0, xbuf1), (ybuf0, ybuf1)
    blk = xbuf0.shape[0]
    tmb = blk // ROW_CHUNKS
    fresh = jnp.logical_or(b == 0, be_ref[b] != be_ref[jnp.maximum(b - 1, 0)])

    def row_tile(ref, row):
        return ref.at[pl.ds(pl.multiple_of(row * ROW_CHUNKS, ROW_CHUNKS), ROW_CHUNKS)]

    def start_gather(src_ref, s):
        for r in range(tmb):
            pltpu.make_async_copy(row_tile(h3_ref, src_ref[0, 0, r]), xbufs[s].at[pl.ds(r * ROW_CHUNKS, ROW_CHUNKS)],
                                  gsem.at[s]).start()

    def wait_gather(s):
        pltpu.make_async_copy(h3_ref.at[pl.ds(0, blk)], xbufs[s], gsem.at[s]).wait()

    def start_scatter(dst_ref, s):
        for r in range(tmb):
            pltpu.make_async_copy(ybufs[s].at[pl.ds(r * ROW_CHUNKS, ROW_CHUNKS)], row_tile(ys_ref, dst_ref[0, 0, r]),
                                  ssem.at[s]).start()

    def wait_scatter(s):
        pltpu.make_async_copy(ybufs[s], ys_ref.at[pl.ds(0, blk)], ssem.at[s]).wait()

    def chunk_rows(c):
        return pl.ds(c, tmb, stride=ROW_CHUNKS)

    def on_parity(cond, fn):
        for p in range(2):
            pl.when(jnp.logical_and(cond, b % 2 == p))(functools.partial(fn, p))

    @pl.when(jnp.logical_and(active, b == 0))
    def _():
        start_gather(src0_ref, 0)

    @pl.when(jnp.logical_and(active, fresh))
    def _():
        even = lax.broadcasted_iota(jnp.int32, (PREP_ROWS, LANES), 1) % 2 == 0

        def split_rows(r, carry):
            rows = pl.ds(pl.multiple_of(r * PREP_ROWS, PREP_ROWS), PREP_ROWS)
            for c in range(D_EXPERT // LANES):
                blk_a = win_ref[rows, 2 * c * LANES:(2 * c + 1) * LANES]
                blk_b = win_ref[rows, (2 * c + 1) * LANES:(2 * c + 2) * LANES]
                wg_s[rows, c * LANES:(c + 1) * LANES] = jnp.where(
                    even, blk_a, pltpu.roll(blk_b, 1, 1)).astype(BF16)
                wl_s[rows, c * LANES:(c + 1) * LANES] = jnp.where(
                    even, pltpu.roll(blk_a, LANES - 1, 1), blk_b).astype(BF16)
            return carry

        lax.fori_loop(0, D_MODEL // PREP_ROWS, split_rows, 0)
        half = LANES // 2
        for c in range(D_EXPERT // LANES):
            for p in range(2):
                for cb in range(D_MODEL // LANES):
                    wo_t[cb, pl.ds(c * LANES + p, half, stride=2), :] = (
                        wout_ref[pl.ds(c * LANES + half * p, half), cb * LANES:(cb + 1) * LANES])

        def cast_rows(r, carry):
            rows = pl.ds(pl.multiple_of(r * PREP_ROWS, PREP_ROWS), PREP_ROWS)
            for cb in range(D_MODEL // LANES):
                wo_s[rows, cb * LANES:(cb + 1) * LANES] = wo_t[cb, rows, :].astype(BF16)
            return carry

        lax.fori_loop(0, D_EXPERT // PREP_ROWS, cast_rows, 0)

    on_parity(active, wait_gather)
    on_parity(jnp.logical_and(active, b >= 2), wait_scatter)

    def compute(p):
        x = jnp.concatenate([xbufs[p][chunk_rows(c), :] for c in range(ROW_CHUNKS)], axis=1).astype(BF16)
        zg = _bdot(x, wg_s[...]) + bg_ref[...]
        zl = _bdot(x, wl_s[...]) + bl_ref[...]
        glu = jnp.minimum(zg, SWIGLU_LIMIT)
        lin = jnp.clip(zl, -SWIGLU_LIMIT, SWIGLU_LIMIT)
        a = glu * jax.nn.sigmoid(SWIGLU_ALPHA * glu) * (lin + 1.0)
        y = _bdot(a.astype(BF16), wo_s[...]) + bo_ref[...]
        for c in range(ROW_CHUNKS):
            ybufs[p][chunk_rows(c), :] = y[:, c * LANES:(c + 1) * LANES]

    @pl.when(jnp.logical_and(active, b == 0))
    def _():
        start_gather(src1_ref, 1)
        compute(0)

    def step(p):
        start_gather(src1_ref, 1 - p)
        start_scatter(dstp_ref, 1 - p)
        compute(p)

    on_parity(jnp.logical_and(active, b > 0), step)

    def drain(p):
        start_scatter(dstc_ref, p)
        wait_gather(1 - p)
        wait_scatter(p)

    on_parity(last, drain)
    on_parity(jnp.logical_and(last, b > 0), lambda p: wait_scatter(1 - p))

    @pl.when(last)
    def _():
        spare0 = ys_ref.shape[0] - 2 * blk
        fills = []
        for s in range(2):
            ybufs[s][...] = jnp.zeros_like(ybufs[s])
            fills.append(pltpu.make_async_copy(ybufs[s], ys_ref.at[pl.ds(spare0 + s * blk, blk)], ssem.at[s]))
        for cp in fills:
            cp.start()
        for cp in fills:
            cp.wait()


def _experts(h3, src_rows, dst_rows, block_e, nact, n_slots, layer, w_in, b_glu, b_lin, w_out, b_out):
    tmb = EXPERT_ROWS
    n_blocks = src_rows.shape[0]
    exp3 = lambda b, be, na: (be[b], 0, 0)
    exp4 = lambda b, be, na: (layer, be[b], 0, 0)
    idx_spec = lambda f: pl.BlockSpec((1, 1, tmb), lambda b, be, na: (f(b), 0, 0), memory_space=pltpu.SMEM)
    grid_spec = pltpu.PrefetchScalarGridSpec(
        num_scalar_prefetch=2,
        grid=(n_blocks,),
        in_specs=[
            idx_spec(lambda b: b),
            idx_spec(lambda b: jnp.minimum(b + 1, n_blocks - 1)),
            idx_spec(lambda b: jnp.maximum(b - 1, 0)),
            idx_spec(lambda b: b),
            pl.BlockSpec(memory_space=pl.ANY),
            pl.BlockSpec((None, None, D_MODEL, 2 * D_EXPERT), exp4),
            pl.BlockSpec((None, 1, D_EXPERT), exp3),
            pl.BlockSpec((None, 1, D_EXPERT), exp3),
            pl.BlockSpec((None, None, D_EXPERT, D_MODEL), exp4),
            pl.BlockSpec((None, 1, D_MODEL), exp3),
        ],
        out_specs=pl.BlockSpec(memory_space=pl.ANY),
        scratch_shapes=[pltpu.VMEM((D_MODEL, D_EXPERT), BF16), pltpu.VMEM((D_MODEL, D_EXPERT), BF16),
                        pltpu.VMEM((D_MODEL // LANES, D_EXPERT, LANES), F32),
                        pltpu.VMEM((D_EXPERT, D_MODEL), BF16),
                        pltpu.VMEM((tmb * ROW_CHUNKS, LANES), F32), pltpu.VMEM((tmb * ROW_CHUNKS, LANES), F32),
                        pltpu.VMEM((tmb * ROW_CHUNKS, LANES), F32), pltpu.VMEM((tmb * ROW_CHUNKS, LANES), F32),
                        pltpu.SemaphoreType.DMA((2,)), pltpu.SemaphoreType.DMA((2,))],
    )
    return pl.pallas_call(
        _expert_kernel,
        grid_spec=grid_spec,
        out_shape=jax.ShapeDtypeStruct((n_slots * ROW_CHUNKS, LANES), F32),
        compiler_params=_cparams(("arbitrary",)),
        name="moe_experts",
    )(block_e, nact, src_rows, src_rows, dst_rows, dst_rows, h3, w_in, b_glu, b_lin, w_out, b_out)


def _moe(h_first, h_second, layer, w_router, b_router, w_in, b_in, w_out, b_out):
    tmb = EXPERT_ROWS
    top_e, gates, rank, counts, h3 = _router(h_first, h_second, w_router, b_router)
    n = h3.shape[0] // ROW_CHUNKS
    m = n * TOP_K
    padded = (counts + tmb - 1) // tmb * tmb
    pend = jnp.cumsum(padded)
    pstart = pend - padded
    hot = top_e[..., None] == jnp.arange(N_EXPERTS)[None, None, :]
    dest = jnp.sum(jnp.where(hot, pstart[None, None, :], 0), axis=-1) + rank
    n_blocks = m // tmb + N_EXPERTS
    blk_row0 = jnp.arange(n_blocks, dtype=jnp.int32) * tmb
    block_e = jnp.minimum(jnp.sum(pend[None, :] <= blk_row0[:, None], axis=1), N_EXPERTS - 1).astype(jnp.int32)
    nact = (pend[-1] // tmb).astype(jnp.int32).reshape(1)
    tok = jnp.broadcast_to(jnp.arange(n, dtype=jnp.int32)[None, :], (TOP_K, n))
    slot = tok + n * jnp.arange(TOP_K, dtype=jnp.int32)[:, None]
    j = jnp.arange(N_EXPERTS * tmb, dtype=jnp.int32)
    off = j % tmb
    pad = padded - counts
    rep = lambda v: jnp.repeat(v, tmb)
    pad_key = jnp.where(off < rep(pad), rep(pstart + counts) + off, pend[-1] + j - rep(jnp.cumsum(pad)))
    pad_dst = m + ((pad_key // tmb) % 2) * tmb + pad_key % tmb
    keys = jnp.concatenate([dest.reshape(m), pad_key]).astype(jnp.int32)
    src = jnp.concatenate([tok.reshape(m), jnp.zeros_like(j)])
    dst = jnp.concatenate([slot.reshape(m), pad_dst]).astype(jnp.int32)
    _, src_rows, dst_rows = lax.sort((keys, src, dst), num_keys=1)
    b_glu = _pair_perm(b_in[:, 0::2]).reshape(N_EXPERTS, 1, D_EXPERT)
    b_lin = _pair_perm(b_in[:, 1::2]).reshape(N_EXPERTS, 1, D_EXPERT)
    ys = _experts(h3, src_rows.reshape(n_blocks, 1, tmb), dst_rows.reshape(n_blocks, 1, tmb), block_e, nact,
                  m + 2 * tmb, layer, w_in, b_glu, b_lin, w_out, b_out.reshape(N_EXPERTS, 1, D_MODEL))
    return ys, gates.T


def _post_kernel(x1_ref, *rest, with_kv):
    ys_refs = rest[:TOP_K]
    gate_ref, mod_ref, lng_ref, lnb_ref = rest[TOP_K:TOP_K + 4]
    rest = rest[TOP_K + 4:]
    g2 = mod_ref[5:6, :]
    tm = x1_ref.shape[0]
    f = None
    for k in range(TOP_K):
        yk = jnp.concatenate([ys_refs[k][pl.ds(c, tm, stride=ROW_CHUNKS), :] for c in range(ROW_CHUNKS)], axis=1)
        term = yk * gate_ref[:, k:k + 1]
        f = term if f is None else f + term
    x2 = _layer_norm(DN_ALPHA * x1_ref[...] + g2 * f, lng_ref[...], lnb_ref[...])
    if not with_kv:
        (x2_ref,) = rest
        x2_ref[...] = x2
        return
    wa_ref, kng_ref, cos_ref, sin_ref, x2_ref, lat_ref, kr_ref, kcat_ref = rest
    x2_ref[...] = x2
    kv = _bdot(x2.astype(BF16), wa_ref[...])
    c = kv[:, :KV_RANK]
    lat = c * lax.rsqrt(jnp.mean(c * c, axis=-1, keepdims=True) + RMS_EPS) * kng_ref[...]
    k = kv[:, KV_RANK:KV_RANK + QK_ROPE]
    k_swapped = kv[:, KV_RANK + QK_ROPE:]
    kr = k * cos_ref[...] + k_swapped * sin_ref[...]
    lat_ref[...] = lat
    kr_ref[...] = kr
    kcat_ref[:, :KV_RANK] = lat.astype(BF16)
    kcat_ref[:, KV_RANK:] = kr.astype(BF16)


def _swap_halves_cols(w, width):
    shp = w.shape
    w4 = w.reshape(shp[:-1] + (shp[-1] // width, 2, width // 2))
    return w4[..., ::-1, :].reshape(shp)


def _rope_tables(pos, reps):
    inv = 1.0 / (ROPE_BASE ** (jnp.arange(ROPE_HALF, dtype=F32) * (2.0 / QK_ROPE)))
    ang = pos.astype(F32)[:, None] * inv[None, :]
    cos, sin = jnp.cos(ang), jnp.sin(ang)
    cos_t = jnp.tile(jnp.concatenate([cos, cos], -1), (1, reps))
    sin_t = jnp.tile(jnp.concatenate([-sin, sin], -1), (1, reps))
    return cos_t, sin_t


def _post_layer(x1, ys, gates, row0, mod, ln_g, ln_b, *, tm, kv=None):
    bsz, s, d = x1.shape
    blk0 = row0 // tm
    per_b = s // tm
    tile = lambda w: pl.BlockSpec((None, tm, w), lambda b, i: (b, i, 0))
    n_tok = gates.shape[0]
    pick_spec = lambda k: pl.BlockSpec((tm * ROW_CHUNKS, LANES),
                                       lambda b, i: (k * (n_tok // tm) + blk0 + b * per_b + i, 0))
    in_specs = [tile(d)] + [pick_spec(k) for k in range(TOP_K)] + [
        pl.BlockSpec((tm, TOP_K), lambda b, i: (blk0 + b * per_b + i, 0)),
        pl.BlockSpec((None, 6, d), lambda b, i: (b, 0, 0)),
        _const_spec((1, d)),
        _const_spec((1, d)),
    ]
    args = [x1] + [ys] * TOP_K + [gates, mod, ln_g.reshape(1, -1), ln_b.reshape(1, -1)]
    out_shape = [jax.ShapeDtypeStruct((bsz, s, d), F32)]
    out_specs = [tile(d)]
    if kv is not None:
        w_a, kn_g, cos_t, sin_t = kv
        w_ext = jnp.concatenate([w_a, _swap_halves_cols(w_a[:, KV_RANK:], QK_ROPE)], axis=1).astype(BF16)
        in_specs += [_const_spec(w_ext.shape), _const_spec((1, KV_RANK)),
                     pl.BlockSpec((tm, QK_ROPE), lambda b, i: (i, 0)),
                     pl.BlockSpec((tm, QK_ROPE), lambda b, i: (i, 0))]
        args += [w_ext, kn_g.reshape(1, -1), cos_t, sin_t]
        out_shape += [jax.ShapeDtypeStruct((bsz, s, KV_RANK), F32), jax.ShapeDtypeStruct((bsz, s, QK_ROPE), F32),
                      jax.ShapeDtypeStruct((bsz, s, QK_LAT), BF16)]
        out_specs += [tile(KV_RANK), tile(QK_ROPE), tile(QK_LAT)]
    return pl.pallas_call(
        functools.partial(_post_kernel, with_kv=kv is not None),
        grid=(bsz, per_b),
        in_specs=in_specs,
        out_specs=out_specs,
        out_shape=out_shape,
        compiler_params=_cparams(("arbitrary", "arbitrary")),
        name="post_moe",
    )(*args)


Q_GROUP = 512
KEY_TILE = 512


def _qproj_kernel(x_ref, mod_ref, wdq_ref, qng_ref, wqn_ref, wqr_ref, wqrs_ref, wuk_ref, cos_ref, sin_ref, q_ref,
                  *, tq):
    x = x_ref[...]
    sh1, sc1 = mod_ref[0:1, :], mod_ref[1:2, :]
    h = (x * (1.0 + sc1) + sh1).astype(BF16)
    tm = x.shape[0]
    cq = lax.dot_general(wdq_ref[...], h, (((1,), (1,)), ((), ())), preferred_element_type=F32)
    cq = (cq * lax.rsqrt(jnp.mean(cq * cq, axis=0, keepdims=True) + RMS_EPS) * qng_ref[...]).astype(BF16)
    qn = _bdot(wqn_ref[...], cq)
    qr = _bdot(wqr_ref[...], cq)
    qrs = _bdot(wqrs_ref[...], cq)
    rope = (qr * cos_ref[...] + qrs * sin_ref[...]).astype(BF16)
    per_group = Q_GROUP // tq
    for hd in range(MLA_HEADS):
        ql = _bdot(wuk_ref[hd], qn[hd * QK_NOPE:(hd + 1) * QK_NOPE, :].astype(BF16)).astype(BF16)
        g, off = hd // per_group, (hd % per_group) * tq
        for t in range(tm // tq):
            q_ref[t, g, 0:KV_RANK, off:off + tq] = ql[:, t * tq:(t + 1) * tq]
            q_ref[t, g, KV_RANK:QK_LAT, off:off + tq] = rope[hd * QK_ROPE:(hd + 1) * QK_ROPE, t * tq:(t + 1) * tq]


def _qproj_layer(x, mod, w_dq, qn_g, w_qb, w_uk, cos_t, sin_t, *, tm, tq):
    bsz, s, d = x.shape
    w_qn = w_qb[:, :, :QK_NOPE].reshape(Q_RANK, MLA_HEADS * QK_NOPE)
    w_qr = w_qb[:, :, QK_NOPE:].reshape(Q_RANK, MLA_HEADS * QK_ROPE)
    w_qrs = _swap_halves_cols(w_qr, QK_ROPE)
    wuk_h = jnp.transpose(w_uk, (1, 0, 2)).astype(BF16)
    hr = MLA_HEADS * QK_ROPE
    n_groups = MLA_HEADS * tq // Q_GROUP
    nt = tm // tq
    return pl.pallas_call(
        functools.partial(_qproj_kernel, tq=tq),
        grid=(bsz, s // tm),
        in_specs=[
            pl.BlockSpec((None, tm, d), lambda b, i: (b, i, 0)),
            pl.BlockSpec((None, 6, d), lambda b, i: (b, 0, 0)),
            _const_spec((Q_RANK, d)),
            _const_spec((Q_RANK, 1)),
            _const_spec((MLA_HEADS * QK_NOPE, Q_RANK)),
            _const_spec((hr, Q_RANK)),
            _const_spec((hr, Q_RANK)),
            _const_spec(wuk_h.shape),
            pl.BlockSpec((hr, tm), lambda b, i: (0, i)),
            pl.BlockSpec((hr, tm), lambda b, i: (0, i)),
        ],
        out_specs=pl.BlockSpec((None, nt, n_groups, QK_LAT, Q_GROUP), lambda b, i: (b, i, 0, 0, 0)),
        out_shape=jax.ShapeDtypeStruct((bsz, s // tq, n_groups, QK_LAT, Q_GROUP), BF16),
        compiler_params=_cparams(("arbitrary", "arbitrary")),
        name="mla_qproj",
    )(x, mod, w_dq.T.astype(BF16), qn_g.reshape(-1, 1), w_qn.T.astype(BF16), w_qr.T.astype(BF16),
      w_qrs.T.astype(BF16), wuk_h, cos_t.T, sin_t.T)


EXP2_SCALE = ATTN_SCALE * 1.4426950408889634


def _attn_kernel(q_ref, *rest, tq, causal, n_main, tail, split_keys):
    if split_keys:
        klat_ref, kkr_ref = rest[:2]
        rest = rest[2:]
    else:
        k_ref = rest[0]
        rest = rest[1:]
    if tail:
        kt_ref = rest[0]
        rest = rest[1:]
    x_ref, mod_ref, wuv_ref, wo_ref, lng_ref, lnb_ref, x1_ref, h2_ref, m_s, l_s, acc_s, ta_s, tb_s = rest
    i = pl.program_id(1)
    n_groups = q_ref.shape[0]
    per_group = Q_GROUP // tq
    m_s[...] = jnp.full_like(m_s, NEG_BIG)
    l_s[...] = jnp.zeros_like(l_s)
    acc_s[...] = jnp.zeros_like(acc_s)

    def tile_rows(j):
        return pl.ds(pl.multiple_of(j * KEY_TILE, KEY_TILE), KEY_TILE)

    def key_rows(j):
        if split_keys:
            return jnp.concatenate([klat_ref[tile_rows(j), :].astype(BF16), kkr_ref[tile_rows(j), :].astype(BF16)],
                                   axis=1)
        return k_ref[tile_rows(j), :]

    def value_rows(j):
        if split_keys:
            return klat_ref[tile_rows(j), :].astype(BF16)
        return k_ref[tile_rows(j), :KV_RANK]

    def softmax_step(g, t, v, key0, masked):
        t = t * EXP2_SCALE
        if masked:
            n = t.shape[0]
            k_chunk = (key0 + lax.broadcasted_iota(jnp.int32, (n, Q_GROUP), 0)) // CHUNK
            q_chunk = (i * tq + lax.broadcasted_iota(jnp.int32, (n, Q_GROUP), 1) % tq) // CHUNK
            t = jnp.where(k_chunk <= q_chunk, t, -jnp.inf)
        m_old = m_s[g]
        m_new = jnp.maximum(m_old, jnp.max(t, axis=0, keepdims=True))
        alpha = jnp.exp2(m_old - m_new)
        p = jnp.exp2(t - m_new)
        l_s[g] = alpha * l_s[g] + jnp.sum(p, axis=0, keepdims=True)
        acc_s[g] = alpha * acc_s[g] + lax.dot_general(
            v, p.astype(BF16), (((0,), (0,)), ((), ())), preferred_element_type=F32)
        m_s[g] = m_new

    def stage(cur, nxt, j, masked, make_next):
        v = value_rows(j)
        k_next = key_rows(j + 1) if make_next else None
        for g in range(n_groups):
            if make_next:
                nxt[g] = _bdot(k_next, q_ref[g])
            softmax_step(g, cur[g], v, j * KEY_TILE, masked)

    n_open = (i * tq) // KEY_TILE if causal else n_main - 1
    k0 = key_rows(0)
    for g in range(n_groups):
        ta_s[g] = _bdot(k0, q_ref[g])

    def pair_body(p, c):
        stage(ta_s, tb_s, 2 * p, False, True)
        stage(tb_s, ta_s, 2 * p + 1, False, True)
        return c

    lax.fori_loop(0, n_open // 2, pair_body, 0)
    if causal:
        @pl.when(n_open % 2 == 0)
        def _():
            stage(ta_s, tb_s, n_open, True, False)

        @pl.when(n_open % 2 == 1)
        def _():
            stage(ta_s, tb_s, n_open - 1, False, True)
            stage(tb_s, ta_s, n_open, True, False)
    elif n_open % 2 == 0:
        stage(ta_s, tb_s, n_open, False, False)
    else:
        stage(ta_s, tb_s, n_open - 1, False, True)
        stage(tb_s, ta_s, n_open, False, False)
    if tail:
        kt = kt_ref[...]
        for g in range(n_groups):
            softmax_step(g, _bdot(kt, q_ref[g]), kt[:, :KV_RANK], n_main * KEY_TILE, False)

    heads = []
    for g in range(n_groups):
        o = (acc_s[g] / l_s[g]).T
        for hh in range(per_group):
            hd = g * per_group + hh
            heads.append(_bdot(o[hh * tq:(hh + 1) * tq].astype(BF16), wuv_ref[hd]))
    oc = jnp.concatenate(heads, axis=-1).astype(BF16)
    y = _bdot(oc, wo_ref[...])
    x = x_ref[...]
    g1, sh2, sc2 = mod_ref[2:3, :], mod_ref[3:4, :], mod_ref[4:5, :]
    x1 = _layer_norm(DN_ALPHA * x + g1 * y, lng_ref[...], lnb_ref[...])
    x1_ref[...] = x1
    h2_ref[...] = x1 * (1.0 + sc2) + sh2


def _attn_layer(q, keys, x, mod, w_uv, w_o, ln_g, ln_b, *, tq, causal, k_tail=None):
    bsz, s, d = x.shape
    split_keys = isinstance(keys, tuple)
    keys = keys if split_keys else (keys,)
    t_main = keys[0].shape[1]
    n_main = t_main // KEY_TILE
    n_groups = q.shape[2]
    tail = k_tail is not None
    wuv_t = jnp.transpose(w_uv, (1, 0, 2)).astype(BF16)
    tile = pl.BlockSpec((None, tq, d), lambda b, i: (b, i, 0))
    in_specs = [pl.BlockSpec((None, None, n_groups, QK_LAT, Q_GROUP), lambda b, i: (b, i, 0, 0, 0))]
    in_specs += [pl.BlockSpec((None, t_main, k.shape[2]), lambda b, i: (b, 0, 0)) for k in keys]
    args = [q, *keys]
    if tail:
        in_specs.append(pl.BlockSpec((None, k_tail.shape[1], QK_LAT), lambda b, i: (b, 0, 0)))
        args.append(k_tail)
    in_specs += [
        tile,
        pl.BlockSpec((None, 6, d), lambda b, i: (b, 0, 0)),
        _const_spec(wuv_t.shape),
        _const_spec((MLA_HEADS * V_DIM, d)),
        _const_spec((1, d)),
        _const_spec((1, d)),
    ]
    args += [x, mod, wuv_t, w_o.astype(BF16), ln_g.reshape(1, -1), ln_b.reshape(1, -1)]
    return pl.pallas_call(
        functools.partial(_attn_kernel, tq=tq, causal=causal, n_main=n_main, tail=tail, split_keys=split_keys),
        grid=(bsz, s // tq),
        in_specs=in_specs,
        out_specs=[tile, tile],
        out_shape=[jax.ShapeDtypeStruct((bsz, s, d), F32), jax.ShapeDtypeStruct((bsz, s, d), F32)],
        scratch_shapes=[pltpu.VMEM((n_groups, 1, Q_GROUP), F32), pltpu.VMEM((n_groups, 1, Q_GROUP), F32),
                        pltpu.VMEM((n_groups, KV_RANK, Q_GROUP), F32),
                        pltpu.VMEM((n_groups, KEY_TILE, Q_GROUP), F32),
                        pltpu.VMEM((n_groups, KEY_TILE, Q_GROUP), F32)],
        compiler_params=_cparams(("arbitrary", "arbitrary")),
        name="mla_attention",
    )(*args)


def kernel(x_prompt, x_sample, cache_kv_latent, cache_k_rope, c_prompt, c_sample, ada_w, ada_b, ln_g, ln_b,
           a_w_in, a_b_in, a_vn_g, a_vn_b, a_w_s, a_b_s, a_w_out, a_b_out, kv_w_a, kv_norm_g, kv_w_uk, kv_w_uv,
           q_w_a, q_norm_g, q_w_b, q_w_o, moe_w_router, moe_b_router, moe_w_in, moe_b_in, moe_w_out, moe_b_out):
    bp, sp, d = x_prompt.shape
    bs, ss, _ = x_sample.shape
    past = cache_kv_latent.shape[1]
    n_p = bp * sp
    n_s = bs * ss

    mod = _ada_mod(jnp.concatenate([c_prompt, c_sample], axis=0), ada_w, ada_b)
    mod = mod.reshape(DEPTH, bp + bs, 6, d)
    mod_p, mod_s = mod[:, :bp], mod[:, bp:]

    pos_p = jnp.arange(sp, dtype=jnp.int32)
    pos_s = past + jnp.arange(ss, dtype=jnp.int32)

    def moe(l, h2_p, h2_s):
        return _moe(h2_p.reshape(n_p, d), h2_s.reshape(n_s, d), l, moe_w_router[l], moe_b_router[l],
                    moe_w_in, moe_b_in[l], moe_w_out, moe_b_out[l])

    gm = functools.partial(_gmlp_layer, w_in=a_w_in[0], b_in=a_b_in[0], vn_g=a_vn_g[0], vn_b=a_vn_b[0],
                           w_s=a_w_s[0], b_s=a_b_s[0], w_out=a_w_out[0], b_out=a_b_out[0],
                           ln_g=ln_g[0, 0], ln_b=ln_b[0, 0])
    x1_p, h2_p = gm(x_prompt, mod_p[0], tm=256, write_v=False)
    x1_s, h2_s, v_s = gm(x_sample, mod_s[0], tm=ss, write_v=True)
    ys, gates = moe(0, h2_p, h2_s)
    x2_p, lat_p, kr_p, kcat_p = _post_layer(x1_p, ys, gates, 0, mod_p[0], ln_g[0, 1], ln_b[0, 1], tm=512,
                                            kv=(kv_w_a, kv_norm_g) + _rope_tables(pos_p, 1))
    x2_s, lat_s, kr_s, kcat_s = _post_layer(x1_s, ys, gates, n_p, mod_s[0], ln_g[0, 1], ln_b[0, 1], tm=ss,
                                            kv=(kv_w_a, kv_norm_g) + _rope_tables(pos_s, 1))

    qp = functools.partial(_qproj_layer, w_dq=q_w_a[0], qn_g=q_norm_g[0], w_qb=q_w_b[0], w_uk=kv_w_uk)
    at = functools.partial(_attn_layer, w_uv=kv_w_uv, w_o=q_w_o[0], ln_g=ln_g[1, 0], ln_b=ln_b[1, 0])
    cos_p, sin_p = _rope_tables(pos_p, MLA_HEADS)
    cos_s, sin_s = _rope_tables(pos_s, MLA_HEADS)
    q_p = qp(x2_p, mod_p[1], cos_t=cos_p, sin_t=sin_p, tm=256, tq=128)
    q_s = qp(x2_s, mod_s[1], cos_t=cos_s, sin_t=sin_s, tm=ss, tq=ss)
    x3_p, h4_p = at(q_p, kcat_p, x2_p, mod_p[1], tq=128, causal=True)
    x3_s, h4_s = at(q_s, (cache_kv_latent, cache_k_rope), x2_s, mod_s[1], tq=ss, causal=False, k_tail=kcat_s)
    ys, gates = moe(1, h4_p, h4_s)
    (y_p,) = _post_layer(x3_p, ys, gates, 0, mod_p[1], ln_g[1, 1], ln_b[1, 1], tm=512)
    (y_s,) = _post_layer(x3_s, ys, gates, n_p, mod_s[1], ln_g[1, 1], ln_b[1, 1], tm=ss)

    return (y_p, y_s, lat_p, kr_p, lat_s, kr_s, v_s[None])
```

```python
import functools

import jax
import jax.numpy as jnp
from jax import lax
from jax.experimental import pallas as pl
from jax.experimental.pallas import tpu as pltpu

D_MODEL = 1024
DEPTH = 2
CHUNK = 64
A_CHUNK = 128
A_HALF = 2 * D_MODEL
A_GROUPS = 8
A_GROUP_W = A_HALF // A_GROUPS
MLA_HEADS = 8
QK_NOPE = 128
QK_ROPE = 64
ROPE_HALF = QK_ROPE // 2
V_DIM = 128
KV_RANK = 256
Q_RANK = 512
QK_LAT = KV_RANK + QK_ROPE
ROPE_BASE = 10000.0
ATTN_SCALE = (QK_NOPE + QK_ROPE) ** -0.5
N_EXPERTS = 32
TOP_K = 4
D_EXPERT = D_MODEL
SWIGLU_LIMIT = 7.0
SWIGLU_ALPHA = 1.702
DN_ALPHA = (2 * DEPTH) ** 0.25
LN_EPS = 1e-5
RMS_EPS = 1e-6

BF16 = jnp.bfloat16
F32 = jnp.float32

VMEM_LIMIT = 56 * 1024 * 1024
EXPERT_ROWS = 256
ROUTER_ROWS = 512
NEG_BIG = -1e30


def _cparams(sem):
    return pltpu.CompilerParams(dimension_semantics=sem, vmem_limit_bytes=VMEM_LIMIT)


def _const_spec(shape):
    nd = len(shape)
    return pl.BlockSpec(shape, lambda *_: (0,) * nd, pipeline_mode=pl.Buffered(1))


def _layer_norm(r, g, b):
    rc = r - jnp.mean(r, axis=-1, keepdims=True)
    var = jnp.mean(rc * rc, axis=-1, keepdims=True)
    return rc * lax.rsqrt(var + LN_EPS) * g + b


def _gelu_tanh(x):
    c = 0.7978845608028654
    return 0.5 * x * (1.0 + jnp.tanh(c * (x + 0.044715 * (x * x * x))))


def _bdot(a, b):
    return jnp.dot(a, b, preferred_element_type=F32)


def _ada_kernel(c_ref, w_ref, b_ref, o_ref):
    c = c_ref[...]
    s = c * jax.nn.sigmoid(c)
    o_ref[...] = jnp.dot(s, w_ref[...], preferred_element_type=F32,
                         precision=lax.Precision.HIGHEST) + b_ref[...]


def _ada_mod(c_all, ada_w, ada_b):
    nb = c_all.shape[0]
    six_d = ada_w.shape[-1]
    tn = D_MODEL
    return pl.pallas_call(
        _ada_kernel,
        grid=(DEPTH, six_d // tn),
        in_specs=[
            pl.BlockSpec((nb, D_MODEL), lambda l, j: (0, 0)),
            pl.BlockSpec((None, D_MODEL, tn), lambda l, j: (l, 0, j)),
            pl.BlockSpec((None, 1, tn), lambda l, j: (l, 0, j)),
        ],
        out_specs=pl.BlockSpec((None, nb, tn), lambda l, j: (l, 0, j)),
        out_shape=jax.ShapeDtypeStruct((DEPTH, nb, six_d), F32),
        compiler_params=_cparams(("arbitrary", "arbitrary")),
        name="ada_mod",
    )(c_all, ada_w, ada_b.reshape(DEPTH, 1, six_d))


def _gmlp_kernel(x_ref, mod_ref, w_in_ref, b_in_ref, vng_ref, vnb_ref, ws_ref, bs_ref,
                 w_out_ref, b_out_ref, lng_ref, lnb_ref, *rest, write_v):
    if write_v:
        x1_ref, h2_ref, v_ref, u_s, v_s, p_s = rest
    else:
        x1_ref, h2_ref, u_s, v_s, p_s = rest
    x = x_ref[...]
    sh1, sc1, g1 = mod_ref[0:1, :], mod_ref[1:2, :], mod_ref[2:3, :]
    sh2, sc2 = mod_ref[3:4, :], mod_ref[4:5, :]
    h = (x * (1.0 + sc1) + sh1).astype(BF16)
    tm = x.shape[0]
    ch = 512
    n_ch = A_HALF // ch
    for j in range(n_ch):
        sl = slice(j * ch, (j + 1) * ch)
        u_s[:, sl] = _gelu_tanh(_bdot(h, w_in_ref[:, sl]) + b_in_ref[:, sl])
    tot = jnp.zeros((tm, 1), F32)
    for j in range(n_ch):
        sl = slice(j * ch, (j + 1) * ch)
        slw = slice(A_HALF + j * ch, A_HALF + (j + 1) * ch)
        g = _gelu_tanh(_bdot(h, w_in_ref[:, slw]) + b_in_ref[:, slw])
        v_s[:, sl] = g
        tot = tot + jnp.sum(g, axis=-1, keepdims=True)
    mean = tot * (1.0 / A_HALF)
    sq = jnp.zeros((tm, 1), F32)
    for j in range(n_ch):
        sl = slice(j * ch, (j + 1) * ch)
        c = v_s[:, sl] - mean
        sq = sq + jnp.sum(c * c, axis=-1, keepdims=True)
    rstd = lax.rsqrt(sq * (1.0 / A_HALF) + LN_EPS)
    for g in range(A_GROUPS):
        sl = slice(g * A_GROUP_W, (g + 1) * A_GROUP_W)
        vn = (v_s[:, sl] - mean) * rstd * vng_ref[:, sl] + vnb_ref[:, sl]
        if write_v:
            v_ref[:, sl] = vn
        sg = _bdot(ws_ref[g], vn.astype(BF16)) + bs_ref[:, g:g + 1]
        p_s[:, sl] = (u_s[:, sl] * sg).astype(BF16)
    y = _bdot(p_s[...], w_out_ref[...]) + b_out_ref[...]
    x1 = _layer_norm(DN_ALPHA * x + g1 * y, lng_ref[...], lnb_ref[...])
    x1_ref[...] = x1
    h2_ref[...] = x1 * (1.0 + sc2) + sh2


def _gmlp_layer(x, mod, w_in, b_in, vn_g, vn_b, w_s, b_s, w_out, b_out, ln_g, ln_b, *, tm, write_v):
    bsz, s, d = x.shape
    seg = min(s, A_CHUNK)
    idx = jnp.arange(seg)
    mask = (idx[:, None] // CHUNK) >= (idx[None, :] // CHUNK)
    wm = jnp.where(mask[None], w_s[:, :seg, :seg], 0.0)
    reps = tm // seg
    eye = jnp.eye(reps, dtype=F32)
    ws_big = jnp.einsum("ab,gij->gaibj", eye, wm).reshape(A_GROUPS, tm, tm).astype(BF16)
    bs_big = jnp.tile(b_s[:, :seg].T, (reps, 1))
    out_shape = [jax.ShapeDtypeStruct((bsz, s, d), F32), jax.ShapeDtypeStruct((bsz, s, d), F32)]
    tile = lambda w: pl.BlockSpec((None, tm, w), lambda b, i: (b, i, 0))
    out_specs = [tile(d), tile(d)]
    if write_v:
        out_shape.append(jax.ShapeDtypeStruct((bsz, s, A_HALF), F32))
        out_specs.append(tile(A_HALF))
    return pl.pallas_call(
        functools.partial(_gmlp_kernel, write_v=write_v),
        grid=(bsz, s // tm),
        in_specs=[
            tile(d),
            pl.BlockSpec((None, 6, d), lambda b, i: (b, 0, 0)),
            _const_spec((d, 2 * A_HALF)),
            _const_spec((1, 2 * A_HALF)),
            _const_spec((1, A_HALF)),
            _const_spec((1, A_HALF)),
            _const_spec((A_GROUPS, tm, tm)),
            _const_spec((tm, A_GROUPS)),
            _const_spec((A_HALF, d)),
            _const_spec((1, d)),
            _const_spec((1, d)),
            _const_spec((1, d)),
        ],
        out_specs=out_specs,
        out_shape=out_shape,
        scratch_shapes=[pltpu.VMEM((tm, A_HALF), F32), pltpu.VMEM((tm, A_HALF), F32),
                        pltpu.VMEM((tm, A_HALF), BF16)],
        compiler_params=_cparams(("arbitrary", "arbitrary")),
        name="gmlp_layer",
    )(x, mod, w_in.astype(BF16), b_in.reshape(1, -1), vn_g.reshape(1, -1), vn_b.reshape(1, -1),
      ws_big, bs_big, w_out.astype(BF16), b_out.reshape(1, -1), ln_g.reshape(1, -1), ln_b.reshape(1, -1))


ROW_CHUNKS = D_MODEL // 128


def _router_kernel(hp_ref, hs_ref, wr_ref, br_ref, tri_ref, e_ref, g_ref, r_ref, cnt_ref, h3_ref, base_s,
                   *, n_first):
    i = pl.program_id(0)

    @pl.when(i == 0)
    def _():
        base_s[...] = jnp.zeros_like(base_s)

    h = jnp.where(i < n_first, hp_ref[...], hs_ref[...])
    rows = h.shape[0]
    for c in range(ROW_CHUNKS):
        h3_ref[pl.ds(c, rows, stride=ROW_CHUNKS), :] = h[:, c * 128:(c + 1) * 128]
    logits = lax.dot_general(wr_ref[...], h, (((1,), (1,)), ((), ())),
                             preferred_element_type=F32,
                             precision=lax.Precision.HIGHEST) + br_ref[...]
    rows = h.shape[0]
    iota = lax.broadcasted_iota(jnp.int32, (N_EXPERTS, rows), 0)
    l = logits
    vals, hots = [], []
    for k in range(TOP_K):
        m = jnp.max(l, axis=0, keepdims=True)
        idx = jnp.min(jnp.where(l == m, iota, N_EXPERTS), axis=0, keepdims=True)
        hot = iota == idx
        vals.append(m)
        hots.append(hot)
        e_ref[k:k + 1, :] = idx
        l = jnp.where(hot, -jnp.inf, l)
    exps = [jnp.exp(v - vals[0]) for v in vals]
    den = exps[0] + exps[1] + exps[2] + exps[3]
    for k in range(TOP_K):
        g_ref[k:k + 1, :] = exps[k] / den
    hot_all = jnp.where(hots[0] | hots[1] | hots[2] | hots[3], 1.0, 0.0)
    before = _bdot(hot_all.astype(BF16), tri_ref[...]) + base_s[:, 0:1]
    for k in range(TOP_K):
        r_ref[k:k + 1, :] = jnp.sum(jnp.where(hots[k], before, 0.0), axis=0, keepdims=True).astype(jnp.int32)
    base_s[...] = base_s[...] + jnp.sum(hot_all, axis=1, keepdims=True)
    cnt_ref[...] = base_s[...].astype(jnp.int32)


def _router(h_first, h_second, w_router, b_router):
    tr = ROUTER_ROWS
    n_first = h_first.shape[0] // tr
    n_second = h_second.shape[0] // tr
    n = h_first.shape[0] + h_second.shape[0]
    tri = (jnp.arange(tr)[:, None] < jnp.arange(tr)[None, :]).astype(BF16)
    sel = pl.BlockSpec((TOP_K, tr), lambda i: (0, i))
    top_e, gates, rank, cnt, h3 = pl.pallas_call(
        functools.partial(_router_kernel, n_first=n_first),
        grid=(n_first + n_second,),
        in_specs=[
            pl.BlockSpec((tr, D_MODEL), lambda i: (jnp.minimum(i, n_first - 1), 0)),
            pl.BlockSpec((tr, D_MODEL), lambda i: (jnp.maximum(i - n_first, 0), 0)),
            _const_spec((N_EXPERTS, D_MODEL)),
            _const_spec((N_EXPERTS, 1)),
            _const_spec((tr, tr)),
        ],
        out_specs=[sel, sel, sel, pl.BlockSpec((N_EXPERTS, 128), lambda i: (0, 0)),
                   pl.BlockSpec((tr * ROW_CHUNKS, 128), lambda i: (i, 0))],
        out_shape=[jax.ShapeDtypeStruct((TOP_K, n), jnp.int32), jax.ShapeDtypeStruct((TOP_K, n), F32),
                   jax.ShapeDtypeStruct((TOP_K, n), jnp.int32),
                   jax.ShapeDtypeStruct((N_EXPERTS, 128), jnp.int32),
                   jax.ShapeDtypeStruct((n * ROW_CHUNKS, 128), F32)],
        scratch_shapes=[pltpu.VMEM((N_EXPERTS, 128), F32)],
        compiler_params=_cparams(("arbitrary",)),
        name="moe_router",
    )(h_first, h_second, w_router.T, b_router.reshape(N_EXPERTS, 1), tri)
    return top_e, gates, rank, cnt[:, 0], h3


LANES = 128
PREP_ROWS = 64


def _pair_perm(v):
    shp = v.shape
    v4 = v.reshape(shp[:-1] + (shp[-1] // LANES, 2, LANES // 2))
    return jnp.swapaxes(v4, -1, -2).reshape(shp)


def _expert_kernel(be_ref, nact_ref, src0_ref, src1_ref, dstp_ref, dstc_ref, h3_ref,
                   win_ref, bg_ref, bl_ref, wout_ref, bo_ref, ys_ref,
                   wg_s, wl_s, wo_t, wo_s, xbuf0, xbuf1, ybuf0, ybuf1, gsem, ssem):
    b = pl.program_id(0)
    nact = nact_ref[0]
    active = b < nact
    last = b == nact - 1
    xbufs, ybufs = (xbuf0, xbuf1), (ybuf0, ybuf1)
    blk = xbuf0.shape[0]
    tmb = blk // ROW_CHUNKS
    fresh = jnp.logical_or(b == 0, be_ref[b] != be_ref[jnp.maximum(b - 1, 0)])

    def row_tile(ref, row):
        return ref.at[pl.ds(pl.multiple_of(row * ROW_CHUNKS, ROW_CHUNKS), ROW_CHUNKS)]

    def start_gather(src_ref, s):
        for r in range(tmb):
            pltpu.make_async_copy(row_tile(h3_ref, src_ref[0, 0, r]), xbufs[s].at[pl.ds(r * ROW_CHUNKS, ROW_CHUNKS)],
                                  gsem.at[s]).start(priority=r % 2)

    def wait_gather(s):
        pltpu.make_async_copy(h3_ref.at[pl.ds(0, blk)], xbufs[s], gsem.at[s]).wait()

    def start_scatter(dst_ref, s):
        for r in range(tmb):
            pltpu.make_async_copy(ybufs[s].at[pl.ds(r * ROW_CHUNKS, ROW_CHUNKS)], row_tile(ys_ref, dst_ref[0, 0, r]),
                                  ssem.at[s]).start(priority=r % 2)

    def wait_scatter(s):
        pltpu.make_async_copy(ybufs[s], ys_ref.at[pl.ds(0, blk)], ssem.at[s]).wait()

    def chunk_rows(c):
        return pl.ds(c, tmb, stride=ROW_CHUNKS)

    def on_parity(cond, fn):
        for p in range(2):
            pl.when(jnp.logical_and(cond, b % 2 == p))(functools.partial(fn, p))

    @pl.when(jnp.logical_and(active, b == 0))
    def _():
        start_gather(src0_ref, 0)

    @pl.when(jnp.logical_and(active, fresh))
    def _():
        even = lax.broadcasted_iota(jnp.int32, (PREP_ROWS, LANES), 1) % 2 == 0

        def split_rows(r, carry):
            rows = pl.ds(pl.multiple_of(r * PREP_ROWS, PREP_ROWS), PREP_ROWS)
            for c in range(D_EXPERT // LANES):
                blk_a = win_ref[rows, 2 * c * LANES:(2 * c + 1) * LANES]
                blk_b = win_ref[rows, (2 * c + 1) * LANES:(2 * c + 2) * LANES]
                wg_s[rows, c * LANES:(c + 1) * LANES] = jnp.where(
                    even, blk_a, pltpu.roll(blk_b, 1, 1)).astype(BF16)
                wl_s[rows, c * LANES:(c + 1) * LANES] = jnp.where(
                    even, pltpu.roll(blk_a, LANES - 1, 1), blk_b).astype(BF16)
            return carry

        lax.fori_loop(0, D_MODEL // PREP_ROWS, split_rows, 0)
        half = LANES // 2
        for c in range(D_EXPERT // LANES):
            for p in range(2):
                for cb in range(D_MODEL // LANES):
                    wo_t[cb, pl.ds(c * LANES + p, half, stride=2), :] = (
                        wout_ref[pl.ds(c * LANES + half * p, half), cb * LANES:(cb + 1) * LANES])

        def cast_rows(r, carry):
            rows = pl.ds(pl.multiple_of(r * PREP_ROWS, PREP_ROWS), PREP_ROWS)
            for cb in range(D_MODEL // LANES):
                wo_s[rows, cb * LANES:(cb + 1) * LANES] = wo_t[cb, rows, :].astype(BF16)
            return carry

        lax.fori_loop(0, D_EXPERT // PREP_ROWS, cast_rows, 0)

    on_parity(active, wait_gather)
    on_parity(jnp.logical_and(active, b >= 2), wait_scatter)

    def compute(p):
        x = jnp.concatenate([xbufs[p][chunk_rows(c), :] for c in range(ROW_CHUNKS)], axis=1).astype(BF16)
        zg = _bdot(x, wg_s[...]) + bg_ref[...]
        zl = _bdot(x, wl_s[...]) + bl_ref[...]
        glu = jnp.minimum(zg, SWIGLU_LIMIT)
        lin = jnp.clip(zl, -SWIGLU_LIMIT, SWIGLU_LIMIT)
        a = glu * jax.nn.sigmoid(SWIGLU_ALPHA * glu) * (lin + 1.0)
        y = _bdot(a.astype(BF16), wo_s[...]) + bo_ref[...]
        for c in range(ROW_CHUNKS):
            ybufs[p][chunk_rows(c), :] = y[:, c * LANES:(c + 1) * LANES]

    @pl.when(jnp.logical_and(active, b == 0))
    def _():
        start_gather(src1_ref, 1)
        compute(0)

    def step(p):
        start_gather(src1_ref, 1 - p)
        start_scatter(dstp_ref, 1 - p)
        compute(p)

    on_parity(jnp.logical_and(active, b > 0), step)

    def drain(p):
        start_scatter(dstc_ref, p)
        wait_gather(1 - p)
        wait_scatter(p)

    on_parity(last, drain)
    on_parity(jnp.logical_and(last, b > 0), lambda p: wait_scatter(1 - p))

    @pl.when(last)
    def _():
        spare0 = ys_ref.shape[0] - 2 * blk
        fills = []
        for s in range(2):
            ybufs[s][...] = jnp.zeros_like(ybufs[s])
            fills.append(pltpu.make_async_copy(ybufs[s], ys_ref.at[pl.ds(spare0 + s * blk, blk)], ssem.at[s]))
        for cp in fills:
            cp.start()
        for cp in fills:
            cp.wait()


def _experts(h3, src_rows, dst_rows, block_e, nact, n_slots, layer, w_in, b_glu, b_lin, w_out, b_out):
    tmb = EXPERT_ROWS
    n_blocks = src_rows.shape[0]
    exp3 = lambda b, be, na: (be[b], 0, 0)
    exp4 = lambda b, be, na: (layer, be[b], 0, 0)
    idx_spec = lambda f: pl.BlockSpec((1, 1, tmb), lambda b, be, na: (f(b), 0, 0), memory_space=pltpu.SMEM)
    grid_spec = pltpu.PrefetchScalarGridSpec(
        num_scalar_prefetch=2,
        grid=(n_blocks,),
        in_specs=[
            idx_spec(lambda b: b),
            idx_spec(lambda b: jnp.minimum(b + 1, n_blocks - 1)),
            idx_spec(lambda b: jnp.maximum(b - 1, 0)),
            idx_spec(lambda b: b),
            pl.BlockSpec(memory_space=pl.ANY),
            pl.BlockSpec((None, None, D_MODEL, 2 * D_EXPERT), exp4),
            pl.BlockSpec((None, 1, D_EXPERT), exp3),
            pl.BlockSpec((None, 1, D_EXPERT), exp3),
            pl.BlockSpec((None, None, D_EXPERT, D_MODEL), exp4),
            pl.BlockSpec((None, 1, D_MODEL), exp3),
        ],
        out_specs=pl.BlockSpec(memory_space=pl.ANY),
        scratch_shapes=[pltpu.VMEM((D_MODEL, D_EXPERT), BF16), pltpu.VMEM((D_MODEL, D_EXPERT), BF16),
                        pltpu.VMEM((D_MODEL // LANES, D_EXPERT, LANES), F32),
                        pltpu.VMEM((D_EXPERT, D_MODEL), BF16),
                        pltpu.VMEM((tmb * ROW_CHUNKS, LANES), F32), pltpu.VMEM((tmb * ROW_CHUNKS, LANES), F32),
                        pltpu.VMEM((tmb * ROW_CHUNKS, LANES), F32), pltpu.VMEM((tmb * ROW_CHUNKS, LANES), F32),
                        pltpu.SemaphoreType.DMA((2,)), pltpu.SemaphoreType.DMA((2,))],
    )
    return pl.pallas_call(
        _expert_kernel,
        grid_spec=grid_spec,
        out_shape=jax.ShapeDtypeStruct((n_slots * ROW_CHUNKS, LANES), F32),
        compiler_params=_cparams(("arbitrary",)),
        name="moe_experts",
    )(block_e, nact, src_rows, src_rows, dst_rows, dst_rows, h3, w_in, b_glu, b_lin, w_out, b_out)


def _moe(h_first, h_second, layer, w_router, b_router, w_in, b_in, w_out, b_out):
    tmb = EXPERT_ROWS
    top_e, gates, rank, counts, h3 = _router(h_first, h_second, w_router, b_router)
    n = h3.shape[0] // ROW_CHUNKS
    m = n * TOP_K
    padded = (counts + tmb - 1) // tmb * tmb
    pend = jnp.cumsum(padded)
    pstart = pend - padded
    hot = top_e[..., None] == jnp.arange(N_EXPERTS)[None, None, :]
    dest = jnp.sum(jnp.where(hot, pstart[None, None, :], 0), axis=-1) + rank
    n_blocks = m // tmb + N_EXPERTS
    blk_row0 = jnp.arange(n_blocks, dtype=jnp.int32) * tmb
    block_e = jnp.minimum(jnp.sum(pend[None, :] <= blk_row0[:, None], axis=1), N_EXPERTS - 1).astype(jnp.int32)
    nact = (pend[-1] // tmb).astype(jnp.int32).reshape(1)
    slot = jnp.arange(n, dtype=jnp.int32)[None, :] + n * jnp.arange(TOP_K, dtype=jnp.int32)[:, None]
    j = jnp.arange(N_EXPERTS * tmb, dtype=jnp.int32)
    off = j % tmb
    pad = padded - counts
    rep = lambda v: jnp.repeat(v, tmb)
    pad_key = jnp.where(off < rep(pad), rep(pstart + counts) + off, pend[-1] + j - rep(jnp.cumsum(pad)))
    keys = jnp.concatenate([dest.reshape(m), pad_key]).astype(jnp.int32)
    vals = jnp.concatenate([slot.reshape(m), jnp.full_like(j, -1)])
    _, row_slot = lax.sort((keys, vals), num_keys=1)
    row = jnp.arange(n_blocks * tmb, dtype=jnp.int32)
    src_rows = jnp.where(row_slot < 0, 0, row_slot % n)
    dst_rows = jnp.where(row_slot < 0, m + ((row // tmb) % 2) * tmb + row % tmb, row_slot)
    b_glu = _pair_perm(b_in[:, 0::2]).reshape(N_EXPERTS, 1, D_EXPERT)
    b_lin = _pair_perm(b_in[:, 1::2]).reshape(N_EXPERTS, 1, D_EXPERT)
    ys = _experts(h3, src_rows.reshape(n_blocks, 1, tmb), dst_rows.reshape(n_blocks, 1, tmb), block_e, nact,
                  m + 2 * tmb, layer, w_in, b_glu, b_lin, w_out, b_out.reshape(N_EXPERTS, 1, D_MODEL))
    return ys, gates.T


def _post_kernel(x1_ref, *rest, with_kv):
    ys_refs = rest[:TOP_K]
    gate_ref, mod_ref, lng_ref, lnb_ref = rest[TOP_K:TOP_K + 4]
    rest = rest[TOP_K + 4:]
    g2 = mod_ref[5:6, :]
    tm = x1_ref.shape[0]
    f = None
    for k in range(TOP_K):
        yk = jnp.concatenate([ys_refs[k][pl.ds(c, tm, stride=ROW_CHUNKS), :] for c in range(ROW_CHUNKS)], axis=1)
        term = yk * gate_ref[:, k:k + 1]
        f = term if f is None else f + term
    x2 = _layer_norm(DN_ALPHA * x1_ref[...] + g2 * f, lng_ref[...], lnb_ref[...])
    if not with_kv:
        (x2_ref,) = rest
        x2_ref[...] = x2
        return
    wa_ref, kng_ref, cos_ref, sin_ref, x2_ref, lat_ref, kr_ref, kcat_ref = rest
    x2_ref[...] = x2
    kv = _bdot(x2.astype(BF16), wa_ref[...])
    c = kv[:, :KV_RANK]
    lat = c * lax.rsqrt(jnp.mean(c * c, axis=-1, keepdims=True) + RMS_EPS) * kng_ref[...]
    k = kv[:, KV_RANK:KV_RANK + QK_ROPE]
    k_swapped = kv[:, KV_RANK + QK_ROPE:]
    kr = k * cos_ref[...] + k_swapped * sin_ref[...]
    lat_ref[...] = lat
    kr_ref[...] = kr
    kcat_ref[:, :KV_RANK] = lat.astype(BF16)
    kcat_ref[:, KV_RANK:] = kr.astype(BF16)


def _swap_halves_cols(w, width):
    shp = w.shape
    w4 = w.reshape(shp[:-1] + (shp[-1] // width, 2, width // 2))
    return w4[..., ::-1, :].reshape(shp)


def _rope_tables(pos, reps):
    inv = 1.0 / (ROPE_BASE ** (jnp.arange(ROPE_HALF, dtype=F32) * (2.0 / QK_ROPE)))
    ang = pos.astype(F32)[:, None] * inv[None, :]
    cos, sin = jnp.cos(ang), jnp.sin(ang)
    cos_t = jnp.tile(jnp.concatenate([cos, cos], -1), (1, reps))
    sin_t = jnp.tile(jnp.concatenate([-sin, sin], -1), (1, reps))
    return cos_t, sin_t


def _post_layer(x1, ys, gates, row0, mod, ln_g, ln_b, *, tm, kv=None):
    bsz, s, d = x1.shape
    blk0 = row0 // tm
    per_b = s // tm
    tile = lambda w: pl.BlockSpec((None, tm, w), lambda b, i: (b, i, 0))
    n_tok = gates.shape[0]
    pick_spec = lambda k: pl.BlockSpec((tm * ROW_CHUNKS, LANES),
                                       lambda b, i: (k * (n_tok // tm) + blk0 + b * per_b + i, 0))
    in_specs = [tile(d)] + [pick_spec(k) for k in range(TOP_K)] + [
        pl.BlockSpec((tm, TOP_K), lambda b, i: (blk0 + b * per_b + i, 0)),
        pl.BlockSpec((None, 6, d), lambda b, i: (b, 0, 0)),
        _const_spec((1, d)),
        _const_spec((1, d)),
    ]
    args = [x1] + [ys] * TOP_K + [gates, mod, ln_g.reshape(1, -1), ln_b.reshape(1, -1)]
    out_shape = [jax.ShapeDtypeStruct((bsz, s, d), F32)]
    out_specs = [tile(d)]
    if kv is not None:
        w_a, kn_g, cos_t, sin_t = kv
        w_ext = jnp.concatenate([w_a, _swap_halves_cols(w_a[:, KV_RANK:], QK_ROPE)], axis=1).astype(BF16)
        in_specs += [_const_spec(w_ext.shape), _const_spec((1, KV_RANK)),
                     pl.BlockSpec((tm, QK_ROPE), lambda b, i: (i, 0)),
                     pl.BlockSpec((tm, QK_ROPE), lambda b, i: (i, 0))]
        args += [w_ext, kn_g.reshape(1, -1), cos_t, sin_t]
        out_shape += [jax.ShapeDtypeStruct((bsz, s, KV_RANK), F32), jax.ShapeDtypeStruct((bsz, s, QK_ROPE), F32),
                      jax.ShapeDtypeStruct((bsz, s, QK_LAT), BF16)]
        out_specs += [tile(KV_RANK), tile(QK_ROPE), tile(QK_LAT)]
    return pl.pallas_call(
        functools.partial(_post_kernel, with_kv=kv is not None),
        grid=(bsz, per_b),
        in_specs=in_specs,
        out_specs=out_specs,
        out_shape=out_shape,
        compiler_params=_cparams(("arbitrary", "arbitrary")),
        name="post_moe",
    )(*args)


Q_GROUP = 512
KEY_TILE = 512


def _qproj_kernel(x_ref, mod_ref, wdq_ref, qng_ref, wqn_ref, wqr_ref, wqrs_ref, wuk_ref, cos_ref, sin_ref, q_ref,
                  *, tq):
    x = x_ref[...]
    sh1, sc1 = mod_ref[0:1, :], mod_ref[1:2, :]
    h = (x * (1.0 + sc1) + sh1).astype(BF16)
    tm = x.shape[0]
    cq = lax.dot_general(wdq_ref[...], h, (((1,), (1,)), ((), ())), preferred_element_type=F32)
    cq = (cq * lax.rsqrt(jnp.mean(cq * cq, axis=0, keepdims=True) + RMS_EPS) * qng_ref[...]).astype(BF16)
    qn = _bdot(wqn_ref[...], cq)
    qr = _bdot(wqr_ref[...], cq)
    qrs = _bdot(wqrs_ref[...], cq)
    rope = (qr * cos_ref[...] + qrs * sin_ref[...]).astype(BF16)
    per_group = Q_GROUP // tq
    for hd in range(MLA_HEADS):
        ql = _bdot(wuk_ref[hd], qn[hd * QK_NOPE:(hd + 1) * QK_NOPE, :].astype(BF16)).astype(BF16)
        g, off = hd // per_group, (hd % per_group) * tq
        for t in range(tm // tq):
            q_ref[t, g, 0:KV_RANK, off:off + tq] = ql[:, t * tq:(t + 1) * tq]
            q_ref[t, g, KV_RANK:QK_LAT, off:off + tq] = rope[hd * QK_ROPE:(hd + 1) * QK_ROPE, t * tq:(t + 1) * tq]


def _qproj_layer(x, mod, w_dq, qn_g, w_qb, w_uk, cos_t, sin_t, *, tm, tq):
    bsz, s, d = x.shape
    w_qn = w_qb[:, :, :QK_NOPE].reshape(Q_RANK, MLA_HEADS * QK_NOPE)
    w_qr = w_qb[:, :, QK_NOPE:].reshape(Q_RANK, MLA_HEADS * QK_ROPE)
    w_qrs = _swap_halves_cols(w_qr, QK_ROPE)
    wuk_h = jnp.transpose(w_uk, (1, 0, 2)).astype(BF16)
    hr = MLA_HEADS * QK_ROPE
    n_groups = MLA_HEADS * tq // Q_GROUP
    nt = tm // tq
    return pl.pallas_call(
        functools.partial(_qproj_kernel, tq=tq),
        grid=(bsz, s // tm),
        in_specs=[
            pl.BlockSpec((None, tm, d), lambda b, i: (b, i, 0)),
            pl.BlockSpec((None, 6, d), lambda b, i: (b, 0, 0)),
            _const_spec((Q_RANK, d)),
            _const_spec((Q_RANK, 1)),
            _const_spec((MLA_HEADS * QK_NOPE, Q_RANK)),
            _const_spec((hr, Q_RANK)),
            _const_spec((hr, Q_RANK)),
            _const_spec(wuk_h.shape),
            pl.BlockSpec((hr, tm), lambda b, i: (0, i)),
            pl.BlockSpec((hr, tm), lambda b, i: (0, i)),
        ],
        out_specs=pl.BlockSpec((None, nt, n_groups, QK_LAT, Q_GROUP), lambda b, i: (b, i, 0, 0, 0)),
        out_shape=jax.ShapeDtypeStruct((bsz, s // tq, n_groups, QK_LAT, Q_GROUP), BF16),
        compiler_params=_cparams(("arbitrary", "arbitrary")),
        name="mla_qproj",
    )(x, mod, w_dq.T.astype(BF16), qn_g.reshape(-1, 1), w_qn.T.astype(BF16), w_qr.T.astype(BF16),
      w_qrs.T.astype(BF16), wuk_h, cos_t.T, sin_t.T)


EXP2_SCALE = ATTN_SCALE * 1.4426950408889634


def _attn_kernel(q_ref, *rest, tq, causal, n_main, tail, split_keys):
    if split_keys:
        klat_ref, kkr_ref = rest[:2]
        rest = rest[2:]
    else:
        k_ref = rest[0]
        rest = rest[1:]
    if tail:
        kt_ref = rest[0]
        rest = rest[1:]
    x_ref, mod_ref, wuv_ref, wo_ref, lng_ref, lnb_ref, x1_ref, h2_ref, m_s, l_s, acc_s, ta_s, tb_s = rest
    i = pl.program_id(1)
    n_groups = q_ref.shape[0]
    per_group = Q_GROUP // tq
    m_s[...] = jnp.full_like(m_s, NEG_BIG)
    l_s[...] = jnp.zeros_like(l_s)
    acc_s[...] = jnp.zeros_like(acc_s)

    def tile_rows(j):
        return pl.ds(pl.multiple_of(j * KEY_TILE, KEY_TILE), KEY_TILE)

    def key_rows(j):
        if split_keys:
            return jnp.concatenate([klat_ref[tile_rows(j), :].astype(BF16), kkr_ref[tile_rows(j), :].astype(BF16)],
                                   axis=1)
        return k_ref[tile_rows(j), :]

    def value_rows(j):
        if split_keys:
            return klat_ref[tile_rows(j), :].astype(BF16)
        return k_ref[tile_rows(j), :KV_RANK]

    def softmax_step(g, t, v, key0, masked):
        t = t * EXP2_SCALE
        if masked:
            n = t.shape[0]
            k_chunk = (key0 + lax.broadcasted_iota(jnp.int32, (n, Q_GROUP), 0)) // CHUNK
            q_chunk = (i * tq + lax.broadcasted_iota(jnp.int32, (n, Q_GROUP), 1) % tq) // CHUNK
            t = jnp.where(k_chunk <= q_chunk, t, -jnp.inf)
        m_old = m_s[g]
        m_new = jnp.maximum(m_old, jnp.max(t, axis=0, keepdims=True))
        alpha = jnp.exp2(m_old - m_new)
        p = jnp.exp2(t - m_new)
        l_s[g] = alpha * l_s[g] + jnp.sum(p, axis=0, keepdims=True)
        acc_s[g] = alpha * acc_s[g] + lax.dot_general(
            v, p.astype(BF16), (((0,), (0,)), ((), ())), preferred_element_type=F32)
        m_s[g] = m_new

    def stage(cur, nxt, j, masked, make_next):
        v = value_rows(j)
        k_next = key_rows(j + 1) if make_next else None
        for g in range(n_groups):
            if make_next:
                nxt[g] = _bdot(k_next, q_ref[g])
            softmax_step(g, cur[g], v, j * KEY_TILE, masked)

    n_open = (i * tq) // KEY_TILE if causal else n_main - 1
    k0 = key_rows(0)
    for g in range(n_groups):
        ta_s[g] = _bdot(k0, q_ref[g])

    def pair_body(p, c):
        stage(ta_s, tb_s, 2 * p, False, True)
        stage(tb_s, ta_s, 2 * p + 1, False, True)
        return c

    lax.fori_loop(0, n_open // 2, pair_body, 0)
    if causal:
        @pl.when(n_open % 2 == 0)
        def _():
            stage(ta_s, tb_s, n_open, True, False)

        @pl.when(n_open % 2 == 1)
        def _():
            stage(ta_s, tb_s, n_open - 1, False, True)
            stage(tb_s, ta_s, n_open, True, False)
    elif n_open % 2 == 0:
        stage(ta_s, tb_s, n_open, False, False)
    else:
        stage(ta_s, tb_s, n_open - 1, False, True)
        stage(tb_s, ta_s, n_open, False, False)
    if tail:
        kt = kt_ref[...]
        for g in range(n_groups):
            softmax_step(g, _bdot(kt, q_ref[g]), kt[:, :KV_RANK], n_main * KEY_TILE, False)

    heads = []
    for g in range(n_groups):
        o = (acc_s[g] / l_s[g]).T
        for hh in range(per_group):
            hd = g * per_group + hh
            heads.append(_bdot(o[hh * tq:(hh + 1) * tq].astype(BF16), wuv_ref[hd]))
    oc = jnp.concatenate(heads, axis=-1).astype(BF16)
    y = _bdot(oc, wo_ref[...])
    x = x_ref[...]
    g1, sh2, sc2 = mod_ref[2:3, :], mod_ref[3:4, :], mod_ref[4:5, :]
    x1 = _layer_norm(DN_ALPHA * x + g1 * y, lng_ref[...], lnb_ref[...])
    x1_ref[...] = x1
    h2_ref[...] = x1 * (1.0 + sc2) + sh2


def _attn_layer(q, keys, x, mod, w_uv, w_o, ln_g, ln_b, *, tq, causal, k_tail=None):
    bsz, s, d = x.shape
    split_keys = isinstance(keys, tuple)
    keys = keys if split_keys else (keys,)
    t_main = keys[0].shape[1]
    n_main = t_main // KEY_TILE
    n_groups = q.shape[2]
    tail = k_tail is not None
    wuv_t = jnp.transpose(w_uv, (1, 0, 2)).astype(BF16)
    tile = pl.BlockSpec((None, tq, d), lambda b, i: (b, i, 0))
    in_specs = [pl.BlockSpec((None, None, n_groups, QK_LAT, Q_GROUP), lambda b, i: (b, i, 0, 0, 0))]
    in_specs += [pl.BlockSpec((None, t_main, k.shape[2]), lambda b, i: (b, 0, 0)) for k in keys]
    args = [q, *keys]
    if tail:
        in_specs.append(pl.BlockSpec((None, k_tail.shape[1], QK_LAT), lambda b, i: (b, 0, 0)))
        args.append(k_tail)
    in_specs += [
        tile,
        pl.BlockSpec((None, 6, d), lambda b, i: (b, 0, 0)),
        _const_spec(wuv_t.shape),
        _const_spec((MLA_HEADS * V_DIM, d)),
        _const_spec((1, d)),
        _const_spec((1, d)),
    ]
    args += [x, mod, wuv_t, w_o.astype(BF16), ln_g.reshape(1, -1), ln_b.reshape(1, -1)]
    return pl.pallas_call(
        functools.partial(_attn_kernel, tq=tq, causal=causal, n_main=n_main, tail=tail, split_keys=split_keys),
        grid=(bsz, s // tq),
        in_specs=in_specs,
        out_specs=[tile, tile],
        out_shape=[jax.ShapeDtypeStruct((bsz, s, d), F32), jax.ShapeDtypeStruct((bsz, s, d), F32)],
        scratch_shapes=[pltpu.VMEM((n_groups, 1, Q_GROUP), F32), pltpu.VMEM((n_groups, 1, Q_GROUP), F32),
                        pltpu.VMEM((n_groups, KV_RANK, Q_GROUP), F32),
                        pltpu.VMEM((n_groups, KEY_TILE, Q_GROUP), F32),
                        pltpu.VMEM((n_groups, KEY_TILE, Q_GROUP), F32)],
        compiler_params=_cparams(("arbitrary", "arbitrary")),
        name="mla_attention",
    )(*args)


def kernel(x_prompt, x_sample, cache_kv_latent, cache_k_rope, c_prompt, c_sample, ada_w, ada_b, ln_g, ln_b,
           a_w_in, a_b_in, a_vn_g, a_vn_b, a_w_s, a_b_s, a_w_out, a_b_out, kv_w_a, kv_norm_g, kv_w_uk, kv_w_uv,
           q_w_a, q_norm_g, q_w_b, q_w_o, moe_w_router, moe_b_router, moe_w_in, moe_b_in, moe_w_out, moe_b_out):
    bp, sp, d = x_prompt.shape
    bs, ss, _ = x_sample.shape
    past = cache_kv_latent.shape[1]
    n_p = bp * sp
    n_s = bs * ss

    mod = _ada_mod(jnp.concatenate([c_prompt, c_sample], axis=0), ada_w, ada_b)
    mod = mod.reshape(DEPTH, bp + bs, 6, d)
    mod_p, mod_s = mod[:, :bp], mod[:, bp:]

    pos_p = jnp.arange(sp, dtype=jnp.int32)
    pos_s = past + jnp.arange(ss, dtype=jnp.int32)

    def moe(l, h2_p, h2_s):
        return _moe(h2_p.reshape(n_p, d), h2_s.reshape(n_s, d), l, moe_w_router[l], moe_b_router[l],
                    moe_w_in, moe_b_in[l], moe_w_out, moe_b_out[l])

    gm = functools.partial(_gmlp_layer, w_in=a_w_in[0], b_in=a_b_in[0], vn_g=a_vn_g[0], vn_b=a_vn_b[0],
                           w_s=a_w_s[0], b_s=a_b_s[0], w_out=a_w_out[0], b_out=a_b_out[0],
                           ln_g=ln_g[0, 0], ln_b=ln_b[0, 0])
    x1_p, h2_p = gm(x_prompt, mod_p[0], tm=256, write_v=False)
    x1_s, h2_s, v_s = gm(x_sample, mod_s[0], tm=ss, write_v=True)
    ys, gates = moe(0, h2_p, h2_s)
    x2_p, lat_p, kr_p, kcat_p = _post_layer(x1_p, ys, gates, 0, mod_p[0], ln_g[0, 1], ln_b[0, 1], tm=512,
                                            kv=(kv_w_a, kv_norm_g) + _rope_tables(pos_p, 1))
    x2_s, lat_s, kr_s, kcat_s = _post_layer(x1_s, ys, gates, n_p, mod_s[0], ln_g[0, 1], ln_b[0, 1], tm=ss,
                                            kv=(kv_w_a, kv_norm_g) + _rope_tables(pos_s, 1))

    qp = functools.partial(_qproj_layer, w_dq=q_w_a[0], qn_g=q_norm_g[0], w_qb=q_w_b[0], w_uk=kv_w_uk)
    at = functools.partial(_attn_layer, w_uv=kv_w_uv, w_o=q_w_o[0], ln_g=ln_g[1, 0], ln_b=ln_b[1, 0])
    cos_p, sin_p = _rope_tables(pos_p, MLA_HEADS)
    cos_s, sin_s = _rope_tables(pos_s, MLA_HEADS)
    q_p = qp(x2_p, mod_p[1], cos_t=cos_p, sin_t=sin_p, tm=256, tq=128)
    q_s = qp(x2_s, mod_s[1], cos_t=cos_s, sin_t=sin_s, tm=ss, tq=ss)
    x3_p, h4_p = at(q_p, kcat_p, x2_p, mod_p[1], tq=128, causal=True)
    x3_s, h4_s = at(q_s, (cache_kv_latent, cache_k_rope), x2_s, mod_s[1], tq=ss, causal=False, k_tail=kcat_s)
    ys, gates = moe(1, h4_p, h4_s)
    (y_p,) = _post_layer(x3_p, ys, gates, 0, mod_p[1], ln_g[1, 1], ln_b[1, 1], tm=512)
    (y_s,) = _post_layer(x3_s, ys, gates, n_p, mod_s[1], ln_g[1, 1], ln_b[1, 1], tm=ss)

    return (y_p, y_s, lat_p, kr_p, lat_s, kr_s, v_s[None])
```

```python
import functools

import jax
import jax.numpy as jnp
from jax import lax
from jax.experimental import pallas as pl
from jax.experimental.pallas import tpu as pltpu

D_MODEL = 1024
DEPTH = 2
CHUNK = 64
A_CHUNK = 128
A_HALF = 2 * D_MODEL
A_GROUPS = 8
A_GROUP_W = A_HALF // A_GROUPS
MLA_HEADS = 8
QK_NOPE = 128
QK_ROPE = 64
ROPE_HALF = QK_ROPE // 2
V_DIM = 128
KV_RANK = 256
Q_RANK = 512
QK_LAT = KV_RANK + QK_ROPE
ROPE_BASE = 10000.0
ATTN_SCALE = (QK_NOPE + QK_ROPE) ** -0.5
N_EXPERTS = 32
TOP_K = 4
D_EXPERT = D_MODEL
SWIGLU_LIMIT = 7.0
SWIGLU_ALPHA = 1.702
DN_ALPHA = (2 * DEPTH) ** 0.25
LN_EPS = 1e-5
RMS_EPS = 1e-6

BF16 = jnp.bfloat16
F32 = jnp.float32

VMEM_LIMIT = 56 * 1024 * 1024
EXPERT_ROWS = 256
ROUTER_ROWS = 512
NEG_BIG = -1e30


def _cparams(sem):
    return pltpu.CompilerParams(dimension_semantics=sem, vmem_limit_bytes=VMEM_LIMIT)


def _const_spec(shape):
    nd = len(shape)
    return pl.BlockSpec(shape, lambda *_: (0,) * nd, pipeline_mode=pl.Buffered(1))


def _layer_norm(r, g, b):
    rc = r - jnp.mean(r, axis=-1, keepdims=True)
    var = jnp.mean(rc * rc, axis=-1, keepdims=True)
    return rc * lax.rsqrt(var + LN_EPS) * g + b


def _gelu_tanh(x):
    c = 0.7978845608028654
    return 0.5 * x * (1.0 + jnp.tanh(c * (x + 0.044715 * (x * x * x))))


def _bdot(a, b):
    return jnp.dot(a, b, preferred_element_type=F32)


def _ada_kernel(c_ref, w_ref, b_ref, o_ref):
    c = c_ref[...]
    s = c * jax.nn.sigmoid(c)
    o_ref[...] = jnp.dot(s, w_ref[...], preferred_element_type=F32,
                         precision=lax.Precision.HIGHEST) + b_ref[...]


def _ada_mod(c_all, ada_w, ada_b):
    nb = c_all.shape[0]
    six_d = ada_w.shape[-1]
    tn = D_MODEL
    return pl.pallas_call(
        _ada_kernel,
        grid=(DEPTH, six_d // tn),
        in_specs=[
            pl.BlockSpec((nb, D_MODEL), lambda l, j: (0, 0)),
            pl.BlockSpec((None, D_MODEL, tn), lambda l, j: (l, 0, j)),
            pl.BlockSpec((None, 1, tn), lambda l, j: (l, 0, j)),
        ],
        out_specs=pl.BlockSpec((None, nb, tn), lambda l, j: (l, 0, j)),
        out_shape=jax.ShapeDtypeStruct((DEPTH, nb, six_d), F32),
        compiler_params=_cparams(("arbitrary", "arbitrary")),
        name="ada_mod",
    )(c_all, ada_w, ada_b.reshape(DEPTH, 1, six_d))


def _gmlp_kernel(x_ref, mod_ref, w_in_ref, b_in_ref, vng_ref, vnb_ref, ws_ref, bs_ref,
                 w_out_ref, b_out_ref, lng_ref, lnb_ref, *rest, write_v):
    if write_v:
        x1_ref, h2_ref, v_ref, u_s, v_s, p_s = rest
    else:
        x1_ref, h2_ref, u_s, v_s, p_s = rest
    x = x_ref[...]
    sh1, sc1, g1 = mod_ref[0:1, :], mod_ref[1:2, :], mod_ref[2:3, :]
    sh2, sc2 = mod_ref[3:4, :], mod_ref[4:5, :]
    h = (x * (1.0 + sc1) + sh1).astype(BF16)
    tm = x.shape[0]
    ch = 512
    n_ch = A_HALF // ch
    for j in range(n_ch):
        sl = slice(j * ch, (j + 1) * ch)
        u_s[:, sl] = _gelu_tanh(_bdot(h, w_in_ref[:, sl]) + b_in_ref[:, sl])
    tot = jnp.zeros((tm, 1), F32)
    for j in range(n_ch):
        sl = slice(j * ch, (j + 1) * ch)
        slw = slice(A_HALF + j * ch, A_HALF + (j + 1) * ch)
        g = _gelu_tanh(_bdot(h, w_in_ref[:, slw]) + b_in_ref[:, slw])
        v_s[:, sl] = g
        tot = tot + jnp.sum(g, axis=-1, keepdims=True)
    mean = tot * (1.0 / A_HALF)
    sq = jnp.zeros((tm, 1), F32)
    for j in range(n_ch):
        sl = slice(j * ch, (j + 1) * ch)
        c = v_s[:, sl] - mean
        sq = sq + jnp.sum(c * c, axis=-1, keepdims=True)
    rstd = lax.rsqrt(sq * (1.0 / A_HALF) + LN_EPS)
    for g in range(A_GROUPS):
        sl = slice(g * A_GROUP_W, (g + 1) * A_GROUP_W)
        vn = (v_s[:, sl] - mean) * rstd * vng_ref[:, sl] + vnb_ref[:, sl]
        if write_v:
            v_ref[:, sl] = vn
        sg = _bdot(ws_ref[g], vn.astype(BF16)) + bs_ref[:, g:g + 1]
        p_s[:, sl] = (u_s[:, sl] * sg).astype(BF16)
    y = _bdot(p_s[...], w_out_ref[...]) + b_out_ref[...]
    x1 = _layer_norm(DN_ALPHA * x + g1 * y, lng_ref[...], lnb_ref[...])
    x1_ref[...] = x1
    h2_ref[...] = x1 * (1.0 + sc2) + sh2


def _gmlp_layer(x, mod, w_in, b_in, vn_g, vn_b, w_s, b_s, w_out, b_out, ln_g, ln_b, *, tm, write_v):
    bsz, s, d = x.shape
    seg = min(s, A_CHUNK)
    idx = jnp.arange(seg)
    mask = (idx[:, None] // CHUNK) >= (idx[None, :] // CHUNK)
    wm = jnp.where(mask[None], w_s[:, :seg, :seg], 0.0)
    reps = tm // seg
    eye = jnp.eye(reps, dtype=F32)
    ws_big = jnp.einsum("ab,gij->gaibj", eye, wm).reshape(A_GROUPS, tm, tm).astype(BF16)
    bs_big = jnp.tile(b_s[:, :seg].T, (reps, 1))
    out_shape = [jax.ShapeDtypeStruct((bsz, s, d), F32), jax.ShapeDtypeStruct((bsz, s, d), F32)]
    tile = lambda w: pl.BlockSpec((None, tm, w), lambda b, i: (b, i, 0))
    out_specs = [tile(d), tile(d)]
    if write_v:
        out_shape.append(jax.ShapeDtypeStruct((bsz, s, A_HALF), F32))
        out_specs.append(tile(A_HALF))
    return pl.pallas_call(
        functools.partial(_gmlp_kernel, write_v=write_v),
        grid=(bsz, s // tm),
        in_specs=[
            tile(d),
            pl.BlockSpec((None, 6, d), lambda b, i: (b, 0, 0)),
            _const_spec((d, 2 * A_HALF)),
            _const_spec((1, 2 * A_HALF)),
            _const_spec((1, A_HALF)),
            _const_spec((1, A_HALF)),
            _const_spec((A_GROUPS, tm, tm)),
            _const_spec((tm, A_GROUPS)),
            _const_spec((A_HALF, d)),
            _const_spec((1, d)),
            _const_spec((1, d)),
            _const_spec((1, d)),
        ],
        out_specs=out_specs,
        out_shape=out_shape,
        scratch_shapes=[pltpu.VMEM((tm, A_HALF), F32), pltpu.VMEM((tm, A_HALF), F32),
                        pltpu.VMEM((tm, A_HALF), BF16)],
        compiler_params=_cparams(("arbitrary", "arbitrary")),
        name="gmlp_layer",
    )(x, mod, w_in.astype(BF16), b_in.reshape(1, -1), vn_g.reshape(1, -1), vn_b.reshape(1, -1),
      ws_big, bs_big, w_out.astype(BF16), b_out.reshape(1, -1), ln_g.reshape(1, -1), ln_b.reshape(1, -1))


ROW_CHUNKS = D_MODEL // 128


def _router_kernel(hp_ref, hs_ref, wr_ref, br_ref, tri_ref, e_ref, g_ref, r_ref, cnt_ref, h3_ref, base_s,
                   *, n_first):
    i = pl.program_id(0)

    @pl.when(i == 0)
    def _():
        base_s[...] = jnp.zeros_like(base_s)

    h = jnp.where(i < n_first, hp_ref[...], hs_ref[...])
    rows = h.shape[0]
    for c in range(ROW_CHUNKS):
        h3_ref[pl.ds(c, rows, stride=ROW_CHUNKS), :] = h[:, c * 128:(c + 1) * 128]
    logits = lax.dot_general(wr_ref[...], h, (((1,), (1,)), ((), ())),
                             preferred_element_type=F32,
                             precision=lax.Precision.HIGHEST) + br_ref[...]
    rows = h.shape[0]
    iota = lax.broadcasted_iota(jnp.int32, (N_EXPERTS, rows), 0)
    l = logits
    vals, hots = [], []
    for k in range(TOP_K):
        m = jnp.max(l, axis=0, keepdims=True)
        idx = jnp.min(jnp.where(l == m, iota, N_EXPERTS), axis=0, keepdims=True)
        hot = iota == idx
        vals.append(m)
        hots.append(hot)
        e_ref[k:k + 1, :] = idx
        l = jnp.where(hot, -jnp.inf, l)
    exps = [jnp.exp(v - vals[0]) for v in vals]
    den = exps[0] + exps[1] + exps[2] + exps[3]
    for k in range(TOP_K):
        g_ref[k:k + 1, :] = exps[k] / den
    hot_all = jnp.where(hots[0] | hots[1] | hots[2] | hots[3], 1.0, 0.0)
    before = _bdot(hot_all.astype(BF16), tri_ref[...]) + base_s[:, 0:1]
    for k in range(TOP_K):
        r_ref[k:k + 1, :] = jnp.sum(jnp.where(hots[k], before, 0.0), axis=0, keepdims=True).astype(jnp.int32)
    base_s[...] = base_s[...] + jnp.sum(hot_all, axis=1, keepdims=True)
    cnt_ref[...] = base_s[...].astype(jnp.int32)


def _router(h_first, h_second, w_router, b_router):
    tr = ROUTER_ROWS
    n_first = h_first.shape[0] // tr
    n_second = h_second.shape[0] // tr
    n = h_first.shape[0] + h_second.shape[0]
    tri = (jnp.arange(tr)[:, None] < jnp.arange(tr)[None, :]).astype(BF16)
    sel = pl.BlockSpec((TOP_K, tr), lambda i: (0, i))
    top_e, gates, rank, cnt, h3 = pl.pallas_call(
        functools.partial(_router_kernel, n_first=n_first),
        grid=(n_first + n_second,),
        in_specs=[
            pl.BlockSpec((tr, D_MODEL), lambda i: (jnp.minimum(i, n_first - 1), 0)),
            pl.BlockSpec((tr, D_MODEL), lambda i: (jnp.maximum(i - n_first, 0), 0)),
            _const_spec((N_EXPERTS, D_MODEL)),
            _const_spec((N_EXPERTS, 1)),
            _const_spec((tr, tr)),
        ],
        out_specs=[sel, sel, sel, pl.BlockSpec((N_EXPERTS, 128), lambda i: (0, 0)),
                   pl.BlockSpec((tr * ROW_CHUNKS, 128), lambda i: (i, 0))],
        out_shape=[jax.ShapeDtypeStruct((TOP_K, n), jnp.int32), jax.ShapeDtypeStruct((TOP_K, n), F32),
                   jax.ShapeDtypeStruct((TOP_K, n), jnp.int32),
                   jax.ShapeDtypeStruct((N_EXPERTS, 128), jnp.int32),
                   jax.ShapeDtypeStruct((n * ROW_CHUNKS, 128), F32)],
        scratch_shapes=[pltpu.VMEM((N_EXPERTS, 128), F32)],
        compiler_params=_cparams(("arbitrary",)),
        name="moe_router",
    )(h_first, h_second, w_router.T, b_router.reshape(N_EXPERTS, 1), tri)
    return top_e, gates, rank, cnt[:, 0], h3


LANES = 128
PREP_ROWS = 64


def _pair_perm(v):
    shp = v.shape
    v4 = v.reshape(shp[:-1] + (shp[-1] // LANES, 2, LANES // 2))
    return jnp.swapaxes(v4, -1, -2).reshape(shp)


ROW_BUFS = 4


def _expert_kernel(be_ref, nact_ref, src0_ref, src1_ref, src2_ref, dstp_ref, dstc_ref, h3_ref,
                   win_ref, bg_ref, bl_ref, wout_ref, bo_ref, ys_ref,
                   wg_s, wl_s, wo_t, wo_s, *bufs_and_sems):
    xbufs, ybufs = bufs_and_sems[:ROW_BUFS], bufs_and_sems[ROW_BUFS:2 * ROW_BUFS]
    gsem, ssem = bufs_and_sems[2 * ROW_BUFS:]
    b = pl.program_id(0)
    nact = nact_ref[0]
    active = b < nact
    last = b == nact - 1
    blk = xbufs[0].shape[0]
    tmb = blk // ROW_CHUNKS
    fresh = jnp.logical_or(b == 0, be_ref[b] != be_ref[jnp.maximum(b - 1, 0)])

    def row_tile(ref, row):
        return ref.at[pl.ds(pl.multiple_of(row * ROW_CHUNKS, ROW_CHUNKS), ROW_CHUNKS)]

    def start_gather(src_ref, s):
        for r in range(tmb):
            pltpu.make_async_copy(row_tile(h3_ref, src_ref[0, 0, r]), xbufs[s].at[pl.ds(r * ROW_CHUNKS, ROW_CHUNKS)],
                                  gsem.at[s]).start(priority=r % 2)

    def wait_gather(s):
        pltpu.make_async_copy(h3_ref.at[pl.ds(0, blk)], xbufs[s], gsem.at[s]).wait()

    def start_scatter(dst_ref, s):
        for r in range(tmb):
            pltpu.make_async_copy(ybufs[s].at[pl.ds(r * ROW_CHUNKS, ROW_CHUNKS)], row_tile(ys_ref, dst_ref[0, 0, r]),
                                  ssem.at[s]).start(priority=r % 2)

    def wait_scatter(s):
        pltpu.make_async_copy(ybufs[s], ys_ref.at[pl.ds(0, blk)], ssem.at[s]).wait()

    def chunk_rows(c):
        return pl.ds(c, tmb, stride=ROW_CHUNKS)

    def on_phase(cond, fn):
        for q in range(ROW_BUFS):
            pl.when(jnp.logical_and(cond, b % ROW_BUFS == q))(functools.partial(fn, q))

    @pl.when(jnp.logical_and(active, b == 0))
    def _():
        start_gather(src0_ref, 0)
        start_gather(src1_ref, 1)

    @pl.when(jnp.logical_and(active, fresh))
    def _():
        even = lax.broadcasted_iota(jnp.int32, (PREP_ROWS, LANES), 1) % 2 == 0

        def split_rows(r, carry):
            rows = pl.ds(pl.multiple_of(r * PREP_ROWS, PREP_ROWS), PREP_ROWS)
            for c in range(D_EXPERT // LANES):
                blk_a = win_ref[rows, 2 * c * LANES:(2 * c + 1) * LANES]
                blk_b = win_ref[rows, (2 * c + 1) * LANES:(2 * c + 2) * LANES]
                wg_s[rows, c * LANES:(c + 1) * LANES] = jnp.where(
                    even, blk_a, pltpu.roll(blk_b, 1, 1)).astype(BF16)
                wl_s[rows, c * LANES:(c + 1) * LANES] = jnp.where(
                    even, pltpu.roll(blk_a, LANES - 1, 1), blk_b).astype(BF16)
            return carry

        lax.fori_loop(0, D_MODEL // PREP_ROWS, split_rows, 0)
        half = LANES // 2
        for c in range(D_EXPERT // LANES):
            for p in range(2):
                for cb in range(D_MODEL // LANES):
                    wo_t[cb, pl.ds(c * LANES + p, half, stride=2), :] = (
                        wout_ref[pl.ds(c * LANES + half * p, half), cb * LANES:(cb + 1) * LANES])

        def cast_rows(r, carry):
            rows = pl.ds(pl.multiple_of(r * PREP_ROWS, PREP_ROWS), PREP_ROWS)
            for cb in range(D_MODEL // LANES):
                wo_s[rows, cb * LANES:(cb + 1) * LANES] = wo_t[cb, rows, :].astype(BF16)
            return carry

        lax.fori_loop(0, D_EXPERT // PREP_ROWS, cast_rows, 0)

    on_phase(active, wait_gather)
    on_phase(jnp.logical_and(active, b >= ROW_BUFS), wait_scatter)

    def compute(p):
        x = jnp.concatenate([xbufs[p][chunk_rows(c), :] for c in range(ROW_CHUNKS)], axis=1).astype(BF16)
        zg = _bdot(x, wg_s[...]) + bg_ref[...]
        zl = _bdot(x, wl_s[...]) + bl_ref[...]
        glu = jnp.minimum(zg, SWIGLU_LIMIT)
        lin = jnp.clip(zl, -SWIGLU_LIMIT, SWIGLU_LIMIT)
        a = glu * jax.nn.sigmoid(SWIGLU_ALPHA * glu) * (lin + 1.0)
        y = _bdot(a.astype(BF16), wo_s[...]) + bo_ref[...]
        for c in range(ROW_CHUNKS):
            ybufs[p][chunk_rows(c), :] = y[:, c * LANES:(c + 1) * LANES]

    @pl.when(jnp.logical_and(active, b == 0))
    def _():
        start_gather(src2_ref, 2)
        compute(0)

    def step(q):
        start_gather(src2_ref, (q + 2) % ROW_BUFS)
        start_scatter(dstp_ref, (q - 1) % ROW_BUFS)
        compute(q)

    on_phase(jnp.logical_and(active, b > 0), step)

    def drain(q):
        start_scatter(dstc_ref, q)
        wait_gather((q + 1) % ROW_BUFS)
        wait_gather((q + 2) % ROW_BUFS)
        wait_scatter(q)

    on_phase(last, drain)
    for back in range(1, ROW_BUFS):
        on_phase(jnp.logical_and(last, b >= back), lambda q, back=back: wait_scatter((q - back) % ROW_BUFS))

    @pl.when(last)
    def _():
        spare0 = ys_ref.shape[0] - 2 * blk
        fills = []
        for s in range(2):
            ybufs[s][...] = jnp.zeros_like(ybufs[s])
            fills.append(pltpu.make_async_copy(ybufs[s], ys_ref.at[pl.ds(spare0 + s * blk, blk)], ssem.at[s]))
        for cp in fills:
            cp.start()
        for cp in fills:
            cp.wait()


def _experts(h3, src_rows, dst_rows, block_e, nact, n_slots, layer, w_in, b_glu, b_lin, w_out, b_out):
    tmb = EXPERT_ROWS
    n_blocks = src_rows.shape[0]
    exp3 = lambda b, be, na: (be[b], 0, 0)
    exp4 = lambda b, be, na: (layer, be[b], 0, 0)
    idx_spec = lambda f: pl.BlockSpec((1, 1, tmb), lambda b, be, na: (f(b), 0, 0), memory_space=pltpu.SMEM)
    grid_spec = pltpu.PrefetchScalarGridSpec(
        num_scalar_prefetch=2,
        grid=(n_blocks,),
        in_specs=[
            idx_spec(lambda b: b),
            idx_spec(lambda b: jnp.minimum(b + 1, n_blocks - 1)),
            idx_spec(lambda b: jnp.minimum(b + 2, n_blocks - 1)),
            idx_spec(lambda b: jnp.maximum(b - 1, 0)),
            idx_spec(lambda b: b),
            pl.BlockSpec(memory_space=pl.ANY),
            pl.BlockSpec((None, None, D_MODEL, 2 * D_EXPERT), exp4),
            pl.BlockSpec((None, 1, D_EXPERT), exp3),
            pl.BlockSpec((None, 1, D_EXPERT), exp3),
            pl.BlockSpec((None, None, D_EXPERT, D_MODEL), exp4),
            pl.BlockSpec((None, 1, D_MODEL), exp3),
        ],
        out_specs=pl.BlockSpec(memory_space=pl.ANY),
        scratch_shapes=[pltpu.VMEM((D_MODEL, D_EXPERT), BF16), pltpu.VMEM((D_MODEL, D_EXPERT), BF16),
                        pltpu.VMEM((D_MODEL // LANES, D_EXPERT, LANES), F32),
                        pltpu.VMEM((D_EXPERT, D_MODEL), BF16)]
        + [pltpu.VMEM((tmb * ROW_CHUNKS, LANES), F32)] * (2 * ROW_BUFS)
        + [pltpu.SemaphoreType.DMA((ROW_BUFS,)), pltpu.SemaphoreType.DMA((ROW_BUFS,))],
    )
    return pl.pallas_call(
        _expert_kernel,
        grid_spec=grid_spec,
        out_shape=jax.ShapeDtypeStruct((n_slots * ROW_CHUNKS, LANES), F32),
        compiler_params=_cparams(("arbitrary",)),
        name="moe_experts",
    )(block_e, nact, src_rows, src_rows, src_rows, dst_rows, dst_rows, h3, w_in, b_glu, b_lin, w_out, b_out)


def _moe(h_first, h_second, layer, w_router, b_router, w_in, b_in, w_out, b_out):
    tmb = EXPERT_ROWS
    top_e, gates, rank, counts, h3 = _router(h_first, h_second, w_router, b_router)
    n = h3.shape[0] // ROW_CHUNKS
    m = n * TOP_K
    padded = (counts + tmb - 1) // tmb * tmb
    pend = jnp.cumsum(padded)
    pstart = pend - padded
    hot = top_e[..., None] == jnp.arange(N_EXPERTS)[None, None, :]
    dest = jnp.sum(jnp.where(hot, pstart[None, None, :], 0), axis=-1) + rank
    n_blocks = m // tmb + N_EXPERTS
    blk_row0 = jnp.arange(n_blocks, dtype=jnp.int32) * tmb
    block_e = jnp.minimum(jnp.sum(pend[None, :] <= blk_row0[:, None], axis=1), N_EXPERTS - 1).astype(jnp.int32)
    nact = (pend[-1] // tmb).astype(jnp.int32).reshape(1)
    slot = jnp.arange(n, dtype=jnp.int32)[None, :] + n * jnp.arange(TOP_K, dtype=jnp.int32)[:, None]
    j = jnp.arange(N_EXPERTS * tmb, dtype=jnp.int32)
    off = j % tmb
    pad = padded - counts
    rep = lambda v: jnp.repeat(v, tmb)
    pad_key = jnp.where(off < rep(pad), rep(pstart + counts) + off, pend[-1] + j - rep(jnp.cumsum(pad)))
    keys = jnp.concatenate([dest.reshape(m), pad_key]).astype(jnp.int32)
    vals = jnp.concatenate([slot.reshape(m), jnp.full_like(j, -1)])
    _, row_slot = lax.sort((keys, vals), num_keys=1)
    row = jnp.arange(n_blocks * tmb, dtype=jnp.int32)
    src_rows = jnp.where(row_slot < 0, 0, row_slot % n)
    dst_rows = jnp.where(row_slot < 0, m + ((row // tmb) % 2) * tmb + row % tmb, row_slot)
    b_glu = _pair_perm(b_in[:, 0::2]).reshape(N_EXPERTS, 1, D_EXPERT)
    b_lin = _pair_perm(b_in[:, 1::2]).reshape(N_EXPERTS, 1, D_EXPERT)
    ys = _experts(h3, src_rows.reshape(n_blocks, 1, tmb), dst_rows.reshape(n_blocks, 1, tmb), block_e, nact,
                  m + 2 * tmb, layer, w_in, b_glu, b_lin, w_out, b_out.reshape(N_EXPERTS, 1, D_MODEL))
    return ys, gates.T


def _post_kernel(x1_ref, *rest, with_kv):
    ys_refs = rest[:TOP_K]
    gate_ref, mod_ref, lng_ref, lnb_ref = rest[TOP_K:TOP_K + 4]
    rest = rest[TOP_K + 4:]
    g2 = mod_ref[5:6, :]
    tm = x1_ref.shape[0]
    f = None
    for k in range(TOP_K):
        yk = jnp.concatenate([ys_refs[k][pl.ds(c, tm, stride=ROW_CHUNKS), :] for c in range(ROW_CHUNKS)], axis=1)
        term = yk * gate_ref[:, k:k + 1]
        f = term if f is None else f + term
    x2 = _layer_norm(DN_ALPHA * x1_ref[...] + g2 * f, lng_ref[...], lnb_ref[...])
    if not with_kv:
        (x2_ref,) = rest
        x2_ref[...] = x2
        return
    wa_ref, kng_ref, cos_ref, sin_ref, x2_ref, lat_ref, kr_ref, kcat_ref = rest
    x2_ref[...] = x2
    kv = _bdot(x2.astype(BF16), wa_ref[...])
    c = kv[:, :KV_RANK]
    lat = c * lax.rsqrt(jnp.mean(c * c, axis=-1, keepdims=True) + RMS_EPS) * kng_ref[...]
    k = kv[:, KV_RANK:KV_RANK + QK_ROPE]
    k_swapped = kv[:, KV_RANK + QK_ROPE:]
    kr = k * cos_ref[...] + k_swapped * sin_ref[...]
    lat_ref[...] = lat
    kr_ref[...] = kr
    kcat_ref[:, :KV_RANK] = lat.astype(BF16)
    kcat_ref[:, KV_RANK:] = kr.astype(BF16)


def _swap_halves_cols(w, width):
    shp = w.shape
    w4 = w.reshape(shp[:-1] + (shp[-1] // width, 2, width // 2))
    return w4[..., ::-1, :].reshape(shp)


def _rope_tables(pos, reps):
    inv = 1.0 / (ROPE_BASE ** (jnp.arange(ROPE_HALF, dtype=F32) * (2.0 / QK_ROPE)))
    ang = pos.astype(F32)[:, None] * inv[None, :]
    cos, sin = jnp.cos(ang), jnp.sin(ang)
    cos_t = jnp.tile(jnp.concatenate([cos, cos], -1), (1, reps))
    sin_t = jnp.tile(jnp.concatenate([-sin, sin], -1), (1, reps))
    return cos_t, sin_t


def _post_layer(x1, ys, gates, row0, mod, ln_g, ln_b, *, tm, kv=None):
    bsz, s, d = x1.shape
    blk0 = row0 // tm
    per_b = s // tm
    tile = lambda w: pl.BlockSpec((None, tm, w), lambda b, i: (b, i, 0))
    n_tok = gates.shape[0]
    pick_spec = lambda k: pl.BlockSpec((tm * ROW_CHUNKS, LANES),
                                       lambda b, i: (k * (n_tok // tm) + blk0 + b * per_b + i, 0))
    in_specs = [tile(d)] + [pick_spec(k) for k in range(TOP_K)] + [
        pl.BlockSpec((tm, TOP_K), lambda b, i: (blk0 + b * per_b + i, 0)),
        pl.BlockSpec((None, 6, d), lambda b, i: (b, 0, 0)),
        _const_spec((1, d)),
        _const_spec((1, d)),
    ]
    args = [x1] + [ys] * TOP_K + [gates, mod, ln_g.reshape(1, -1), ln_b.reshape(1, -1)]
    out_shape = [jax.ShapeDtypeStruct((bsz, s, d), F32)]
    out_specs = [tile(d)]
    if kv is not None:
        w_a, kn_g, cos_t, sin_t = kv
        w_ext = jnp.concatenate([w_a, _swap_halves_cols(w_a[:, KV_RANK:], QK_ROPE)], axis=1).astype(BF16)
        in_specs += [_const_spec(w_ext.shape), _const_spec((1, KV_RANK)),
                     pl.BlockSpec((tm, QK_ROPE), lambda b, i: (i, 0)),
                     pl.BlockSpec((tm, QK_ROPE), lambda b, i: (i, 0))]
        args += [w_ext, kn_g.reshape(1, -1), cos_t, sin_t]
        out_shape += [jax.ShapeDtypeStruct((bsz, s, KV_RANK), F32), jax.ShapeDtypeStruct((bsz, s, QK_ROPE), F32),
                      jax.ShapeDtypeStruct((bsz, s, QK_LAT), BF16)]
        out_specs += [tile(KV_RANK), tile(QK_ROPE), tile(QK_LAT)]
    return pl.pallas_call(
        functools.partial(_post_kernel, with_kv=kv is not None),
        grid=(bsz, per_b),
        in_specs=in_specs,
        out_specs=out_specs,
        out_shape=out_shape,
        compiler_params=_cparams(("arbitrary", "arbitrary")),
        name="post_moe",
    )(*args)


Q_GROUP = 512
KEY_TILE = 512


def _qproj_kernel(x_ref, mod_ref, wdq_ref, qng_ref, wqn_ref, wqr_ref, wqrs_ref, wuk_ref, cos_ref, sin_ref, q_ref,
                  *, tq):
    x = x_ref[...]
    sh1, sc1 = mod_ref[0:1, :], mod_ref[1:2, :]
    h = (x * (1.0 + sc1) + sh1).astype(BF16)
    tm = x.shape[0]
    cq = lax.dot_general(wdq_ref[...], h, (((1,), (1,)), ((), ())), preferred_element_type=F32)
    cq = (cq * lax.rsqrt(jnp.mean(cq * cq, axis=0, keepdims=True) + RMS_EPS) * qng_ref[...]).astype(BF16)
    qn = _bdot(wqn_ref[...], cq)
    qr = _bdot(wqr_ref[...], cq)
    qrs = _bdot(wqrs_ref[...], cq)
    rope = (qr * cos_ref[...] + qrs * sin_ref[...]).astype(BF16)
    per_group = Q_GROUP // tq
    for hd in range(MLA_HEADS):
        ql = _bdot(wuk_ref[hd], qn[hd * QK_NOPE:(hd + 1) * QK_NOPE, :].astype(BF16)).astype(BF16)
        g, off = hd // per_group, (hd % per_group) * tq
        for t in range(tm // tq):
            q_ref[t, g, 0:KV_RANK, off:off + tq] = ql[:, t * tq:(t + 1) * tq]
            q_ref[t, g, KV_RANK:QK_LAT, off:off + tq] = rope[hd * QK_ROPE:(hd + 1) * QK_ROPE, t * tq:(t + 1) * tq]


def _qproj_layer(x, mod, w_dq, qn_g, w_qb, w_uk, cos_t, sin_t, *, tm, tq):
    bsz, s, d = x.shape
    w_qn = w_qb[:, :, :QK_NOPE].reshape(Q_RANK, MLA_HEADS * QK_NOPE)
    w_qr = w_qb[:, :, QK_NOPE:].reshape(Q_RANK, MLA_HEADS * QK_ROPE)
    w_qrs = _swap_halves_cols(w_qr, QK_ROPE)
    wuk_h = jnp.transpose(w_uk, (1, 0, 2)).astype(BF16)
    hr = MLA_HEADS * QK_ROPE
    n_groups = MLA_HEADS * tq // Q_GROUP
    nt = tm // tq
    return pl.pallas_call(
        functools.partial(_qproj_kernel, tq=tq),
        grid=(bsz, s // tm),
        in_specs=[
            pl.BlockSpec((None, tm, d), lambda b, i: (b, i, 0)),
            pl.BlockSpec((None, 6, d), lambda b, i: (b, 0, 0)),
            _const_spec((Q_RANK, d)),
            _const_spec((Q_RANK, 1)),
            _const_spec((MLA_HEADS * QK_NOPE, Q_RANK)),
            _const_spec((hr, Q_RANK)),
            _const_spec((hr, Q_RANK)),
            _const_spec(wuk_h.shape),
            pl.BlockSpec((hr, tm), lambda b, i: (0, i)),
            pl.BlockSpec((hr, tm), lambda b, i: (0, i)),
        ],
        out_specs=pl.BlockSpec((None, nt, n_groups, QK_LAT, Q_GROUP), lambda b, i: (b, i, 0, 0, 0)),
        out_shape=jax.ShapeDtypeStruct((bsz, s // tq, n_groups, QK_LAT, Q_GROUP), BF16),
        compiler_params=_cparams(("arbitrary", "arbitrary")),
        name="mla_qproj",
    )(x, mod, w_dq.T.astype(BF16), qn_g.reshape(-1, 1), w_qn.T.astype(BF16), w_qr.T.astype(BF16),
      w_qrs.T.astype(BF16), wuk_h, cos_t.T, sin_t.T)


EXP2_SCALE = ATTN_SCALE * 1.4426950408889634


def _attn_kernel(q_ref, *rest, tq, causal, n_main, tail, split_keys):
    if split_keys:
        klat_ref, kkr_ref = rest[:2]
        rest = rest[2:]
    else:
        k_ref = rest[0]
        rest = rest[1:]
    if tail:
        kt_ref = rest[0]
        rest = rest[1:]
    x_ref, mod_ref, wuv_ref, wo_ref, lng_ref, lnb_ref, x1_ref, h2_ref, m_s, l_s, acc_s, ta_s, tb_s = rest
    i = pl.program_id(1)
    n_groups = q_ref.shape[0]
    per_group = Q_GROUP // tq
    m_s[...] = jnp.full_like(m_s, NEG_BIG)
    l_s[...] = jnp.zeros_like(l_s)
    acc_s[...] = jnp.zeros_like(acc_s)

    def tile_rows(j):
        return pl.ds(pl.multiple_of(j * KEY_TILE, KEY_TILE), KEY_TILE)

    def key_rows(j):
        if split_keys:
            return jnp.concatenate([klat_ref[tile_rows(j), :].astype(BF16), kkr_ref[tile_rows(j), :].astype(BF16)],
                                   axis=1)
        return k_ref[tile_rows(j), :]

    def value_rows(j):
        if split_keys:
            return klat_ref[tile_rows(j), :].astype(BF16)
        return k_ref[tile_rows(j), :KV_RANK]

    def softmax_step(g, t, v, key0, masked):
        t = t * EXP2_SCALE
        if masked:
            n = t.shape[0]
            k_chunk = (key0 + lax.broadcasted_iota(jnp.int32, (n, Q_GROUP), 0)) // CHUNK
            q_chunk = (i * tq + lax.broadcasted_iota(jnp.int32, (n, Q_GROUP), 1) % tq) // CHUNK
            t = jnp.where(k_chunk <= q_chunk, t, -jnp.inf)
        m_old = m_s[g]
        m_new = jnp.maximum(m_old, jnp.max(t, axis=0, keepdims=True))
        alpha = jnp.exp2(m_old - m_new)
        p = jnp.exp2(t - m_new)
        l_s[g] = alpha * l_s[g] + jnp.sum(p, axis=0, keepdims=True)
        acc_s[g] = alpha * acc_s[g] + lax.dot_general(
            v, p.astype(BF16), (((0,), (0,)), ((), ())), preferred_element_type=F32)
        m_s[g] = m_new

    def stage(cur, nxt, j, masked, make_next):
        v = value_rows(j)
        k_next = key_rows(j + 1) if make_next else None
        for g in range(n_groups):
            if make_next:
                nxt[g] = _bdot(k_next, q_ref[g])
            softmax_step(g, cur[g], v, j * KEY_TILE, masked)

    n_open = (i * tq) // KEY_TILE if causal else n_main - 1
    k0 = key_rows(0)
    for g in range(n_groups):
        ta_s[g] = _bdot(k0, q_ref[g])

    def pair_body(p, c):
        stage(ta_s, tb_s, 2 * p, False, True)
        stage(tb_s, ta_s, 2 * p + 1, False, True)
        return c

    lax.fori_loop(0, n_open // 2, pair_body, 0)
    if causal:
        @pl.when(n_open % 2 == 0)
        def _():
            stage(ta_s, tb_s, n_open, True, False)

        @pl.when(n_open % 2 == 1)
        def _():
            stage(ta_s, tb_s, n_open - 1, False, True)
            stage(tb_s, ta_s, n_open, True, False)
    elif n_open % 2 == 0:
        stage(ta_s, tb_s, n_open, False, False)
    else:
        stage(ta_s, tb_s, n_open - 1, False, True)
        stage(tb_s, ta_s, n_open, False, False)
    if tail:
        kt = kt_ref[...]
        for g in range(n_groups):
            softmax_step(g, _bdot(kt, q_ref[g]), kt[:, :KV_RANK], n_main * KEY_TILE, False)

    heads = []
    for g in range(n_groups):
        o = (acc_s[g] / l_s[g]).T
        for hh in range(per_group):
            hd = g * per_group + hh
            heads.append(_bdot(o[hh * tq:(hh + 1) * tq].astype(BF16), wuv_ref[hd]))
    oc = jnp.concatenate(heads, axis=-1).astype(BF16)
    y = _bdot(oc, wo_ref[...])
    x = x_ref[...]
    g1, sh2, sc2 = mod_ref[2:3, :], mod_ref[3:4, :], mod_ref[4:5, :]
    x1 = _layer_norm(DN_ALPHA * x + g1 * y, lng_ref[...], lnb_ref[...])
    x1_ref[...] = x1
    h2_ref[...] = x1 * (1.0 + sc2) + sh2


def _attn_layer(q, keys, x, mod, w_uv, w_o, ln_g, ln_b, *, tq, causal, k_tail=None):
    bsz, s, d = x.shape
    split_keys = isinstance(keys, tuple)
    keys = keys if split_keys else (keys,)
    t_main = keys[0].shape[1]
    n_main = t_main // KEY_TILE
    n_groups = q.shape[2]
    tail = k_tail is not None
    wuv_t = jnp.transpose(w_uv, (1, 0, 2)).astype(BF16)
    tile = pl.BlockSpec((None, tq, d), lambda b, i: (b, i, 0))
    in_specs = [pl.BlockSpec((None, None, n_groups, QK_LAT, Q_GROUP), lambda b, i: (b, i, 0, 0, 0))]
    in_specs += [pl.BlockSpec((None, t_main, k.shape[2]), lambda b, i: (b, 0, 0)) for k in keys]
    args = [q, *keys]
    if tail:
        in_specs.append(pl.BlockSpec((None, k_tail.shape[1], QK_LAT), lambda b, i: (b, 0, 0)))
        args.append(k_tail)
    in_specs += [
        tile,
        pl.BlockSpec((None, 6, d), lambda b, i: (b, 0, 0)),
        _const_spec(wuv_t.shape),
        _const_spec((MLA_HEADS * V_DIM, d)),
        _const_spec((1, d)),
        _const_spec((1, d)),
    ]
    args += [x, mod, wuv_t, w_o.astype(BF16), ln_g.reshape(1, -1), ln_b.reshape(1, -1)]
    return pl.pallas_call(
        functools.partial(_attn_kernel, tq=tq, causal=causal, n_main=n_main, tail=tail, split_keys=split_keys),
        grid=(bsz, s // tq),
        in_specs=in_specs,
        out_specs=[tile, tile],
        out_shape=[jax.ShapeDtypeStruct((bsz, s, d), F32), jax.ShapeDtypeStruct((bsz, s, d), F32)],
        scratch_shapes=[pltpu.VMEM((n_groups, 1, Q_GROUP), F32), pltpu.VMEM((n_groups, 1, Q_GROUP), F32),
                        pltpu.VMEM((n_groups, KV_RANK, Q_GROUP), F32),
                        pltpu.VMEM((n_groups, KEY_TILE, Q_GROUP), F32),
                        pltpu.VMEM((n_groups, KEY_TILE, Q_GROUP), F32)],
        compiler_params=_cparams(("arbitrary", "arbitrary")),
        name="mla_attention",
    )(*args)


def kernel(x_prompt, x_sample, cache_kv_latent, cache_k_rope, c_prompt, c_sample, ada_w, ada_b, ln_g, ln_b,
           a_w_in, a_b_in, a_vn_g, a_vn_b, a_w_s, a_b_s, a_w_out, a_b_out, kv_w_a, kv_norm_g, kv_w_uk, kv_w_uv,
           q_w_a, q_norm_g, q_w_b, q_w_o, moe_w_router, moe_b_router, moe_w_in, moe_b_in, moe_w_out, moe_b_out):
    bp, sp, d = x_prompt.shape
    bs, ss, _ = x_sample.shape
    past = cache_kv_latent.shape[1]
    n_p = bp * sp
    n_s = bs * ss

    mod = _ada_mod(jnp.concatenate([c_prompt, c_sample], axis=0), ada_w, ada_b)
    mod = mod.reshape(DEPTH, bp + bs, 6, d)
    mod_p, mod_s = mod[:, :bp], mod[:, bp:]

    pos_p = jnp.arange(sp, dtype=jnp.int32)
    pos_s = past + jnp.arange(ss, dtype=jnp.int32)

    def moe(l, h2_p, h2_s):
        return _moe(h2_p.reshape(n_p, d), h2_s.reshape(n_s, d), l, moe_w_router[l], moe_b_router[l],
                    moe_w_in, moe_b_in[l], moe_w_out, moe_b_out[l])

    gm = functools.partial(_gmlp_layer, w_in=a_w_in[0], b_in=a_b_in[0], vn_g=a_vn_g[0], vn_b=a_vn_b[0],
                           w_s=a_w_s[0], b_s=a_b_s[0], w_out=a_w_out[0], b_out=a_b_out[0],
                           ln_g=ln_g[0, 0], ln_b=ln_b[0, 0])
    x1_p, h2_p = gm(x_prompt, mod_p[0], tm=256, write_v=False)
    x1_s, h2_s, v_s = gm(x_sample, mod_s[0], tm=ss, write_v=True)
    ys, gates = moe(0, h2_p, h2_s)
    x2_p, lat_p, kr_p, kcat_p = _post_layer(x1_p, ys, gates, 0, mod_p[0], ln_g[0, 1], ln_b[0, 1], tm=512,
                                            kv=(kv_w_a, kv_norm_g) + _rope_tables(pos_p, 1))
    x2_s, lat_s, kr_s, kcat_s = _post_layer(x1_s, ys, gates, n_p, mod_s[0], ln_g[0, 1], ln_b[0, 1], tm=ss,
                                            kv=(kv_w_a, kv_norm_g) + _rope_tables(pos_s, 1))

    qp = functools.partial(_qproj_layer, w_dq=q_w_a[0], qn_g=q_norm_g[0], w_qb=q_w_b[0], w_uk=kv_w_uk)
    at = functools.partial(_attn_layer, w_uv=kv_w_uv, w_o=q_w_o[0], ln_g=ln_g[1, 0], ln_b=ln_b[1, 0])
    cos_p, sin_p = _rope_tables(pos_p, MLA_HEADS)
    cos_s, sin_s = _rope_tables(pos_s, MLA_HEADS)
    q_p = qp(x2_p, mod_p[1], cos_t=cos_p, sin_t=sin_p, tm=256, tq=128)
    q_s = qp(x2_s, mod_s[1], cos_t=cos_s, sin_t=sin_s, tm=ss, tq=ss)
    x3_p, h4_p = at(q_p, kcat_p, x2_p, mod_p[1], tq=128, causal=True)
    x3_s, h4_s = at(q_s, (cache_kv_latent, cache_k_rope), x2_s, mod_s[1], tq=ss, causal=False, k_tail=kcat_s)
    ys, gates = moe(1, h4_p, h4_s)
    (y_p,) = _post_layer(x3_p, ys, gates, 0, mod_p[1], ln_g[1, 1], ln_b[1, 1], tm=512)
    (y_s,) = _post_layer(x3_s, ys, gates, n_p, mod_s[1], ln_g[1, 1], ln_b[1, 1], tm=ss)

    return (y_p, y_s, lat_p, kr_p, lat_s, kr_s, v_s[None])
```

```python
import functools

import jax
import jax.numpy as jnp
from jax import lax
from jax.experimental import pallas as pl
from jax.experimental.pallas import tpu as pltpu

D_MODEL = 1024
DEPTH = 2
CHUNK = 64
A_CHUNK = 128
A_HALF = 2 * D_MODEL
A_GROUPS = 8
A_GROUP_W = A_HALF // A_GROUPS
MLA_HEADS = 8
QK_NOPE = 128
QK_ROPE = 64
ROPE_HALF = QK_ROPE // 2
V_DIM = 128
KV_RANK = 256
Q_RANK = 512
QK_LAT = KV_RANK + QK_ROPE
ROPE_BASE = 10000.0
ATTN_SCALE = (QK_NOPE + QK_ROPE) ** -0.5
N_EXPERTS = 32
TOP_K = 4
D_EXPERT = D_MODEL
SWIGLU_LIMIT = 7.0
SWIGLU_ALPHA = 1.702
DN_ALPHA = (2 * DEPTH) ** 0.25
LN_EPS = 1e-5
RMS_EPS = 1e-6

BF16 = jnp.bfloat16
F32 = jnp.float32

VMEM_LIMIT = 56 * 1024 * 1024
EXPERT_ROWS = 256
ROUTER_ROWS = 512
NEG_BIG = -1e30


def _cparams(sem):
    return pltpu.CompilerParams(dimension_semantics=sem, vmem_limit_bytes=VMEM_LIMIT)


def _const_spec(shape):
    nd = len(shape)
    return pl.BlockSpec(shape, lambda *_: (0,) * nd, pipeline_mode=pl.Buffered(1))


def _layer_norm(r, g, b):
    rc = r - jnp.mean(r, axis=-1, keepdims=True)
    var = jnp.mean(rc * rc, axis=-1, keepdims=True)
    return rc * lax.rsqrt(var + LN_EPS) * g + b


def _gelu_tanh(x):
    c = 0.7978845608028654
    return 0.5 * x * (1.0 + jnp.tanh(c * (x + 0.044715 * (x * x * x))))


def _bdot(a, b):
    return jnp.dot(a, b, preferred_element_type=F32)


def _ada_kernel(c_ref, w_ref, b_ref, o_ref):
    c = c_ref[...]
    s = c * jax.nn.sigmoid(c)
    o_ref[...] = jnp.dot(s, w_ref[...], preferred_element_type=F32,
                         precision=lax.Precision.HIGHEST) + b_ref[...]


def _ada_mod(c_all, ada_w, ada_b):
    nb = c_all.shape[0]
    six_d = ada_w.shape[-1]
    tn = D_MODEL
    return pl.pallas_call(
        _ada_kernel,
        grid=(DEPTH, six_d // tn),
        in_specs=[
            pl.BlockSpec((nb, D_MODEL), lambda l, j: (0, 0)),
            pl.BlockSpec((None, D_MODEL, tn), lambda l, j: (l, 0, j)),
            pl.BlockSpec((None, 1, tn), lambda l, j: (l, 0, j)),
        ],
        out_specs=pl.BlockSpec((None, nb, tn), lambda l, j: (l, 0, j)),
        out_shape=jax.ShapeDtypeStruct((DEPTH, nb, six_d), F32),
        compiler_params=_cparams(("arbitrary", "arbitrary")),
        name="ada_mod",
    )(c_all, ada_w, ada_b.reshape(DEPTH, 1, six_d))


def _gmlp_kernel(x_ref, mod_ref, w_in_ref, b_in_ref, vng_ref, vnb_ref, ws_ref, bs_ref,
                 w_out_ref, b_out_ref, lng_ref, lnb_ref, *rest, write_v):
    if write_v:
        x1_ref, h2_ref, v_ref, u_s, v_s, p_s = rest
    else:
        x1_ref, h2_ref, u_s, v_s, p_s = rest
    x = x_ref[...]
    sh1, sc1, g1 = mod_ref[0:1, :], mod_ref[1:2, :], mod_ref[2:3, :]
    sh2, sc2 = mod_ref[3:4, :], mod_ref[4:5, :]
    h = (x * (1.0 + sc1) + sh1).astype(BF16)
    tm = x.shape[0]
    ch = 512
    n_ch = A_HALF // ch
    for j in range(n_ch):
        sl = slice(j * ch, (j + 1) * ch)
        u_s[:, sl] = _gelu_tanh(_bdot(h, w_in_ref[:, sl]) + b_in_ref[:, sl])
    tot = jnp.zeros((tm, 1), F32)
    for j in range(n_ch):
        sl = slice(j * ch, (j + 1) * ch)
        slw = slice(A_HALF + j * ch, A_HALF + (j + 1) * ch)
        g = _gelu_tanh(_bdot(h, w_in_ref[:, slw]) + b_in_ref[:, slw])
        v_s[:, sl] = g
        tot = tot + jnp.sum(g, axis=-1, keepdims=True)
    mean = tot * (1.0 / A_HALF)
    sq = jnp.zeros((tm, 1), F32)
    for j in range(n_ch):
        sl = slice(j * ch, (j + 1) * ch)
        c = v_s[:, sl] - mean
        sq = sq + jnp.sum(c * c, axis=-1, keepdims=True)
    rstd = lax.rsqrt(sq * (1.0 / A_HALF) + LN_EPS)
    for g in range(A_GROUPS):
        sl = slice(g * A_GROUP_W, (g + 1) * A_GROUP_W)
        vn = (v_s[:, sl] - mean) * rstd * vng_ref[:, sl] + vnb_ref[:, sl]
        if write_v:
            v_ref[:, sl] = vn
        sg = _bdot(ws_ref[g], vn.astype(BF16)) + bs_ref[:, g:g + 1]
        p_s[:, sl] = (u_s[:, sl] * sg).astype(BF16)
    y = _bdot(p_s[...], w_out_ref[...]) + b_out_ref[...]
    x1 = _layer_norm(DN_ALPHA * x + g1 * y, lng_ref[...], lnb_ref[...])
    x1_ref[...] = x1
    h2_ref[...] = x1 * (1.0 + sc2) + sh2


def _gmlp_layer(x, mod, w_in, b_in, vn_g, vn_b, w_s, b_s, w_out, b_out, ln_g, ln_b, *, tm, write_v):
    bsz, s, d = x.shape
    seg = min(s, A_CHUNK)
    idx = jnp.arange(seg)
    mask = (idx[:, None] // CHUNK) >= (idx[None, :] // CHUNK)
    wm = jnp.where(mask[None], w_s[:, :seg, :seg], 0.0)
    reps = tm // seg
    eye = jnp.eye(reps, dtype=F32)
    ws_big = jnp.einsum("ab,gij->gaibj", eye, wm).reshape(A_GROUPS, tm, tm).astype(BF16)
    bs_big = jnp.tile(b_s[:, :seg].T, (reps, 1))
    out_shape = [jax.ShapeDtypeStruct((bsz, s, d), F32), jax.ShapeDtypeStruct((bsz, s, d), F32)]
    tile = lambda w: pl.BlockSpec((None, tm, w), lambda b, i: (b, i, 0))
    out_specs = [tile(d), tile(d)]
    if write_v:
        out_shape.append(jax.ShapeDtypeStruct((bsz, s, A_HALF), F32))
        out_specs.append(tile(A_HALF))
    return pl.pallas_call(
        functools.partial(_gmlp_kernel, write_v=write_v),
        grid=(bsz, s // tm),
        in_specs=[
            tile(d),
            pl.BlockSpec((None, 6, d), lambda b, i: (b, 0, 0)),
            _const_spec((d, 2 * A_HALF)),
            _const_spec((1, 2 * A_HALF)),
            _const_spec((1, A_HALF)),
            _const_spec((1, A_HALF)),
            _const_spec((A_GROUPS, tm, tm)),
            _const_spec((tm, A_GROUPS)),
            _const_spec((A_HALF, d)),
            _const_spec((1, d)),
            _const_spec((1, d)),
            _const_spec((1, d)),
        ],
        out_specs=out_specs,
        out_shape=out_shape,
        scratch_shapes=[pltpu.VMEM((tm, A_HALF), F32), pltpu.VMEM((tm, A_HALF), F32),
                        pltpu.VMEM((tm, A_HALF), BF16)],
        compiler_params=_cparams(("arbitrary", "arbitrary")),
        name="gmlp_layer",
    )(x, mod, w_in.astype(BF16), b_in.reshape(1, -1), vn_g.reshape(1, -1), vn_b.reshape(1, -1),
      ws_big, bs_big, w_out.astype(BF16), b_out.reshape(1, -1), ln_g.reshape(1, -1), ln_b.reshape(1, -1))


ROW_CHUNKS = D_MODEL // 128


def _router_kernel(hp_ref, hs_ref, wr_ref, br_ref, tri_ref, e_ref, g_ref, r_ref, cnt_ref, h3_ref, base_s,
                   *, n_first):
    i = pl.program_id(0)

    @pl.when(i == 0)
    def _():
        base_s[...] = jnp.zeros_like(base_s)

    h = jnp.where(i < n_first, hp_ref[...], hs_ref[...])
    rows = h.shape[0]
    for c in range(ROW_CHUNKS):
        h3_ref[pl.ds(c, rows, stride=ROW_CHUNKS), :] = h[:, c * 128:(c + 1) * 128]
    logits = lax.dot_general(wr_ref[...], h, (((1,), (1,)), ((), ())),
                             preferred_element_type=F32,
                             precision=lax.Precision.HIGHEST) + br_ref[...]
    rows = h.shape[0]
    iota = lax.broadcasted_iota(jnp.int32, (N_EXPERTS, rows), 0)
    l = logits
    vals, hots = [], []
    for k in range(TOP_K):
        m = jnp.max(l, axis=0, keepdims=True)
        idx = jnp.min(jnp.where(l == m, iota, N_EXPERTS), axis=0, keepdims=True)
        hot = iota == idx
        vals.append(m)
        hots.append(hot)
        e_ref[k:k + 1, :] = idx
        l = jnp.where(hot, -jnp.inf, l)
    exps = [jnp.exp(v - vals[0]) for v in vals]
    den = exps[0] + exps[1] + exps[2] + exps[3]
    for k in range(TOP_K):
        g_ref[k:k + 1, :] = exps[k] / den
    hot_all = jnp.where(hots[0] | hots[1] | hots[2] | hots[3], 1.0, 0.0)
    before = _bdot(hot_all.astype(BF16), tri_ref[...]) + base_s[:, 0:1]
    for k in range(TOP_K):
        r_ref[k:k + 1, :] = jnp.sum(jnp.where(hots[k], before, 0.0), axis=0, keepdims=True).astype(jnp.int32)
    base_s[...] = base_s[...] + jnp.sum(hot_all, axis=1, keepdims=True)
    cnt_ref[...] = base_s[...].astype(jnp.int32)


def _router(h_first, h_second, w_router, b_router):
    tr = ROUTER_ROWS
    n_first = h_first.shape[0] // tr
    n_second = h_second.shape[0] // tr
    n = h_first.shape[0] + h_second.shape[0]
    tri = (jnp.arange(tr)[:, None] < jnp.arange(tr)[None, :]).astype(BF16)
    sel = pl.BlockSpec((TOP_K, tr), lambda i: (0, i))
    top_e, gates, rank, cnt, h3 = pl.pallas_call(
        functools.partial(_router_kernel, n_first=n_first),
        grid=(n_first + n_second,),
        in_specs=[
            pl.BlockSpec((tr, D_MODEL), lambda i: (jnp.minimum(i, n_first - 1), 0)),
            pl.BlockSpec((tr, D_MODEL), lambda i: (jnp.maximum(i - n_first, 0), 0)),
            _const_spec((N_EXPERTS, D_MODEL)),
            _const_spec((N_EXPERTS, 1)),
            _const_spec((tr, tr)),
        ],
        out_specs=[sel, sel, sel, pl.BlockSpec((N_EXPERTS, 128), lambda i: (0, 0)),
                   pl.BlockSpec((tr * ROW_CHUNKS, 128), lambda i: (i, 0))],
        out_shape=[jax.ShapeDtypeStruct((TOP_K, n), jnp.int32), jax.ShapeDtypeStruct((TOP_K, n), F32),
                   jax.ShapeDtypeStruct((TOP_K, n), jnp.int32),
                   jax.ShapeDtypeStruct((N_EXPERTS, 128), jnp.int32),
                   jax.ShapeDtypeStruct((n * ROW_CHUNKS, 128), F32)],
        scratch_shapes=[pltpu.VMEM((N_EXPERTS, 128), F32)],
        compiler_params=_cparams(("arbitrary",)),
        name="moe_router",
    )(h_first, h_second, w_router.T, b_router.reshape(N_EXPERTS, 1), tri)
    return top_e, gates, rank, cnt[:, 0], h3


LANES = 128
PREP_ROWS = 64


def _pair_perm(v):
    shp = v.shape
    v4 = v.reshape(shp[:-1] + (shp[-1] // LANES, 2, LANES // 2))
    return jnp.swapaxes(v4, -1, -2).reshape(shp)


ROW_BUFS = 4


def _expert_kernel(be_ref, nact_ref, src0_ref, src1_ref, src2_ref, dstp_ref, dstc_ref, h3_ref,
                   win_ref, bg_ref, bl_ref, wout_ref, bo_ref, ys_ref,
                   wg_s, wl_s, wo_t, wo_s, *bufs_and_sems):
    xbufs, ybufs = bufs_and_sems[:ROW_BUFS], bufs_and_sems[ROW_BUFS:2 * ROW_BUFS]
    gsem, ssem = bufs_and_sems[2 * ROW_BUFS:]
    b = pl.program_id(0)
    nact = nact_ref[0]
    active = b < nact
    last = b == nact - 1
    blk = xbufs[0].shape[0]
    tmb = blk // ROW_CHUNKS
    fresh = jnp.logical_or(b == 0, be_ref[b] != be_ref[jnp.maximum(b - 1, 0)])

    def row_tile(ref, row):
        return ref.at[pl.ds(pl.multiple_of(row * ROW_CHUNKS, ROW_CHUNKS), ROW_CHUNKS)]

    def start_gather(src_ref, s):
        for r in range(tmb):
            pltpu.make_async_copy(row_tile(h3_ref, src_ref[0, 0, r]), xbufs[s].at[pl.ds(r * ROW_CHUNKS, ROW_CHUNKS)],
                                  gsem.at[s]).start(priority=r % 2)

    def wait_gather(s):
        pltpu.make_async_copy(h3_ref.at[pl.ds(0, blk)], xbufs[s], gsem.at[s]).wait()

    def start_scatter(dst_ref, s):
        for r in range(tmb):
            pltpu.make_async_copy(ybufs[s].at[pl.ds(r * ROW_CHUNKS, ROW_CHUNKS)], row_tile(ys_ref, dst_ref[0, 0, r]),
                                  ssem.at[s]).start(priority=r % 2)

    def wait_scatter(s):
        pltpu.make_async_copy(ybufs[s], ys_ref.at[pl.ds(0, blk)], ssem.at[s]).wait()

    def chunk_rows(c):
        return pl.ds(c, tmb, stride=ROW_CHUNKS)

    def on_phase(cond, fn):
        for q in range(ROW_BUFS):
            pl.when(jnp.logical_and(cond, b % ROW_BUFS == q))(functools.partial(fn, q))

    @pl.when(jnp.logical_and(active, b == 0))
    def _():
        start_gather(src0_ref, 0)
        start_gather(src1_ref, 1)

    @pl.when(jnp.logical_and(active, fresh))
    def _():
        even = lax.broadcasted_iota(jnp.int32, (PREP_ROWS, LANES), 1) % 2 == 0

        def split_rows(r, carry):
            rows = pl.ds(pl.multiple_of(r * PREP_ROWS, PREP_ROWS), PREP_ROWS)
            for c in range(D_EXPERT // LANES):
                blk_a = win_ref[rows, 2 * c * LANES:(2 * c + 1) * LANES]
                blk_b = win_ref[rows, (2 * c + 1) * LANES:(2 * c + 2) * LANES]
                wg_s[rows, c * LANES:(c + 1) * LANES] = jnp.where(
                    even, blk_a, pltpu.roll(blk_b, 1, 1)).astype(BF16)
                wl_s[rows, c * LANES:(c + 1) * LANES] = jnp.where(
                    even, pltpu.roll(blk_a, LANES - 1, 1), blk_b).astype(BF16)
            return carry

        lax.fori_loop(0, D_MODEL // PREP_ROWS, split_rows, 0)
        half = LANES // 2
        for c in range(D_EXPERT // LANES):
            for p in range(2):
                for cb in range(D_MODEL // LANES):
                    wo_t[cb, pl.ds(c * LANES + p, half, stride=2), :] = (
                        wout_ref[pl.ds(c * LANES + half * p, half), cb * LANES:(cb + 1) * LANES])

        def cast_rows(r, carry):
            rows = pl.ds(pl.multiple_of(r * PREP_ROWS, PREP_ROWS), PREP_ROWS)
            for cb in range(D_MODEL // LANES):
                wo_s[rows, cb * LANES:(cb + 1) * LANES] = wo_t[cb, rows, :].astype(BF16)
            return carry

        lax.fori_loop(0, D_EXPERT // PREP_ROWS, cast_rows, 0)

    on_phase(active, wait_gather)
    on_phase(jnp.logical_and(active, b >= ROW_BUFS), wait_scatter)

    def compute(p):
        x = jnp.concatenate([xbufs[p][chunk_rows(c), :] for c in range(ROW_CHUNKS)], axis=1).astype(BF16)
        zg = _bdot(x, wg_s[...]) + bg_ref[...]
        zl = _bdot(x, wl_s[...]) + bl_ref[...]
        glu = jnp.minimum(zg, SWIGLU_LIMIT)
        lin = jnp.clip(zl, -SWIGLU_LIMIT, SWIGLU_LIMIT)
        a = glu * jax.nn.sigmoid(SWIGLU_ALPHA * glu) * (lin + 1.0)
        y = _bdot(a.astype(BF16), wo_s[...]) + bo_ref[...]
        for c in range(ROW_CHUNKS):
            ybufs[p][chunk_rows(c), :] = y[:, c * LANES:(c + 1) * LANES]

    @pl.when(jnp.logical_and(active, b == 0))
    def _():
        start_gather(src2_ref, 2)
        compute(0)

    def step(q):
        start_gather(src2_ref, (q + 2) % ROW_BUFS)
        start_scatter(dstp_ref, (q - 1) % ROW_BUFS)
        compute(q)

    on_phase(jnp.logical_and(active, b > 0), step)

    def drain(q):
        start_scatter(dstc_ref, q)
        wait_gather((q + 1) % ROW_BUFS)
        wait_gather((q + 2) % ROW_BUFS)
        wait_scatter(q)

    on_phase(last, drain)
    for back in range(1, ROW_BUFS):
        on_phase(jnp.logical_and(last, b >= back), lambda q, back=back: wait_scatter((q - back) % ROW_BUFS))

    @pl.when(last)
    def _():
        spare0 = ys_ref.shape[0] - 2 * blk
        fills = []
        for s in range(2):
            ybufs[s][...] = jnp.zeros_like(ybufs[s])
            fills.append(pltpu.make_async_copy(ybufs[s], ys_ref.at[pl.ds(spare0 + s * blk, blk)], ssem.at[s]))
        for cp in fills:
            cp.start()
        for cp in fills:
            cp.wait()


def _experts(h3, src_rows, dst_rows, block_e, nact, n_slots, layer, w_in, b_glu, b_lin, w_out, b_out):
    tmb = EXPERT_ROWS
    n_blocks = src_rows.shape[0]
    exp3 = lambda b, be, na: (be[b], 0, 0)
    exp4 = lambda b, be, na: (layer, be[b], 0, 0)
    idx_spec = lambda f: pl.BlockSpec((1, 1, tmb), lambda b, be, na: (f(b), 0, 0), memory_space=pltpu.SMEM)
    grid_spec = pltpu.PrefetchScalarGridSpec(
        num_scalar_prefetch=2,
        grid=(n_blocks,),
        in_specs=[
            idx_spec(lambda b: b),
            idx_spec(lambda b: jnp.minimum(b + 1, n_blocks - 1)),
            idx_spec(lambda b: jnp.minimum(b + 2, n_blocks - 1)),
            idx_spec(lambda b: jnp.maximum(b - 1, 0)),
            idx_spec(lambda b: b),
            pl.BlockSpec(memory_space=pl.ANY),
            pl.BlockSpec((None, None, D_MODEL, 2 * D_EXPERT), exp4),
            pl.BlockSpec((None, 1, D_EXPERT), exp3),
            pl.BlockSpec((None, 1, D_EXPERT), exp3),
            pl.BlockSpec((None, None, D_EXPERT, D_MODEL), exp4),
            pl.BlockSpec((None, 1, D_MODEL), exp3),
        ],
        out_specs=pl.BlockSpec(memory_space=pl.ANY),
        scratch_shapes=[pltpu.VMEM((D_MODEL, D_EXPERT), BF16), pltpu.VMEM((D_MODEL, D_EXPERT), BF16),
                        pltpu.VMEM((D_MODEL // LANES, D_EXPERT, LANES), F32),
                        pltpu.VMEM((D_EXPERT, D_MODEL), BF16)]
        + [pltpu.VMEM((tmb * ROW_CHUNKS, LANES), F32)] * (2 * ROW_BUFS)
        + [pltpu.SemaphoreType.DMA((ROW_BUFS,)), pltpu.SemaphoreType.DMA((ROW_BUFS,))],
    )
    return pl.pallas_call(
        _expert_kernel,
        grid_spec=grid_spec,
        out_shape=jax.ShapeDtypeStruct((n_slots * ROW_CHUNKS, LANES), F32),
        compiler_params=_cparams(("arbitrary",)),
        name="moe_experts",
    )(block_e, nact, src_rows, src_rows, src_rows, dst_rows, dst_rows, h3, w_in, b_glu, b_lin, w_out, b_out)


def _moe(h_first, h_second, layer, w_router, b_router, w_in, b_in, w_out, b_out):
    tmb = EXPERT_ROWS
    top_e, gates, rank, counts, h3 = _router(h_first, h_second, w_router, b_router)
    n = h3.shape[0] // ROW_CHUNKS
    m = n * TOP_K
    padded = (counts + tmb - 1) // tmb * tmb
    pend = jnp.cumsum(padded)
    pstart = pend - padded
    hot = top_e[..., None] == jnp.arange(N_EXPERTS)[None, None, :]
    dest = jnp.sum(jnp.where(hot, pstart[None, None, :], 0), axis=-1) + rank
    n_blocks = m // tmb + N_EXPERTS
    blk_row0 = jnp.arange(n_blocks, dtype=jnp.int32) * tmb
    block_e = jnp.minimum(jnp.sum(pend[None, :] <= blk_row0[:, None], axis=1), N_EXPERTS - 1).astype(jnp.int32)
    nact = (pend[-1] // tmb).astype(jnp.int32).reshape(1)
    slot = jnp.arange(n, dtype=jnp.int32)[None, :] + n * jnp.arange(TOP_K, dtype=jnp.int32)[:, None]
    j = jnp.arange(N_EXPERTS * tmb, dtype=jnp.int32)
    off = j % tmb
    pad = padded - counts
    rep = lambda v: jnp.repeat(v, tmb)
    pad_key = jnp.where(off < rep(pad), rep(pstart + counts) + off, pend[-1] + j - rep(jnp.cumsum(pad)))
    keys = jnp.concatenate([dest.reshape(m), pad_key]).astype(jnp.int32)
    vals = jnp.concatenate([slot.reshape(m), jnp.full_like(j, -1)])
    _, row_slot = lax.sort((keys, vals), num_keys=1)
    row = jnp.arange(n_blocks * tmb, dtype=jnp.int32)
    src_rows = jnp.where(row_slot < 0, 0, row_slot % n)
    dst_rows = jnp.where(row_slot < 0, m + ((row // tmb) % 2) * tmb + row % tmb, row_slot)
    b_glu = _pair_perm(b_in[:, 0::2]).reshape(N_EXPERTS, 1, D_EXPERT)
    b_lin = _pair_perm(b_in[:, 1::2]).reshape(N_EXPERTS, 1, D_EXPERT)
    ys = _experts(h3, src_rows.reshape(n_blocks, 1, tmb), dst_rows.reshape(n_blocks, 1, tmb), block_e, nact,
                  m + 2 * tmb, layer, w_in, b_glu, b_lin, w_out, b_out.reshape(N_EXPERTS, 1, D_MODEL))
    return ys, gates.T


def _post_kernel(x1_ref, *rest, with_kv):
    ys_refs = rest[:TOP_K]
    gate_ref, mod_ref, lng_ref, lnb_ref = rest[TOP_K:TOP_K + 4]
    rest = rest[TOP_K + 4:]
    g2 = mod_ref[5:6, :]
    tm = x1_ref.shape[0]
    f = None
    for k in range(TOP_K):
        yk = jnp.concatenate([ys_refs[k][pl.ds(c, tm, stride=ROW_CHUNKS), :] for c in range(ROW_CHUNKS)], axis=1)
        term = yk * gate_ref[:, k:k + 1]
        f = term if f is None else f + term
    x2 = _layer_norm(DN_ALPHA * x1_ref[...] + g2 * f, lng_ref[...], lnb_ref[...])
    if not with_kv:
        (x2_ref,) = rest
        x2_ref[...] = x2
        return
    wa_ref, kng_ref, cos_ref, sin_ref, x2_ref, lat_ref, kr_ref, kcat_ref = rest
    x2_ref[...] = x2
    kv = _bdot(x2.astype(BF16), wa_ref[...])
    c = kv[:, :KV_RANK]
    lat = c * lax.rsqrt(jnp.mean(c * c, axis=-1, keepdims=True) + RMS_EPS) * kng_ref[...]
    k = kv[:, KV_RANK:KV_RANK + QK_ROPE]
    k_swapped = kv[:, KV_RANK + QK_ROPE:]
    kr = k * cos_ref[...] + k_swapped * sin_ref[...]
    lat_ref[...] = lat
    kr_ref[...] = kr
    kcat_ref[:, :KV_RANK] = lat.astype(BF16)
    kcat_ref[:, KV_RANK:] = kr.astype(BF16)


def _swap_halves_cols(w, width):
    shp = w.shape
    w4 = w.reshape(shp[:-1] + (shp[-1] // width, 2, width // 2))
    return w4[..., ::-1, :].reshape(shp)


def _rope_tables(pos, reps):
    inv = 1.0 / (ROPE_BASE ** (jnp.arange(ROPE_HALF, dtype=F32) * (2.0 / QK_ROPE)))
    ang = pos.astype(F32)[:, None] * inv[None, :]
    cos, sin = jnp.cos(ang), jnp.sin(ang)
    cos_t = jnp.tile(jnp.concatenate([cos, cos], -1), (1, reps))
    sin_t = jnp.tile(jnp.concatenate([-sin, sin], -1), (1, reps))
    return cos_t, sin_t


def _post_layer(x1, ys, gates, row0, mod, ln_g, ln_b, *, tm, kv=None):
    bsz, s, d = x1.shape
    blk0 = row0 // tm
    per_b = s // tm
    tile = lambda w: pl.BlockSpec((None, tm, w), lambda b, i: (b, i, 0))
    n_tok = gates.shape[0]
    pick_spec = lambda k: pl.BlockSpec((tm * ROW_CHUNKS, LANES),
                                       lambda b, i: (k * (n_tok // tm) + blk0 + b * per_b + i, 0))
    in_specs = [tile(d)] + [pick_spec(k) for k in range(TOP_K)] + [
        pl.BlockSpec((tm, TOP_K), lambda b, i: (blk0 + b * per_b + i, 0)),
        pl.BlockSpec((None, 6, d), lambda b, i: (b, 0, 0)),
        _const_spec((1, d)),
        _const_spec((1, d)),
    ]
    args = [x1] + [ys] * TOP_K + [gates, mod, ln_g.reshape(1, -1), ln_b.reshape(1, -1)]
    out_shape = [jax.ShapeDtypeStruct((bsz, s, d), F32)]
    out_specs = [tile(d)]
    if kv is not None:
        w_a, kn_g, cos_t, sin_t = kv
        w_ext = jnp.concatenate([w_a, _swap_halves_cols(w_a[:, KV_RANK:], QK_ROPE)], axis=1).astype(BF16)
        in_specs += [_const_spec(w_ext.shape), _const_spec((1, KV_RANK)),
                     pl.BlockSpec((tm, QK_ROPE), lambda b, i: (i, 0)),
                     pl.BlockSpec((tm, QK_ROPE), lambda b, i: (i, 0))]
        args += [w_ext, kn_g.reshape(1, -1), cos_t, sin_t]
        out_shape += [jax.ShapeDtypeStruct((bsz, s, KV_RANK), F32), jax.ShapeDtypeStruct((bsz, s, QK_ROPE), F32),
                      jax.ShapeDtypeStruct((bsz, s, QK_LAT), BF16)]
        out_specs += [tile(KV_RANK), tile(QK_ROPE), tile(QK_LAT)]
    return pl.pallas_call(
        functools.partial(_post_kernel, with_kv=kv is not None),
        grid=(bsz, per_b),
        in_specs=in_specs,
        out_specs=out_specs,
        out_shape=out_shape,
        compiler_params=_cparams(("arbitrary", "arbitrary")),
        name="post_moe",
    )(*args)


Q_GROUP = 512
KEY_TILE = 512


def _qproj_kernel(x_ref, mod_ref, wdq_ref, qng_ref, wqn_ref, wqr_ref, wqrs_ref, wuk_ref, cos_ref, sin_ref, q_ref,
                  *, tq):
    x = x_ref[...]
    sh1, sc1 = mod_ref[0:1, :], mod_ref[1:2, :]
    h = (x * (1.0 + sc1) + sh1).astype(BF16)
    tm = x.shape[0]
    cq = lax.dot_general(wdq_ref[...], h, (((1,), (1,)), ((), ())), preferred_element_type=F32)
    cq = (cq * lax.rsqrt(jnp.mean(cq * cq, axis=0, keepdims=True) + RMS_EPS) * qng_ref[...]).astype(BF16)
    qn = _bdot(wqn_ref[...], cq)
    qr = _bdot(wqr_ref[...], cq)
    qrs = _bdot(wqrs_ref[...], cq)
    rope = (qr * cos_ref[...] + qrs * sin_ref[...]).astype(BF16)
    per_group = Q_GROUP // tq
    for hd in range(MLA_HEADS):
        ql = _bdot(wuk_ref[hd], qn[hd * QK_NOPE:(hd + 1) * QK_NOPE, :].astype(BF16)).astype(BF16)
        g, off = hd // per_group, (hd % per_group) * tq
        for t in range(tm // tq):
            q_ref[t, g, 0:KV_RANK, off:off + tq] = ql[:, t * tq:(t + 1) * tq]
            q_ref[t, g, KV_RANK:QK_LAT, off:off + tq] = rope[hd * QK_ROPE:(hd + 1) * QK_ROPE, t * tq:(t + 1) * tq]


def _qproj_layer(x, mod, w_dq, qn_g, w_qb, w_uk, cos_t, sin_t, *, tm, tq):
    bsz, s, d = x.shape
    w_qn = w_qb[:, :, :QK_NOPE].reshape(Q_RANK, MLA_HEADS * QK_NOPE)
    w_qr = w_qb[:, :, QK_NOPE:].reshape(Q_RANK, MLA_HEADS * QK_ROPE)
    w_qrs = _swap_halves_cols(w_qr, QK_ROPE)
    wuk_h = jnp.transpose(w_uk, (1, 0, 2)).astype(BF16)
    hr = MLA_HEADS * QK_ROPE
    n_groups = MLA_HEADS * tq // Q_GROUP
    nt = tm // tq
    return pl.pallas_call(
        functools.partial(_qproj_kernel, tq=tq),
        grid=(bsz, s // tm),
        in_specs=[
            pl.BlockSpec((None, tm, d), lambda b, i: (b, i, 0)),
            pl.BlockSpec((None, 6, d), lambda b, i: (b, 0, 0)),
            _const_spec((Q_RANK, d)),
            _const_spec((Q_RANK, 1)),
            _const_spec((MLA_HEADS * QK_NOPE, Q_RANK)),
            _const_spec((hr, Q_RANK)),
            _const_spec((hr, Q_RANK)),
            _const_spec(wuk_h.shape),
            pl.BlockSpec((hr, tm), lambda b, i: (0, i)),
            pl.BlockSpec((hr, tm), lambda b, i: (0, i)),
        ],
        out_specs=pl.BlockSpec((None, nt, n_groups, QK_LAT, Q_GROUP), lambda b, i: (b, i, 0, 0, 0)),
        out_shape=jax.ShapeDtypeStruct((bsz, s // tq, n_groups, QK_LAT, Q_GROUP), BF16),
        compiler_params=_cparams(("arbitrary", "arbitrary")),
        name="mla_qproj",
    )(x, mod, w_dq.T.astype(BF16), qn_g.reshape(-1, 1), w_qn.T.astype(BF16), w_qr.T.astype(BF16),
      w_qrs.T.astype(BF16), wuk_h, cos_t.T, sin_t.T)


EXP2_SCALE = ATTN_SCALE * 1.4426950408889634


def _attn_kernel(q_ref, *rest, tq, causal, n_main, tail, split_keys):
    if split_keys:
        klat_ref, kkr_ref = rest[:2]
        rest = rest[2:]
    else:
        k_ref = rest[0]
        rest = rest[1:]
    if tail:
        kt_ref = rest[0]
        rest = rest[1:]
    x_ref, mod_ref, wuv_ref, wo_ref, lng_ref, lnb_ref, x1_ref, h2_ref, m_s, l_s, acc_s, ta_s, tb_s = rest
    i = pl.program_id(1)
    n_groups = q_ref.shape[0]
    per_group = Q_GROUP // tq
    m_s[...] = jnp.full_like(m_s, NEG_BIG)
    l_s[...] = jnp.zeros_like(l_s)
    acc_s[...] = jnp.zeros_like(acc_s)

    def tile_rows(j):
        return pl.ds(pl.multiple_of(j * KEY_TILE, KEY_TILE), KEY_TILE)

    def key_rows(j):
        if split_keys:
            return jnp.concatenate([klat_ref[tile_rows(j), :].astype(BF16), kkr_ref[tile_rows(j), :].astype(BF16)],
                                   axis=1)
        return k_ref[tile_rows(j), :]

    def value_rows(j):
        if split_keys:
            return klat_ref[tile_rows(j), :].astype(BF16)
        return k_ref[tile_rows(j), :KV_RANK]

    def softmax_step(g, t, v, key0, masked):
        t = t * EXP2_SCALE
        if masked:
            n = t.shape[0]
            k_chunk = (key0 + lax.broadcasted_iota(jnp.int32, (n, Q_GROUP), 0)) // CHUNK
            q_chunk = (i * tq + lax.broadcasted_iota(jnp.int32, (n, Q_GROUP), 1) % tq) // CHUNK
            t = jnp.where(k_chunk <= q_chunk, t, -jnp.inf)
        m_old = m_s[g]
        m_new = jnp.maximum(m_old, jnp.max(t, axis=0, keepdims=True))
        alpha = jnp.exp2(m_old - m_new)
        p = jnp.exp2(t - m_new)
        l_s[g] = alpha * l_s[g] + jnp.sum(p, axis=0, keepdims=True)
        acc_s[g] = alpha * acc_s[g] + lax.dot_general(
            v, p.astype(BF16), (((0,), (0,)), ((), ())), preferred_element_type=F32)
        m_s[g] = m_new

    def stage(cur, nxt, j, masked, make_next):
        v = value_rows(j)
        k_next = key_rows(j + 1) if make_next else None
        for g in range(n_groups):
            if make_next:
                nxt[g] = _bdot(k_next, q_ref[g])
            softmax_step(g, cur[g], v, j * KEY_TILE, masked)

    n_open = (i * tq) // KEY_TILE if causal else n_main - 1
    k0 = key_rows(0)
    for g in range(n_groups):
        ta_s[g] = _bdot(k0, q_ref[g])

    def pair_body(p, c):
        stage(ta_s, tb_s, 2 * p, False, True)
        stage(tb_s, ta_s, 2 * p + 1, False, True)
        return c

    lax.fori_loop(0, n_open // 2, pair_body, 0)
    if causal:
        @pl.when(n_open % 2 == 0)
        def _():
            stage(ta_s, tb_s, n_open, True, False)

        @pl.when(n_open % 2 == 1)
        def _():
            stage(ta_s, tb_s, n_open - 1, False, True)
            stage(tb_s, ta_s, n_open, True, False)
    elif n_open % 2 == 0:
        stage(ta_s, tb_s, n_open, False, False)
    else:
        stage(ta_s, tb_s, n_open - 1, False, True)
        stage(tb_s, ta_s, n_open, False, False)
    if tail:
        kt = kt_ref[...]
        for g in range(n_groups):
            softmax_step(g, _bdot(kt, q_ref[g]), kt[:, :KV_RANK], n_main * KEY_TILE, False)

    heads = []
    for g in range(n_groups):
        o = (acc_s[g] / l_s[g]).T
        for hh in range(per_group):
            hd = g * per_group + hh
            heads.append(_bdot(o[hh * tq:(hh + 1) * tq].astype(BF16), wuv_ref[hd]))
    oc = jnp.concatenate(heads, axis=-1).astype(BF16)
    y = _bdot(oc, wo_ref[...])
    x = x_ref[...]
    g1, sh2, sc2 = mod_ref[2:3, :], mod_ref[3:4, :], mod_ref[4:5, :]
    x1 = _layer_norm(DN_ALPHA * x + g1 * y, lng_ref[...], lnb_ref[...])
    x1_ref[...] = x1
    h2_ref[...] = x1 * (1.0 + sc2) + sh2


def _attn_layer(q, keys, x, mod, w_uv, w_o, ln_g, ln_b, *, tq, causal, k_tail=None):
    bsz, s, d = x.shape
    split_keys = isinstance(keys, tuple)
    keys = keys if split_keys else (keys,)
    t_main = keys[0].shape[1]
    n_main = t_main // KEY_TILE
    n_groups = q.shape[2]
    tail = k_tail is not None
    wuv_t = jnp.transpose(w_uv, (1, 0, 2)).astype(BF16)
    tile = pl.BlockSpec((None, tq, d), lambda b, i: (b, i, 0))
    in_specs = [pl.BlockSpec((None, None, n_groups, QK_LAT, Q_GROUP), lambda b, i: (b, i, 0, 0, 0))]
    in_specs += [pl.BlockSpec((None, t_main, k.shape[2]), lambda b, i: (b, 0, 0)) for k in keys]
    args = [q, *keys]
    if tail:
        in_specs.append(pl.BlockSpec((None, k_tail.shape[1], QK_LAT), lambda b, i: (b, 0, 0)))
        args.append(k_tail)
    in_specs += [
        tile,
        pl.BlockSpec((None, 6, d), lambda b, i: (b, 0, 0)),
        _const_spec(wuv_t.shape),
        _const_spec((MLA_HEADS * V_DIM, d)),
        _const_spec((1, d)),
        _const_spec((1, d)),
    ]
    args += [x, mod, wuv_t, w_o.astype(BF16), ln_g.reshape(1, -1), ln_b.reshape(1, -1)]
    return pl.pallas_call(
        functools.partial(_attn_kernel, tq=tq, causal=causal, n_main=n_main, tail=tail, split_keys=split_keys),
        grid=(bsz, s // tq),
        in_specs=in_specs,
        out_specs=[tile, tile],
        out_shape=[jax.ShapeDtypeStruct((bsz, s, d), F32), jax.ShapeDtypeStruct((bsz, s, d), F32)],
        scratch_shapes=[pltpu.VMEM((n_groups, 1, Q_GROUP), F32), pltpu.VMEM((n_groups, 1, Q_GROUP), F32),
                        pltpu.VMEM((n_groups, KV_RANK, Q_GROUP), F32),
                        pltpu.VMEM((n_groups, KEY_TILE, Q_GROUP), F32),
                        pltpu.VMEM((n_groups, KEY_TILE, Q_GROUP), F32)],
        compiler_params=_cparams(("arbitrary", "arbitrary")),
        name="mla_attention",
    )(*args)


def kernel(x_prompt, x_sample, cache_kv_latent, cache_k_rope, c_prompt, c_sample, ada_w, ada_b, ln_g, ln_b,
           a_w_in, a_b_in, a_vn_g, a_vn_b, a_w_s, a_b_s, a_w_out, a_b_out, kv_w_a, kv_norm_g, kv_w_uk, kv_w_uv,
           q_w_a, q_norm_g, q_w_b, q_w_o, moe_w_router, moe_b_router, moe_w_in, moe_b_in, moe_w_out, moe_b_out):
    bp, sp, d = x_prompt.shape
    bs, ss, _ = x_sample.shape
    past = cache_kv_latent.shape[1]
    n_p = bp * sp
    n_s = bs * ss

    mod = _ada_mod(jnp.concatenate([c_prompt, c_sample], axis=0), ada_w, ada_b)
    mod = mod.reshape(DEPTH, bp + bs, 6, d)
    mod_p, mod_s = mod[:, :bp], mod[:, bp:]

    pos_p = jnp.arange(sp, dtype=jnp.int32)
    pos_s = past + jnp.arange(ss, dtype=jnp.int32)

    def moe(l, h2_p, h2_s):
        return _moe(h2_p.reshape(n_p, d), h2_s.reshape(n_s, d), l, moe_w_router[l], moe_b_router[l],
                    moe_w_in, moe_b_in[l], moe_w_out, moe_b_out[l])

    gm = functools.partial(_gmlp_layer, w_in=a_w_in[0], b_in=a_b_in[0], vn_g=a_vn_g[0], vn_b=a_vn_b[0],
                           w_s=a_w_s[0], b_s=a_b_s[0], w_out=a_w_out[0], b_out=a_b_out[0],
                           ln_g=ln_g[0, 0], ln_b=ln_b[0, 0])
    x1_p, h2_p = gm(x_prompt, mod_p[0], tm=256, write_v=False)
    x1_s, h2_s, v_s = gm(x_sample, mod_s[0], tm=ss, write_v=True)
    ys, gates = moe(0, h2_p, h2_s)
    x2_p, lat_p, kr_p, kcat_p = _post_layer(x1_p, ys, gates, 0, mod_p[0], ln_g[0, 1], ln_b[0, 1], tm=512,
                                            kv=(kv_w_a, kv_norm_g) + _rope_tables(pos_p, 1))
    x2_s, lat_s, kr_s, kcat_s = _post_layer(x1_s, ys, gates, n_p, mod_s[0], ln_g[0, 1], ln_b[0, 1], tm=ss,
                                            kv=(kv_w_a, kv_norm_g) + _rope_tables(pos_s, 1))

    qp = functools.partial(_qproj_layer, w_dq=q_w_a[0], qn_g=q_norm_g[0], w_qb=q_w_b[0], w_uk=kv_w_uk)
    at = functools.partial(_attn_layer, w_uv=kv_w_uv, w_o=q_w_o[0], ln_g=ln_g[1, 0], ln_b=ln_b[1, 0])
    cos_p, sin_p = _rope_tables(pos_p, MLA_HEADS)
    cos_s, sin_s = _rope_tables(pos_s, MLA_HEADS)
    q_p = qp(x2_p, mod_p[1], cos_t=cos_p, sin_t=sin_p, tm=512, tq=128)
    q_s = qp(x2_s, mod_s[1], cos_t=cos_s, sin_t=sin_s, tm=ss, tq=ss)
    x3_p, h4_p = at(q_p, kcat_p, x2_p, mod_p[1], tq=128, causal=True)
    x3_s, h4_s = at(q_s, (cache_kv_latent, cache_k_rope), x2_s, mod_s[1], tq=ss, causal=False, k_tail=kcat_s)
    ys, gates = moe(1, h4_p, h4_s)
    (y_p,) = _post_layer(x3_p, ys, gates, 0, mod_p[1], ln_g[1, 1], ln_b[1, 1], tm=512)
    (y_s,) = _post_layer(x3_s, ys, gates, n_p, mod_s[1], ln_g[1, 1], ln_b[1, 1], tm=ss)

    return (y_p, y_s, lat_p, kr_p, lat_s, kr_s, v_s[None])
```

```python
import functools

import jax
import jax.numpy as jnp
from jax import lax
from jax.experimental import pallas as pl
from jax.experimental.pallas import tpu as pltpu

D_MODEL = 1024
DEPTH = 2
CHUNK = 64
A_CHUNK = 128
A_HALF = 2 * D_MODEL
A_GROUPS = 8
A_GROUP_W = A_HALF // A_GROUPS
MLA_HEADS = 8
QK_NOPE = 128
QK_ROPE = 64
ROPE_HALF = QK_ROPE // 2
V_DIM = 128
KV_RANK = 256
Q_RANK = 512
QK_LAT = KV_RANK + QK_ROPE
ROPE_BASE = 10000.0
ATTN_SCALE = (QK_NOPE + QK_ROPE) ** -0.5
N_EXPERTS = 32
TOP_K = 4
D_EXPERT = D_MODEL
SWIGLU_LIMIT = 7.0
SWIGLU_ALPHA = 1.702
DN_ALPHA = (2 * DEPTH) ** 0.25
LN_EPS = 1e-5
RMS_EPS = 1e-6

BF16 = jnp.bfloat16
F32 = jnp.float32

VMEM_LIMIT = 56 * 1024 * 1024
EXPERT_ROWS = 256
ROUTER_ROWS = 512
NEG_BIG = -1e30


def _cparams(sem):
    return pltpu.CompilerParams(dimension_semantics=sem, vmem_limit_bytes=VMEM_LIMIT)


def _const_spec(shape):
    nd = len(shape)
    return pl.BlockSpec(shape, lambda *_: (0,) * nd, pipeline_mode=pl.Buffered(1))


def _layer_norm(r, g, b):
    rc = r - jnp.mean(r, axis=-1, keepdims=True)
    var = jnp.mean(rc * rc, axis=-1, keepdims=True)
    return rc * lax.rsqrt(var + LN_EPS) * g + b


def _gelu_tanh(x):
    c = 0.7978845608028654
    return 0.5 * x * (1.0 + jnp.tanh(c * (x + 0.044715 * (x * x * x))))


def _bdot(a, b):
    return jnp.dot(a, b, preferred_element_type=F32)


def _ada_kernel(c_ref, w_ref, b_ref, o_ref):
    c = c_ref[...]
    s = c * jax.nn.sigmoid(c)
    o_ref[...] = jnp.dot(s, w_ref[...], preferred_element_type=F32,
                         precision=lax.Precision.HIGHEST) + b_ref[...]


def _ada_mod(c_all, ada_w, ada_b):
    nb = c_all.shape[0]
    six_d = ada_w.shape[-1]
    tn = D_MODEL
    return pl.pallas_call(
        _ada_kernel,
        grid=(DEPTH, six_d // tn),
        in_specs=[
            pl.BlockSpec((nb, D_MODEL), lambda l, j: (0, 0)),
            pl.BlockSpec((None, D_MODEL, tn), lambda l, j: (l, 0, j)),
            pl.BlockSpec((None, 1, tn), lambda l, j: (l, 0, j)),
        ],
        out_specs=pl.BlockSpec((None, nb, tn), lambda l, j: (l, 0, j)),
        out_shape=jax.ShapeDtypeStruct((DEPTH, nb, six_d), F32),
        compiler_params=_cparams(("arbitrary", "arbitrary")),
        name="ada_mod",
    )(c_all, ada_w, ada_b.reshape(DEPTH, 1, six_d))


def _gmlp_kernel(x_ref, mod_ref, w_in_ref, b_in_ref, vng_ref, vnb_ref, ws_ref, bs_ref,
                 w_out_ref, b_out_ref, lng_ref, lnb_ref, *rest, write_v):
    if write_v:
        x1_ref, h2_ref, v_ref, u_s, v_s, p_s = rest
    else:
        x1_ref, h2_ref, u_s, v_s, p_s = rest
    x = x_ref[...]
    sh1, sc1, g1 = mod_ref[0:1, :], mod_ref[1:2, :], mod_ref[2:3, :]
    sh2, sc2 = mod_ref[3:4, :], mod_ref[4:5, :]
    h = (x * (1.0 + sc1) + sh1).astype(BF16)
    tm = x.shape[0]
    ch = 512
    n_ch = A_HALF // ch
    for j in range(n_ch):
        sl = slice(j * ch, (j + 1) * ch)
        u_s[:, sl] = _gelu_tanh(_bdot(h, w_in_ref[:, sl]) + b_in_ref[:, sl])
    tot = jnp.zeros((tm, 1), F32)
    for j in range(n_ch):
        sl = slice(j * ch, (j + 1) * ch)
        slw = slice(A_HALF + j * ch, A_HALF + (j + 1) * ch)
        g = _gelu_tanh(_bdot(h, w_in_ref[:, slw]) + b_in_ref[:, slw])
        v_s[:, sl] = g
        tot = tot + jnp.sum(g, axis=-1, keepdims=True)
    mean = tot * (1.0 / A_HALF)
    sq = jnp.zeros((tm, 1), F32)
    for j in range(n_ch):
        sl = slice(j * ch, (j + 1) * ch)
        c = v_s[:, sl] - mean
        sq = sq + jnp.sum(c * c, axis=-1, keepdims=True)
    rstd = lax.rsqrt(sq * (1.0 / A_HALF) + LN_EPS)
    for g in range(A_GROUPS):
        sl = slice(g * A_GROUP_W, (g + 1) * A_GROUP_W)
        vn = (v_s[:, sl] - mean) * rstd * vng_ref[:, sl] + vnb_ref[:, sl]
        if write_v:
            v_ref[:, sl] = vn
        sg = _bdot(ws_ref[g], vn.astype(BF16)) + bs_ref[:, g:g + 1]
        p_s[:, sl] = (u_s[:, sl] * sg).astype(BF16)
    y = _bdot(p_s[...], w_out_ref[...]) + b_out_ref[...]
    x1 = _layer_norm(DN_ALPHA * x + g1 * y, lng_ref[...], lnb_ref[...])
    x1_ref[...] = x1
    h2_ref[...] = x1 * (1.0 + sc2) + sh2


def _gmlp_layer(x, mod, w_in, b_in, vn_g, vn_b, w_s, b_s, w_out, b_out, ln_g, ln_b, *, tm, write_v):
    bsz, s, d = x.shape
    seg = min(s, A_CHUNK)
    idx = jnp.arange(seg)
    mask = (idx[:, None] // CHUNK) >= (idx[None, :] // CHUNK)
    wm = jnp.where(mask[None], w_s[:, :seg, :seg], 0.0)
    reps = tm // seg
    eye = jnp.eye(reps, dtype=F32)
    ws_big = jnp.einsum("ab,gij->gaibj", eye, wm).reshape(A_GROUPS, tm, tm).astype(BF16)
    bs_big = jnp.tile(b_s[:, :seg].T, (reps, 1))
    out_shape = [jax.ShapeDtypeStruct((bsz, s, d), F32), jax.ShapeDtypeStruct((bsz, s, d), F32)]
    tile = lambda w: pl.BlockSpec((None, tm, w), lambda b, i: (b, i, 0))
    out_specs = [tile(d), tile(d)]
    if write_v:
        out_shape.append(jax.ShapeDtypeStruct((bsz, s, A_HALF), F32))
        out_specs.append(tile(A_HALF))
    return pl.pallas_call(
        functools.partial(_gmlp_kernel, write_v=write_v),
        grid=(bsz, s // tm),
        in_specs=[
            tile(d),
            pl.BlockSpec((None, 6, d), lambda b, i: (b, 0, 0)),
            _const_spec((d, 2 * A_HALF)),
            _const_spec((1, 2 * A_HALF)),
            _const_spec((1, A_HALF)),
            _const_spec((1, A_HALF)),
            _const_spec((A_GROUPS, tm, tm)),
            _const_spec((tm, A_GROUPS)),
            _const_spec((A_HALF, d)),
            _const_spec((1, d)),
            _const_spec((1, d)),
            _const_spec((1, d)),
        ],
        out_specs=out_specs,
        out_shape=out_shape,
        scratch_shapes=[pltpu.VMEM((tm, A_HALF), F32), pltpu.VMEM((tm, A_HALF), F32),
                        pltpu.VMEM((tm, A_HALF), BF16)],
        compiler_params=_cparams(("arbitrary", "arbitrary")),
        name="gmlp_layer",
    )(x, mod, w_in.astype(BF16), b_in.reshape(1, -1), vn_g.reshape(1, -1), vn_b.reshape(1, -1),
      ws_big, bs_big, w_out.astype(BF16), b_out.reshape(1, -1), ln_g.reshape(1, -1), ln_b.reshape(1, -1))


ROW_CHUNKS = D_MODEL // 128


def _router_kernel(hp_ref, hs_ref, wr_ref, br_ref, tri_ref, e_ref, g_ref, r_ref, cnt_ref, h3_ref, base_s,
                   *, n_first):
    i = pl.program_id(0)

    @pl.when(i == 0)
    def _():
        base_s[...] = jnp.zeros_like(base_s)

    h = jnp.where(i < n_first, hp_ref[...], hs_ref[...])
    rows = h.shape[0]
    for c in range(ROW_CHUNKS):
        h3_ref[pl.ds(c, rows, stride=ROW_CHUNKS), :] = h[:, c * 128:(c + 1) * 128]
    logits = lax.dot_general(wr_ref[...], h, (((1,), (1,)), ((), ())),
                             preferred_element_type=F32,
                             precision=lax.Precision.HIGHEST) + br_ref[...]
    rows = h.shape[0]
    iota = lax.broadcasted_iota(jnp.int32, (N_EXPERTS, rows), 0)
    l = logits
    vals, hots = [], []
    for k in range(TOP_K):
        m = jnp.max(l, axis=0, keepdims=True)
        idx = jnp.min(jnp.where(l == m, iota, N_EXPERTS), axis=0, keepdims=True)
        hot = iota == idx
        vals.append(m)
        hots.append(hot)
        e_ref[k:k + 1, :] = idx
        l = jnp.where(hot, -jnp.inf, l)
    exps = [jnp.exp(v - vals[0]) for v in vals]
    den = exps[0] + exps[1] + exps[2] + exps[3]
    for k in range(TOP_K):
        g_ref[k:k + 1, :] = exps[k] / den
    hot_all = jnp.where(hots[0] | hots[1] | hots[2] | hots[3], 1.0, 0.0)
    before = _bdot(hot_all.astype(BF16), tri_ref[...]) + base_s[:, 0:1]
    for k in range(TOP_K):
        r_ref[k:k + 1, :] = jnp.sum(jnp.where(hots[k], before, 0.0), axis=0, keepdims=True).astype(jnp.int32)
    base_s[...] = base_s[...] + jnp.sum(hot_all, axis=1, keepdims=True)
    cnt_ref[...] = base_s[...].astype(jnp.int32)


def _router(h_first, h_second, w_router, b_router):
    tr = ROUTER_ROWS
    n_first = h_first.shape[0] // tr
    n_second = h_second.shape[0] // tr
    n = h_first.shape[0] + h_second.shape[0]
    tri = (jnp.arange(tr)[:, None] < jnp.arange(tr)[None, :]).astype(BF16)
    sel = pl.BlockSpec((TOP_K, tr), lambda i: (0, i))
    top_e, gates, rank, cnt, h3 = pl.pallas_call(
        functools.partial(_router_kernel, n_first=n_first),
        grid=(n_first + n_second,),
        in_specs=[
            pl.BlockSpec((tr, D_MODEL), lambda i: (jnp.minimum(i, n_first - 1), 0)),
            pl.BlockSpec((tr, D_MODEL), lambda i: (jnp.maximum(i - n_first, 0), 0)),
            _const_spec((N_EXPERTS, D_MODEL)),
            _const_spec((N_EXPERTS, 1)),
            _const_spec((tr, tr)),
        ],
        out_specs=[sel, sel, sel, pl.BlockSpec((N_EXPERTS, 128), lambda i: (0, 0)),
                   pl.BlockSpec((tr * ROW_CHUNKS, 128), lambda i: (i, 0))],
        out_shape=[jax.ShapeDtypeStruct((TOP_K, n), jnp.int32), jax.ShapeDtypeStruct((TOP_K, n), F32),
                   jax.ShapeDtypeStruct((TOP_K, n), jnp.int32),
                   jax.ShapeDtypeStruct((N_EXPERTS, 128), jnp.int32),
                   jax.ShapeDtypeStruct((n * ROW_CHUNKS, 128), F32)],
        scratch_shapes=[pltpu.VMEM((N_EXPERTS, 128), F32)],
        compiler_params=_cparams(("arbitrary",)),
        name="moe_router",
    )(h_first, h_second, w_router.T, b_router.reshape(N_EXPERTS, 1), tri)
    return top_e, gates, rank, cnt[:, 0], h3


LANES = 128
PREP_ROWS = 64


def _pair_perm(v):
    shp = v.shape
    v4 = v.reshape(shp[:-1] + (shp[-1] // LANES, 2, LANES // 2))
    return jnp.swapaxes(v4, -1, -2).reshape(shp)


ROW_BUFS = 4


def _expert_kernel(be_ref, nact_ref, src0_ref, src1_ref, src2_ref, dstp_ref, dstc_ref, h3_ref,
                   win_ref, bg_ref, bl_ref, wout_ref, bo_ref, ys_ref,
                   wg_s, wl_s, wo_t, wo_s, *bufs_and_sems):
    xbufs, ybufs = bufs_and_sems[:ROW_BUFS], bufs_and_sems[ROW_BUFS:2 * ROW_BUFS]
    gsem, ssem = bufs_and_sems[2 * ROW_BUFS:]
    b = pl.program_id(0)
    nact = nact_ref[0]
    active = b < nact
    last = b == nact - 1
    blk = xbufs[0].shape[0]
    tmb = blk // ROW_CHUNKS
    fresh = jnp.logical_or(b == 0, be_ref[b] != be_ref[jnp.maximum(b - 1, 0)])

    def row_tile(ref, row):
        return ref.at[pl.ds(pl.multiple_of(row * ROW_CHUNKS, ROW_CHUNKS), ROW_CHUNKS)]

    def start_gather(src_ref, s):
        for r in range(tmb):
            pltpu.make_async_copy(row_tile(h3_ref, src_ref[0, 0, r]), xbufs[s].at[pl.ds(r * ROW_CHUNKS, ROW_CHUNKS)],
                                  gsem.at[s]).start(priority=r % 2)

    def wait_gather(s):
        pltpu.make_async_copy(h3_ref.at[pl.ds(0, blk)], xbufs[s], gsem.at[s]).wait()

    def start_scatter(dst_ref, s):
        for r in range(tmb):
            pltpu.make_async_copy(ybufs[s].at[pl.ds(r * ROW_CHUNKS, ROW_CHUNKS)], row_tile(ys_ref, dst_ref[0, 0, r]),
                                  ssem.at[s]).start(priority=r % 2)

    def wait_scatter(s):
        pltpu.make_async_copy(ybufs[s], ys_ref.at[pl.ds(0, blk)], ssem.at[s]).wait()

    def chunk_rows(c):
        return pl.ds(c, tmb, stride=ROW_CHUNKS)

    def on_phase(cond, fn):
        for q in range(ROW_BUFS):
            pl.when(jnp.logical_and(cond, b % ROW_BUFS == q))(functools.partial(fn, q))

    @pl.when(jnp.logical_and(active, b == 0))
    def _():
        start_gather(src0_ref, 0)
        start_gather(src1_ref, 1)

    @pl.when(jnp.logical_and(active, fresh))
    def _():
        even = lax.broadcasted_iota(jnp.int32, (PREP_ROWS, LANES), 1) % 2 == 0

        def split_rows(r, carry):
            rows = pl.ds(pl.multiple_of(r * PREP_ROWS, PREP_ROWS), PREP_ROWS)
            for c in range(D_EXPERT // LANES):
                blk_a = win_ref[rows, 2 * c * LANES:(2 * c + 1) * LANES]
                blk_b = win_ref[rows, (2 * c + 1) * LANES:(2 * c + 2) * LANES]
                wg_s[rows, c * LANES:(c + 1) * LANES] = jnp.where(
                    even, blk_a, pltpu.roll(blk_b, 1, 1)).astype(BF16)
                wl_s[rows, c * LANES:(c + 1) * LANES] = jnp.where(
                    even, pltpu.roll(blk_a, LANES - 1, 1), blk_b).astype(BF16)
            return carry

        lax.fori_loop(0, D_MODEL // PREP_ROWS, split_rows, 0)
        half = LANES // 2
        for c in range(D_EXPERT // LANES):
            for p in range(2):
                for cb in range(D_MODEL // LANES):
                    wo_t[cb, pl.ds(c * LANES + p, half, stride=2), :] = (
                        wout_ref[pl.ds(c * LANES + half * p, half), cb * LANES:(cb + 1) * LANES])

        def cast_rows(r, carry):
            rows = pl.ds(pl.multiple_of(r * PREP_ROWS, PREP_ROWS), PREP_ROWS)
            for cb in range(D_MODEL // LANES):
                wo_s[rows, cb * LANES:(cb + 1) * LANES] = wo_t[cb, rows, :].astype(BF16)
            return carry

        lax.fori_loop(0, D_EXPERT // PREP_ROWS, cast_rows, 0)

    on_phase(active, wait_gather)
    on_phase(jnp.logical_and(active, b >= ROW_BUFS), wait_scatter)

    def compute(p):
        x = jnp.concatenate([xbufs[p][chunk_rows(c), :] for c in range(ROW_CHUNKS)], axis=1).astype(BF16)
        zg = _bdot(x, wg_s[...]) + bg_ref[...]
        zl = _bdot(x, wl_s[...]) + bl_ref[...]
        glu = jnp.minimum(zg, SWIGLU_LIMIT)
        lin = jnp.clip(zl, -SWIGLU_LIMIT, SWIGLU_LIMIT)
        a = glu * jax.nn.sigmoid(SWIGLU_ALPHA * glu) * (lin + 1.0)
        y = _bdot(a.astype(BF16), wo_s[...]) + bo_ref[...]
        for c in range(ROW_CHUNKS):
            ybufs[p][chunk_rows(c), :] = y[:, c * LANES:(c + 1) * LANES]

    @pl.when(jnp.logical_and(active, b == 0))
    def _():
        start_gather(src2_ref, 2)
        compute(0)

    def step(q):
        start_gather(src2_ref, (q + 2) % ROW_BUFS)
        start_scatter(dstp_ref, (q - 1) % ROW_BUFS)
        compute(q)

    on_phase(jnp.logical_and(active, b > 0), step)

    def drain(q):
        start_scatter(dstc_ref, q)
        wait_gather((q + 1) % ROW_BUFS)
        wait_gather((q + 2) % ROW_BUFS)
        wait_scatter(q)

    on_phase(last, drain)
    for back in range(1, ROW_BUFS):
        on_phase(jnp.logical_and(last, b >= back), lambda q, back=back: wait_scatter((q - back) % ROW_BUFS))

    @pl.when(last)
    def _():
        spare0 = ys_ref.shape[0] - 2 * blk
        fills = []
        for s in range(2):
            ybufs[s][...] = jnp.zeros_like(ybufs[s])
            fills.append(pltpu.make_async_copy(ybufs[s], ys_ref.at[pl.ds(spare0 + s * blk, blk)], ssem.at[s]))
        for cp in fills:
            cp.start()
        for cp in fills:
            cp.wait()


def _experts(h3, src_rows, dst_rows, block_e, nact, n_slots, layer, w_in, b_glu, b_lin, w_out, b_out):
    tmb = EXPERT_ROWS
    n_blocks = src_rows.shape[0]
    exp3 = lambda b, be, na: (be[b], 0, 0)
    exp4 = lambda b, be, na: (layer, be[b], 0, 0)
    idx_spec = lambda f: pl.BlockSpec((1, 1, tmb), lambda b, be, na: (f(b), 0, 0), memory_space=pltpu.SMEM)
    grid_spec = pltpu.PrefetchScalarGridSpec(
        num_scalar_prefetch=2,
        grid=(n_blocks,),
        in_specs=[
            idx_spec(lambda b: b),
            idx_spec(lambda b: jnp.minimum(b + 1, n_blocks - 1)),
            idx_spec(lambda b: jnp.minimum(b + 2, n_blocks - 1)),
            idx_spec(lambda b: jnp.maximum(b - 1, 0)),
            idx_spec(lambda b: b),
            pl.BlockSpec(memory_space=pl.ANY),
            pl.BlockSpec((None, None, D_MODEL, 2 * D_EXPERT), exp4),
            pl.BlockSpec((None, 1, D_EXPERT), exp3),
            pl.BlockSpec((None, 1, D_EXPERT), exp3),
            pl.BlockSpec((None, None, D_EXPERT, D_MODEL), exp4),
            pl.BlockSpec((None, 1, D_MODEL), exp3),
        ],
        out_specs=pl.BlockSpec(memory_space=pl.ANY),
        scratch_shapes=[pltpu.VMEM((D_MODEL, D_EXPERT), BF16), pltpu.VMEM((D_MODEL, D_EXPERT), BF16),
                        pltpu.VMEM((D_MODEL // LANES, D_EXPERT, LANES), F32),
                        pltpu.VMEM((D_EXPERT, D_MODEL), BF16)]
        + [pltpu.VMEM((tmb * ROW_CHUNKS, LANES), F32)] * (2 * ROW_BUFS)
        + [pltpu.SemaphoreType.DMA((ROW_BUFS,)), pltpu.SemaphoreType.DMA((ROW_BUFS,))],
    )
    return pl.pallas_call(
        _expert_kernel,
        grid_spec=grid_spec,
        out_shape=jax.ShapeDtypeStruct((n_slots * ROW_CHUNKS, LANES), F32),
        compiler_params=_cparams(("arbitrary",)),
        name="moe_experts",
    )(block_e, nact, src_rows, src_rows, src_rows, dst_rows, dst_rows, h3, w_in, b_glu, b_lin, w_out, b_out)


def _moe(h_first, h_second, layer, w_router, b_router, w_in, b_in, w_out, b_out):
    tmb = EXPERT_ROWS
    top_e, gates, rank, counts, h3 = _router(h_first, h_second, w_router, b_router)
    n = h3.shape[0] // ROW_CHUNKS
    m = n * TOP_K
    padded = (counts + tmb - 1) // tmb * tmb
    pend = jnp.cumsum(padded)
    pstart = pend - padded
    hot = top_e[..., None] == jnp.arange(N_EXPERTS)[None, None, :]
    dest = jnp.sum(jnp.where(hot, pstart[None, None, :], 0), axis=-1) + rank
    n_blocks = m // tmb + N_EXPERTS
    blk_row0 = jnp.arange(n_blocks, dtype=jnp.int32) * tmb
    block_e = jnp.minimum(jnp.sum(pend[None, :] <= blk_row0[:, None], axis=1), N_EXPERTS - 1).astype(jnp.int32)
    nact = (pend[-1] // tmb).astype(jnp.int32).reshape(1)
    slot = jnp.arange(n, dtype=jnp.int32)[None, :] + n * jnp.arange(TOP_K, dtype=jnp.int32)[:, None]
    j = jnp.arange(N_EXPERTS * tmb, dtype=jnp.int32)
    off = j % tmb
    pad = padded - counts
    rep = lambda v: jnp.repeat(v, tmb)
    pad_key = jnp.where(off < rep(pad), rep(pstart + counts) + off, pend[-1] + j - rep(jnp.cumsum(pad)))
    keys = jnp.concatenate([dest.reshape(m), pad_key]).astype(jnp.int32)
    vals = jnp.concatenate([slot.reshape(m), jnp.full_like(j, -1)])
    _, row_slot = lax.sort((keys, vals), num_keys=1, is_stable=False)
    row = jnp.arange(n_blocks * tmb, dtype=jnp.int32)
    src_rows = jnp.where(row_slot < 0, 0, row_slot % n)
    dst_rows = jnp.where(row_slot < 0, m + ((row // tmb) % 2) * tmb + row % tmb, row_slot)
    b_glu = _pair_perm(b_in[:, 0::2]).reshape(N_EXPERTS, 1, D_EXPERT)
    b_lin = _pair_perm(b_in[:, 1::2]).reshape(N_EXPERTS, 1, D_EXPERT)
    ys = _experts(h3, src_rows.reshape(n_blocks, 1, tmb), dst_rows.reshape(n_blocks, 1, tmb), block_e, nact,
                  m + 2 * tmb, layer, w_in, b_glu, b_lin, w_out, b_out.reshape(N_EXPERTS, 1, D_MODEL))
    return ys, gates.T


def _post_kernel(x1_ref, *rest, with_kv):
    ys_refs = rest[:TOP_K]
    gate_ref, mod_ref, lng_ref, lnb_ref = rest[TOP_K:TOP_K + 4]
    rest = rest[TOP_K + 4:]
    g2 = mod_ref[5:6, :]
    tm = x1_ref.shape[0]
    f = None
    for k in range(TOP_K):
        yk = jnp.concatenate([ys_refs[k][pl.ds(c, tm, stride=ROW_CHUNKS), :] for c in range(ROW_CHUNKS)], axis=1)
        term = yk * gate_ref[:, k:k + 1]
        f = term if f is None else f + term
    x2 = _layer_norm(DN_ALPHA * x1_ref[...] + g2 * f, lng_ref[...], lnb_ref[...])
    if not with_kv:
        (x2_ref,) = rest
        x2_ref[...] = x2
        return
    wa_ref, kng_ref, cos_ref, sin_ref, x2_ref, lat_ref, kr_ref, kcat_ref = rest
    x2_ref[...] = x2
    kv = _bdot(x2.astype(BF16), wa_ref[...])
    c = kv[:, :KV_RANK]
    lat = c * lax.rsqrt(jnp.mean(c * c, axis=-1, keepdims=True) + RMS_EPS) * kng_ref[...]
    k = kv[:, KV_RANK:KV_RANK + QK_ROPE]
    k_swapped = kv[:, KV_RANK + QK_ROPE:]
    kr = k * cos_ref[...] + k_swapped * sin_ref[...]
    lat_ref[...] = lat
    kr_ref[...] = kr
    kcat_ref[:, :KV_RANK] = lat.astype(BF16)
    kcat_ref[:, KV_RANK:] = kr.astype(BF16)


def _swap_halves_cols(w, width):
    shp = w.shape
    w4 = w.reshape(shp[:-1] + (shp[-1] // width, 2, width // 2))
    return w4[..., ::-1, :].reshape(shp)


def _rope_tables(pos, reps):
    inv = 1.0 / (ROPE_BASE ** (jnp.arange(ROPE_HALF, dtype=F32) * (2.0 / QK_ROPE)))
    ang = pos.astype(F32)[:, None] * inv[None, :]
    cos, sin = jnp.cos(ang), jnp.sin(ang)
    cos_t = jnp.tile(jnp.concatenate([cos, cos], -1), (1, reps))
    sin_t = jnp.tile(jnp.concatenate([-sin, sin], -1), (1, reps))
    return cos_t, sin_t


def _post_layer(x1, ys, gates, row0, mod, ln_g, ln_b, *, tm, kv=None):
    bsz, s, d = x1.shape
    blk0 = row0 // tm
    per_b = s // tm
    tile = lambda w: pl.BlockSpec((None, tm, w), lambda b, i: (b, i, 0))
    n_tok = gates.shape[0]
    pick_spec = lambda k: pl.BlockSpec((tm * ROW_CHUNKS, LANES),
                                       lambda b, i: (k * (n_tok // tm) + blk0 + b * per_b + i, 0))
    in_specs = [tile(d)] + [pick_spec(k) for k in range(TOP_K)] + [
        pl.BlockSpec((tm, TOP_K), lambda b, i: (blk0 + b * per_b + i, 0)),
        pl.BlockSpec((None, 6, d), lambda b, i: (b, 0, 0)),
        _const_spec((1, d)),
        _const_spec((1, d)),
    ]
    args = [x1] + [ys] * TOP_K + [gates, mod, ln_g.reshape(1, -1), ln_b.reshape(1, -1)]
    out_shape = [jax.ShapeDtypeStruct((bsz, s, d), F32)]
    out_specs = [tile(d)]
    if kv is not None:
        w_a, kn_g, cos_t, sin_t = kv
        w_ext = jnp.concatenate([w_a, _swap_halves_cols(w_a[:, KV_RANK:], QK_ROPE)], axis=1).astype(BF16)
        in_specs += [_const_spec(w_ext.shape), _const_spec((1, KV_RANK)),
                     pl.BlockSpec((tm, QK_ROPE), lambda b, i: (i, 0)),
                     pl.BlockSpec((tm, QK_ROPE), lambda b, i: (i, 0))]
        args += [w_ext, kn_g.reshape(1, -1), cos_t, sin_t]
        out_shape += [jax.ShapeDtypeStruct((bsz, s, KV_RANK), F32), jax.ShapeDtypeStruct((bsz, s, QK_ROPE), F32),
                      jax.ShapeDtypeStruct((bsz, s, QK_LAT), BF16)]
        out_specs += [tile(KV_RANK), tile(QK_ROPE), tile(QK_LAT)]
    return pl.pallas_call(
        functools.partial(_post_kernel, with_kv=kv is not None),
        grid=(bsz, per_b),
        in_specs=in_specs,
        out_specs=out_specs,
        out_shape=out_shape,
        compiler_params=_cparams(("arbitrary", "arbitrary")),
        name="post_moe",
    )(*args)


Q_GROUP = 512
KEY_TILE = 256


def _qproj_kernel(x_ref, mod_ref, wdq_ref, qng_ref, wqn_ref, wqr_ref, wqrs_ref, wuk_ref, cos_ref, sin_ref, q_ref,
                  *, tq):
    x = x_ref[...]
    sh1, sc1 = mod_ref[0:1, :], mod_ref[1:2, :]
    h = (x * (1.0 + sc1) + sh1).astype(BF16)
    tm = x.shape[0]
    cq = lax.dot_general(wdq_ref[...], h, (((1,), (1,)), ((), ())), preferred_element_type=F32)
    cq = (cq * lax.rsqrt(jnp.mean(cq * cq, axis=0, keepdims=True) + RMS_EPS) * qng_ref[...]).astype(BF16)
    qn = _bdot(wqn_ref[...], cq)
    qr = _bdot(wqr_ref[...], cq)
    qrs = _bdot(wqrs_ref[...], cq)
    rope = (qr * cos_ref[...] + qrs * sin_ref[...]).astype(BF16)
    per_group = Q_GROUP // tq
    for hd in range(MLA_HEADS):
        ql = _bdot(wuk_ref[hd], qn[hd * QK_NOPE:(hd + 1) * QK_NOPE, :].astype(BF16)).astype(BF16)
        g, off = hd // per_group, (hd % per_group) * tq
        for t in range(tm // tq):
            q_ref[t, g, 0:KV_RANK, off:off + tq] = ql[:, t * tq:(t + 1) * tq]
            q_ref[t, g, KV_RANK:QK_LAT, off:off + tq] = rope[hd * QK_ROPE:(hd + 1) * QK_ROPE, t * tq:(t + 1) * tq]


def _qproj_layer(x, mod, w_dq, qn_g, w_qb, w_uk, cos_t, sin_t, *, tm, tq):
    bsz, s, d = x.shape
    w_qn = w_qb[:, :, :QK_NOPE].reshape(Q_RANK, MLA_HEADS * QK_NOPE)
    w_qr = w_qb[:, :, QK_NOPE:].reshape(Q_RANK, MLA_HEADS * QK_ROPE)
    w_qrs = _swap_halves_cols(w_qr, QK_ROPE)
    wuk_h = jnp.transpose(w_uk, (1, 0, 2)).astype(BF16)
    hr = MLA_HEADS * QK_ROPE
    n_groups = MLA_HEADS * tq // Q_GROUP
    nt = tm // tq
    return pl.pallas_call(
        functools.partial(_qproj_kernel, tq=tq),
        grid=(bsz, s // tm),
        in_specs=[
            pl.BlockSpec((None, tm, d), lambda b, i: (b, i, 0)),
            pl.BlockSpec((None, 6, d), lambda b, i: (b, 0, 0)),
            _const_spec((Q_RANK, d)),
            _const_spec((Q_RANK, 1)),
            _const_spec((MLA_HEADS * QK_NOPE, Q_RANK)),
            _const_spec((hr, Q_RANK)),
            _const_spec((hr, Q_RANK)),
            _const_spec(wuk_h.shape),
            pl.BlockSpec((hr, tm), lambda b, i: (0, i)),
            pl.BlockSpec((hr, tm), lambda b, i: (0, i)),
        ],
        out_specs=pl.BlockSpec((None, nt, n_groups, QK_LAT, Q_GROUP), lambda b, i: (b, i, 0, 0, 0)),
        out_shape=jax.ShapeDtypeStruct((bsz, s // tq, n_groups, QK_LAT, Q_GROUP), BF16),
        compiler_params=_cparams(("arbitrary", "arbitrary")),
        name="mla_qproj",
    )(x, mod, w_dq.T.astype(BF16), qn_g.reshape(-1, 1), w_qn.T.astype(BF16), w_qr.T.astype(BF16),
      w_qrs.T.astype(BF16), wuk_h, cos_t.T, sin_t.T)


EXP2_SCALE = ATTN_SCALE * 1.4426950408889634


def _attn_kernel(q_ref, *rest, tq, causal, n_main, tail, split_keys):
    if split_keys:
        klat_ref, kkr_ref = rest[:2]
        rest = rest[2:]
    else:
        k_ref = rest[0]
        rest = rest[1:]
    if tail:
        kt_ref = rest[0]
        rest = rest[1:]
    x_ref, mod_ref, wuv_ref, wo_ref, lng_ref, lnb_ref, x1_ref, h2_ref, m_s, l_s, acc_s, ta_s, tb_s = rest
    i = pl.program_id(1)
    n_groups = q_ref.shape[0]
    per_group = Q_GROUP // tq
    m_s[...] = jnp.full_like(m_s, NEG_BIG)
    l_s[...] = jnp.zeros_like(l_s)
    acc_s[...] = jnp.zeros_like(acc_s)

    def tile_rows(j):
        return pl.ds(pl.multiple_of(j * KEY_TILE, KEY_TILE), KEY_TILE)

    def key_rows(j):
        if split_keys:
            return jnp.concatenate([klat_ref[tile_rows(j), :].astype(BF16), kkr_ref[tile_rows(j), :].astype(BF16)],
                                   axis=1)
        return k_ref[tile_rows(j), :]

    def value_rows(j):
        if split_keys:
            return klat_ref[tile_rows(j), :].astype(BF16)
        return k_ref[tile_rows(j), :KV_RANK]

    def softmax_step(g, t, v, key0, masked):
        t = t * EXP2_SCALE
        if masked:
            n = t.shape[0]
            k_chunk = (key0 + lax.broadcasted_iota(jnp.int32, (n, Q_GROUP), 0)) // CHUNK
            q_chunk = (i * tq + lax.broadcasted_iota(jnp.int32, (n, Q_GROUP), 1) % tq) // CHUNK
            t = jnp.where(k_chunk <= q_chunk, t, -jnp.inf)
        m_old = m_s[g]
        m_new = jnp.maximum(m_old, jnp.max(t, axis=0, keepdims=True))
        alpha = jnp.exp2(m_old - m_new)
        p = jnp.exp2(t - m_new)
        l_s[g] = alpha * l_s[g] + jnp.sum(p, axis=0, keepdims=True)
        acc_s[g] = alpha * acc_s[g] + lax.dot_general(
            v, p.astype(BF16), (((0,), (0,)), ((), ())), preferred_element_type=F32)
        m_s[g] = m_new

    def stage(cur, nxt, j, masked, make_next):
        v = value_rows(j)
        k_next = key_rows(j + 1) if make_next else None
        for g in range(n_groups):
            if make_next:
                nxt[g] = _bdot(k_next, q_ref[g])
            softmax_step(g, cur[g], v, j * KEY_TILE, masked)

    n_open = (i * tq) // KEY_TILE if causal else n_main - 1
    k0 = key_rows(0)
    for g in range(n_groups):
        ta_s[g] = _bdot(k0, q_ref[g])

    def pair_body(p, c):
        stage(ta_s, tb_s, 2 * p, False, True)
        stage(tb_s, ta_s, 2 * p + 1, False, True)
        return c

    lax.fori_loop(0, n_open // 2, pair_body, 0)
    if causal:
        @pl.when(n_open % 2 == 0)
        def _():
            stage(ta_s, tb_s, n_open, True, False)

        @pl.when(n_open % 2 == 1)
        def _():
            stage(ta_s, tb_s, n_open - 1, False, True)
            stage(tb_s, ta_s, n_open, True, False)
    elif n_open % 2 == 0:
        stage(ta_s, tb_s, n_open, False, False)
    else:
        stage(ta_s, tb_s, n_open - 1, False, True)
        stage(tb_s, ta_s, n_open, False, False)
    if tail:
        kt = kt_ref[...]
        for g in range(n_groups):
            softmax_step(g, _bdot(kt, q_ref[g]), kt[:, :KV_RANK], n_main * KEY_TILE, False)

    heads = []
    for g in range(n_groups):
        o = (acc_s[g] / l_s[g]).T
        for hh in range(per_group):
            hd = g * per_group + hh
            heads.append(_bdot(o[hh * tq:(hh + 1) * tq].astype(BF16), wuv_ref[hd]))
    oc = jnp.concatenate(heads, axis=-1).astype(BF16)
    y = _bdot(oc, wo_ref[...])
    x = x_ref[...]
    g1, sh2, sc2 = mod_ref[2:3, :], mod_ref[3:4, :], mod_ref[4:5, :]
    x1 = _layer_norm(DN_ALPHA * x + g1 * y, lng_ref[...], lnb_ref[...])
    x1_ref[...] = x1
    h2_ref[...] = x1 * (1.0 + sc2) + sh2


def _attn_layer(q, keys, x, mod, w_uv, w_o, ln_g, ln_b, *, tq, causal, k_tail=None):
    bsz, s, d = x.shape
    split_keys = isinstance(keys, tuple)
    keys = keys if split_keys else (keys,)
    t_main = keys[0].shape[1]
    n_main = t_main // KEY_TILE
    n_groups = q.shape[2]
    tail = k_tail is not None
    wuv_t = jnp.transpose(w_uv, (1, 0, 2)).astype(BF16)
    tile = pl.BlockSpec((None, tq, d), lambda b, i: (b, i, 0))
    in_specs = [pl.BlockSpec((None, None, n_groups, QK_LAT, Q_GROUP), lambda b, i: (b, i, 0, 0, 0))]
    in_specs += [pl.BlockSpec((None, t_main, k.shape[2]), lambda b, i: (b, 0, 0)) for k in keys]
    args = [q, *keys]
    if tail:
        in_specs.append(pl.BlockSpec((None, k_tail.shape[1], QK_LAT), lambda b, i: (b, 0, 0)))
        args.append(k_tail)
    in_specs += [
        tile,
        pl.BlockSpec((None, 6, d), lambda b, i: (b, 0, 0)),
        _const_spec(wuv_t.shape),
        _const_spec((MLA_HEADS * V_DIM, d)),
        _const_spec((1, d)),
        _const_spec((1, d)),
    ]
    args += [x, mod, wuv_t, w_o.astype(BF16), ln_g.reshape(1, -1), ln_b.reshape(1, -1)]
    return pl.pallas_call(
        functools.partial(_attn_kernel, tq=tq, causal=causal, n_main=n_main, tail=tail, split_keys=split_keys),
        grid=(bsz, s // tq),
        in_specs=in_specs,
        out_specs=[tile, tile],
        out_shape=[jax.ShapeDtypeStruct((bsz, s, d), F32), jax.ShapeDtypeStruct((bsz, s, d), F32)],
        scratch_shapes=[pltpu.VMEM((n_groups, 1, Q_GROUP), F32), pltpu.VMEM((n_groups, 1, Q_GROUP), F32),
                        pltpu.VMEM((n_groups, KV_RANK, Q_GROUP), F32),
                        pltpu.VMEM((n_groups, KEY_TILE, Q_GROUP), F32),
                        pltpu.VMEM((n_groups, KEY_TILE, Q_GROUP), F32)],
        compiler_params=_cparams(("arbitrary", "arbitrary")),
        name="mla_attention",
    )(*args)


def kernel(x_prompt, x_sample, cache_kv_latent, cache_k_rope, c_prompt, c_sample, ada_w, ada_b, ln_g, ln_b,
           a_w_in, a_b_in, a_vn_g, a_vn_b, a_w_s, a_b_s, a_w_out, a_b_out, kv_w_a, kv_norm_g, kv_w_uk, kv_w_uv,
           q_w_a, q_norm_g, q_w_b, q_w_o, moe_w_router, moe_b_router, moe_w_in, moe_b_in, moe_w_out, moe_b_out):
    bp, sp, d = x_prompt.shape
    bs, ss, _ = x_sample.shape
    past = cache_kv_latent.shape[1]
    n_p = bp * sp
    n_s = bs * ss

    mod = _ada_mod(jnp.concatenate([c_prompt, c_sample], axis=0), ada_w, ada_b)
    mod = mod.reshape(DEPTH, bp + bs, 6, d)
    mod_p, mod_s = mod[:, :bp], mod[:, bp:]

    pos_p = jnp.arange(sp, dtype=jnp.int32)
    pos_s = past + jnp.arange(ss, dtype=jnp.int32)

    def moe(l, h2_p, h2_s):
        return _moe(h2_p.reshape(n_p, d), h2_s.reshape(n_s, d), l, moe_w_router[l], moe_b_router[l],
                    moe_w_in, moe_b_in[l], moe_w_out, moe_b_out[l])

    gm = functools.partial(_gmlp_layer, w_in=a_w_in[0], b_in=a_b_in[0], vn_g=a_vn_g[0], vn_b=a_vn_b[0],
                           w_s=a_w_s[0], b_s=a_b_s[0], w_out=a_w_out[0], b_out=a_b_out[0],
                           ln_g=ln_g[0, 0], ln_b=ln_b[0, 0])
    x1_p, h2_p = gm(x_prompt, mod_p[0], tm=256, write_v=False)
    x1_s, h2_s, v_s = gm(x_sample, mod_s[0], tm=ss, write_v=True)
    ys, gates = moe(0, h2_p, h2_s)
    x2_p, lat_p, kr_p, kcat_p = _post_layer(x1_p, ys, gates, 0, mod_p[0], ln_g[0, 1], ln_b[0, 1], tm=512,
                                            kv=(kv_w_a, kv_norm_g) + _rope_tables(pos_p, 1))
    x2_s, lat_s, kr_s, kcat_s = _post_layer(x1_s, ys, gates, n_p, mod_s[0], ln_g[0, 1], ln_b[0, 1], tm=ss,
                                            kv=(kv_w_a, kv_norm_g) + _rope_tables(pos_s, 1))

    qp = functools.partial(_qproj_layer, w_dq=q_w_a[0], qn_g=q_norm_g[0], w_qb=q_w_b[0], w_uk=kv_w_uk)
    at = functools.partial(_attn_layer, w_uv=kv_w_uv, w_o=q_w_o[0], ln_g=ln_g[1, 0], ln_b=ln_b[1, 0])
    cos_p, sin_p = _rope_tables(pos_p, MLA_HEADS)
    cos_s, sin_s = _rope_tables(pos_s, MLA_HEADS)
    q_p = qp(x2_p, mod_p[1], cos_t=cos_p, sin_t=sin_p, tm=512, tq=128)
    q_s = qp(x2_s, mod_s[1], cos_t=cos_s, sin_t=sin_s, tm=ss, tq=ss)
    x3_p, h4_p = at(q_p, kcat_p, x2_p, mod_p[1], tq=128, causal=True)
    x3_s, h4_s = at(q_s, (cache_kv_latent, cache_k_rope), x2_s, mod_s[1], tq=ss, causal=False, k_tail=kcat_s)
    ys, gates = moe(1, h4_p, h4_s)
    (y_p,) = _post_layer(x3_p, ys, gates, 0, mod_p[1], ln_g[1, 1], ln_b[1, 1], tm=512)
    (y_s,) = _post_layer(x3_s, ys, gates, n_p, mod_s[1], ln_g[1, 1], ln_b[1, 1], tm=ss)

    return (y_p, y_s, lat_p, kr_p, lat_s, kr_s, v_s[None])
```

```python
import functools

import jax
import jax.numpy as jnp
from jax import lax
from jax.experimental import pallas as pl
from jax.experimental.pallas import tpu as pltpu

D_MODEL = 1024
DEPTH = 2
CHUNK = 64
A_CHUNK = 128
A_HALF = 2 * D_MODEL
A_GROUPS = 8
A_GROUP_W = A_HALF // A_GROUPS
MLA_HEADS = 8
QK_NOPE = 128
QK_ROPE = 64
ROPE_HALF = QK_ROPE // 2
V_DIM = 128
KV_RANK = 256
Q_RANK = 512
QK_LAT = KV_RANK + QK_ROPE
ROPE_BASE = 10000.0
ATTN_SCALE = (QK_NOPE + QK_ROPE) ** -0.5
N_EXPERTS = 32
TOP_K = 4
D_EXPERT = D_MODEL
SWIGLU_LIMIT = 7.0
SWIGLU_ALPHA = 1.702
DN_ALPHA = (2 * DEPTH) ** 0.25
LN_EPS = 1e-5
RMS_EPS = 1e-6

BF16 = jnp.bfloat16
F32 = jnp.float32

VMEM_LIMIT = 56 * 1024 * 1024
EXPERT_ROWS = 256
ROUTER_ROWS = 512
NEG_BIG = -1e30


def _cparams(sem):
    return pltpu.CompilerParams(dimension_semantics=sem, vmem_limit_bytes=VMEM_LIMIT)


def _const_spec(shape):
    nd = len(shape)
    return pl.BlockSpec(shape, lambda *_: (0,) * nd, pipeline_mode=pl.Buffered(1))


def _layer_norm(r, g, b):
    rc = r - jnp.mean(r, axis=-1, keepdims=True)
    var = jnp.mean(rc * rc, axis=-1, keepdims=True)
    return rc * lax.rsqrt(var + LN_EPS) * g + b


def _gelu_tanh(x):
    c = 0.7978845608028654
    return 0.5 * x * (1.0 + jnp.tanh(c * (x + 0.044715 * (x * x * x))))


def _bdot(a, b):
    return jnp.dot(a, b, preferred_element_type=F32)


def _ada_kernel(c_ref, w_ref, b_ref, o_ref):
    c = c_ref[...]
    s = c * jax.nn.sigmoid(c)
    o_ref[...] = jnp.dot(s, w_ref[...], preferred_element_type=F32,
                         precision=lax.Precision.HIGHEST) + b_ref[...]


def _ada_mod(c_all, ada_w, ada_b):
    nb = c_all.shape[0]
    six_d = ada_w.shape[-1]
    tn = D_MODEL
    return pl.pallas_call(
        _ada_kernel,
        grid=(DEPTH, six_d // tn),
        in_specs=[
            pl.BlockSpec((nb, D_MODEL), lambda l, j: (0, 0)),
            pl.BlockSpec((None, D_MODEL, tn), lambda l, j: (l, 0, j)),
            pl.BlockSpec((None, 1, tn), lambda l, j: (l, 0, j)),
        ],
        out_specs=pl.BlockSpec((None, nb, tn), lambda l, j: (l, 0, j)),
        out_shape=jax.ShapeDtypeStruct((DEPTH, nb, six_d), F32),
        compiler_params=_cparams(("arbitrary", "arbitrary")),
        name="ada_mod",
    )(c_all, ada_w, ada_b.reshape(DEPTH, 1, six_d))


def _gmlp_kernel(x_ref, mod_ref, w_in_ref, b_in_ref, vng_ref, vnb_ref, ws_ref, bs_ref,
                 w_out_ref, b_out_ref, lng_ref, lnb_ref, *rest, write_v):
    if write_v:
        x1_ref, h2_ref, v_ref, u_s, v_s, p_s = rest
    else:
        x1_ref, h2_ref, u_s, v_s, p_s = rest
    x = x_ref[...]
    sh1, sc1, g1 = mod_ref[0:1, :], mod_ref[1:2, :], mod_ref[2:3, :]
    sh2, sc2 = mod_ref[3:4, :], mod_ref[4:5, :]
    h = (x * (1.0 + sc1) + sh1).astype(BF16)
    tm = x.shape[0]
    ch = 512
    n_ch = A_HALF // ch
    for j in range(n_ch):
        sl = slice(j * ch, (j + 1) * ch)
        u_s[:, sl] = _gelu_tanh(_bdot(h, w_in_ref[:, sl]) + b_in_ref[:, sl])
    tot = jnp.zeros((tm, 1), F32)
    for j in range(n_ch):
        sl = slice(j * ch, (j + 1) * ch)
        slw = slice(A_HALF + j * ch, A_HALF + (j + 1) * ch)
        g = _gelu_tanh(_bdot(h, w_in_ref[:, slw]) + b_in_ref[:, slw])
        v_s[:, sl] = g
        tot = tot + jnp.sum(g, axis=-1, keepdims=True)
    mean = tot * (1.0 / A_HALF)
    sq = jnp.zeros((tm, 1), F32)
    for j in range(n_ch):
        sl = slice(j * ch, (j + 1) * ch)
        c = v_s[:, sl] - mean
        sq = sq + jnp.sum(c * c, axis=-1, keepdims=True)
    rstd = lax.rsqrt(sq * (1.0 / A_HALF) + LN_EPS)
    for g in range(A_GROUPS):
        sl = slice(g * A_GROUP_W, (g + 1) * A_GROUP_W)
        vn = (v_s[:, sl] - mean) * rstd * vng_ref[:, sl] + vnb_ref[:, sl]
        if write_v:
            v_ref[:, sl] = vn
        sg = _bdot(ws_ref[g], vn.astype(BF16)) + bs_ref[:, g:g + 1]
        p_s[:, sl] = (u_s[:, sl] * sg).astype(BF16)
    y = _bdot(p_s[...], w_out_ref[...]) + b_out_ref[...]
    x1 = _layer_norm(DN_ALPHA * x + g1 * y, lng_ref[...], lnb_ref[...])
    x1_ref[...] = x1
    h2_ref[...] = x1 * (1.0 + sc2) + sh2


def _gmlp_layer(x, mod, w_in, b_in, vn_g, vn_b, w_s, b_s, w_out, b_out, ln_g, ln_b, *, tm, write_v):
    bsz, s, d = x.shape
    seg = min(s, A_CHUNK)
    idx = jnp.arange(seg)
    mask = (idx[:, None] // CHUNK) >= (idx[None, :] // CHUNK)
    wm = jnp.where(mask[None], w_s[:, :seg, :seg], 0.0)
    reps = tm // seg
    eye = jnp.eye(reps, dtype=F32)
    ws_big = jnp.einsum("ab,gij->gaibj", eye, wm).reshape(A_GROUPS, tm, tm).astype(BF16)
    bs_big = jnp.tile(b_s[:, :seg].T, (reps, 1))
    out_shape = [jax.ShapeDtypeStruct((bsz, s, d), F32), jax.ShapeDtypeStruct((bsz, s, d), F32)]
    tile = lambda w: pl.BlockSpec((None, tm, w), lambda b, i: (b, i, 0))
    out_specs = [tile(d), tile(d)]
    if write_v:
        out_shape.append(jax.ShapeDtypeStruct((bsz, s, A_HALF), F32))
        out_specs.append(tile(A_HALF))
    return pl.pallas_call(
        functools.partial(_gmlp_kernel, write_v=write_v),
        grid=(bsz, s // tm),
        in_specs=[
            tile(d),
            pl.BlockSpec((None, 6, d), lambda b, i: (b, 0, 0)),
            _const_spec((d, 2 * A_HALF)),
            _const_spec((1, 2 * A_HALF)),
            _const_spec((1, A_HALF)),
            _const_spec((1, A_HALF)),
            _const_spec((A_GROUPS, tm, tm)),
            _const_spec((tm, A_GROUPS)),
            _const_spec((A_HALF, d)),
            _const_spec((1, d)),
            _const_spec((1, d)),
            _const_spec((1, d)),
        ],
        out_specs=out_specs,
        out_shape=out_shape,
        scratch_shapes=[pltpu.VMEM((tm, A_HALF), F32), pltpu.VMEM((tm, A_HALF), F32),
                        pltpu.VMEM((tm, A_HALF), BF16)],
        compiler_params=_cparams(("arbitrary", "arbitrary")),
        name="gmlp_layer",
    )(x, mod, w_in.astype(BF16), b_in.reshape(1, -1), vn_g.reshape(1, -1), vn_b.reshape(1, -1),
      ws_big, bs_big, w_out.astype(BF16), b_out.reshape(1, -1), ln_g.reshape(1, -1), ln_b.reshape(1, -1))


ROW_CHUNKS = D_MODEL // 128


def _router_kernel(hp_ref, hs_ref, wr_ref, br_ref, tri_ref, e_ref, g_ref, r_ref, cnt_ref, h3_ref, base_s,
                   *, n_first):
    i = pl.program_id(0)

    @pl.when(i == 0)
    def _():
        base_s[...] = jnp.zeros_like(base_s)

    h = jnp.where(i < n_first, hp_ref[...], hs_ref[...])
    rows = h.shape[0]
    for c in range(ROW_CHUNKS):
        h3_ref[pl.ds(c, rows, stride=ROW_CHUNKS), :] = h[:, c * 128:(c + 1) * 128]
    logits = lax.dot_general(wr_ref[...], h, (((1,), (1,)), ((), ())),
                             preferred_element_type=F32,
                             precision=lax.Precision.HIGHEST) + br_ref[...]
    rows = h.shape[0]
    iota = lax.broadcasted_iota(jnp.int32, (N_EXPERTS, rows), 0)
    l = logits
    vals, hots = [], []
    for k in range(TOP_K):
        m = jnp.max(l, axis=0, keepdims=True)
        idx = jnp.min(jnp.where(l == m, iota, N_EXPERTS), axis=0, keepdims=True)
        hot = iota == idx
        vals.append(m)
        hots.append(hot)
        e_ref[k:k + 1, :] = idx
        l = jnp.where(hot, -jnp.inf, l)
    exps = [jnp.exp(v - vals[0]) for v in vals]
    den = exps[0] + exps[1] + exps[2] + exps[3]
    for k in range(TOP_K):
        g_ref[k:k + 1, :] = exps[k] / den
    hot_all = jnp.where(hots[0] | hots[1] | hots[2] | hots[3], 1.0, 0.0)
    before = _bdot(hot_all.astype(BF16), tri_ref[...]) + base_s[:, 0:1]
    for k in range(TOP_K):
        r_ref[k:k + 1, :] = jnp.sum(jnp.where(hots[k], before, 0.0), axis=0, keepdims=True).astype(jnp.int32)
    base_s[...] = base_s[...] + jnp.sum(hot_all, axis=1, keepdims=True)
    cnt_ref[...] = base_s[...].astype(jnp.int32)


def _router(h_first, h_second, w_router, b_router):
    tr = ROUTER_ROWS
    n_first = h_first.shape[0] // tr
    n_second = h_second.shape[0] // tr
    n = h_first.shape[0] + h_second.shape[0]
    tri = (jnp.arange(tr)[:, None] < jnp.arange(tr)[None, :]).astype(BF16)
    sel = pl.BlockSpec((TOP_K, tr), lambda i: (0, i))
    top_e, gates, rank, cnt, h3 = pl.pallas_call(
        functools.partial(_router_kernel, n_first=n_first),
        grid=(n_first + n_second,),
        in_specs=[
            pl.BlockSpec((tr, D_MODEL), lambda i: (jnp.minimum(i, n_first - 1), 0)),
            pl.BlockSpec((tr, D_MODEL), lambda i: (jnp.maximum(i - n_first, 0), 0)),
            _const_spec((N_EXPERTS, D_MODEL)),
            _const_spec((N_EXPERTS, 1)),
            _const_spec((tr, tr)),
        ],
        out_specs=[sel, sel, sel, pl.BlockSpec((N_EXPERTS, 128), lambda i: (0, 0)),
                   pl.BlockSpec((tr * ROW_CHUNKS, 128), lambda i: (i, 0))],
        out_shape=[jax.ShapeDtypeStruct((TOP_K, n), jnp.int32), jax.ShapeDtypeStruct((TOP_K, n), F32),
                   jax.ShapeDtypeStruct((TOP_K, n), jnp.int32),
                   jax.ShapeDtypeStruct((N_EXPERTS, 128), jnp.int32),
                   jax.ShapeDtypeStruct((n * ROW_CHUNKS, 128), F32)],
        scratch_shapes=[pltpu.VMEM((N_EXPERTS, 128), F32)],
        compiler_params=_cparams(("arbitrary",)),
        name="moe_router",
    )(h_first, h_second, w_router.T, b_router.reshape(N_EXPERTS, 1), tri)
    return top_e, gates, rank, cnt[:, 0], h3


LANES = 128
PREP_ROWS = 64


def _pair_perm(v):
    shp = v.shape
    v4 = v.reshape(shp[:-1] + (shp[-1] // LANES, 2, LANES // 2))
    return jnp.swapaxes(v4, -1, -2).reshape(shp)


ROW_BUFS = 4


def _expert_kernel(be_ref, nact_ref, src0_ref, src1_ref, src2_ref, dstp_ref, dstc_ref, h3_ref,
                   win_ref, bg_ref, bl_ref, wout_ref, bo_ref, ys_ref,
                   wg_s, wl_s, wo_t, wo_s, *bufs_and_sems):
    xbufs, ybufs = bufs_and_sems[:ROW_BUFS], bufs_and_sems[ROW_BUFS:2 * ROW_BUFS]
    gsem, ssem = bufs_and_sems[2 * ROW_BUFS:]
    b = pl.program_id(0)
    nact = nact_ref[0]
    active = b < nact
    last = b == nact - 1
    blk = xbufs[0].shape[0]
    tmb = blk // ROW_CHUNKS
    fresh = jnp.logical_or(b == 0, be_ref[b] != be_ref[jnp.maximum(b - 1, 0)])

    def row_tile(ref, row):
        return ref.at[pl.ds(pl.multiple_of(row * ROW_CHUNKS, ROW_CHUNKS), ROW_CHUNKS)]

    def start_gather(src_ref, s):
        for r in range(tmb):
            pltpu.make_async_copy(row_tile(h3_ref, src_ref[0, 0, r]), xbufs[s].at[pl.ds(r * ROW_CHUNKS, ROW_CHUNKS)],
                                  gsem.at[s]).start(priority=r % 2)

    def wait_gather(s):
        pltpu.make_async_copy(h3_ref.at[pl.ds(0, blk)], xbufs[s], gsem.at[s]).wait()

    def start_scatter(dst_ref, s):
        for r in range(tmb):
            pltpu.make_async_copy(ybufs[s].at[pl.ds(r * ROW_CHUNKS, ROW_CHUNKS)], row_tile(ys_ref, dst_ref[0, 0, r]),
                                  ssem.at[s]).start(priority=r % 2)

    def wait_scatter(s):
        pltpu.make_async_copy(ybufs[s], ys_ref.at[pl.ds(0, blk)], ssem.at[s]).wait()

    def chunk_rows(c):
        return pl.ds(c, tmb, stride=ROW_CHUNKS)

    def on_phase(cond, fn):
        for q in range(ROW_BUFS):
            pl.when(jnp.logical_and(cond, b % ROW_BUFS == q))(functools.partial(fn, q))

    @pl.when(jnp.logical_and(active, b == 0))
    def _():
        start_gather(src0_ref, 0)
        start_gather(src1_ref, 1)

    @pl.when(jnp.logical_and(active, fresh))
    def _():
        even = lax.broadcasted_iota(jnp.int32, (PREP_ROWS, LANES), 1) % 2 == 0

        def split_rows(r, carry):
            rows = pl.ds(pl.multiple_of(r * PREP_ROWS, PREP_ROWS), PREP_ROWS)
            for c in range(D_EXPERT // LANES):
                blk_a = win_ref[rows, 2 * c * LANES:(2 * c + 1) * LANES]
                blk_b = win_ref[rows, (2 * c + 1) * LANES:(2 * c + 2) * LANES]
                wg_s[rows, c * LANES:(c + 1) * LANES] = jnp.where(
                    even, blk_a, pltpu.roll(blk_b, 1, 1)).astype(BF16)
                wl_s[rows, c * LANES:(c + 1) * LANES] = jnp.where(
                    even, pltpu.roll(blk_a, LANES - 1, 1), blk_b).astype(BF16)
            return carry

        lax.fori_loop(0, D_MODEL // PREP_ROWS, split_rows, 0)
        half = LANES // 2
        for c in range(D_EXPERT // LANES):
            for p in range(2):
                for cb in range(D_MODEL // LANES):
                    wo_t[cb, pl.ds(c * LANES + p, half, stride=2), :] = (
                        wout_ref[pl.ds(c * LANES + half * p, half), cb * LANES:(cb + 1) * LANES])

        def cast_rows(r, carry):
            rows = pl.ds(pl.multiple_of(r * PREP_ROWS, PREP_ROWS), PREP_ROWS)
            for cb in range(D_MODEL // LANES):
                wo_s[rows, cb * LANES:(cb + 1) * LANES] = wo_t[cb, rows, :].astype(BF16)
            return carry

        lax.fori_loop(0, D_EXPERT // PREP_ROWS, cast_rows, 0)

    on_phase(active, wait_gather)
    on_phase(jnp.logical_and(active, b >= ROW_BUFS), wait_scatter)

    def compute(p):
        x = jnp.concatenate([xbufs[p][chunk_rows(c), :] for c in range(ROW_CHUNKS)], axis=1).astype(BF16)
        zg = _bdot(x, wg_s[...]) + bg_ref[...]
        zl = _bdot(x, wl_s[...]) + bl_ref[...]
        glu = jnp.minimum(zg, SWIGLU_LIMIT)
        lin = jnp.clip(zl, -SWIGLU_LIMIT, SWIGLU_LIMIT)
        a = glu * jax.nn.sigmoid(SWIGLU_ALPHA * glu) * (lin + 1.0)
        y = _bdot(a.astype(BF16), wo_s[...]) + bo_ref[...]
        for c in range(ROW_CHUNKS):
            ybufs[p][chunk_rows(c), :] = y[:, c * LANES:(c + 1) * LANES]

    @pl.when(jnp.logical_and(active, b == 0))
    def _():
        start_gather(src2_ref, 2)
        compute(0)

    def step(q):
        start_gather(src2_ref, (q + 2) % ROW_BUFS)
        start_scatter(dstp_ref, (q - 1) % ROW_BUFS)
        compute(q)

    on_phase(jnp.logical_and(active, b > 0), step)

    def drain(q):
        start_scatter(dstc_ref, q)
        wait_gather((q + 1) % ROW_BUFS)
        wait_gather((q + 2) % ROW_BUFS)
        wait_scatter(q)

    on_phase(last, drain)
    for back in range(1, ROW_BUFS):
        on_phase(jnp.logical_and(last, b >= back), lambda q, back=back: wait_scatter((q - back) % ROW_BUFS))

    @pl.when(last)
    def _():
        spare0 = ys_ref.shape[0] - 2 * blk
        fills = []
        for s in range(2):
            ybufs[s][...] = jnp.zeros_like(ybufs[s])
            fills.append(pltpu.make_async_copy(ybufs[s], ys_ref.at[pl.ds(spare0 + s * blk, blk)], ssem.at[s]))
        for cp in fills:
            cp.start()
        for cp in fills:
            cp.wait()


def _experts(h3, src_rows, dst_rows, block_e, nact, n_slots, layer, w_in, b_glu, b_lin, w_out, b_out):
    tmb = EXPERT_ROWS
    n_blocks = src_rows.shape[0]
    exp3 = lambda b, be, na: (be[b], 0, 0)
    exp4 = lambda b, be, na: (layer, be[b], 0, 0)
    idx_spec = lambda f: pl.BlockSpec((1, 1, tmb), lambda b, be, na: (f(b), 0, 0), memory_space=pltpu.SMEM)
    grid_spec = pltpu.PrefetchScalarGridSpec(
        num_scalar_prefetch=2,
        grid=(n_blocks,),
        in_specs=[
            idx_spec(lambda b: b),
            idx_spec(lambda b: jnp.minimum(b + 1, n_blocks - 1)),
            idx_spec(lambda b: jnp.minimum(b + 2, n_blocks - 1)),
            idx_spec(lambda b: jnp.maximum(b - 1, 0)),
            idx_spec(lambda b: b),
            pl.BlockSpec(memory_space=pl.ANY),
            pl.BlockSpec((None, None, D_MODEL, 2 * D_EXPERT), exp4),
            pl.BlockSpec((None, 1, D_EXPERT), exp3),
            pl.BlockSpec((None, 1, D_EXPERT), exp3),
            pl.BlockSpec((None, None, D_EXPERT, D_MODEL), exp4),
            pl.BlockSpec((None, 1, D_MODEL), exp3),
        ],
        out_specs=pl.BlockSpec(memory_space=pl.ANY),
        scratch_shapes=[pltpu.VMEM((D_MODEL, D_EXPERT), BF16), pltpu.VMEM((D_MODEL, D_EXPERT), BF16),
                        pltpu.VMEM((D_MODEL // LANES, D_EXPERT, LANES), F32),
                        pltpu.VMEM((D_EXPERT, D_MODEL), BF16)]
        + [pltpu.VMEM((tmb * ROW_CHUNKS, LANES), F32)] * (2 * ROW_BUFS)
        + [pltpu.SemaphoreType.DMA((ROW_BUFS,)), pltpu.SemaphoreType.DMA((ROW_BUFS,))],
    )
    return pl.pallas_call(
        _expert_kernel,
        grid_spec=grid_spec,
        out_shape=jax.ShapeDtypeStruct((n_slots * ROW_CHUNKS, LANES), F32),
        compiler_params=_cparams(("arbitrary",)),
        name="moe_experts",
    )(block_e, nact, src_rows, src_rows, src_rows, dst_rows, dst_rows, h3, w_in, b_glu, b_lin, w_out, b_out)


def _moe(h_first, h_second, layer, w_router, b_router, w_in, b_in, w_out, b_out):
    tmb = EXPERT_ROWS
    top_e, gates, rank, counts, h3 = _router(h_first, h_second, w_router, b_router)
    n = h3.shape[0] // ROW_CHUNKS
    m = n * TOP_K
    padded = (counts + tmb - 1) // tmb * tmb
    pend = jnp.cumsum(padded)
    pstart = pend - padded
    hot = top_e[..., None] == jnp.arange(N_EXPERTS)[None, None, :]
    dest = jnp.sum(jnp.where(hot, pstart[None, None, :], 0), axis=-1) + rank
    n_blocks = m // tmb + N_EXPERTS
    blk_row0 = jnp.arange(n_blocks, dtype=jnp.int32) * tmb
    block_e = jnp.minimum(jnp.sum(pend[None, :] <= blk_row0[:, None], axis=1), N_EXPERTS - 1).astype(jnp.int32)
    nact = (pend[-1] // tmb).astype(jnp.int32).reshape(1)
    slot = jnp.arange(n, dtype=jnp.int32)[None, :] + n * jnp.arange(TOP_K, dtype=jnp.int32)[:, None]
    j = jnp.arange(N_EXPERTS * tmb, dtype=jnp.int32)
    off = j % tmb
    pad = padded - counts
    rep = lambda v: jnp.repeat(v, tmb)
    pad_key = jnp.where(off < rep(pad), rep(pstart + counts) + off, pend[-1] + j - rep(jnp.cumsum(pad)))
    keys = jnp.concatenate([dest.reshape(m), pad_key]).astype(jnp.int32)
    vals = jnp.concatenate([slot.reshape(m), jnp.full_like(j, -1)])
    _, row_slot = lax.sort((keys, vals), num_keys=1, is_stable=False)
    row = jnp.arange(n_blocks * tmb, dtype=jnp.int32)
    src_rows = jnp.where(row_slot < 0, 0, row_slot % n)
    dst_rows = jnp.where(row_slot < 0, m + ((row // tmb) % 2) * tmb + row % tmb, row_slot)
    b_glu = _pair_perm(b_in[:, 0::2]).reshape(N_EXPERTS, 1, D_EXPERT)
    b_lin = _pair_perm(b_in[:, 1::2]).reshape(N_EXPERTS, 1, D_EXPERT)
    ys = _experts(h3, src_rows.reshape(n_blocks, 1, tmb), dst_rows.reshape(n_blocks, 1, tmb), block_e, nact,
                  m + 2 * tmb, layer, w_in, b_glu, b_lin, w_out, b_out.reshape(N_EXPERTS, 1, D_MODEL))
    return ys, gates.T


def _post_kernel(x1_ref, *rest, with_kv):
    ys_refs = rest[:TOP_K]
    gate_ref, mod_ref, lng_ref, lnb_ref = rest[TOP_K:TOP_K + 4]
    rest = rest[TOP_K + 4:]
    g2 = mod_ref[5:6, :]
    tm = x1_ref.shape[0]
    f = None
    for k in range(TOP_K):
        yk = jnp.concatenate([ys_refs[k][pl.ds(c, tm, stride=ROW_CHUNKS), :] for c in range(ROW_CHUNKS)], axis=1)
        term = yk * gate_ref[:, k:k + 1]
        f = term if f is None else f + term
    x2 = _layer_norm(DN_ALPHA * x1_ref[...] + g2 * f, lng_ref[...], lnb_ref[...])
    if not with_kv:
        (x2_ref,) = rest
        x2_ref[...] = x2
        return
    wa_ref, kng_ref, cos_ref, sin_ref, x2_ref, lat_ref, kr_ref, kcat_ref = rest
    x2_ref[...] = x2
    kv = _bdot(x2.astype(BF16), wa_ref[...])
    c = kv[:, :KV_RANK]
    lat = c * lax.rsqrt(jnp.mean(c * c, axis=-1, keepdims=True) + RMS_EPS) * kng_ref[...]
    k = kv[:, KV_RANK:KV_RANK + QK_ROPE]
    k_swapped = kv[:, KV_RANK + QK_ROPE:]
    kr = k * cos_ref[...] + k_swapped * sin_ref[...]
    lat_ref[...] = lat
    kr_ref[...] = kr
    kcat_ref[:, :KV_RANK] = lat.astype(BF16)
    kcat_ref[:, KV_RANK:] = kr.astype(BF16)


def _swap_halves_cols(w, width):
    shp = w.shape
    w4 = w.reshape(shp[:-1] + (shp[-1] // width, 2, width // 2))
    return w4[..., ::-1, :].reshape(shp)


def _rope_tables(pos, reps):
    inv = 1.0 / (ROPE_BASE ** (jnp.arange(ROPE_HALF, dtype=F32) * (2.0 / QK_ROPE)))
    ang = pos.astype(F32)[:, None] * inv[None, :]
    cos, sin = jnp.cos(ang), jnp.sin(ang)
    cos_t = jnp.tile(jnp.concatenate([cos, cos], -1), (1, reps))
    sin_t = jnp.tile(jnp.concatenate([-sin, sin], -1), (1, reps))
    return cos_t, sin_t


def _post_layer(x1, ys, gates, row0, mod, ln_g, ln_b, *, tm, kv=None):
    bsz, s, d = x1.shape
    blk0 = row0 // tm
    per_b = s // tm
    tile = lambda w: pl.BlockSpec((None, tm, w), lambda b, i: (b, i, 0))
    n_tok = gates.shape[0]
    pick_spec = lambda k: pl.BlockSpec((tm * ROW_CHUNKS, LANES),
                                       lambda b, i: (k * (n_tok // tm) + blk0 + b * per_b + i, 0))
    in_specs = [tile(d)] + [pick_spec(k) for k in range(TOP_K)] + [
        pl.BlockSpec((tm, TOP_K), lambda b, i: (blk0 + b * per_b + i, 0)),
        pl.BlockSpec((None, 6, d), lambda b, i: (b, 0, 0)),
        _const_spec((1, d)),
        _const_spec((1, d)),
    ]
    args = [x1] + [ys] * TOP_K + [gates, mod, ln_g.reshape(1, -1), ln_b.reshape(1, -1)]
    out_shape = [jax.ShapeDtypeStruct((bsz, s, d), F32)]
    out_specs = [tile(d)]
    if kv is not None:
        w_a, kn_g, cos_t, sin_t = kv
        w_ext = jnp.concatenate([w_a, _swap_halves_cols(w_a[:, KV_RANK:], QK_ROPE)], axis=1).astype(BF16)
        in_specs += [_const_spec(w_ext.shape), _const_spec((1, KV_RANK)),
                     pl.BlockSpec((tm, QK_ROPE), lambda b, i: (i, 0)),
                     pl.BlockSpec((tm, QK_ROPE), lambda b, i: (i, 0))]
        args += [w_ext, kn_g.reshape(1, -1), cos_t, sin_t]
        out_shape += [jax.ShapeDtypeStruct((bsz, s, KV_RANK), F32), jax.ShapeDtypeStruct((bsz, s, QK_ROPE), F32),
                      jax.ShapeDtypeStruct((bsz, s, QK_LAT), BF16)]
        out_specs += [tile(KV_RANK), tile(QK_ROPE), tile(QK_LAT)]
    return pl.pallas_call(
        functools.partial(_post_kernel, with_kv=kv is not None),
        grid=(bsz, per_b),
        in_specs=in_specs,
        out_specs=out_specs,
        out_shape=out_shape,
        compiler_params=_cparams(("arbitrary", "arbitrary")),
        name="post_moe",
    )(*args)


Q_GROUP = 512
KEY_TILE = 512


def _qproj_kernel(x_ref, mod_ref, wdq_ref, qng_ref, wqn_ref, wqr_ref, wqrs_ref, wuk_ref, cos_ref, sin_ref, q_ref,
                  *, tq):
    x = x_ref[...]
    sh1, sc1 = mod_ref[0:1, :], mod_ref[1:2, :]
    h = (x * (1.0 + sc1) + sh1).astype(BF16)
    tm = x.shape[0]
    cq = lax.dot_general(wdq_ref[...], h, (((1,), (1,)), ((), ())), preferred_element_type=F32)
    cq = (cq * lax.rsqrt(jnp.mean(cq * cq, axis=0, keepdims=True) + RMS_EPS) * qng_ref[...]).astype(BF16)
    qn = _bdot(wqn_ref[...], cq)
    qr = _bdot(wqr_ref[...], cq)
    qrs = _bdot(wqrs_ref[...], cq)
    rope = (qr * cos_ref[...] + qrs * sin_ref[...]).astype(BF16)
    per_group = Q_GROUP // tq
    for hd in range(MLA_HEADS):
        ql = _bdot(wuk_ref[hd], qn[hd * QK_NOPE:(hd + 1) * QK_NOPE, :].astype(BF16)).astype(BF16)
        g, off = hd // per_group, (hd % per_group) * tq
        for t in range(tm // tq):
            q_ref[t, g, 0:KV_RANK, off:off + tq] = ql[:, t * tq:(t + 1) * tq]
            q_ref[t, g, KV_RANK:QK_LAT, off:off + tq] = rope[hd * QK_ROPE:(hd + 1) * QK_ROPE, t * tq:(t + 1) * tq]


def _qproj_layer(x, mod, w_dq, qn_g, w_qb, w_uk, cos_t, sin_t, *, tm, tq):
    bsz, s, d = x.shape
    w_qn = w_qb[:, :, :QK_NOPE].reshape(Q_RANK, MLA_HEADS * QK_NOPE)
    w_qr = w_qb[:, :, QK_NOPE:].reshape(Q_RANK, MLA_HEADS * QK_ROPE)
    w_qrs = _swap_halves_cols(w_qr, QK_ROPE)
    wuk_h = jnp.transpose(w_uk, (1, 0, 2)).astype(BF16)
    hr = MLA_HEADS * QK_ROPE
    n_groups = MLA_HEADS * tq // Q_GROUP
    nt = tm // tq
    return pl.pallas_call(
        functools.partial(_qproj_kernel, tq=tq),
        grid=(bsz, s // tm),
        in_specs=[
            pl.BlockSpec((None, tm, d), lambda b, i: (b, i, 0)),
            pl.BlockSpec((None, 6, d), lambda b, i: (b, 0, 0)),
            _const_spec((Q_RANK, d)),
            _const_spec((Q_RANK, 1)),
            _const_spec((MLA_HEADS * QK_NOPE, Q_RANK)),
            _const_spec((hr, Q_RANK)),
            _const_spec((hr, Q_RANK)),
            _const_spec(wuk_h.shape),
            pl.BlockSpec((hr, tm), lambda b, i: (0, i)),
            pl.BlockSpec((hr, tm), lambda b, i: (0, i)),
        ],
        out_specs=pl.BlockSpec((None, nt, n_groups, QK_LAT, Q_GROUP), lambda b, i: (b, i, 0, 0, 0)),
        out_shape=jax.ShapeDtypeStruct((bsz, s // tq, n_groups, QK_LAT, Q_GROUP), BF16),
        compiler_params=_cparams(("arbitrary", "arbitrary")),
        name="mla_qproj",
    )(x, mod, w_dq.T.astype(BF16), qn_g.reshape(-1, 1), w_qn.T.astype(BF16), w_qr.T.astype(BF16),
      w_qrs.T.astype(BF16), wuk_h, cos_t.T, sin_t.T)


EXP2_SCALE = ATTN_SCALE * 1.4426950408889634


def _attn_kernel(q_ref, *rest, tq, causal, n_main, tail, split_keys):
    if split_keys:
        klat_ref, kkr_ref = rest[:2]
        rest = rest[2:]
    else:
        k_ref = rest[0]
        rest = rest[1:]
    if tail:
        kt_ref = rest[0]
        rest = rest[1:]
    x_ref, mod_ref, wuv_ref, wo_ref, lng_ref, lnb_ref, x1_ref, h2_ref, m_s, l_s, acc_s, ta_s, tb_s = rest
    i = pl.program_id(1)
    n_groups = q_ref.shape[0]
    per_group = Q_GROUP // tq
    m_s[...] = jnp.full_like(m_s, NEG_BIG)
    l_s[...] = jnp.zeros_like(l_s)
    acc_s[...] = jnp.zeros_like(acc_s)

    def tile_rows(j):
        return pl.ds(pl.multiple_of(j * KEY_TILE, KEY_TILE), KEY_TILE)

    def key_rows(j):
        if split_keys:
            return jnp.concatenate([klat_ref[tile_rows(j), :].astype(BF16), kkr_ref[tile_rows(j), :].astype(BF16)],
                                   axis=1)
        return k_ref[tile_rows(j), :]

    def value_rows(j):
        if split_keys:
            return klat_ref[tile_rows(j), :].astype(BF16)
        return k_ref[tile_rows(j), :KV_RANK]

    def softmax_step(g, t, v, key0, masked):
        t = t * EXP2_SCALE
        if masked:
            n = t.shape[0]
            k_chunk = (key0 + lax.broadcasted_iota(jnp.int32, (n, 1), 0)) // CHUNK
            q_chunk = (i * tq + lax.broadcasted_iota(jnp.int32, (1, Q_GROUP), 1) % tq) // CHUNK
            t = jnp.where(k_chunk <= q_chunk, t, -jnp.inf)
        m_old = m_s[g]
        m_new = jnp.maximum(m_old, jnp.max(t, axis=0, keepdims=True))
        alpha = jnp.exp2(m_old - m_new)
        p = jnp.exp2(t - m_new)
        l_s[g] = alpha * l_s[g] + jnp.sum(p, axis=0, keepdims=True)
        acc_s[g] = alpha * acc_s[g] + lax.dot_general(
            v, p.astype(BF16), (((0,), (0,)), ((), ())), preferred_element_type=F32)
        m_s[g] = m_new

    def stage(cur, nxt, j, masked, make_next):
        v = value_rows(j)
        k_next = key_rows(j + 1) if make_next else None
        for g in range(n_groups):
            if make_next:
                nxt[g] = _bdot(k_next, q_ref[g])
            softmax_step(g, cur[g], v, j * KEY_TILE, masked)

    n_open = (i * tq) // KEY_TILE if causal else n_main - 1
    k0 = key_rows(0)
    for g in range(n_groups):
        ta_s[g] = _bdot(k0, q_ref[g])

    def pair_body(p, c):
        stage(ta_s, tb_s, 2 * p, False, True)
        stage(tb_s, ta_s, 2 * p + 1, False, True)
        return c

    lax.fori_loop(0, n_open // 2, pair_body, 0)
    if causal:
        @pl.when(n_open % 2 == 0)
        def _():
            stage(ta_s, tb_s, n_open, True, False)

        @pl.when(n_open % 2 == 1)
        def _():
            stage(ta_s, tb_s, n_open - 1, False, True)
            stage(tb_s, ta_s, n_open, True, False)
    elif n_open % 2 == 0:
        stage(ta_s, tb_s, n_open, False, False)
    else:
        stage(ta_s, tb_s, n_open - 1, False, True)
        stage(tb_s, ta_s, n_open, False, False)
    if tail:
        kt = kt_ref[...]
        for g in range(n_groups):
            softmax_step(g, _bdot(kt, q_ref[g]), kt[:, :KV_RANK], n_main * KEY_TILE, False)

    heads = []
    for g in range(n_groups):
        o = (acc_s[g] / l_s[g]).T
        for hh in range(per_group):
            hd = g * per_group + hh
            heads.append(_bdot(o[hh * tq:(hh + 1) * tq].astype(BF16), wuv_ref[hd]))
    oc = jnp.concatenate(heads, axis=-1).astype(BF16)
    y = _bdot(oc, wo_ref[...])
    x = x_ref[...]
    g1, sh2, sc2 = mod_ref[2:3, :], mod_ref[3:4, :], mod_ref[4:5, :]
    x1 = _layer_norm(DN_ALPHA * x + g1 * y, lng_ref[...], lnb_ref[...])
    x1_ref[...] = x1
    h2_ref[...] = x1 * (1.0 + sc2) + sh2


def _attn_layer(q, keys, x, mod, w_uv, w_o, ln_g, ln_b, *, tq, causal, k_tail=None):
    bsz, s, d = x.shape
    split_keys = isinstance(keys, tuple)
    keys = keys if split_keys else (keys,)
    t_main = keys[0].shape[1]
    n_main = t_main // KEY_TILE
    n_groups = q.shape[2]
    tail = k_tail is not None
    wuv_t = jnp.transpose(w_uv, (1, 0, 2)).astype(BF16)
    tile = pl.BlockSpec((None, tq, d), lambda b, i: (b, i, 0))
    in_specs = [pl.BlockSpec((None, None, n_groups, QK_LAT, Q_GROUP), lambda b, i: (b, i, 0, 0, 0))]
    in_specs += [pl.BlockSpec((None, t_main, k.shape[2]), lambda b, i: (b, 0, 0)) for k in keys]
    args = [q, *keys]
    if tail:
        in_specs.append(pl.BlockSpec((None, k_tail.shape[1], QK_LAT), lambda b, i: (b, 0, 0)))
        args.append(k_tail)
    in_specs += [
        tile,
        pl.BlockSpec((None, 6, d), lambda b, i: (b, 0, 0)),
        _const_spec(wuv_t.shape),
        _const_spec((MLA_HEADS * V_DIM, d)),
        _const_spec((1, d)),
        _const_spec((1, d)),
    ]
    args += [x, mod, wuv_t, w_o.astype(BF16), ln_g.reshape(1, -1), ln_b.reshape(1, -1)]
    return pl.pallas_call(
        functools.partial(_attn_kernel, tq=tq, causal=causal, n_main=n_main, tail=tail, split_keys=split_keys),
        grid=(bsz, s // tq),
        in_specs=in_specs,
        out_specs=[tile, tile],
        out_shape=[jax.ShapeDtypeStruct((bsz, s, d), F32), jax.ShapeDtypeStruct((bsz, s, d), F32)],
        scratch_shapes=[pltpu.VMEM((n_groups, 1, Q_GROUP), F32), pltpu.VMEM((n_groups, 1, Q_GROUP), F32),
                        pltpu.VMEM((n_groups, KV_RANK, Q_GROUP), F32),
                        pltpu.VMEM((n_groups, KEY_TILE, Q_GROUP), F32),
                        pltpu.VMEM((n_groups, KEY_TILE, Q_GROUP), F32)],
        compiler_params=_cparams(("arbitrary", "arbitrary")),
        name="mla_attention",
    )(*args)


def kernel(x_prompt, x_sample, cache_kv_latent, cache_k_rope, c_prompt, c_sample, ada_w, ada_b, ln_g, ln_b,
           a_w_in, a_b_in, a_vn_g, a_vn_b, a_w_s, a_b_s, a_w_out, a_b_out, kv_w_a, kv_norm_g, kv_w_uk, kv_w_uv,
           q_w_a, q_norm_g, q_w_b, q_w_o, moe_w_router, moe_b_router, moe_w_in, moe_b_in, moe_w_out, moe_b_out):
    bp, sp, d = x_prompt.shape
    bs, ss, _ = x_sample.shape
    past = cache_kv_latent.shape[1]
    n_p = bp * sp
    n_s = bs * ss

    mod = _ada_mod(jnp.concatenate([c_prompt, c_sample], axis=0), ada_w, ada_b)
    mod = mod.reshape(DEPTH, bp + bs, 6, d)
    mod_p, mod_s = mod[:, :bp], mod[:, bp:]

    pos_p = jnp.arange(sp, dtype=jnp.int32)
    pos_s = past + jnp.arange(ss, dtype=jnp.int32)

    def moe(l, h2_p, h2_s):
        return _moe(h2_p.reshape(n_p, d), h2_s.reshape(n_s, d), l, moe_w_router[l], moe_b_router[l],
                    moe_w_in, moe_b_in[l], moe_w_out, moe_b_out[l])

    gm = functools.partial(_gmlp_layer, w_in=a_w_in[0], b_in=a_b_in[0], vn_g=a_vn_g[0], vn_b=a_vn_b[0],
                           w_s=a_w_s[0], b_s=a_b_s[0], w_out=a_w_out[0], b_out=a_b_out[0],
                           ln_g=ln_g[0, 0], ln_b=ln_b[0, 0])
    x1_p, h2_p = gm(x_prompt, mod_p[0], tm=256, write_v=False)
    x1_s, h2_s, v_s = gm(x_sample, mod_s[0], tm=ss, write_v=True)
    ys, gates = moe(0, h2_p, h2_s)
    x2_p, lat_p, kr_p, kcat_p = _post_layer(x1_p, ys, gates, 0, mod_p[0], ln_g[0, 1], ln_b[0, 1], tm=512,
                                            kv=(kv_w_a, kv_norm_g) + _rope_tables(pos_p, 1))
    x2_s, lat_s, kr_s, kcat_s = _post_layer(x1_s, ys, gates, n_p, mod_s[0], ln_g[0, 1], ln_b[0, 1], tm=ss,
                                            kv=(kv_w_a, kv_norm_g) + _rope_tables(pos_s, 1))

    qp = functools.partial(_qproj_layer, w_dq=q_w_a[0], qn_g=q_norm_g[0], w_qb=q_w_b[0], w_uk=kv_w_uk)
    at = functools.partial(_attn_layer, w_uv=kv_w_uv, w_o=q_w_o[0], ln_g=ln_g[1, 0], ln_b=ln_b[1, 0])
    cos_p, sin_p = _rope_tables(pos_p, MLA_HEADS)
    cos_s, sin_s = _rope_tables(pos_s, MLA_HEADS)
    q_p = qp(x2_p, mod_p[1], cos_t=cos_p, sin_t=sin_p, tm=512, tq=128)
    q_s = qp(x2_s, mod_s[1], cos_t=cos_s, sin_t=sin_s, tm=ss, tq=ss)
    x3_p, h4_p = at(q_p, kcat_p, x2_p, mod_p[1], tq=128, causal=True)
    x3_s, h4_s = at(q_s, (cache_kv_latent, cache_k_rope), x2_s, mod_s[1], tq=ss, causal=False, k_tail=kcat_s)
    ys, gates = moe(1, h4_p, h4_s)
    (y_p,) = _post_layer(x3_p, ys, gates, 0, mod_p[1], ln_g[1, 1], ln_b[1, 1], tm=512)
    (y_s,) = _post_layer(x3_s, ys, gates, n_p, mod_s[1], ln_g[1, 1], ln_b[1, 1], tm=ss)

    return (y_p, y_s, lat_p, kr_p, lat_s, kr_s, v_s[None])
```

```python
import functools

import jax
import jax.numpy as jnp
from jax import lax
from jax.experimental import pallas as pl
from jax.experimental.pallas import tpu as pltpu

D_MODEL = 1024
DEPTH = 2
CHUNK = 64
A_CHUNK = 128
A_HALF = 2 * D_MODEL
A_GROUPS = 8
A_GROUP_W = A_HALF // A_GROUPS
MLA_HEADS = 8
QK_NOPE = 128
QK_ROPE = 64
ROPE_HALF = QK_ROPE // 2
V_DIM = 128
KV_RANK = 256
Q_RANK = 512
QK_LAT = KV_RANK + QK_ROPE
ROPE_BASE = 10000.0
ATTN_SCALE = (QK_NOPE + QK_ROPE) ** -0.5
N_EXPERTS = 32
TOP_K = 4
D_EXPERT = D_MODEL
SWIGLU_LIMIT = 7.0
SWIGLU_ALPHA = 1.702
DN_ALPHA = (2 * DEPTH) ** 0.25
LN_EPS = 1e-5
RMS_EPS = 1e-6

BF16 = jnp.bfloat16
F32 = jnp.float32

LANES = 128
ROW_CHUNKS = D_MODEL // LANES
VMEM_LIMIT = 56 * 1024 * 1024
EXPERT_ROWS = 256
ROUTER_ROWS = 512
NEG_BIG = -1e30


def _cparams(sem):
    return pltpu.CompilerParams(dimension_semantics=sem, vmem_limit_bytes=VMEM_LIMIT)


def _const_spec(shape):
    nd = len(shape)
    return pl.BlockSpec(shape, lambda *_: (0,) * nd, pipeline_mode=pl.Buffered(1))


def _layer_norm(r, g, b):
    rc = r - jnp.mean(r, axis=-1, keepdims=True)
    var = jnp.mean(rc * rc, axis=-1, keepdims=True)
    return rc * lax.rsqrt(var + LN_EPS) * g + b


def _gelu_tanh(x):
    c = 0.7978845608028654
    return 0.5 * x * (1.0 + jnp.tanh(c * (x + 0.044715 * (x * x * x))))


def _bdot(a, b):
    return jnp.dot(a, b, preferred_element_type=F32)


def _ada_kernel(c_ref, w_ref, b_ref, o_ref):
    c = c_ref[...]
    s = c * jax.nn.sigmoid(c)
    o_ref[...] = jnp.dot(s, w_ref[...], preferred_element_type=F32,
                         precision=lax.Precision.HIGHEST) + b_ref[...]


def _ada_mod(c_all, ada_w, ada_b):
    nb = c_all.shape[0]
    six_d = ada_w.shape[-1]
    tn = D_MODEL
    return pl.pallas_call(
        _ada_kernel,
        grid=(DEPTH, six_d // tn),
        in_specs=[
            pl.BlockSpec((nb, D_MODEL), lambda l, j: (0, 0)),
            pl.BlockSpec((None, D_MODEL, tn), lambda l, j: (l, 0, j)),
            pl.BlockSpec((None, 1, tn), lambda l, j: (l, 0, j)),
        ],
        out_specs=pl.BlockSpec((None, nb, tn), lambda l, j: (l, 0, j)),
        out_shape=jax.ShapeDtypeStruct((DEPTH, nb, six_d), F32),
        compiler_params=_cparams(("arbitrary", "arbitrary")),
        name="ada_mod",
    )(c_all, ada_w, ada_b.reshape(DEPTH, 1, six_d))


def _gmlp_kernel(x_ref, mod_ref, w_in_ref, b_in_ref, vng_ref, vnb_ref, ws_ref, bs_ref,
                 w_out_ref, b_out_ref, lng_ref, lnb_ref, *rest, write_v):
    if write_v:
        x1_ref, h2_ref, v_ref, u_s, v_s, p_s = rest
    else:
        x1_ref, h2_ref, u_s, v_s, p_s = rest
    x = x_ref[...]
    sh1, sc1, g1 = mod_ref[0:1, :], mod_ref[1:2, :], mod_ref[2:3, :]
    sh2, sc2 = mod_ref[3:4, :], mod_ref[4:5, :]
    h = (x * (1.0 + sc1) + sh1).astype(BF16)
    tm = x.shape[0]
    ch = 512
    n_ch = A_HALF // ch
    for j in range(n_ch):
        sl = slice(j * ch, (j + 1) * ch)
        u_s[:, sl] = _gelu_tanh(_bdot(h, w_in_ref[:, sl]) + b_in_ref[:, sl])
    tot = jnp.zeros((tm, 1), F32)
    for j in range(n_ch):
        sl = slice(j * ch, (j + 1) * ch)
        slw = slice(A_HALF + j * ch, A_HALF + (j + 1) * ch)
        g = _gelu_tanh(_bdot(h, w_in_ref[:, slw]) + b_in_ref[:, slw])
        v_s[:, sl] = g
        tot = tot + jnp.sum(g, axis=-1, keepdims=True)
    mean = tot * (1.0 / A_HALF)
    sq = jnp.zeros((tm, 1), F32)
    for j in range(n_ch):
        sl = slice(j * ch, (j + 1) * ch)
        c = v_s[:, sl] - mean
        sq = sq + jnp.sum(c * c, axis=-1, keepdims=True)
    rstd = lax.rsqrt(sq * (1.0 / A_HALF) + LN_EPS)
    for g in range(A_GROUPS):
        sl = slice(g * A_GROUP_W, (g + 1) * A_GROUP_W)
        vn = (v_s[:, sl] - mean) * rstd * vng_ref[:, sl] + vnb_ref[:, sl]
        if write_v:
            v_ref[:, sl] = vn
        sg = _bdot(ws_ref[g], vn.astype(BF16)) + bs_ref[:, g:g + 1]
        p_s[:, sl] = (u_s[:, sl] * sg).astype(BF16)
    y = _bdot(p_s[...], w_out_ref[...]) + b_out_ref[...]
    x1 = _layer_norm(DN_ALPHA * x + g1 * y, lng_ref[...], lnb_ref[...])
    x1_ref[...] = x1
    h2_ref[...] = x1 * (1.0 + sc2) + sh2


def _gmlp_layer(x, mod, w_in, b_in, vn_g, vn_b, w_s, b_s, w_out, b_out, ln_g, ln_b, *, tm, write_v):
    bsz, s, d = x.shape
    seg = min(s, A_CHUNK)
    idx = jnp.arange(seg)
    mask = (idx[:, None] // CHUNK) >= (idx[None, :] // CHUNK)
    wm = jnp.where(mask[None], w_s[:, :seg, :seg], 0.0)
    reps = tm // seg
    eye = jnp.eye(reps, dtype=F32)
    ws_big = jnp.einsum("ab,gij->gaibj", eye, wm).reshape(A_GROUPS, tm, tm).astype(BF16)
    bs_big = jnp.tile(b_s[:, :seg].T, (reps, 1))
    out_shape = [jax.ShapeDtypeStruct((bsz, s, d), F32), jax.ShapeDtypeStruct((bsz, s, d), F32)]
    tile = lambda w: pl.BlockSpec((None, tm, w), lambda b, i: (b, i, 0))
    out_specs = [tile(d), tile(d)]
    if write_v:
        out_shape.append(jax.ShapeDtypeStruct((bsz, s, A_HALF), F32))
        out_specs.append(tile(A_HALF))
    return pl.pallas_call(
        functools.partial(_gmlp_kernel, write_v=write_v),
        grid=(bsz, s // tm),
        in_specs=[
            tile(d),
            pl.BlockSpec((None, 6, d), lambda b, i: (b, 0, 0)),
            _const_spec((d, 2 * A_HALF)),
            _const_spec((1, 2 * A_HALF)),
            _const_spec((1, A_HALF)),
            _const_spec((1, A_HALF)),
            _const_spec((A_GROUPS, tm, tm)),
            _const_spec((tm, A_GROUPS)),
            _const_spec((A_HALF, d)),
            _const_spec((1, d)),
            _const_spec((1, d)),
            _const_spec((1, d)),
        ],
        out_specs=out_specs,
        out_shape=out_shape,
        scratch_shapes=[pltpu.VMEM((tm, A_HALF), F32), pltpu.VMEM((tm, A_HALF), F32),
                        pltpu.VMEM((tm, A_HALF), BF16)],
        compiler_params=_cparams(("arbitrary", "arbitrary")),
        name="gmlp_layer",
    )(x, mod, w_in.astype(BF16), b_in.reshape(1, -1), vn_g.reshape(1, -1), vn_b.reshape(1, -1),
      ws_big, bs_big, w_out.astype(BF16), b_out.reshape(1, -1), ln_g.reshape(1, -1), ln_b.reshape(1, -1))


def _router_kernel(hp_ref, hs_ref, wr_ref, br_ref, tri_ref, e_ref, g_ref, r_ref, cnt_ref, h3_ref, base_s,
                   *, n_first):
    i = pl.program_id(0)

    @pl.when(i == 0)
    def _():
        base_s[...] = jnp.zeros_like(base_s)

    h = jnp.where(i < n_first, hp_ref[...], hs_ref[...])
    rows = h.shape[0]
    for c in range(ROW_CHUNKS):
        h3_ref[pl.ds(c, rows, stride=ROW_CHUNKS), :] = h[:, c * LANES:(c + 1) * LANES]
    logits = lax.dot_general(wr_ref[...], h, (((1,), (1,)), ((), ())),
                             preferred_element_type=F32,
                             precision=lax.Precision.HIGHEST) + br_ref[...]
    iota = lax.broadcasted_iota(jnp.int32, (N_EXPERTS, rows), 0)
    l = logits
    vals, hots = [], []
    for k in range(TOP_K):
        m = jnp.max(l, axis=0, keepdims=True)
        idx = jnp.min(jnp.where(l == m, iota, N_EXPERTS), axis=0, keepdims=True)
        hot = iota == idx
        vals.append(m)
        hots.append(hot)
        e_ref[k:k + 1, :] = idx
        l = jnp.where(hot, -jnp.inf, l)
    exps = [jnp.exp(v - vals[0]) for v in vals]
    den = exps[0] + exps[1] + exps[2] + exps[3]
    for k in range(TOP_K):
        g_ref[k:k + 1, :] = exps[k] / den
    hot_all = jnp.where(hots[0] | hots[1] | hots[2] | hots[3], 1.0, 0.0)
    before = _bdot(hot_all.astype(BF16), tri_ref[...]) + base_s[:, 0:1]
    for k in range(TOP_K):
        r_ref[k:k + 1, :] = jnp.sum(jnp.where(hots[k], before, 0.0), axis=0, keepdims=True).astype(jnp.int32)
    base_s[...] = base_s[...] + jnp.sum(hot_all, axis=1, keepdims=True)
    cnt_ref[...] = base_s[...].astype(jnp.int32)


def _router(h_first, h_second, w_router, b_router):
    tr = ROUTER_ROWS
    n_first = h_first.shape[0] // tr
    n_second = h_second.shape[0] // tr
    n = h_first.shape[0] + h_second.shape[0]
    tri = (jnp.arange(tr)[:, None] < jnp.arange(tr)[None, :]).astype(BF16)
    sel = pl.BlockSpec((TOP_K, tr), lambda i: (0, i))
    top_e, gates, rank, cnt, h3 = pl.pallas_call(
        functools.partial(_router_kernel, n_first=n_first),
        grid=(n_first + n_second,),
        in_specs=[
            pl.BlockSpec((tr, D_MODEL), lambda i: (jnp.minimum(i, n_first - 1), 0)),
            pl.BlockSpec((tr, D_MODEL), lambda i: (jnp.maximum(i - n_first, 0), 0)),
            _const_spec((N_EXPERTS, D_MODEL)),
            _const_spec((N_EXPERTS, 1)),
            _const_spec((tr, tr)),
        ],
        out_specs=[sel, sel, sel, pl.BlockSpec((N_EXPERTS, LANES), lambda i: (0, 0)),
                   pl.BlockSpec((tr * ROW_CHUNKS, LANES), lambda i: (i, 0))],
        out_shape=[jax.ShapeDtypeStruct((TOP_K, n), jnp.int32), jax.ShapeDtypeStruct((TOP_K, n), F32),
                   jax.ShapeDtypeStruct((TOP_K, n), jnp.int32),
                   jax.ShapeDtypeStruct((N_EXPERTS, LANES), jnp.int32),
                   jax.ShapeDtypeStruct((n * ROW_CHUNKS, LANES), F32)],
        scratch_shapes=[pltpu.VMEM((N_EXPERTS, LANES), F32)],
        compiler_params=_cparams(("arbitrary",)),
        name="moe_router",
    )(h_first, h_second, w_router.T, b_router.reshape(N_EXPERTS, 1), tri)
    return top_e, gates, rank, cnt[:, 0], h3


PREP_ROWS = 64


def _pair_perm(v):
    shp = v.shape
    v4 = v.reshape(shp[:-1] + (shp[-1] // LANES, 2, LANES // 2))
    return jnp.swapaxes(v4, -1, -2).reshape(shp)


ROW_BUFS = 4


def _expert_kernel(be_ref, nact_ref, src0_ref, src1_ref, src2_ref, dstp_ref, dstc_ref, h3_ref,
                   win_ref, bg_ref, bl_ref, wout_ref, bo_ref, ys_ref,
                   wg_s, wl_s, wo_t, wo_s, *bufs_and_sems):
    xbufs, ybufs = bufs_and_sems[:ROW_BUFS], bufs_and_sems[ROW_BUFS:2 * ROW_BUFS]
    gsem, ssem = bufs_and_sems[2 * ROW_BUFS:]
    b = pl.program_id(0)
    nact = nact_ref[0]
    active = b < nact
    last = b == nact - 1
    blk = xbufs[0].shape[0]
    tmb = blk // ROW_CHUNKS
    fresh = jnp.logical_or(b == 0, be_ref[b] != be_ref[jnp.maximum(b - 1, 0)])

    def row_tile(ref, row):
        return ref.at[pl.ds(pl.multiple_of(row * ROW_CHUNKS, ROW_CHUNKS), ROW_CHUNKS)]

    def start_gather(src_ref, s):
        for r in range(tmb):
            pltpu.make_async_copy(row_tile(h3_ref, src_ref[0, 0, r]), xbufs[s].at[pl.ds(r * ROW_CHUNKS, ROW_CHUNKS)],
                                  gsem.at[s]).start(priority=r % 2)

    def wait_gather(s):
        pltpu.make_async_copy(h3_ref.at[pl.ds(0, blk)], xbufs[s], gsem.at[s]).wait()

    def start_scatter(dst_ref, s):
        for r in range(tmb):
            pltpu.make_async_copy(ybufs[s].at[pl.ds(r * ROW_CHUNKS, ROW_CHUNKS)], row_tile(ys_ref, dst_ref[0, 0, r]),
                                  ssem.at[s]).start(priority=r % 2)

    def wait_scatter(s):
        pltpu.make_async_copy(ybufs[s], ys_ref.at[pl.ds(0, blk)], ssem.at[s]).wait()

    def chunk_rows(c):
        return pl.ds(c, tmb, stride=ROW_CHUNKS)

    def on_phase(cond, fn):
        for q in range(ROW_BUFS):
            pl.when(jnp.logical_and(cond, b % ROW_BUFS == q))(functools.partial(fn, q))

    @pl.when(jnp.logical_and(active, b == 0))
    def _():
        start_gather(src0_ref, 0)
        start_gather(src1_ref, 1)

    @pl.when(jnp.logical_and(active, fresh))
    def _():
        even = lax.broadcasted_iota(jnp.int32, (PREP_ROWS, LANES), 1) % 2 == 0

        def split_rows(r, carry):
            rows = pl.ds(pl.multiple_of(r * PREP_ROWS, PREP_ROWS), PREP_ROWS)
            for c in range(D_EXPERT // LANES):
                blk_a = win_ref[rows, 2 * c * LANES:(2 * c + 1) * LANES]
                blk_b = win_ref[rows, (2 * c + 1) * LANES:(2 * c + 2) * LANES]
                wg_s[rows, c * LANES:(c + 1) * LANES] = jnp.where(
                    even, blk_a, pltpu.roll(blk_b, 1, 1)).astype(BF16)
                wl_s[rows, c * LANES:(c + 1) * LANES] = jnp.where(
                    even, pltpu.roll(blk_a, LANES - 1, 1), blk_b).astype(BF16)
            return carry

        lax.fori_loop(0, D_MODEL // PREP_ROWS, split_rows, 0)
        half = LANES // 2
        for c in range(D_EXPERT // LANES):
            for p in range(2):
                for cb in range(D_MODEL // LANES):
                    wo_t[cb, pl.ds(c * LANES + p, half, stride=2), :] = (
                        wout_ref[pl.ds(c * LANES + half * p, half), cb * LANES:(cb + 1) * LANES])

        def cast_rows(r, carry):
            rows = pl.ds(pl.multiple_of(r * PREP_ROWS, PREP_ROWS), PREP_ROWS)
            for cb in range(D_MODEL // LANES):
                wo_s[rows, cb * LANES:(cb + 1) * LANES] = wo_t[cb, rows, :].astype(BF16)
            return carry

        lax.fori_loop(0, D_EXPERT // PREP_ROWS, cast_rows, 0)

    on_phase(active, wait_gather)
    on_phase(jnp.logical_and(active, b >= ROW_BUFS), wait_scatter)

    def compute(p):
        x = jnp.concatenate([xbufs[p][chunk_rows(c), :] for c in range(ROW_CHUNKS)], axis=1).astype(BF16)
        zg = _bdot(x, wg_s[...]) + bg_ref[...]
        zl = _bdot(x, wl_s[...]) + bl_ref[...]
        glu = jnp.minimum(zg, SWIGLU_LIMIT)
        lin = jnp.clip(zl, -SWIGLU_LIMIT, SWIGLU_LIMIT)
        a = glu * jax.nn.sigmoid(SWIGLU_ALPHA * glu) * (lin + 1.0)
        y = _bdot(a.astype(BF16), wo_s[...]) + bo_ref[...]
        for c in range(ROW_CHUNKS):
            ybufs[p][chunk_rows(c), :] = y[:, c * LANES:(c + 1) * LANES]

    @pl.when(jnp.logical_and(active, b == 0))
    def _():
        start_gather(src2_ref, 2)
        compute(0)

    def step(q):
        start_gather(src2_ref, (q + 2) % ROW_BUFS)
        start_scatter(dstp_ref, (q - 1) % ROW_BUFS)
        compute(q)

    on_phase(jnp.logical_and(active, b > 0), step)

    def drain(q):
        start_scatter(dstc_ref, q)
        wait_gather((q + 1) % ROW_BUFS)
        wait_gather((q + 2) % ROW_BUFS)
        wait_scatter(q)

    on_phase(last, drain)
    for back in range(1, ROW_BUFS):
        on_phase(jnp.logical_and(last, b >= back), lambda q, back=back: wait_scatter((q - back) % ROW_BUFS))

    @pl.when(last)
    def _():
        spare0 = ys_ref.shape[0] - 2 * blk
        fills = []
        for s in range(2):
            ybufs[s][...] = jnp.zeros_like(ybufs[s])
            fills.append(pltpu.make_async_copy(ybufs[s], ys_ref.at[pl.ds(spare0 + s * blk, blk)], ssem.at[s]))
        for cp in fills:
            cp.start()
        for cp in fills:
            cp.wait()


def _experts(h3, src_rows, dst_rows, block_e, nact, n_slots, layer, w_in, b_glu, b_lin, w_out, b_out):
    tmb = EXPERT_ROWS
    n_blocks = src_rows.shape[0]
    exp3 = lambda b, be, na: (be[b], 0, 0)
    exp4 = lambda b, be, na: (layer, be[b], 0, 0)
    idx_spec = lambda f: pl.BlockSpec((1, 1, tmb), lambda b, be, na: (f(b), 0, 0), memory_space=pltpu.SMEM)
    grid_spec = pltpu.PrefetchScalarGridSpec(
        num_scalar_prefetch=2,
        grid=(n_blocks,),
        in_specs=[
            idx_spec(lambda b: b),
            idx_spec(lambda b: jnp.minimum(b + 1, n_blocks - 1)),
            idx_spec(lambda b: jnp.minimum(b + 2, n_blocks - 1)),
            idx_spec(lambda b: jnp.maximum(b - 1, 0)),
            idx_spec(lambda b: b),
            pl.BlockSpec(memory_space=pl.ANY),
            pl.BlockSpec((None, None, D_MODEL, 2 * D_EXPERT), exp4),
            pl.BlockSpec((None, 1, D_EXPERT), exp3),
            pl.BlockSpec((None, 1, D_EXPERT), exp3),
            pl.BlockSpec((None, None, D_EXPERT, D_MODEL), exp4),
            pl.BlockSpec((None, 1, D_MODEL), exp3),
        ],
        out_specs=pl.BlockSpec(memory_space=pl.ANY),
        scratch_shapes=[pltpu.VMEM((D_MODEL, D_EXPERT), BF16), pltpu.VMEM((D_MODEL, D_EXPERT), BF16),
                        pltpu.VMEM((D_MODEL // LANES, D_EXPERT, LANES), F32),
                        pltpu.VMEM((D_EXPERT, D_MODEL), BF16)]
        + [pltpu.VMEM((tmb * ROW_CHUNKS, LANES), F32)] * (2 * ROW_BUFS)
        + [pltpu.SemaphoreType.DMA((ROW_BUFS,)), pltpu.SemaphoreType.DMA((ROW_BUFS,))],
    )
    return pl.pallas_call(
        _expert_kernel,
        grid_spec=grid_spec,
        out_shape=jax.ShapeDtypeStruct((n_slots * ROW_CHUNKS, LANES), F32),
        compiler_params=_cparams(("arbitrary",)),
        name="moe_experts",
    )(block_e, nact, src_rows, src_rows, src_rows, dst_rows, dst_rows, h3, w_in, b_glu, b_lin, w_out, b_out)


def _moe(h_first, h_second, layer, w_router, b_router, w_in, b_in, w_out, b_out):
    tmb = EXPERT_ROWS
    top_e, gates, rank, counts, h3 = _router(h_first, h_second, w_router, b_router)
    n = h3.shape[0] // ROW_CHUNKS
    m = n * TOP_K
    padded = (counts + tmb - 1) // tmb * tmb
    pend = jnp.cumsum(padded)
    pstart = pend - padded
    hot = top_e[..., None] == jnp.arange(N_EXPERTS)[None, None, :]
    dest = jnp.sum(jnp.where(hot, pstart[None, None, :], 0), axis=-1) + rank
    n_blocks = m // tmb + N_EXPERTS
    blk_row0 = jnp.arange(n_blocks, dtype=jnp.int32) * tmb
    block_e = jnp.minimum(jnp.sum(pend[None, :] <= blk_row0[:, None], axis=1), N_EXPERTS - 1).astype(jnp.int32)
    nact = (pend[-1] // tmb).astype(jnp.int32).reshape(1)
    slot = jnp.arange(n, dtype=jnp.int32)[None, :] + n * jnp.arange(TOP_K, dtype=jnp.int32)[:, None]
    j = jnp.arange(N_EXPERTS * tmb, dtype=jnp.int32)
    off = j % tmb
    pad = padded - counts
    rep = lambda v: jnp.repeat(v, tmb)
    pad_key = jnp.where(off < rep(pad), rep(pstart + counts) + off, pend[-1] + j - rep(jnp.cumsum(pad)))
    keys = jnp.concatenate([dest.reshape(m), pad_key]).astype(jnp.int32)
    vals = jnp.concatenate([slot.reshape(m), jnp.full_like(j, -1)])
    _, row_slot = lax.sort((keys, vals), num_keys=1, is_stable=False)
    row = jnp.arange(n_blocks * tmb, dtype=jnp.int32)
    src_rows = jnp.where(row_slot < 0, 0, row_slot % n)
    dst_rows = jnp.where(row_slot < 0, m + ((row // tmb) % 2) * tmb + row % tmb, row_slot)
    b_glu = _pair_perm(b_in[:, 0::2]).reshape(N_EXPERTS, 1, D_EXPERT)
    b_lin = _pair_perm(b_in[:, 1::2]).reshape(N_EXPERTS, 1, D_EXPERT)
    ys = _experts(h3, src_rows.reshape(n_blocks, 1, tmb), dst_rows.reshape(n_blocks, 1, tmb), block_e, nact,
                  m + 2 * tmb, layer, w_in, b_glu, b_lin, w_out, b_out.reshape(N_EXPERTS, 1, D_MODEL))
    return ys, gates.T


def _post_kernel(x1_ref, *rest, with_kv):
    ys_refs = rest[:TOP_K]
    gate_ref, mod_ref, lng_ref, lnb_ref = rest[TOP_K:TOP_K + 4]
    rest = rest[TOP_K + 4:]
    g2 = mod_ref[5:6, :]
    tm = x1_ref.shape[0]
    f = None
    for k in range(TOP_K):
        yk = jnp.concatenate([ys_refs[k][pl.ds(c, tm, stride=ROW_CHUNKS), :] for c in range(ROW_CHUNKS)], axis=1)
        term = yk * gate_ref[:, k:k + 1]
        f = term if f is None else f + term
    x2 = _layer_norm(DN_ALPHA * x1_ref[...] + g2 * f, lng_ref[...], lnb_ref[...])
    if not with_kv:
        (x2_ref,) = rest
        x2_ref[...] = x2
        return
    wa_ref, kng_ref, cos_ref, sin_ref, x2_ref, lat_ref, kr_ref, kcat_ref = rest
    x2_ref[...] = x2
    kv = _bdot(x2.astype(BF16), wa_ref[...])
    c = kv[:, :KV_RANK]
    lat = c * lax.rsqrt(jnp.mean(c * c, axis=-1, keepdims=True) + RMS_EPS) * kng_ref[...]
    k = kv[:, KV_RANK:KV_RANK + QK_ROPE]
    k_swapped = kv[:, KV_RANK + QK_ROPE:]
    kr = k * cos_ref[...] + k_swapped * sin_ref[...]
    lat_ref[...] = lat
    kr_ref[...] = kr
    kcat_ref[:, :KV_RANK] = lat.astype(BF16)
    kcat_ref[:, KV_RANK:] = kr.astype(BF16)


def _swap_halves_cols(w, width):
    shp = w.shape
    w4 = w.reshape(shp[:-1] + (shp[-1] // width, 2, width // 2))
    return w4[..., ::-1, :].reshape(shp)


def _rope_tables(pos, reps):
    inv = 1.0 / (ROPE_BASE ** (jnp.arange(ROPE_HALF, dtype=F32) * (2.0 / QK_ROPE)))
    ang = pos.astype(F32)[:, None] * inv[None, :]
    cos, sin = jnp.cos(ang), jnp.sin(ang)
    cos_t = jnp.tile(jnp.concatenate([cos, cos], -1), (1, reps))
    sin_t = jnp.tile(jnp.concatenate([-sin, sin], -1), (1, reps))
    return cos_t, sin_t


def _post_layer(x1, ys, gates, row0, mod, ln_g, ln_b, *, tm, kv=None):
    bsz, s, d = x1.shape
    blk0 = row0 // tm
    per_b = s // tm
    tile = lambda w: pl.BlockSpec((None, tm, w), lambda b, i: (b, i, 0))
    n_tok = gates.shape[0]
    pick_spec = lambda k: pl.BlockSpec((tm * ROW_CHUNKS, LANES),
                                       lambda b, i: (k * (n_tok // tm) + blk0 + b * per_b + i, 0))
    in_specs = [tile(d)] + [pick_spec(k) for k in range(TOP_K)] + [
        pl.BlockSpec((tm, TOP_K), lambda b, i: (blk0 + b * per_b + i, 0)),
        pl.BlockSpec((None, 6, d), lambda b, i: (b, 0, 0)),
        _const_spec((1, d)),
        _const_spec((1, d)),
    ]
    args = [x1] + [ys] * TOP_K + [gates, mod, ln_g.reshape(1, -1), ln_b.reshape(1, -1)]
    out_shape = [jax.ShapeDtypeStruct((bsz, s, d), F32)]
    out_specs = [tile(d)]
    if kv is not None:
        w_a, kn_g, cos_t, sin_t = kv
        w_ext = jnp.concatenate([w_a, _swap_halves_cols(w_a[:, KV_RANK:], QK_ROPE)], axis=1).astype(BF16)
        in_specs += [_const_spec(w_ext.shape), _const_spec((1, KV_RANK)),
                     pl.BlockSpec((tm, QK_ROPE), lambda b, i: (i, 0)),
                     pl.BlockSpec((tm, QK_ROPE), lambda b, i: (i, 0))]
        args += [w_ext, kn_g.reshape(1, -1), cos_t, sin_t]
        out_shape += [jax.ShapeDtypeStruct((bsz, s, KV_RANK), F32), jax.ShapeDtypeStruct((bsz, s, QK_ROPE), F32),
                      jax.ShapeDtypeStruct((bsz, s, QK_LAT), BF16)]
        out_specs += [tile(KV_RANK), tile(QK_ROPE), tile(QK_LAT)]
    return pl.pallas_call(
        functools.partial(_post_kernel, with_kv=kv is not None),
        grid=(bsz, per_b),
        in_specs=in_specs,
        out_specs=out_specs,
        out_shape=out_shape,
        compiler_params=_cparams(("arbitrary", "arbitrary")),
        name="post_moe",
    )(*args)


Q_GROUP = 512
KEY_TILE = 512


def _qproj_kernel(x_ref, mod_ref, wdq_ref, qng_ref, wqn_ref, wqr_ref, wqrs_ref, wuk_ref, cos_ref, sin_ref, q_ref,
                  *, tq):
    x = x_ref[...]
    sh1, sc1 = mod_ref[0:1, :], mod_ref[1:2, :]
    h = (x * (1.0 + sc1) + sh1).astype(BF16)
    tm = x.shape[0]
    cq = lax.dot_general(wdq_ref[...], h, (((1,), (1,)), ((), ())), preferred_element_type=F32)
    cq = (cq * lax.rsqrt(jnp.mean(cq * cq, axis=0, keepdims=True) + RMS_EPS) * qng_ref[...]).astype(BF16)
    qn = _bdot(wqn_ref[...], cq)
    qr = _bdot(wqr_ref[...], cq)
    qrs = _bdot(wqrs_ref[...], cq)
    rope = (qr * cos_ref[...] + qrs * sin_ref[...]).astype(BF16)
    per_group = Q_GROUP // tq
    for hd in range(MLA_HEADS):
        ql = _bdot(wuk_ref[hd], qn[hd * QK_NOPE:(hd + 1) * QK_NOPE, :].astype(BF16)).astype(BF16)
        g, off = hd // per_group, (hd % per_group) * tq
        for t in range(tm // tq):
            q_ref[t, g, 0:KV_RANK, off:off + tq] = ql[:, t * tq:(t + 1) * tq]
            q_ref[t, g, KV_RANK:QK_LAT, off:off + tq] = rope[hd * QK_ROPE:(hd + 1) * QK_ROPE, t * tq:(t + 1) * tq]


def _qproj_layer(x, mod, w_dq, qn_g, w_qb, w_uk, cos_t, sin_t, *, tm, tq):
    bsz, s, d = x.shape
    w_qn = w_qb[:, :, :QK_NOPE].reshape(Q_RANK, MLA_HEADS * QK_NOPE)
    w_qr = w_qb[:, :, QK_NOPE:].reshape(Q_RANK, MLA_HEADS * QK_ROPE)
    w_qrs = _swap_halves_cols(w_qr, QK_ROPE)
    wuk_h = jnp.transpose(w_uk, (1, 0, 2)).astype(BF16)
    hr = MLA_HEADS * QK_ROPE
    n_groups = MLA_HEADS * tq // Q_GROUP
    nt = tm // tq
    return pl.pallas_call(
        functools.partial(_qproj_kernel, tq=tq),
        grid=(bsz, s // tm),
        in_specs=[
            pl.BlockSpec((None, tm, d), lambda b, i: (b, i, 0)),
            pl.BlockSpec((None, 6, d), lambda b, i: (b, 0, 0)),
            _const_spec((Q_RANK, d)),
            _const_spec((Q_RANK, 1)),
            _const_spec((MLA_HEADS * QK_NOPE, Q_RANK)),
            _const_spec((hr, Q_RANK)),
            _const_spec((hr, Q_RANK)),
            _const_spec(wuk_h.shape),
            pl.BlockSpec((hr, tm), lambda b, i: (0, i)),
            pl.BlockSpec((hr, tm), lambda b, i: (0, i)),
        ],
        out_specs=pl.BlockSpec((None, nt, n_groups, QK_LAT, Q_GROUP), lambda b, i: (b, i, 0, 0, 0)),
        out_shape=jax.ShapeDtypeStruct((bsz, s // tq, n_groups, QK_LAT, Q_GROUP), BF16),
        compiler_params=_cparams(("arbitrary", "arbitrary")),
        name="mla_qproj",
    )(x, mod, w_dq.T.astype(BF16), qn_g.reshape(-1, 1), w_qn.T.astype(BF16), w_qr.T.astype(BF16),
      w_qrs.T.astype(BF16), wuk_h, cos_t.T, sin_t.T)


EXP2_SCALE = ATTN_SCALE * 1.4426950408889634


def _attn_kernel(q_ref, *rest, tq, causal, n_main, tail, split_keys):
    if split_keys:
        klat_ref, kkr_ref = rest[:2]
        rest = rest[2:]
    else:
        k_ref = rest[0]
        rest = rest[1:]
    if tail:
        kt_ref = rest[0]
        rest = rest[1:]
    x_ref, mod_ref, wuv_ref, wo_ref, lng_ref, lnb_ref, x1_ref, h2_ref, m_s, l_s, acc_s, ta_s, tb_s = rest
    i = pl.program_id(1)
    n_groups = q_ref.shape[0]
    per_group = Q_GROUP // tq
    m_s[...] = jnp.full_like(m_s, NEG_BIG)
    l_s[...] = jnp.zeros_like(l_s)
    acc_s[...] = jnp.zeros_like(acc_s)

    def tile_rows(j):
        return pl.ds(pl.multiple_of(j * KEY_TILE, KEY_TILE), KEY_TILE)

    def key_rows(j):
        if split_keys:
            return jnp.concatenate([klat_ref[tile_rows(j), :].astype(BF16), kkr_ref[tile_rows(j), :].astype(BF16)],
                                   axis=1)
        return k_ref[tile_rows(j), :]

    def value_rows(j):
        if split_keys:
            return klat_ref[tile_rows(j), :].astype(BF16)
        return k_ref[tile_rows(j), :KV_RANK]

    def softmax_step(g, t, v, key0, masked):
        t = t * EXP2_SCALE
        if masked:
            n = t.shape[0]
            k_chunk = (key0 + lax.broadcasted_iota(jnp.int32, (n, 1), 0)) // CHUNK
            q_chunk = (i * tq + lax.broadcasted_iota(jnp.int32, (1, Q_GROUP), 1) % tq) // CHUNK
            t = jnp.where(k_chunk <= q_chunk, t, -jnp.inf)
        m_old = m_s[g]
        m_new = jnp.maximum(m_old, jnp.max(t, axis=0, keepdims=True))
        alpha = jnp.exp2(m_old - m_new)
        p = jnp.exp2(t - m_new)
        l_s[g] = alpha * l_s[g] + jnp.sum(p, axis=0, keepdims=True)
        acc_s[g] = alpha * acc_s[g] + lax.dot_general(
            v, p.astype(BF16), (((0,), (0,)), ((), ())), preferred_element_type=F32)
        m_s[g] = m_new

    def stage(cur, nxt, j, masked, make_next):
        v = value_rows(j)
        k_next = key_rows(j + 1) if make_next else None
        for g in range(n_groups):
            if make_next:
                nxt[g] = _bdot(k_next, q_ref[g])
            softmax_step(g, cur[g], v, j * KEY_TILE, masked)

    n_open = (i * tq) // KEY_TILE if causal else n_main - 1
    k0 = key_rows(0)
    for g in range(n_groups):
        ta_s[g] = _bdot(k0, q_ref[g])

    def pair_body(p, c):
        stage(ta_s, tb_s, 2 * p, False, True)
        stage(tb_s, ta_s, 2 * p + 1, False, True)
        return c

    lax.fori_loop(0, n_open // 2, pair_body, 0)
    if causal:
        @pl.when(n_open % 2 == 0)
        def _():
            stage(ta_s, tb_s, n_open, True, False)

        @pl.when(n_open % 2 == 1)
        def _():
            stage(ta_s, tb_s, n_open - 1, False, True)
            stage(tb_s, ta_s, n_open, True, False)
    elif n_open % 2 == 0:
        stage(ta_s, tb_s, n_open, False, False)
    else:
        stage(ta_s, tb_s, n_open - 1, False, True)
        stage(tb_s, ta_s, n_open, False, False)
    if tail:
        kt = kt_ref[...]
        for g in range(n_groups):
            softmax_step(g, _bdot(kt, q_ref[g]), kt[:, :KV_RANK], n_main * KEY_TILE, False)

    heads = []
    for g in range(n_groups):
        o = (acc_s[g] / l_s[g]).T
        for hh in range(per_group):
            hd = g * per_group + hh
            heads.append(_bdot(o[hh * tq:(hh + 1) * tq].astype(BF16), wuv_ref[hd]))
    oc = jnp.concatenate(heads, axis=-1).astype(BF16)
    y = _bdot(oc, wo_ref[...])
    x = x_ref[...]
    g1, sh2, sc2 = mod_ref[2:3, :], mod_ref[3:4, :], mod_ref[4:5, :]
    x1 = _layer_norm(DN_ALPHA * x + g1 * y, lng_ref[...], lnb_ref[...])
    x1_ref[...] = x1
    h2_ref[...] = x1 * (1.0 + sc2) + sh2


def _attn_layer(q, keys, x, mod, w_uv, w_o, ln_g, ln_b, *, tq, causal, k_tail=None):
    bsz, s, d = x.shape
    split_keys = isinstance(keys, tuple)
    keys = keys if split_keys else (keys,)
    t_main = keys[0].shape[1]
    n_main = t_main // KEY_TILE
    n_groups = q.shape[2]
    tail = k_tail is not None
    wuv_t = jnp.transpose(w_uv, (1, 0, 2)).astype(BF16)
    tile = pl.BlockSpec((None, tq, d), lambda b, i: (b, i, 0))
    in_specs = [pl.BlockSpec((None, None, n_groups, QK_LAT, Q_GROUP), lambda b, i: (b, i, 0, 0, 0))]
    in_specs += [pl.BlockSpec((None, t_main, k.shape[2]), lambda b, i: (b, 0, 0)) for k in keys]
    args = [q, *keys]
    if tail:
        in_specs.append(pl.BlockSpec((None, k_tail.shape[1], QK_LAT), lambda b, i: (b, 0, 0)))
        args.append(k_tail)
    in_specs += [
        tile,
        pl.BlockSpec((None, 6, d), lambda b, i: (b, 0, 0)),
        _const_spec(wuv_t.shape),
        _const_spec((MLA_HEADS * V_DIM, d)),
        _const_spec((1, d)),
        _const_spec((1, d)),
    ]
    args += [x, mod, wuv_t, w_o.astype(BF16), ln_g.reshape(1, -1), ln_b.reshape(1, -1)]
    return pl.pallas_call(
        functools.partial(_attn_kernel, tq=tq, causal=causal, n_main=n_main, tail=tail, split_keys=split_keys),
        grid=(bsz, s // tq),
        in_specs=in_specs,
        out_specs=[tile, tile],
        out_shape=[jax.ShapeDtypeStruct((bsz, s, d), F32), jax.ShapeDtypeStruct((bsz, s, d), F32)],
        scratch_shapes=[pltpu.VMEM((n_groups, 1, Q_GROUP), F32), pltpu.VMEM((n_groups, 1, Q_GROUP), F32),
                        pltpu.VMEM((n_groups, KV_RANK, Q_GROUP), F32),
                        pltpu.VMEM((n_groups, KEY_TILE, Q_GROUP), F32),
                        pltpu.VMEM((n_groups, KEY_TILE, Q_GROUP), F32)],
        compiler_params=_cparams(("arbitrary", "arbitrary")),
        name="mla_attention",
    )(*args)


GMLP_ROWS = 256
TOKEN_ROWS = 512
ATTN_Q_ROWS = 128


def kernel(x_prompt, x_sample, cache_kv_latent, cache_k_rope, c_prompt, c_sample, ada_w, ada_b, ln_g, ln_b,
           a_w_in, a_b_in, a_vn_g, a_vn_b, a_w_s, a_b_s, a_w_out, a_b_out, kv_w_a, kv_norm_g, kv_w_uk, kv_w_uv,
           q_w_a, q_norm_g, q_w_b, q_w_o, moe_w_router, moe_b_router, moe_w_in, moe_b_in, moe_w_out, moe_b_out):
    bp, sp, d = x_prompt.shape
    bs, ss, _ = x_sample.shape
    past = cache_kv_latent.shape[1]
    n_p = bp * sp
    n_s = bs * ss
    assert d == D_MODEL and sp % TOKEN_ROWS == 0 and sp % GMLP_ROWS == 0 and sp % KEY_TILE == 0
    assert ss <= CHUNK and Q_GROUP % ss == 0 and past % KEY_TILE == 0
    assert n_p % ROUTER_ROWS == 0 and n_s % ROUTER_ROWS == 0 and (n_p + n_s) * TOP_K % EXPERT_ROWS == 0

    mod = _ada_mod(jnp.concatenate([c_prompt, c_sample], axis=0), ada_w, ada_b)
    mod = mod.reshape(DEPTH, bp + bs, 6, d)
    mod_p, mod_s = mod[:, :bp], mod[:, bp:]

    pos_p = jnp.arange(sp, dtype=jnp.int32)
    pos_s = past + jnp.arange(ss, dtype=jnp.int32)

    def moe(l, h2_p, h2_s):
        return _moe(h2_p.reshape(n_p, d), h2_s.reshape(n_s, d), l, moe_w_router[l], moe_b_router[l],
                    moe_w_in, moe_b_in[l], moe_w_out, moe_b_out[l])

    gm = functools.partial(_gmlp_layer, w_in=a_w_in[0], b_in=a_b_in[0], vn_g=a_vn_g[0], vn_b=a_vn_b[0],
                           w_s=a_w_s[0], b_s=a_b_s[0], w_out=a_w_out[0], b_out=a_b_out[0],
                           ln_g=ln_g[0, 0], ln_b=ln_b[0, 0])
    x1_p, h2_p = gm(x_prompt, mod_p[0], tm=GMLP_ROWS, write_v=False)
    x1_s, h2_s, v_s = gm(x_sample, mod_s[0], tm=ss, write_v=True)
    ys, gates = moe(0, h2_p, h2_s)
    x2_p, lat_p, kr_p, kcat_p = _post_layer(x1_p, ys, gates, 0, mod_p[0], ln_g[0, 1], ln_b[0, 1], tm=TOKEN_ROWS,
                                            kv=(kv_w_a, kv_norm_g) + _rope_tables(pos_p, 1))
    x2_s, lat_s, kr_s, kcat_s = _post_layer(x1_s, ys, gates, n_p, mod_s[0], ln_g[0, 1], ln_b[0, 1], tm=ss,
                                            kv=(kv_w_a, kv_norm_g) + _rope_tables(pos_s, 1))

    qp = functools.partial(_qproj_layer, w_dq=q_w_a[0], qn_g=q_norm_g[0], w_qb=q_w_b[0], w_uk=kv_w_uk)
    at = functools.partial(_attn_layer, w_uv=kv_w_uv, w_o=q_w_o[0], ln_g=ln_g[1, 0], ln_b=ln_b[1, 0])
    cos_p, sin_p = _rope_tables(pos_p, MLA_HEADS)
    cos_s, sin_s = _rope_tables(pos_s, MLA_HEADS)
    q_p = qp(x2_p, mod_p[1], cos_t=cos_p, sin_t=sin_p, tm=TOKEN_ROWS, tq=ATTN_Q_ROWS)
    q_s = qp(x2_s, mod_s[1], cos_t=cos_s, sin_t=sin_s, tm=ss, tq=ss)
    x3_p, h4_p = at(q_p, kcat_p, x2_p, mod_p[1], tq=ATTN_Q_ROWS, causal=True)
    x3_s, h4_s = at(q_s, (cache_kv_latent, cache_k_rope), x2_s, mod_s[1], tq=ss, causal=False, k_tail=kcat_s)
    ys, gates = moe(1, h4_p, h4_s)
    (y_p,) = _post_layer(x3_p, ys, gates, 0, mod_p[1], ln_g[1, 1], ln_b[1, 1], tm=TOKEN_ROWS)
    (y_s,) = _post_layer(x3_s, ys, gates, n_p, mod_s[1], ln_g[1, 1], ln_b[1, 1], tm=ss)

    return (y_p, y_s, lat_p, kr_p, lat_s, kr_s, v_s[None])
```

```python
import functools

import jax
import jax.numpy as jnp
from jax import lax
from jax.experimental import pallas as pl
from jax.experimental.pallas import tpu as pltpu

D_MODEL = 1024
DEPTH = 2
CHUNK = 64
A_CHUNK = 128
A_HALF = 2 * D_MODEL
A_GROUPS = 8
A_GROUP_W = A_HALF // A_GROUPS
MLA_HEADS = 8
QK_NOPE = 128
QK_ROPE = 64
ROPE_HALF = QK_ROPE // 2
V_DIM = 128
KV_RANK = 256
Q_RANK = 512
QK_LAT = KV_RANK + QK_ROPE
ROPE_BASE = 10000.0
ATTN_SCALE = (QK_NOPE + QK_ROPE) ** -0.5
N_EXPERTS = 32
TOP_K = 4
D_EXPERT = D_MODEL
SWIGLU_LIMIT = 7.0
SWIGLU_ALPHA = 1.702
DN_ALPHA = (2 * DEPTH) ** 0.25
LN_EPS = 1e-5
RMS_EPS = 1e-6

BF16 = jnp.bfloat16
F32 = jnp.float32

LANES = 128
ROW_CHUNKS = D_MODEL // LANES
VMEM_LIMIT = 56 * 1024 * 1024
EXPERT_ROWS = 256
ROUTER_ROWS = 512
NEG_BIG = -1e30


def _cparams(sem):
    return pltpu.CompilerParams(dimension_semantics=sem, vmem_limit_bytes=VMEM_LIMIT)


def _const_spec(shape):
    nd = len(shape)
    return pl.BlockSpec(shape, lambda *_: (0,) * nd, pipeline_mode=pl.Buffered(1))


def _layer_norm(r, g, b):
    rc = r - jnp.mean(r, axis=-1, keepdims=True)
    var = jnp.mean(rc * rc, axis=-1, keepdims=True)
    return rc * lax.rsqrt(var + LN_EPS) * g + b


def _gelu_tanh(x):
    c = 0.7978845608028654
    return 0.5 * x * (1.0 + jnp.tanh(c * (x + 0.044715 * (x * x * x))))


def _bdot(a, b):
    return jnp.dot(a, b, preferred_element_type=F32)


def _ada_kernel(c_ref, w_ref, b_ref, o_ref):
    c = c_ref[...]
    s = c * jax.nn.sigmoid(c)
    o_ref[...] = jnp.dot(s, w_ref[...], preferred_element_type=F32,
                         precision=lax.Precision.HIGHEST) + b_ref[...]


def _ada_mod(c_all, ada_w, ada_b):
    nb = c_all.shape[0]
    six_d = ada_w.shape[-1]
    tn = D_MODEL
    return pl.pallas_call(
        _ada_kernel,
        grid=(DEPTH, six_d // tn),
        in_specs=[
            pl.BlockSpec((nb, D_MODEL), lambda l, j: (0, 0)),
            pl.BlockSpec((None, D_MODEL, tn), lambda l, j: (l, 0, j)),
            pl.BlockSpec((None, 1, tn), lambda l, j: (l, 0, j)),
        ],
        out_specs=pl.BlockSpec((None, nb, tn), lambda l, j: (l, 0, j)),
        out_shape=jax.ShapeDtypeStruct((DEPTH, nb, six_d), F32),
        compiler_params=_cparams(("arbitrary", "arbitrary")),
        name="ada_mod",
    )(c_all, ada_w, ada_b.reshape(DEPTH, 1, six_d))


def _gmlp_kernel(x_ref, mod_ref, w_in_ref, b_in_ref, vng_ref, vnb_ref, ws_ref, bs_ref,
                 w_out_ref, b_out_ref, lng_ref, lnb_ref, *rest, write_v):
    if write_v:
        x1_ref, h2_ref, v_ref, u_s, v_s, p_s = rest
    else:
        x1_ref, h2_ref, u_s, v_s, p_s = rest
    x = x_ref[...]
    sh1, sc1, g1 = mod_ref[0:1, :], mod_ref[1:2, :], mod_ref[2:3, :]
    sh2, sc2 = mod_ref[3:4, :], mod_ref[4:5, :]
    h = (x * (1.0 + sc1) + sh1).astype(BF16)
    tm = x.shape[0]
    ch = 512
    n_ch = A_HALF // ch
    for j in range(n_ch):
        sl = slice(j * ch, (j + 1) * ch)
        u_s[:, sl] = _gelu_tanh(_bdot(h, w_in_ref[:, sl]) + b_in_ref[:, sl])
    tot = jnp.zeros((tm, 1), F32)
    for j in range(n_ch):
        sl = slice(j * ch, (j + 1) * ch)
        slw = slice(A_HALF + j * ch, A_HALF + (j + 1) * ch)
        g = _gelu_tanh(_bdot(h, w_in_ref[:, slw]) + b_in_ref[:, slw])
        v_s[:, sl] = g
        tot = tot + jnp.sum(g, axis=-1, keepdims=True)
    mean = tot * (1.0 / A_HALF)
    sq = jnp.zeros((tm, 1), F32)
    for j in range(n_ch):
        sl = slice(j * ch, (j + 1) * ch)
        c = v_s[:, sl] - mean
        sq = sq + jnp.sum(c * c, axis=-1, keepdims=True)
    rstd = lax.rsqrt(sq * (1.0 / A_HALF) + LN_EPS)
    for g in range(A_GROUPS):
        sl = slice(g * A_GROUP_W, (g + 1) * A_GROUP_W)
        vn = (v_s[:, sl] - mean) * rstd * vng_ref[:, sl] + vnb_ref[:, sl]
        if write_v:
            v_ref[:, sl] = vn
        sg = _bdot(ws_ref[g], vn.astype(BF16)) + bs_ref[:, g:g + 1]
        p_s[:, sl] = (u_s[:, sl] * sg).astype(BF16)
    y = _bdot(p_s[...], w_out_ref[...]) + b_out_ref[...]
    x1 = _layer_norm(DN_ALPHA * x + g1 * y, lng_ref[...], lnb_ref[...])
    x1_ref[...] = x1
    h2_ref[...] = x1 * (1.0 + sc2) + sh2


def _gmlp_layer(x, mod, w_in, b_in, vn_g, vn_b, w_s, b_s, w_out, b_out, ln_g, ln_b, *, tm, write_v):
    bsz, s, d = x.shape
    seg = min(s, A_CHUNK)
    idx = jnp.arange(seg)
    mask = (idx[:, None] // CHUNK) >= (idx[None, :] // CHUNK)
    wm = jnp.where(mask[None], w_s[:, :seg, :seg], 0.0)
    reps = tm // seg
    eye = jnp.eye(reps, dtype=F32)
    ws_big = jnp.einsum("ab,gij->gaibj", eye, wm).reshape(A_GROUPS, tm, tm).astype(BF16)
    bs_big = jnp.tile(b_s[:, :seg].T, (reps, 1))
    out_shape = [jax.ShapeDtypeStruct((bsz, s, d), F32), jax.ShapeDtypeStruct((bsz, s, d), F32)]
    tile = lambda w: pl.BlockSpec((None, tm, w), lambda b, i: (b, i, 0))
    out_specs = [tile(d), tile(d)]
    if write_v:
        out_shape.append(jax.ShapeDtypeStruct((bsz, s, A_HALF), F32))
        out_specs.append(tile(A_HALF))
    return pl.pallas_call(
        functools.partial(_gmlp_kernel, write_v=write_v),
        grid=(bsz, s // tm),
        in_specs=[
            tile(d),
            pl.BlockSpec((None, 6, d), lambda b, i: (b, 0, 0)),
            _const_spec((d, 2 * A_HALF)),
            _const_spec((1, 2 * A_HALF)),
            _const_spec((1, A_HALF)),
            _const_spec((1, A_HALF)),
            _const_spec((A_GROUPS, tm, tm)),
            _const_spec((tm, A_GROUPS)),
            _const_spec((A_HALF, d)),
            _const_spec((1, d)),
            _const_spec((1, d)),
            _const_spec((1, d)),
        ],
        out_specs=out_specs,
        out_shape=out_shape,
        scratch_shapes=[pltpu.VMEM((tm, A_HALF), F32), pltpu.VMEM((tm, A_HALF), F32),
                        pltpu.VMEM((tm, A_HALF), BF16)],
        compiler_params=_cparams(("arbitrary", "arbitrary")),
        name="gmlp_layer",
    )(x, mod, w_in.astype(BF16), b_in.reshape(1, -1), vn_g.reshape(1, -1), vn_b.reshape(1, -1),
      ws_big, bs_big, w_out.astype(BF16), b_out.reshape(1, -1), ln_g.reshape(1, -1), ln_b.reshape(1, -1))


def _router_kernel(hp_ref, hs_ref, wr_ref, br_ref, tri_ref, e_ref, g_ref, r_ref, cnt_ref, h3_ref, base_s,
                   *, n_first):
    i = pl.program_id(0)

    @pl.when(i == 0)
    def _():
        base_s[...] = jnp.zeros_like(base_s)

    h = jnp.where(i < n_first, hp_ref[...], hs_ref[...])
    rows = h.shape[0]
    for c in range(ROW_CHUNKS):
        h3_ref[pl.ds(c, rows, stride=ROW_CHUNKS), :] = h[:, c * LANES:(c + 1) * LANES]
    logits = lax.dot_general(wr_ref[...], h, (((1,), (1,)), ((), ())),
                             preferred_element_type=F32,
                             precision=lax.Precision.HIGHEST) + br_ref[...]
    iota = lax.broadcasted_iota(jnp.int32, (N_EXPERTS, rows), 0)
    l = logits
    vals, hots = [], []
    for k in range(TOP_K):
        m = jnp.max(l, axis=0, keepdims=True)
        idx = jnp.min(jnp.where(l == m, iota, N_EXPERTS), axis=0, keepdims=True)
        hot = iota == idx
        vals.append(m)
        hots.append(hot)
        e_ref[k:k + 1, :] = idx
        l = jnp.where(hot, -jnp.inf, l)
    exps = [jnp.exp(v - vals[0]) for v in vals]
    den = exps[0] + exps[1] + exps[2] + exps[3]
    for k in range(TOP_K):
        g_ref[k:k + 1, :] = exps[k] / den
    hot_all = jnp.where(hots[0] | hots[1] | hots[2] | hots[3], 1.0, 0.0)
    before = _bdot(hot_all.astype(BF16), tri_ref[...]) + base_s[:, 0:1]
    for k in range(TOP_K):
        r_ref[k:k + 1, :] = jnp.sum(jnp.where(hots[k], before, 0.0), axis=0, keepdims=True).astype(jnp.int32)
    base_s[...] = base_s[...] + jnp.sum(hot_all, axis=1, keepdims=True)
    cnt_ref[...] = base_s[...].astype(jnp.int32)


def _router(h_first, h_second, w_router, b_router):
    tr = ROUTER_ROWS
    n_first = h_first.shape[0] // tr
    n_second = h_second.shape[0] // tr
    n = h_first.shape[0] + h_second.shape[0]
    tri = (jnp.arange(tr)[:, None] < jnp.arange(tr)[None, :]).astype(BF16)
    sel = pl.BlockSpec((TOP_K, tr), lambda i: (0, i))
    top_e, gates, rank, cnt, h3 = pl.pallas_call(
        functools.partial(_router_kernel, n_first=n_first),
        grid=(n_first + n_second,),
        in_specs=[
            pl.BlockSpec((tr, D_MODEL), lambda i: (jnp.minimum(i, n_first - 1), 0)),
            pl.BlockSpec((tr, D_MODEL), lambda i: (jnp.maximum(i - n_first, 0), 0)),
            _const_spec((N_EXPERTS, D_MODEL)),
            _const_spec((N_EXPERTS, 1)),
            _const_spec((tr, tr)),
        ],
        out_specs=[sel, sel, sel, pl.BlockSpec((N_EXPERTS, LANES), lambda i: (0, 0)),
                   pl.BlockSpec((tr * ROW_CHUNKS, LANES), lambda i: (i, 0))],
        out_shape=[jax.ShapeDtypeStruct((TOP_K, n), jnp.int32), jax.ShapeDtypeStruct((TOP_K, n), F32),
                   jax.ShapeDtypeStruct((TOP_K, n), jnp.int32),
                   jax.ShapeDtypeStruct((N_EXPERTS, LANES), jnp.int32),
                   jax.ShapeDtypeStruct((n * ROW_CHUNKS, LANES), F32)],
        scratch_shapes=[pltpu.VMEM((N_EXPERTS, LANES), F32)],
        compiler_params=_cparams(("arbitrary",)),
        name="moe_router",
    )(h_first, h_second, w_router.T, b_router.reshape(N_EXPERTS, 1), tri)
    return top_e, gates, rank, cnt[:, 0], h3


PREP_ROWS = 64


def _pair_perm(v):
    shp = v.shape
    v4 = v.reshape(shp[:-1] + (shp[-1] // LANES, 2, LANES // 2))
    return jnp.swapaxes(v4, -1, -2).reshape(shp)


ROW_BUFS = 4


def _expert_kernel(be_ref, nact_ref, src0_ref, src1_ref, src2_ref, dstp_ref, dstc_ref, h3_ref,
                   win_ref, bg_ref, bl_ref, wout_ref, bo_ref, ys_ref,
                   wg_s, wl_s, wo_t, wo_s, *bufs_and_sems):
    xbufs, ybufs = bufs_and_sems[:ROW_BUFS], bufs_and_sems[ROW_BUFS:2 * ROW_BUFS]
    gsem, ssem = bufs_and_sems[2 * ROW_BUFS:]
    b = pl.program_id(0)
    nact = nact_ref[0]
    active = b < nact
    last = b == nact - 1
    blk = xbufs[0].shape[0]
    tmb = blk // ROW_CHUNKS
    fresh = jnp.logical_or(b == 0, be_ref[b] != be_ref[jnp.maximum(b - 1, 0)])

    def row_tile(ref, row):
        return ref.at[pl.ds(pl.multiple_of(row * ROW_CHUNKS, ROW_CHUNKS), ROW_CHUNKS)]

    def start_gather(src_ref, s):
        for r in range(tmb):
            pltpu.make_async_copy(row_tile(h3_ref, src_ref[0, 0, r]), xbufs[s].at[pl.ds(r * ROW_CHUNKS, ROW_CHUNKS)],
                                  gsem.at[s]).start(priority=r % 2)

    def wait_gather(s):
        pltpu.make_async_copy(h3_ref.at[pl.ds(0, blk)], xbufs[s], gsem.at[s]).wait()

    def start_scatter(dst_ref, s):
        for r in range(tmb):
            pltpu.make_async_copy(ybufs[s].at[pl.ds(r * ROW_CHUNKS, ROW_CHUNKS)], row_tile(ys_ref, dst_ref[0, 0, r]),
                                  ssem.at[s]).start(priority=r % 2)

    def wait_scatter(s):
        pltpu.make_async_copy(ybufs[s], ys_ref.at[pl.ds(0, blk)], ssem.at[s]).wait()

    def chunk_rows(c):
        return pl.ds(c, tmb, stride=ROW_CHUNKS)

    def on_phase(cond, fn):
        for q in range(ROW_BUFS):
            pl.when(jnp.logical_and(cond, b % ROW_BUFS == q))(functools.partial(fn, q))

    @pl.when(jnp.logical_and(active, b == 0))
    def _():
        start_gather(src0_ref, 0)
        start_gather(src1_ref, 1)

    @pl.when(jnp.logical_and(active, fresh))
    def _():
        even = lax.broadcasted_iota(jnp.int32, (PREP_ROWS, LANES), 1) % 2 == 0

        def split_rows(r, carry):
            rows = pl.ds(pl.multiple_of(r * PREP_ROWS, PREP_ROWS), PREP_ROWS)
            for c in range(D_EXPERT // LANES):
                blk_a = win_ref[rows, 2 * c * LANES:(2 * c + 1) * LANES]
                blk_b = win_ref[rows, (2 * c + 1) * LANES:(2 * c + 2) * LANES]
                wg_s[rows, c * LANES:(c + 1) * LANES] = jnp.where(
                    even, blk_a, pltpu.roll(blk_b, 1, 1)).astype(BF16)
                wl_s[rows, c * LANES:(c + 1) * LANES] = jnp.where(
                    even, pltpu.roll(blk_a, LANES - 1, 1), blk_b).astype(BF16)
            return carry

        lax.fori_loop(0, D_MODEL // PREP_ROWS, split_rows, 0)
        half = LANES // 2
        for c in range(D_EXPERT // LANES):
            for p in range(2):
                for cb in range(D_MODEL // LANES):
                    wo_t[cb, pl.ds(c * LANES + p, half, stride=2), :] = (
                        wout_ref[pl.ds(c * LANES + half * p, half), cb * LANES:(cb + 1) * LANES])

        def cast_rows(r, carry):
            rows = pl.ds(pl.multiple_of(r * PREP_ROWS, PREP_ROWS), PREP_ROWS)
            for cb in range(D_MODEL // LANES):
                wo_s[rows, cb * LANES:(cb + 1) * LANES] = wo_t[cb, rows, :].astype(BF16)
            return carry

        lax.fori_loop(0, D_EXPERT // PREP_ROWS, cast_rows, 0)

    def compute(p):
        x = jnp.concatenate([xbufs[p][chunk_rows(c), :] for c in range(ROW_CHUNKS)], axis=1).astype(BF16)
        zg = _bdot(x, wg_s[...]) + bg_ref[...]
        zl = _bdot(x, wl_s[...]) + bl_ref[...]
        glu = jnp.minimum(zg, SWIGLU_LIMIT)
        lin = jnp.clip(zl, -SWIGLU_LIMIT, SWIGLU_LIMIT)
        a = glu * jax.nn.sigmoid(SWIGLU_ALPHA * glu) * (lin + 1.0)
        y = _bdot(a.astype(BF16), wo_s[...]) + bo_ref[...]
        for c in range(ROW_CHUNKS):
            ybufs[p][chunk_rows(c), :] = y[:, c * LANES:(c + 1) * LANES]

    @pl.when(jnp.logical_and(active, b == 0))
    def _():
        wait_gather(0)
        start_gather(src2_ref, 2)
        compute(0)

    def step(q):
        wait_gather(q)
        pl.when(b >= ROW_BUFS)(functools.partial(wait_scatter, q))
        start_gather(src2_ref, (q + 2) % ROW_BUFS)
        start_scatter(dstp_ref, (q - 1) % ROW_BUFS)
        compute(q)

    on_phase(jnp.logical_and(active, b > 0), step)

    def drain(q):
        start_scatter(dstc_ref, q)
        wait_gather((q + 1) % ROW_BUFS)
        wait_gather((q + 2) % ROW_BUFS)
        wait_scatter(q)
        for back in range(1, ROW_BUFS):
            pl.when(b >= back)(functools.partial(wait_scatter, (q - back) % ROW_BUFS))

    @pl.when(last)
    def _():
        on_phase(last, drain)
        spare0 = ys_ref.shape[0] - 2 * blk
        fills = []
        for s in range(2):
            ybufs[s][...] = jnp.zeros_like(ybufs[s])
            fills.append(pltpu.make_async_copy(ybufs[s], ys_ref.at[pl.ds(spare0 + s * blk, blk)], ssem.at[s]))
        for cp in fills:
            cp.start()
        for cp in fills:
            cp.wait()


def _experts(h3, src_rows, dst_rows, block_e, nact, n_slots, layer, w_in, b_glu, b_lin, w_out, b_out):
    tmb = EXPERT_ROWS
    n_blocks = src_rows.shape[0]
    exp3 = lambda b, be, na: (be[b], 0, 0)
    exp4 = lambda b, be, na: (layer, be[b], 0, 0)
    idx_spec = lambda f: pl.BlockSpec((1, 1, tmb), lambda b, be, na: (f(b), 0, 0), memory_space=pltpu.SMEM)
    grid_spec = pltpu.PrefetchScalarGridSpec(
        num_scalar_prefetch=2,
        grid=(n_blocks,),
        in_specs=[
            idx_spec(lambda b: b),
            idx_spec(lambda b: jnp.minimum(b + 1, n_blocks - 1)),
            idx_spec(lambda b: jnp.minimum(b + 2, n_blocks - 1)),
            idx_spec(lambda b: jnp.maximum(b - 1, 0)),
            idx_spec(lambda b: b),
            pl.BlockSpec(memory_space=pl.ANY),
            pl.BlockSpec((None, None, D_MODEL, 2 * D_EXPERT), exp4),
            pl.BlockSpec((None, 1, D_EXPERT), exp3),
            pl.BlockSpec((None, 1, D_EXPERT), exp3),
            pl.BlockSpec((None, None, D_EXPERT, D_MODEL), exp4),
            pl.BlockSpec((None, 1, D_MODEL), exp3),
        ],
        out_specs=pl.BlockSpec(memory_space=pl.ANY),
        scratch_shapes=[pltpu.VMEM((D_MODEL, D_EXPERT), BF16), pltpu.VMEM((D_MODEL, D_EXPERT), BF16),
                        pltpu.VMEM((D_MODEL // LANES, D_EXPERT, LANES), F32),
                        pltpu.VMEM((D_EXPERT, D_MODEL), BF16)]
        + [pltpu.VMEM((tmb * ROW_CHUNKS, LANES), F32)] * (2 * ROW_BUFS)
        + [pltpu.SemaphoreType.DMA((ROW_BUFS,)), pltpu.SemaphoreType.DMA((ROW_BUFS,))],
    )
    return pl.pallas_call(
        _expert_kernel,
        grid_spec=grid_spec,
        out_shape=jax.ShapeDtypeStruct((n_slots * ROW_CHUNKS, LANES), F32),
        compiler_params=_cparams(("arbitrary",)),
        name="moe_experts",
    )(block_e, nact, src_rows, src_rows, src_rows, dst_rows, dst_rows, h3, w_in, b_glu, b_lin, w_out, b_out)


def _moe(h_first, h_second, layer, w_router, b_router, w_in, b_in, w_out, b_out):
    tmb = EXPERT_ROWS
    top_e, gates, rank, counts, h3 = _router(h_first, h_second, w_router, b_router)
    n = h3.shape[0] // ROW_CHUNKS
    m = n * TOP_K
    padded = (counts + tmb - 1) // tmb * tmb
    pend = jnp.cumsum(padded)
    pstart = pend - padded
    hot = top_e[..., None] == jnp.arange(N_EXPERTS)[None, None, :]
    dest = jnp.sum(jnp.where(hot, pstart[None, None, :], 0), axis=-1) + rank
    n_blocks = m // tmb + N_EXPERTS
    blk_row0 = jnp.arange(n_blocks, dtype=jnp.int32) * tmb
    block_e = jnp.minimum(jnp.sum(pend[None, :] <= blk_row0[:, None], axis=1), N_EXPERTS - 1).astype(jnp.int32)
    nact = (pend[-1] // tmb).astype(jnp.int32).reshape(1)
    slot = jnp.arange(n, dtype=jnp.int32)[None, :] + n * jnp.arange(TOP_K, dtype=jnp.int32)[:, None]
    j = jnp.arange(N_EXPERTS * tmb, dtype=jnp.int32)
    off = j % tmb
    pad = padded - counts
    rep = lambda v: jnp.repeat(v, tmb)
    pad_key = jnp.where(off < rep(pad), rep(pstart + counts) + off, pend[-1] + j - rep(jnp.cumsum(pad)))
    keys = jnp.concatenate([dest.reshape(m), pad_key]).astype(jnp.int32)
    vals = jnp.concatenate([slot.reshape(m), jnp.full_like(j, -1)])
    _, row_slot = lax.sort((keys, vals), num_keys=1, is_stable=False)
    row = jnp.arange(n_blocks * tmb, dtype=jnp.int32)
    src_rows = jnp.where(row_slot < 0, 0, row_slot % n)
    dst_rows = jnp.where(row_slot < 0, m + ((row // tmb) % 2) * tmb + row % tmb, row_slot)
    b_glu = _pair_perm(b_in[:, 0::2]).reshape(N_EXPERTS, 1, D_EXPERT)
    b_lin = _pair_perm(b_in[:, 1::2]).reshape(N_EXPERTS, 1, D_EXPERT)
    ys = _experts(h3, src_rows.reshape(n_blocks, 1, tmb), dst_rows.reshape(n_blocks, 1, tmb), block_e, nact,
                  m + 2 * tmb, layer, w_in, b_glu, b_lin, w_out, b_out.reshape(N_EXPERTS, 1, D_MODEL))
    return ys, gates.T


def _post_kernel(x1_ref, *rest, with_kv):
    ys_refs = rest[:TOP_K]
    gate_ref, mod_ref, lng_ref, lnb_ref = rest[TOP_K:TOP_K + 4]
    rest = rest[TOP_K + 4:]
    g2 = mod_ref[5:6, :]
    tm = x1_ref.shape[0]
    f = None
    for k in range(TOP_K):
        yk = jnp.concatenate([ys_refs[k][pl.ds(c, tm, stride=ROW_CHUNKS), :] for c in range(ROW_CHUNKS)], axis=1)
        term = yk * gate_ref[:, k:k + 1]
        f = term if f is None else f + term
    x2 = _layer_norm(DN_ALPHA * x1_ref[...] + g2 * f, lng_ref[...], lnb_ref[...])
    if not with_kv:
        (x2_ref,) = rest
        x2_ref[...] = x2
        return
    wa_ref, kng_ref, cos_ref, sin_ref, x2_ref, lat_ref, kr_ref, kcat_ref = rest
    x2_ref[...] = x2
    kv = _bdot(x2.astype(BF16), wa_ref[...])
    c = kv[:, :KV_RANK]
    lat = c * lax.rsqrt(jnp.mean(c * c, axis=-1, keepdims=True) + RMS_EPS) * kng_ref[...]
    k = kv[:, KV_RANK:KV_RANK + QK_ROPE]
    k_swapped = kv[:, KV_RANK + QK_ROPE:]
    kr = k * cos_ref[...] + k_swapped * sin_ref[...]
    lat_ref[...] = lat
    kr_ref[...] = kr
    kcat_ref[:, :KV_RANK] = lat.astype(BF16)
    kcat_ref[:, KV_RANK:] = kr.astype(BF16)


def _swap_halves_cols(w, width):
    shp = w.shape
    w4 = w.reshape(shp[:-1] + (shp[-1] // width, 2, width // 2))
    return w4[..., ::-1, :].reshape(shp)


def _rope_tables(pos, reps):
    inv = 1.0 / (ROPE_BASE ** (jnp.arange(ROPE_HALF, dtype=F32) * (2.0 / QK_ROPE)))
    ang = pos.astype(F32)[:, None] * inv[None, :]
    cos, sin = jnp.cos(ang), jnp.sin(ang)
    cos_t = jnp.tile(jnp.concatenate([cos, cos], -1), (1, reps))
    sin_t = jnp.tile(jnp.concatenate([-sin, sin], -1), (1, reps))
    return cos_t, sin_t


def _post_layer(x1, ys, gates, row0, mod, ln_g, ln_b, *, tm, kv=None):
    bsz, s, d = x1.shape
    blk0 = row0 // tm
    per_b = s // tm
    tile = lambda w: pl.BlockSpec((None, tm, w), lambda b, i: (b, i, 0))
    n_tok = gates.shape[0]
    pick_spec = lambda k: pl.BlockSpec((tm * ROW_CHUNKS, LANES),
                                       lambda b, i: (k * (n_tok // tm) + blk0 + b * per_b + i, 0))
    in_specs = [tile(d)] + [pick_spec(k) for k in range(TOP_K)] + [
        pl.BlockSpec((tm, TOP_K), lambda b, i: (blk0 + b * per_b + i, 0)),
        pl.BlockSpec((None, 6, d), lambda b, i: (b, 0, 0)),
        _const_spec((1, d)),
        _const_spec((1, d)),
    ]
    args = [x1] + [ys] * TOP_K + [gates, mod, ln_g.reshape(1, -1), ln_b.reshape(1, -1)]
    out_shape = [jax.ShapeDtypeStruct((bsz, s, d), F32)]
    out_specs = [tile(d)]
    if kv is not None:
        w_a, kn_g, cos_t, sin_t = kv
        w_ext = jnp.concatenate([w_a, _swap_halves_cols(w_a[:, KV_RANK:], QK_ROPE)], axis=1).astype(BF16)
        in_specs += [_const_spec(w_ext.shape), _const_spec((1, KV_RANK)),
                     pl.BlockSpec((tm, QK_ROPE), lambda b, i: (i, 0)),
                     pl.BlockSpec((tm, QK_ROPE), lambda b, i: (i, 0))]
        args += [w_ext, kn_g.reshape(1, -1), cos_t, sin_t]
        out_shape += [jax.ShapeDtypeStruct((bsz, s, KV_RANK), F32), jax.ShapeDtypeStruct((bsz, s, QK_ROPE), F32),
                      jax.ShapeDtypeStruct((bsz, s, QK_LAT), BF16)]
        out_specs += [tile(KV_RANK), tile(QK_ROPE), tile(QK_LAT)]
    return pl.pallas_call(
        functools.partial(_post_kernel, with_kv=kv is not None),
        grid=(bsz, per_b),
        in_specs=in_specs,
        out_specs=out_specs,
        out_shape=out_shape,
        compiler_params=_cparams(("arbitrary", "arbitrary")),
        name="post_moe",
    )(*args)


Q_GROUP = 512
KEY_TILE = 512


def _qproj_kernel(x_ref, mod_ref, wdq_ref, qng_ref, wqn_ref, wqr_ref, wqrs_ref, wuk_ref, cos_ref, sin_ref, q_ref,
                  *, tq):
    x = x_ref[...]
    sh1, sc1 = mod_ref[0:1, :], mod_ref[1:2, :]
    h = (x * (1.0 + sc1) + sh1).astype(BF16)
    tm = x.shape[0]
    cq = lax.dot_general(wdq_ref[...], h, (((1,), (1,)), ((), ())), preferred_element_type=F32)
    cq = (cq * lax.rsqrt(jnp.mean(cq * cq, axis=0, keepdims=True) + RMS_EPS) * qng_ref[...]).astype(BF16)
    qn = _bdot(wqn_ref[...], cq)
    qr = _bdot(wqr_ref[...], cq)
    qrs = _bdot(wqrs_ref[...], cq)
    rope = (qr * cos_ref[...] + qrs * sin_ref[...]).astype(BF16)
    per_group = Q_GROUP // tq
    for hd in range(MLA_HEADS):
        ql = _bdot(wuk_ref[hd], qn[hd * QK_NOPE:(hd + 1) * QK_NOPE, :].astype(BF16)).astype(BF16)
        g, off = hd // per_group, (hd % per_group) * tq
        for t in range(tm // tq):
            q_ref[t, g, 0:KV_RANK, off:off + tq] = ql[:, t * tq:(t + 1) * tq]
            q_ref[t, g, KV_RANK:QK_LAT, off:off + tq] = rope[hd * QK_ROPE:(hd + 1) * QK_ROPE, t * tq:(t + 1) * tq]


def _qproj_layer(x, mod, w_dq, qn_g, w_qb, w_uk, cos_t, sin_t, *, tm, tq):
    bsz, s, d = x.shape
    w_qn = w_qb[:, :, :QK_NOPE].reshape(Q_RANK, MLA_HEADS * QK_NOPE)
    w_qr = w_qb[:, :, QK_NOPE:].reshape(Q_RANK, MLA_HEADS * QK_ROPE)
    w_qrs = _swap_halves_cols(w_qr, QK_ROPE)
    wuk_h = jnp.transpose(w_uk, (1, 0, 2)).astype(BF16)
    hr = MLA_HEADS * QK_ROPE
    n_groups = MLA_HEADS * tq // Q_GROUP
    nt = tm // tq
    return pl.pallas_call(
        functools.partial(_qproj_kernel, tq=tq),
        grid=(bsz, s // tm),
        in_specs=[
            pl.BlockSpec((None, tm, d), lambda b, i: (b, i, 0)),
            pl.BlockSpec((None, 6, d), lambda b, i: (b, 0, 0)),
            _const_spec((Q_RANK, d)),
            _const_spec((Q_RANK, 1)),
            _const_spec((MLA_HEADS * QK_NOPE, Q_RANK)),
            _const_spec((hr, Q_RANK)),
            _const_spec((hr, Q_RANK)),
            _const_spec(wuk_h.shape),
            pl.BlockSpec((hr, tm), lambda b, i: (0, i)),
            pl.BlockSpec((hr, tm), lambda b, i: (0, i)),
        ],
        out_specs=pl.BlockSpec((None, nt, n_groups, QK_LAT, Q_GROUP), lambda b, i: (b, i, 0, 0, 0)),
        out_shape=jax.ShapeDtypeStruct((bsz, s // tq, n_groups, QK_LAT, Q_GROUP), BF16),
        compiler_params=_cparams(("arbitrary", "arbitrary")),
        name="mla_qproj",
    )(x, mod, w_dq.T.astype(BF16), qn_g.reshape(-1, 1), w_qn.T.astype(BF16), w_qr.T.astype(BF16),
      w_qrs.T.astype(BF16), wuk_h, cos_t.T, sin_t.T)


EXP2_SCALE = ATTN_SCALE * 1.4426950408889634


def _attn_kernel(q_ref, *rest, tq, causal, n_main, tail, split_keys):
    if split_keys:
        klat_ref, kkr_ref = rest[:2]
        rest = rest[2:]
    else:
        k_ref = rest[0]
        rest = rest[1:]
    if tail:
        kt_ref = rest[0]
        rest = rest[1:]
    x_ref, mod_ref, wuv_ref, wo_ref, lng_ref, lnb_ref, x1_ref, h2_ref, m_s, l_s, acc_s, ta_s, tb_s = rest
    i = pl.program_id(1)
    n_groups = q_ref.shape[0]
    per_group = Q_GROUP // tq
    m_s[...] = jnp.full_like(m_s, NEG_BIG)
    l_s[...] = jnp.zeros_like(l_s)
    acc_s[...] = jnp.zeros_like(acc_s)

    def tile_rows(j):
        return pl.ds(pl.multiple_of(j * KEY_TILE, KEY_TILE), KEY_TILE)

    def key_rows(j):
        if split_keys:
            return jnp.concatenate([klat_ref[tile_rows(j), :].astype(BF16), kkr_ref[tile_rows(j), :].astype(BF16)],
                                   axis=1)
        return k_ref[tile_rows(j), :]

    def value_rows(j):
        if split_keys:
            return klat_ref[tile_rows(j), :].astype(BF16)
        return k_ref[tile_rows(j), :KV_RANK]

    def softmax_step(g, t, v, key0, masked):
        t = t * EXP2_SCALE
        if masked:
            n = t.shape[0]
            k_chunk = (key0 + lax.broadcasted_iota(jnp.int32, (n, 1), 0)) // CHUNK
            q_chunk = (i * tq + lax.broadcasted_iota(jnp.int32, (1, Q_GROUP), 1) % tq) // CHUNK
            t = jnp.where(k_chunk <= q_chunk, t, -jnp.inf)
        m_old = m_s[g]
        m_new = jnp.maximum(m_old, jnp.max(t, axis=0, keepdims=True))
        alpha = jnp.exp2(m_old - m_new)
        p = jnp.exp2(t - m_new)
        l_s[g] = alpha * l_s[g] + jnp.sum(p, axis=0, keepdims=True)
        acc_s[g] = alpha * acc_s[g] + lax.dot_general(
            v, p.astype(BF16), (((0,), (0,)), ((), ())), preferred_element_type=F32)
        m_s[g] = m_new

    def stage(cur, nxt, j, masked, make_next):
        v = value_rows(j)
        k_next = key_rows(j + 1) if make_next else None
        for g in range(n_groups):
            if make_next:
                nxt[g] = _bdot(k_next, q_ref[g])
            softmax_step(g, cur[g], v, j * KEY_TILE, masked)

    n_open = (i * tq) // KEY_TILE if causal else n_main - 1
    k0 = key_rows(0)
    for g in range(n_groups):
        ta_s[g] = _bdot(k0, q_ref[g])

    def pair_body(p, c):
        stage(ta_s, tb_s, 2 * p, False, True)
        stage(tb_s, ta_s, 2 * p + 1, False, True)
        return c

    lax.fori_loop(0, n_open // 2, pair_body, 0)
    if causal:
        @pl.when(n_open % 2 == 0)
        def _():
            stage(ta_s, tb_s, n_open, True, False)

        @pl.when(n_open % 2 == 1)
        def _():
            stage(ta_s, tb_s, n_open - 1, False, True)
            stage(tb_s, ta_s, n_open, True, False)
    elif n_open % 2 == 0:
        stage(ta_s, tb_s, n_open, False, False)
    else:
        stage(ta_s, tb_s, n_open - 1, False, True)
        stage(tb_s, ta_s, n_open, False, False)
    if tail:
        kt = kt_ref[...]
        for g in range(n_groups):
            softmax_step(g, _bdot(kt, q_ref[g]), kt[:, :KV_RANK], n_main * KEY_TILE, False)

    heads = []
    for g in range(n_groups):
        o = (acc_s[g] / l_s[g]).T
        for hh in range(per_group):
            hd = g * per_group + hh
            heads.append(_bdot(o[hh * tq:(hh + 1) * tq].astype(BF16), wuv_ref[hd]))
    oc = jnp.concatenate(heads, axis=-1).astype(BF16)
    y = _bdot(oc, wo_ref[...])
    x = x_ref[...]
    g1, sh2, sc2 = mod_ref[2:3, :], mod_ref[3:4, :], mod_ref[4:5, :]
    x1 = _layer_norm(DN_ALPHA * x + g1 * y, lng_ref[...], lnb_ref[...])
    x1_ref[...] = x1
    h2_ref[...] = x1 * (1.0 + sc2) + sh2


def _attn_layer(q, keys, x, mod, w_uv, w_o, ln_g, ln_b, *, tq, causal, k_tail=None):
    bsz, s, d = x.shape
    split_keys = isinstance(keys, tuple)
    keys = keys if split_keys else (keys,)
    t_main = keys[0].shape[1]
    n_main = t_main // KEY_TILE
    n_groups = q.shape[2]
    tail = k_tail is not None
    wuv_t = jnp.transpose(w_uv, (1, 0, 2)).astype(BF16)
    tile = pl.BlockSpec((None, tq, d), lambda b, i: (b, i, 0))
    in_specs = [pl.BlockSpec((None, None, n_groups, QK_LAT, Q_GROUP), lambda b, i: (b, i, 0, 0, 0))]
    in_specs += [pl.BlockSpec((None, t_main, k.shape[2]), lambda b, i: (b, 0, 0)) for k in keys]
    args = [q, *keys]
    if tail:
        in_specs.append(pl.BlockSpec((None, k_tail.shape[1], QK_LAT), lambda b, i: (b, 0, 0)))
        args.append(k_tail)
    in_specs += [
        tile,
        pl.BlockSpec((None, 6, d), lambda b, i: (b, 0, 0)),
        _const_spec(wuv_t.shape),
        _const_spec((MLA_HEADS * V_DIM, d)),
        _const_spec((1, d)),
        _const_spec((1, d)),
    ]
    args += [x, mod, wuv_t, w_o.astype(BF16), ln_g.reshape(1, -1), ln_b.reshape(1, -1)]
    return pl.pallas_call(
        functools.partial(_attn_kernel, tq=tq, causal=causal, n_main=n_main, tail=tail, split_keys=split_keys),
        grid=(bsz, s // tq),
        in_specs=in_specs,
        out_specs=[tile, tile],
        out_shape=[jax.ShapeDtypeStruct((bsz, s, d), F32), jax.ShapeDtypeStruct((bsz, s, d), F32)],
        scratch_shapes=[pltpu.VMEM((n_groups, 1, Q_GROUP), F32), pltpu.VMEM((n_groups, 1, Q_GROUP), F32),
                        pltpu.VMEM((n_groups, KV_RANK, Q_GROUP), F32),
                        pltpu.VMEM((n_groups, KEY_TILE, Q_GROUP), F32),
                        pltpu.VMEM((n_groups, KEY_TILE, Q_GROUP), F32)],
        compiler_params=_cparams(("arbitrary", "arbitrary")),
        name="mla_attention",
    )(*args)


GMLP_ROWS = 256
TOKEN_ROWS = 512
ATTN_Q_ROWS = 128


def kernel(x_prompt, x_sample, cache_kv_latent, cache_k_rope, c_prompt, c_sample, ada_w, ada_b, ln_g, ln_b,
           a_w_in, a_b_in, a_vn_g, a_vn_b, a_w_s, a_b_s, a_w_out, a_b_out, kv_w_a, kv_norm_g, kv_w_uk, kv_w_uv,
           q_w_a, q_norm_g, q_w_b, q_w_o, moe_w_router, moe_b_router, moe_w_in, moe_b_in, moe_w_out, moe_b_out):
    bp, sp, d = x_prompt.shape
    bs, ss, _ = x_sample.shape
    past = cache_kv_latent.shape[1]
    n_p = bp * sp
    n_s = bs * ss
    assert d == D_MODEL and sp % TOKEN_ROWS == 0 and sp % GMLP_ROWS == 0 and sp % KEY_TILE == 0
    assert ss <= CHUNK and Q_GROUP % ss == 0 and past % KEY_TILE == 0
    assert n_p % ROUTER_ROWS == 0 and n_s % ROUTER_ROWS == 0 and (n_p + n_s) * TOP_K % EXPERT_ROWS == 0

    mod = _ada_mod(jnp.concatenate([c_prompt, c_sample], axis=0), ada_w, ada_b)
    mod = mod.reshape(DEPTH, bp + bs, 6, d)
    mod_p, mod_s = mod[:, :bp], mod[:, bp:]

    pos_p = jnp.arange(sp, dtype=jnp.int32)
    pos_s = past + jnp.arange(ss, dtype=jnp.int32)

    def moe(l, h2_p, h2_s):
        return _moe(h2_p.reshape(n_p, d), h2_s.reshape(n_s, d), l, moe_w_router[l], moe_b_router[l],
                    moe_w_in, moe_b_in[l], moe_w_out, moe_b_out[l])

    gm = functools.partial(_gmlp_layer, w_in=a_w_in[0], b_in=a_b_in[0], vn_g=a_vn_g[0], vn_b=a_vn_b[0],
                           w_s=a_w_s[0], b_s=a_b_s[0], w_out=a_w_out[0], b_out=a_b_out[0],
                           ln_g=ln_g[0, 0], ln_b=ln_b[0, 0])
    x1_p, h2_p = gm(x_prompt, mod_p[0], tm=GMLP_ROWS, write_v=False)
    x1_s, h2_s, v_s = gm(x_sample, mod_s[0], tm=ss, write_v=True)
    ys, gates = moe(0, h2_p, h2_s)
    x2_p, lat_p, kr_p, kcat_p = _post_layer(x1_p, ys, gates, 0, mod_p[0], ln_g[0, 1], ln_b[0, 1], tm=TOKEN_ROWS,
                                            kv=(kv_w_a, kv_norm_g) + _rope_tables(pos_p, 1))
    x2_s, lat_s, kr_s, kcat_s = _post_layer(x1_s, ys, gates, n_p, mod_s[0], ln_g[0, 1], ln_b[0, 1], tm=ss,
                                            kv=(kv_w_a, kv_norm_g) + _rope_tables(pos_s, 1))

    qp = functools.partial(_qproj_layer, w_dq=q_w_a[0], qn_g=q_norm_g[0], w_qb=q_w_b[0], w_uk=kv_w_uk)
    at = functools.partial(_attn_layer, w_uv=kv_w_uv, w_o=q_w_o[0], ln_g=ln_g[1, 0], ln_b=ln_b[1, 0])
    cos_p, sin_p = _rope_tables(pos_p, MLA_HEADS)
    cos_s, sin_s = _rope_tables(pos_s, MLA_HEADS)
    q_p = qp(x2_p, mod_p[1], cos_t=cos_p, sin_t=sin_p, tm=TOKEN_ROWS, tq=ATTN_Q_ROWS)
    q_s = qp(x2_s, mod_s[1], cos_t=cos_s, sin_t=sin_s, tm=ss, tq=ss)
    x3_p, h4_p = at(q_p, kcat_p, x2_p, mod_p[1], tq=ATTN_Q_ROWS, causal=True)
    x3_s, h4_s = at(q_s, (cache_kv_latent, cache_k_rope), x2_s, mod_s[1], tq=ss, causal=False, k_tail=kcat_s)
    ys, gates = moe(1, h4_p, h4_s)
    (y_p,) = _post_layer(x3_p, ys, gates, 0, mod_p[1], ln_g[1, 1], ln_b[1, 1], tm=TOKEN_ROWS)
    (y_s,) = _post_layer(x3_s, ys, gates, n_p, mod_s[1], ln_g[1, 1], ln_b[1, 1], tm=ss)

    return (y_p, y_s, lat_p, kr_p, lat_s, kr_s, v_s[None])
```

```python
import functools

import jax
import jax.numpy as jnp
from jax import lax
from jax.experimental import pallas as pl
from jax.experimental.pallas import tpu as pltpu

D_MODEL = 1024
DEPTH = 2
CHUNK = 64
A_CHUNK = 128
A_HALF = 2 * D_MODEL
A_GROUPS = 8
A_GROUP_W = A_HALF // A_GROUPS
MLA_HEADS = 8
QK_NOPE = 128
QK_ROPE = 64
ROPE_HALF = QK_ROPE // 2
V_DIM = 128
KV_RANK = 256
Q_RANK = 512
QK_LAT = KV_RANK + QK_ROPE
ROPE_BASE = 10000.0
ATTN_SCALE = (QK_NOPE + QK_ROPE) ** -0.5
N_EXPERTS = 32
TOP_K = 4
D_EXPERT = D_MODEL
SWIGLU_LIMIT = 7.0
SWIGLU_ALPHA = 1.702
DN_ALPHA = (2 * DEPTH) ** 0.25
LN_EPS = 1e-5
RMS_EPS = 1e-6

BF16 = jnp.bfloat16
F32 = jnp.float32

LANES = 128
ROW_CHUNKS = D_MODEL // LANES
VMEM_LIMIT = 56 * 1024 * 1024
EXPERT_ROWS = 256
ROUTER_ROWS = 512
NEG_BIG = -1e30


def _cparams(sem):
    return pltpu.CompilerParams(dimension_semantics=sem, vmem_limit_bytes=VMEM_LIMIT)


def _const_spec(shape):
    nd = len(shape)
    return pl.BlockSpec(shape, lambda *_: (0,) * nd, pipeline_mode=pl.Buffered(1))


def _layer_norm(r, g, b):
    rc = r - jnp.mean(r, axis=-1, keepdims=True)
    var = jnp.mean(rc * rc, axis=-1, keepdims=True)
    return rc * lax.rsqrt(var + LN_EPS) * g + b


def _gelu_tanh(x):
    c = 0.7978845608028654
    return 0.5 * x * (1.0 + jnp.tanh(c * (x + 0.044715 * (x * x * x))))


def _bdot(a, b):
    return jnp.dot(a, b, preferred_element_type=F32)


def _ada_kernel(c_ref, w_ref, b_ref, o_ref):
    c = c_ref[...]
    s = c * jax.nn.sigmoid(c)
    o_ref[...] = jnp.dot(s, w_ref[...], preferred_element_type=F32,
                         precision=lax.Precision.HIGHEST) + b_ref[...]


def _ada_mod(c_all, ada_w, ada_b):
    nb = c_all.shape[0]
    six_d = ada_w.shape[-1]
    tn = D_MODEL
    return pl.pallas_call(
        _ada_kernel,
        grid=(DEPTH, six_d // tn),
        in_specs=[
            pl.BlockSpec((nb, D_MODEL), lambda l, j: (0, 0)),
            pl.BlockSpec((None, D_MODEL, tn), lambda l, j: (l, 0, j)),
            pl.BlockSpec((None, 1, tn), lambda l, j: (l, 0, j)),
        ],
        out_specs=pl.BlockSpec((None, nb, tn), lambda l, j: (l, 0, j)),
        out_shape=jax.ShapeDtypeStruct((DEPTH, nb, six_d), F32),
        compiler_params=_cparams(("arbitrary", "arbitrary")),
        name="ada_mod",
    )(c_all, ada_w, ada_b.reshape(DEPTH, 1, six_d))


def _gmlp_kernel(x_ref, mod_ref, w_in_ref, b_in_ref, vng_ref, vnb_ref, ws_ref, bs_ref,
                 w_out_ref, b_out_ref, lng_ref, lnb_ref, *rest, write_v):
    if write_v:
        x1_ref, h2_ref, v_ref, u_s, v_s, p_s = rest
    else:
        x1_ref, h2_ref, u_s, v_s, p_s = rest
    x = x_ref[...]
    sh1, sc1, g1 = mod_ref[0:1, :], mod_ref[1:2, :], mod_ref[2:3, :]
    sh2, sc2 = mod_ref[3:4, :], mod_ref[4:5, :]
    h = (x * (1.0 + sc1) + sh1).astype(BF16)
    tm = x.shape[0]
    ch = 512
    n_ch = A_HALF // ch
    for j in range(n_ch):
        sl = slice(j * ch, (j + 1) * ch)
        u_s[:, sl] = _gelu_tanh(_bdot(h, w_in_ref[:, sl]) + b_in_ref[:, sl])
    tot = jnp.zeros((tm, 1), F32)
    for j in range(n_ch):
        sl = slice(j * ch, (j + 1) * ch)
        slw = slice(A_HALF + j * ch, A_HALF + (j + 1) * ch)
        g = _gelu_tanh(_bdot(h, w_in_ref[:, slw]) + b_in_ref[:, slw])
        v_s[:, sl] = g
        tot = tot + jnp.sum(g, axis=-1, keepdims=True)
    mean = tot * (1.0 / A_HALF)
    sq = jnp.zeros((tm, 1), F32)
    for j in range(n_ch):
        sl = slice(j * ch, (j + 1) * ch)
        c = v_s[:, sl] - mean
        sq = sq + jnp.sum(c * c, axis=-1, keepdims=True)
    rstd = lax.rsqrt(sq * (1.0 / A_HALF) + LN_EPS)
    for g in range(A_GROUPS):
        sl = slice(g * A_GROUP_W, (g + 1) * A_GROUP_W)
        vn = (v_s[:, sl] - mean) * rstd * vng_ref[:, sl] + vnb_ref[:, sl]
        if write_v:
            v_ref[:, sl] = vn
        sg = _bdot(ws_ref[g], vn.astype(BF16)) + bs_ref[:, g:g + 1]
        p_s[:, sl] = (u_s[:, sl] * sg).astype(BF16)
    y = _bdot(p_s[...], w_out_ref[...]) + b_out_ref[...]
    x1 = _layer_norm(DN_ALPHA * x + g1 * y, lng_ref[...], lnb_ref[...])
    x1_ref[...] = x1
    h2_ref[...] = x1 * (1.0 + sc2) + sh2


def _gmlp_layer(x, mod, w_in, b_in, vn_g, vn_b, w_s, b_s, w_out, b_out, ln_g, ln_b, *, tm, write_v):
    bsz, s, d = x.shape
    seg = min(s, A_CHUNK)
    idx = jnp.arange(seg)
    mask = (idx[:, None] // CHUNK) >= (idx[None, :] // CHUNK)
    wm = jnp.where(mask[None], w_s[:, :seg, :seg], 0.0)
    reps = tm // seg
    eye = jnp.eye(reps, dtype=F32)
    ws_big = jnp.einsum("ab,gij->gaibj", eye, wm).reshape(A_GROUPS, tm, tm).astype(BF16)
    bs_big = jnp.tile(b_s[:, :seg].T, (reps, 1))
    out_shape = [jax.ShapeDtypeStruct((bsz, s, d), F32), jax.ShapeDtypeStruct((bsz, s, d), F32)]
    tile = lambda w: pl.BlockSpec((None, tm, w), lambda b, i: (b, i, 0))
    out_specs = [tile(d), tile(d)]
    if write_v:
        out_shape.append(jax.ShapeDtypeStruct((bsz, s, A_HALF), F32))
        out_specs.append(tile(A_HALF))
    return pl.pallas_call(
        functools.partial(_gmlp_kernel, write_v=write_v),
        grid=(bsz, s // tm),
        in_specs=[
            tile(d),
            pl.BlockSpec((None, 6, d), lambda b, i: (b, 0, 0)),
            _const_spec((d, 2 * A_HALF)),
            _const_spec((1, 2 * A_HALF)),
            _const_spec((1, A_HALF)),
            _const_spec((1, A_HALF)),
            _const_spec((A_GROUPS, tm, tm)),
            _const_spec((tm, A_GROUPS)),
            _const_spec((A_HALF, d)),
            _const_spec((1, d)),
            _const_spec((1, d)),
            _const_spec((1, d)),
        ],
        out_specs=out_specs,
        out_shape=out_shape,
        scratch_shapes=[pltpu.VMEM((tm, A_HALF), F32), pltpu.VMEM((tm, A_HALF), F32),
                        pltpu.VMEM((tm, A_HALF), BF16)],
        compiler_params=_cparams(("arbitrary", "arbitrary")),
        name="gmlp_layer",
    )(x, mod, w_in.astype(BF16), b_in.reshape(1, -1), vn_g.reshape(1, -1), vn_b.reshape(1, -1),
      ws_big, bs_big, w_out.astype(BF16), b_out.reshape(1, -1), ln_g.reshape(1, -1), ln_b.reshape(1, -1))


def _router_kernel(hp_ref, hs_ref, wr_ref, br_ref, tri_ref, e_ref, g_ref, r_ref, cnt_ref, h3_ref, base_s,
                   *, n_first):
    i = pl.program_id(0)

    @pl.when(i == 0)
    def _():
        base_s[...] = jnp.zeros_like(base_s)

    h = jnp.where(i < n_first, hp_ref[...], hs_ref[...])
    rows = h.shape[0]
    for c in range(ROW_CHUNKS):
        h3_ref[pl.ds(c, rows, stride=ROW_CHUNKS), :] = h[:, c * LANES:(c + 1) * LANES]
    logits = lax.dot_general(wr_ref[...], h, (((1,), (1,)), ((), ())),
                             preferred_element_type=F32,
                             precision=lax.Precision.HIGHEST) + br_ref[...]
    iota = lax.broadcasted_iota(jnp.int32, (N_EXPERTS, rows), 0)
    l = logits
    vals, hots = [], []
    for k in range(TOP_K):
        m = jnp.max(l, axis=0, keepdims=True)
        idx = jnp.min(jnp.where(l == m, iota, N_EXPERTS), axis=0, keepdims=True)
        hot = iota == idx
        vals.append(m)
        hots.append(hot)
        e_ref[k:k + 1, :] = idx
        l = jnp.where(hot, -jnp.inf, l)
    exps = [jnp.exp(v - vals[0]) for v in vals]
    den = exps[0] + exps[1] + exps[2] + exps[3]
    for k in range(TOP_K):
        g_ref[k:k + 1, :] = exps[k] / den
    hot_all = jnp.where(hots[0] | hots[1] | hots[2] | hots[3], 1.0, 0.0)
    before = _bdot(hot_all.astype(BF16), tri_ref[...]) + base_s[:, 0:1]
    for k in range(TOP_K):
        r_ref[k:k + 1, :] = jnp.sum(jnp.where(hots[k], before, 0.0), axis=0, keepdims=True).astype(jnp.int32)
    base_s[...] = base_s[...] + jnp.sum(hot_all, axis=1, keepdims=True)
    cnt_ref[...] = base_s[...].astype(jnp.int32)


def _router(h_first, h_second, w_router, b_router):
    tr = ROUTER_ROWS
    n_first = h_first.shape[0] // tr
    n_second = h_second.shape[0] // tr
    n = h_first.shape[0] + h_second.shape[0]
    tri = (jnp.arange(tr)[:, None] < jnp.arange(tr)[None, :]).astype(BF16)
    sel = pl.BlockSpec((TOP_K, tr), lambda i: (0, i))
    top_e, gates, rank, cnt, h3 = pl.pallas_call(
        functools.partial(_router_kernel, n_first=n_first),
        grid=(n_first + n_second,),
        in_specs=[
            pl.BlockSpec((tr, D_MODEL), lambda i: (jnp.minimum(i, n_first - 1), 0)),
            pl.BlockSpec((tr, D_MODEL), lambda i: (jnp.maximum(i - n_first, 0), 0)),
            _const_spec((N_EXPERTS, D_MODEL)),
            _const_spec((N_EXPERTS, 1)),
            _const_spec((tr, tr)),
        ],
        out_specs=[sel, sel, sel, pl.BlockSpec((N_EXPERTS, LANES), lambda i: (0, 0)),
                   pl.BlockSpec((tr * ROW_CHUNKS, LANES), lambda i: (i, 0))],
        out_shape=[jax.ShapeDtypeStruct((TOP_K, n), jnp.int32), jax.ShapeDtypeStruct((TOP_K, n), F32),
                   jax.ShapeDtypeStruct((TOP_K, n), jnp.int32),
                   jax.ShapeDtypeStruct((N_EXPERTS, LANES), jnp.int32),
                   jax.ShapeDtypeStruct((n * ROW_CHUNKS, LANES), F32)],
        scratch_shapes=[pltpu.VMEM((N_EXPERTS, LANES), F32)],
        compiler_params=_cparams(("arbitrary",)),
        name="moe_router",
    )(h_first, h_second, w_router.T, b_router.reshape(N_EXPERTS, 1), tri)
    return top_e, gates, rank, cnt[:, 0], h3


PREP_ROWS = 64


def _pair_perm(v):
    shp = v.shape
    v4 = v.reshape(shp[:-1] + (shp[-1] // LANES, 2, LANES // 2))
    return jnp.swapaxes(v4, -1, -2).reshape(shp)


ROW_BUFS = 4


IDX_SRC0, IDX_SRC1, IDX_SRC2, IDX_DST_PREV, IDX_DST = range(5)


def _expert_kernel(be_ref, nact_ref, idx_ref, h3_ref,
                   win_ref, bg_ref, bl_ref, wout_ref, bo_ref, ys_ref,
                   wg_s, wl_s, wo_t, wo_s, *bufs_and_sems):
    xbufs, ybufs = bufs_and_sems[:ROW_BUFS], bufs_and_sems[ROW_BUFS:2 * ROW_BUFS]
    gsem, ssem = bufs_and_sems[2 * ROW_BUFS:]
    b = pl.program_id(0)
    nact = nact_ref[0]
    active = b < nact
    last = b == nact - 1
    blk = xbufs[0].shape[0]
    tmb = blk // ROW_CHUNKS
    fresh = jnp.logical_or(b == 0, be_ref[b] != be_ref[jnp.maximum(b - 1, 0)])

    def row_tile(ref, row):
        return ref.at[pl.ds(pl.multiple_of(row * ROW_CHUNKS, ROW_CHUNKS), ROW_CHUNKS)]

    def start_gather(section, s):
        for r in range(tmb):
            src = idx_ref[0, 0, section * tmb + r]
            pltpu.make_async_copy(row_tile(h3_ref, src), xbufs[s].at[pl.ds(r * ROW_CHUNKS, ROW_CHUNKS)],
                                  gsem.at[s]).start(priority=r % 2)

    def wait_gather(s):
        pltpu.make_async_copy(h3_ref.at[pl.ds(0, blk)], xbufs[s], gsem.at[s]).wait()

    def start_scatter(section, s):
        for r in range(tmb):
            dst = idx_ref[0, 0, section * tmb + r]
            pltpu.make_async_copy(ybufs[s].at[pl.ds(r * ROW_CHUNKS, ROW_CHUNKS)], row_tile(ys_ref, dst),
                                  ssem.at[s]).start(priority=r % 2)

    def wait_scatter(s):
        pltpu.make_async_copy(ybufs[s], ys_ref.at[pl.ds(0, blk)], ssem.at[s]).wait()

    def chunk_rows(c):
        return pl.ds(c, tmb, stride=ROW_CHUNKS)

    def on_phase(cond, fn):
        for q in range(ROW_BUFS):
            pl.when(jnp.logical_and(cond, b % ROW_BUFS == q))(functools.partial(fn, q))

    @pl.when(jnp.logical_and(active, b == 0))
    def _():
        start_gather(IDX_SRC0, 0)
        start_gather(IDX_SRC1, 1)

    @pl.when(jnp.logical_and(active, fresh))
    def _():
        even = lax.broadcasted_iota(jnp.int32, (PREP_ROWS, LANES), 1) % 2 == 0

        def split_rows(r, carry):
            rows = pl.ds(pl.multiple_of(r * PREP_ROWS, PREP_ROWS), PREP_ROWS)
            for c in range(D_EXPERT // LANES):
                blk_a = win_ref[rows, 2 * c * LANES:(2 * c + 1) * LANES]
                blk_b = win_ref[rows, (2 * c + 1) * LANES:(2 * c + 2) * LANES]
                wg_s[rows, c * LANES:(c + 1) * LANES] = jnp.where(
                    even, blk_a, pltpu.roll(blk_b, 1, 1)).astype(BF16)
                wl_s[rows, c * LANES:(c + 1) * LANES] = jnp.where(
                    even, pltpu.roll(blk_a, LANES - 1, 1), blk_b).astype(BF16)
            return carry

        lax.fori_loop(0, D_MODEL // PREP_ROWS, split_rows, 0)
        half = LANES // 2
        for c in range(D_EXPERT // LANES):
            for p in range(2):
                for cb in range(D_MODEL // LANES):
                    wo_t[cb, pl.ds(c * LANES + p, half, stride=2), :] = (
                        wout_ref[pl.ds(c * LANES + half * p, half), cb * LANES:(cb + 1) * LANES])

        def cast_rows(r, carry):
            rows = pl.ds(pl.multiple_of(r * PREP_ROWS, PREP_ROWS), PREP_ROWS)
            for cb in range(D_MODEL // LANES):
                wo_s[rows, cb * LANES:(cb + 1) * LANES] = wo_t[cb, rows, :].astype(BF16)
            return carry

        lax.fori_loop(0, D_EXPERT // PREP_ROWS, cast_rows, 0)

    def compute(p):
        x = jnp.concatenate([xbufs[p][chunk_rows(c), :] for c in range(ROW_CHUNKS)], axis=1).astype(BF16)
        zg = _bdot(x, wg_s[...]) + bg_ref[...]
        zl = _bdot(x, wl_s[...]) + bl_ref[...]
        glu = jnp.minimum(zg, SWIGLU_LIMIT)
        lin = jnp.clip(zl, -SWIGLU_LIMIT, SWIGLU_LIMIT)
        a = glu * jax.nn.sigmoid(SWIGLU_ALPHA * glu) * (lin + 1.0)
        y = _bdot(a.astype(BF16), wo_s[...]) + bo_ref[...]
        for c in range(ROW_CHUNKS):
            ybufs[p][chunk_rows(c), :] = y[:, c * LANES:(c + 1) * LANES]

    @pl.when(jnp.logical_and(active, b == 0))
    def _():
        wait_gather(0)
        start_gather(IDX_SRC2, 2)
        compute(0)

    def step(q):
        wait_gather(q)
        pl.when(b >= ROW_BUFS)(functools.partial(wait_scatter, q))
        start_gather(IDX_SRC2, (q + 2) % ROW_BUFS)
        start_scatter(IDX_DST_PREV, (q - 1) % ROW_BUFS)
        compute(q)

    on_phase(jnp.logical_and(active, b > 0), step)

    def drain(q):
        start_scatter(IDX_DST, q)
        wait_gather((q + 1) % ROW_BUFS)
        wait_gather((q + 2) % ROW_BUFS)
        wait_scatter(q)
        for back in range(1, ROW_BUFS):
            pl.when(b >= back)(functools.partial(wait_scatter, (q - back) % ROW_BUFS))

    @pl.when(last)
    def _():
        on_phase(last, drain)
        spare0 = ys_ref.shape[0] - 2 * blk
        fills = []
        for s in range(2):
            ybufs[s][...] = jnp.zeros_like(ybufs[s])
            fills.append(pltpu.make_async_copy(ybufs[s], ys_ref.at[pl.ds(spare0 + s * blk, blk)], ssem.at[s]))
        for cp in fills:
            cp.start()
        for cp in fills:
            cp.wait()


def _experts(h3, src_rows, dst_rows, block_e, nact, n_slots, layer, w_in, b_glu, b_lin, w_out, b_out):
    tmb = EXPERT_ROWS
    n_blocks = src_rows.shape[0]
    exp3 = lambda b, be, na: (be[b], 0, 0)
    exp4 = lambda b, be, na: (layer, be[b], 0, 0)
    src2d, dst2d = src_rows.reshape(n_blocks, tmb), dst_rows.reshape(n_blocks, tmb)
    ahead = lambda a, k: jnp.concatenate([a[k:]] + [a[-1:]] * k, axis=0)
    idx = jnp.concatenate([src2d, ahead(src2d, 1), ahead(src2d, 2),
                           jnp.concatenate([dst2d[:1], dst2d[:-1]], axis=0), dst2d], axis=1)
    idx = idx.reshape(n_blocks, 1, 5 * tmb)
    grid_spec = pltpu.PrefetchScalarGridSpec(
        num_scalar_prefetch=2,
        grid=(n_blocks,),
        in_specs=[
            pl.BlockSpec((1, 1, 5 * tmb), lambda b, be, na: (b, 0, 0), memory_space=pltpu.SMEM),
            pl.BlockSpec(memory_space=pl.ANY),
            pl.BlockSpec((None, None, D_MODEL, 2 * D_EXPERT), exp4),
            pl.BlockSpec((None, 1, D_EXPERT), exp3),
            pl.BlockSpec((None, 1, D_EXPERT), exp3),
            pl.BlockSpec((None, None, D_EXPERT, D_MODEL), exp4),
            pl.BlockSpec((None, 1, D_MODEL), exp3),
        ],
        out_specs=pl.BlockSpec(memory_space=pl.ANY),
        scratch_shapes=[pltpu.VMEM((D_MODEL, D_EXPERT), BF16), pltpu.VMEM((D_MODEL, D_EXPERT), BF16),
                        pltpu.VMEM((D_MODEL // LANES, D_EXPERT, LANES), F32),
                        pltpu.VMEM((D_EXPERT, D_MODEL), BF16)]
        + [pltpu.VMEM((tmb * ROW_CHUNKS, LANES), F32)] * (2 * ROW_BUFS)
        + [pltpu.SemaphoreType.DMA((ROW_BUFS,)), pltpu.SemaphoreType.DMA((ROW_BUFS,))],
    )
    return pl.pallas_call(
        _expert_kernel,
        grid_spec=grid_spec,
        out_shape=jax.ShapeDtypeStruct((n_slots * ROW_CHUNKS, LANES), F32),
        compiler_params=_cparams(("arbitrary",)),
        name="moe_experts",
    )(block_e, nact, idx, h3, w_in, b_glu, b_lin, w_out, b_out)


def _moe(h_first, h_second, layer, w_router, b_router, w_in, b_in, w_out, b_out):
    tmb = EXPERT_ROWS
    top_e, gates, rank, counts, h3 = _router(h_first, h_second, w_router, b_router)
    n = h3.shape[0] // ROW_CHUNKS
    m = n * TOP_K
    padded = (counts + tmb - 1) // tmb * tmb
    pend = jnp.cumsum(padded)
    pstart = pend - padded
    hot = top_e[..., None] == jnp.arange(N_EXPERTS)[None, None, :]
    dest = jnp.sum(jnp.where(hot, pstart[None, None, :], 0), axis=-1) + rank
    n_blocks = m // tmb + N_EXPERTS
    blk_row0 = jnp.arange(n_blocks, dtype=jnp.int32) * tmb
    block_e = jnp.minimum(jnp.sum(pend[None, :] <= blk_row0[:, None], axis=1), N_EXPERTS - 1).astype(jnp.int32)
    nact = (pend[-1] // tmb).astype(jnp.int32).reshape(1)
    slot = jnp.arange(n, dtype=jnp.int32)[None, :] + n * jnp.arange(TOP_K, dtype=jnp.int32)[:, None]
    j = jnp.arange(N_EXPERTS * tmb, dtype=jnp.int32)
    off = j % tmb
    pad = padded - counts
    rep = lambda v: jnp.repeat(v, tmb)
    pad_key = jnp.where(off < rep(pad), rep(pstart + counts) + off, pend[-1] + j - rep(jnp.cumsum(pad)))
    keys = jnp.concatenate([dest.reshape(m), pad_key]).astype(jnp.int32)
    vals = jnp.concatenate([slot.reshape(m), jnp.full_like(j, -1)])
    _, row_slot = lax.sort((keys, vals), num_keys=1, is_stable=False)
    row = jnp.arange(n_blocks * tmb, dtype=jnp.int32)
    src_rows = jnp.where(row_slot < 0, 0, row_slot % n)
    dst_rows = jnp.where(row_slot < 0, m + ((row // tmb) % 2) * tmb + row % tmb, row_slot)
    b_glu = _pair_perm(b_in[:, 0::2]).reshape(N_EXPERTS, 1, D_EXPERT)
    b_lin = _pair_perm(b_in[:, 1::2]).reshape(N_EXPERTS, 1, D_EXPERT)
    ys = _experts(h3, src_rows.reshape(n_blocks, 1, tmb), dst_rows.reshape(n_blocks, 1, tmb), block_e, nact,
                  m + 2 * tmb, layer, w_in, b_glu, b_lin, w_out, b_out.reshape(N_EXPERTS, 1, D_MODEL))
    return ys, gates.T


def _post_kernel(x1_ref, *rest, with_kv):
    ys_refs = rest[:TOP_K]
    gate_ref, mod_ref, lng_ref, lnb_ref = rest[TOP_K:TOP_K + 4]
    rest = rest[TOP_K + 4:]
    g2 = mod_ref[5:6, :]
    tm = x1_ref.shape[0]
    f = None
    for k in range(TOP_K):
        yk = jnp.concatenate([ys_refs[k][pl.ds(c, tm, stride=ROW_CHUNKS), :] for c in range(ROW_CHUNKS)], axis=1)
        term = yk * gate_ref[:, k:k + 1]
        f = term if f is None else f + term
    x2 = _layer_norm(DN_ALPHA * x1_ref[...] + g2 * f, lng_ref[...], lnb_ref[...])
    if not with_kv:
        (x2_ref,) = rest
        x2_ref[...] = x2
        return
    wa_ref, kng_ref, cos_ref, sin_ref, x2_ref, lat_ref, kr_ref, kcat_ref = rest
    x2_ref[...] = x2
    kv = _bdot(x2.astype(BF16), wa_ref[...])
    c = kv[:, :KV_RANK]
    lat = c * lax.rsqrt(jnp.mean(c * c, axis=-1, keepdims=True) + RMS_EPS) * kng_ref[...]
    k = kv[:, KV_RANK:KV_RANK + QK_ROPE]
    k_swapped = kv[:, KV_RANK + QK_ROPE:]
    kr = k * cos_ref[...] + k_swapped * sin_ref[...]
    lat_ref[...] = lat
    kr_ref[...] = kr
    kcat_ref[:, :KV_RANK] = lat.astype(BF16)
    kcat_ref[:, KV_RANK:] = kr.astype(BF16)


def _swap_halves_cols(w, width):
    shp = w.shape
    w4 = w.reshape(shp[:-1] + (shp[-1] // width, 2, width // 2))
    return w4[..., ::-1, :].reshape(shp)


def _rope_tables(pos, reps):
    inv = 1.0 / (ROPE_BASE ** (jnp.arange(ROPE_HALF, dtype=F32) * (2.0 / QK_ROPE)))
    ang = pos.astype(F32)[:, None] * inv[None, :]
    cos, sin = jnp.cos(ang), jnp.sin(ang)
    cos_t = jnp.tile(jnp.concatenate([cos, cos], -1), (1, reps))
    sin_t = jnp.tile(jnp.concatenate([-sin, sin], -1), (1, reps))
    return cos_t, sin_t


def _post_layer(x1, ys, gates, row0, mod, ln_g, ln_b, *, tm, kv=None):
    bsz, s, d = x1.shape
    blk0 = row0 // tm
    per_b = s // tm
    tile = lambda w: pl.BlockSpec((None, tm, w), lambda b, i: (b, i, 0))
    n_tok = gates.shape[0]
    pick_spec = lambda k: pl.BlockSpec((tm * ROW_CHUNKS, LANES),
                                       lambda b, i: (k * (n_tok // tm) + blk0 + b * per_b + i, 0))
    in_specs = [tile(d)] + [pick_spec(k) for k in range(TOP_K)] + [
        pl.BlockSpec((tm, TOP_K), lambda b, i: (blk0 + b * per_b + i, 0)),
        pl.BlockSpec((None, 6, d), lambda b, i: (b, 0, 0)),
        _const_spec((1, d)),
        _const_spec((1, d)),
    ]
    args = [x1] + [ys] * TOP_K + [gates, mod, ln_g.reshape(1, -1), ln_b.reshape(1, -1)]
    out_shape = [jax.ShapeDtypeStruct((bsz, s, d), F32)]
    out_specs = [tile(d)]
    if kv is not None:
        w_a, kn_g, cos_t, sin_t = kv
        w_ext = jnp.concatenate([w_a, _swap_halves_cols(w_a[:, KV_RANK:], QK_ROPE)], axis=1).astype(BF16)
        in_specs += [_const_spec(w_ext.shape), _const_spec((1, KV_RANK)),
                     pl.BlockSpec((tm, QK_ROPE), lambda b, i: (i, 0)),
                     pl.BlockSpec((tm, QK_ROPE), lambda b, i: (i, 0))]
        args += [w_ext, kn_g.reshape(1, -1), cos_t, sin_t]
        out_shape += [jax.ShapeDtypeStruct((bsz, s, KV_RANK), F32), jax.ShapeDtypeStruct((bsz, s, QK_ROPE), F32),
                      jax.ShapeDtypeStruct((bsz, s, QK_LAT), BF16)]
        out_specs += [tile(KV_RANK), tile(QK_ROPE), tile(QK_LAT)]
    return pl.pallas_call(
        functools.partial(_post_kernel, with_kv=kv is not None),
        grid=(bsz, per_b),
        in_specs=in_specs,
        out_specs=out_specs,
        out_shape=out_shape,
        compiler_params=_cparams(("arbitrary", "arbitrary")),
        name="post_moe",
    )(*args)


Q_GROUP = 512
KEY_TILE = 512


def _qproj_kernel(x_ref, mod_ref, wdq_ref, qng_ref, wqn_ref, wqr_ref, wqrs_ref, wuk_ref, cos_ref, sin_ref, q_ref,
                  *, tq):
    x = x_ref[...]
    sh1, sc1 = mod_ref[0:1, :], mod_ref[1:2, :]
    h = (x * (1.0 + sc1) + sh1).astype(BF16)
    tm = x.shape[0]
    cq = lax.dot_general(wdq_ref[...], h, (((1,), (1,)), ((), ())), preferred_element_type=F32)
    cq = (cq * lax.rsqrt(jnp.mean(cq * cq, axis=0, keepdims=True) + RMS_EPS) * qng_ref[...]).astype(BF16)
    qn = _bdot(wqn_ref[...], cq)
    qr = _bdot(wqr_ref[...], cq)
    qrs = _bdot(wqrs_ref[...], cq)
    rope = (qr * cos_ref[...] + qrs * sin_ref[...]).astype(BF16)
    per_group = Q_GROUP // tq
    for hd in range(MLA_HEADS):
        ql = _bdot(wuk_ref[hd], qn[hd * QK_NOPE:(hd + 1) * QK_NOPE, :].astype(BF16)).astype(BF16)
        g, off = hd // per_group, (hd % per_group) * tq
        for t in range(tm // tq):
            q_ref[t, g, 0:KV_RANK, off:off + tq] = ql[:, t * tq:(t + 1) * tq]
            q_ref[t, g, KV_RANK:QK_LAT, off:off + tq] = rope[hd * QK_ROPE:(hd + 1) * QK_ROPE, t * tq:(t + 1) * tq]


def _qproj_layer(x, mod, w_dq, qn_g, w_qb, w_uk, cos_t, sin_t, *, tm, tq):
    bsz, s, d = x.shape
    w_qn = w_qb[:, :, :QK_NOPE].reshape(Q_RANK, MLA_HEADS * QK_NOPE)
    w_qr = w_qb[:, :, QK_NOPE:].reshape(Q_RANK, MLA_HEADS * QK_ROPE)
    w_qrs = _swap_halves_cols(w_qr, QK_ROPE)
    wuk_h = jnp.transpose(w_uk, (1, 0, 2)).astype(BF16)
    hr = MLA_HEADS * QK_ROPE
    n_groups = MLA_HEADS * tq // Q_GROUP
    nt = tm // tq
    return pl.pallas_call(
        functools.partial(_qproj_kernel, tq=tq),
        grid=(bsz, s // tm),
        in_specs=[
            pl.BlockSpec((None, tm, d), lambda b, i: (b, i, 0)),
            pl.BlockSpec((None, 6, d), lambda b, i: (b, 0, 0)),
            _const_spec((Q_RANK, d)),
            _const_spec((Q_RANK, 1)),
            _const_spec((MLA_HEADS * QK_NOPE, Q_RANK)),
            _const_spec((hr, Q_RANK)),
            _const_spec((hr, Q_RANK)),
            _const_spec(wuk_h.shape),
            pl.BlockSpec((hr, tm), lambda b, i: (0, i)),
            pl.BlockSpec((hr, tm), lambda b, i: (0, i)),
        ],
        out_specs=pl.BlockSpec((None, nt, n_groups, QK_LAT, Q_GROUP), lambda b, i: (b, i, 0, 0, 0)),
        out_shape=jax.ShapeDtypeStruct((bsz, s // tq, n_groups, QK_LAT, Q_GROUP), BF16),
        compiler_params=_cparams(("arbitrary", "arbitrary")),
        name="mla_qproj",
    )(x, mod, w_dq.T.astype(BF16), qn_g.reshape(-1, 1), w_qn.T.astype(BF16), w_qr.T.astype(BF16),
      w_qrs.T.astype(BF16), wuk_h, cos_t.T, sin_t.T)


EXP2_SCALE = ATTN_SCALE * 1.4426950408889634


def _attn_kernel(q_ref, *rest, tq, causal, n_main, tail, split_keys):
    if split_keys:
        klat_ref, kkr_ref = rest[:2]
        rest = rest[2:]
    else:
        k_ref = rest[0]
        rest = rest[1:]
    if tail:
        kt_ref = rest[0]
        rest = rest[1:]
    x_ref, mod_ref, wuv_ref, wo_ref, lng_ref, lnb_ref, x1_ref, h2_ref, m_s, l_s, acc_s, ta_s, tb_s = rest
    i = pl.program_id(1)
    n_groups = q_ref.shape[0]
    per_group = Q_GROUP // tq
    m_s[...] = jnp.full_like(m_s, NEG_BIG)
    l_s[...] = jnp.zeros_like(l_s)
    acc_s[...] = jnp.zeros_like(acc_s)

    def tile_rows(j):
        return pl.ds(pl.multiple_of(j * KEY_TILE, KEY_TILE), KEY_TILE)

    def key_rows(j):
        if split_keys:
            return jnp.concatenate([klat_ref[tile_rows(j), :].astype(BF16), kkr_ref[tile_rows(j), :].astype(BF16)],
                                   axis=1)
        return k_ref[tile_rows(j), :]

    def value_rows(j):
        if split_keys:
            return klat_ref[tile_rows(j), :].astype(BF16)
        return k_ref[tile_rows(j), :KV_RANK]

    def softmax_step(g, t, v, key0, masked):
        t = t * EXP2_SCALE
        if masked:
            n = t.shape[0]
            k_chunk = (key0 + lax.broadcasted_iota(jnp.int32, (n, 1), 0)) // CHUNK
            q_chunk = (i * tq + lax.broadcasted_iota(jnp.int32, (1, Q_GROUP), 1) % tq) // CHUNK
            t = jnp.where(k_chunk <= q_chunk, t, -jnp.inf)
        m_old = m_s[g]
        m_new = jnp.maximum(m_old, jnp.max(t, axis=0, keepdims=True))
        alpha = jnp.exp2(m_old - m_new)
        p = jnp.exp2(t - m_new)
        l_s[g] = alpha * l_s[g] + jnp.sum(p, axis=0, keepdims=True)
        acc_s[g] = alpha * acc_s[g] + lax.dot_general(
            v, p.astype(BF16), (((0,), (0,)), ((), ())), preferred_element_type=F32)
        m_s[g] = m_new

    def stage(cur, nxt, j, masked, make_next):
        v = value_rows(j)
        k_next = key_rows(j + 1) if make_next else None
        for g in range(n_groups):
            if make_next:
                nxt[g] = _bdot(k_next, q_ref[g])
            softmax_step(g, cur[g], v, j * KEY_TILE, masked)

    n_open = (i * tq) // KEY_TILE if causal else n_main - 1
    k0 = key_rows(0)
    for g in range(n_groups):
        ta_s[g] = _bdot(k0, q_ref[g])

    def pair_body(p, c):
        stage(ta_s, tb_s, 2 * p, False, True)
        stage(tb_s, ta_s, 2 * p + 1, False, True)
        return c

    lax.fori_loop(0, n_open // 2, pair_body, 0)
    if causal:
        @pl.when(n_open % 2 == 0)
        def _():
            stage(ta_s, tb_s, n_open, True, False)

        @pl.when(n_open % 2 == 1)
        def _():
            stage(ta_s, tb_s, n_open - 1, False, True)
            stage(tb_s, ta_s, n_open, True, False)
    elif n_open % 2 == 0:
        stage(ta_s, tb_s, n_open, False, False)
    else:
        stage(ta_s, tb_s, n_open - 1, False, True)
        stage(tb_s, ta_s, n_open, False, False)
    if tail:
        kt = kt_ref[...]
        for g in range(n_groups):
            softmax_step(g, _bdot(kt, q_ref[g]), kt[:, :KV_RANK], n_main * KEY_TILE, False)

    heads = []
    for g in range(n_groups):
        o = (acc_s[g] / l_s[g]).T
        for hh in range(per_group):
            hd = g * per_group + hh
            heads.append(_bdot(o[hh * tq:(hh + 1) * tq].astype(BF16), wuv_ref[hd]))
    oc = jnp.concatenate(heads, axis=-1).astype(BF16)
    y = _bdot(oc, wo_ref[...])
    x = x_ref[...]
    g1, sh2, sc2 = mod_ref[2:3, :], mod_ref[3:4, :], mod_ref[4:5, :]
    x1 = _layer_norm(DN_ALPHA * x + g1 * y, lng_ref[...], lnb_ref[...])
    x1_ref[...] = x1
    h2_ref[...] = x1 * (1.0 + sc2) + sh2


def _attn_layer(q, keys, x, mod, w_uv, w_o, ln_g, ln_b, *, tq, causal, k_tail=None):
    bsz, s, d = x.shape
    split_keys = isinstance(keys, tuple)
    keys = keys if split_keys else (keys,)
    t_main = keys[0].shape[1]
    n_main = t_main // KEY_TILE
    n_groups = q.shape[2]
    tail = k_tail is not None
    wuv_t = jnp.transpose(w_uv, (1, 0, 2)).astype(BF16)
    tile = pl.BlockSpec((None, tq, d), lambda b, i: (b, i, 0))
    in_specs = [pl.BlockSpec((None, None, n_groups, QK_LAT, Q_GROUP), lambda b, i: (b, i, 0, 0, 0))]
    in_specs += [pl.BlockSpec((None, t_main, k.shape[2]), lambda b, i: (b, 0, 0)) for k in keys]
    args = [q, *keys]
    if tail:
        in_specs.append(pl.BlockSpec((None, k_tail.shape[1], QK_LAT), lambda b, i: (b, 0, 0)))
        args.append(k_tail)
    in_specs += [
        tile,
        pl.BlockSpec((None, 6, d), lambda b, i: (b, 0, 0)),
        _const_spec(wuv_t.shape),
        _const_spec((MLA_HEADS * V_DIM, d)),
        _const_spec((1, d)),
        _const_spec((1, d)),
    ]
    args += [x, mod, wuv_t, w_o.astype(BF16), ln_g.reshape(1, -1), ln_b.reshape(1, -1)]
    return pl.pallas_call(
        functools.partial(_attn_kernel, tq=tq, causal=causal, n_main=n_main, tail=tail, split_keys=split_keys),
        grid=(bsz, s // tq),
        in_specs=in_specs,
        out_specs=[tile, tile],
        out_shape=[jax.ShapeDtypeStruct((bsz, s, d), F32), jax.ShapeDtypeStruct((bsz, s, d), F32)],
        scratch_shapes=[pltpu.VMEM((n_groups, 1, Q_GROUP), F32), pltpu.VMEM((n_groups, 1, Q_GROUP), F32),
                        pltpu.VMEM((n_groups, KV_RANK, Q_GROUP), F32),
                        pltpu.VMEM((n_groups, KEY_TILE, Q_GROUP), F32),
                        pltpu.VMEM((n_groups, KEY_TILE, Q_GROUP), F32)],
        compiler_params=_cparams(("arbitrary", "arbitrary")),
        name="mla_attention",
    )(*args)


GMLP_ROWS = 256
TOKEN_ROWS = 512
ATTN_Q_ROWS = 128


def kernel(x_prompt, x_sample, cache_kv_latent, cache_k_rope, c_prompt, c_sample, ada_w, ada_b, ln_g, ln_b,
           a_w_in, a_b_in, a_vn_g, a_vn_b, a_w_s, a_b_s, a_w_out, a_b_out, kv_w_a, kv_norm_g, kv_w_uk, kv_w_uv,
           q_w_a, q_norm_g, q_w_b, q_w_o, moe_w_router, moe_b_router, moe_w_in, moe_b_in, moe_w_out, moe_b_out):
    bp, sp, d = x_prompt.shape
    bs, ss, _ = x_sample.shape
    past = cache_kv_latent.shape[1]
    n_p = bp * sp
    n_s = bs * ss
    assert d == D_MODEL and sp % TOKEN_ROWS == 0 and sp % GMLP_ROWS == 0 and sp % KEY_TILE == 0
    assert ss <= CHUNK and Q_GROUP % ss == 0 and past % KEY_TILE == 0
    assert n_p % ROUTER_ROWS == 0 and n_s % ROUTER_ROWS == 0 and (n_p + n_s) * TOP_K % EXPERT_ROWS == 0

    mod = _ada_mod(jnp.concatenate([c_prompt, c_sample], axis=0), ada_w, ada_b)
    mod = mod.reshape(DEPTH, bp + bs, 6, d)
    mod_p, mod_s = mod[:, :bp], mod[:, bp:]

    pos_p = jnp.arange(sp, dtype=jnp.int32)
    pos_s = past + jnp.arange(ss, dtype=jnp.int32)

    def moe(l, h2_p, h2_s):
        return _moe(h2_p.reshape(n_p, d), h2_s.reshape(n_s, d), l, moe_w_router[l], moe_b_router[l],
                    moe_w_in, moe_b_in[l], moe_w_out, moe_b_out[l])

    gm = functools.partial(_gmlp_layer, w_in=a_w_in[0], b_in=a_b_in[0], vn_g=a_vn_g[0], vn_b=a_vn_b[0],
                           w_s=a_w_s[0], b_s=a_b_s[0], w_out=a_w_out[0], b_out=a_b_out[0],
                           ln_g=ln_g[0, 0], ln_b=ln_b[0, 0])
    x1_p, h2_p = gm(x_prompt, mod_p[0], tm=GMLP_ROWS, write_v=False)
    x1_s, h2_s, v_s = gm(x_sample, mod_s[0], tm=ss, write_v=True)
    ys, gates = moe(0, h2_p, h2_s)
    x2_p, lat_p, kr_p, kcat_p = _post_layer(x1_p, ys, gates, 0, mod_p[0], ln_g[0, 1], ln_b[0, 1], tm=TOKEN_ROWS,
                                            kv=(kv_w_a, kv_norm_g) + _rope_tables(pos_p, 1))
    x2_s, lat_s, kr_s, kcat_s = _post_layer(x1_s, ys, gates, n_p, mod_s[0], ln_g[0, 1], ln_b[0, 1], tm=ss,
                                            kv=(kv_w_a, kv_norm_g) + _rope_tables(pos_s, 1))

    qp = functools.partial(_qproj_layer, w_dq=q_w_a[0], qn_g=q_norm_g[0], w_qb=q_w_b[0], w_uk=kv_w_uk)
    at = functools.partial(_attn_layer, w_uv=kv_w_uv, w_o=q_w_o[0], ln_g=ln_g[1, 0], ln_b=ln_b[1, 0])
    cos_p, sin_p = _rope_tables(pos_p, MLA_HEADS)
    cos_s, sin_s = _rope_tables(pos_s, MLA_HEADS)
    q_p = qp(x2_p, mod_p[1], cos_t=cos_p, sin_t=sin_p, tm=TOKEN_ROWS, tq=ATTN_Q_ROWS)
    q_s = qp(x2_s, mod_s[1], cos_t=cos_s, sin_t=sin_s, tm=ss, tq=ss)
    x3_p, h4_p = at(q_p, kcat_p, x2_p, mod_p[1], tq=ATTN_Q_ROWS, causal=True)
    x3_s, h4_s = at(q_s, (cache_kv_latent, cache_k_rope), x2_s, mod_s[1], tq=ss, causal=False, k_tail=kcat_s)
    ys, gates = moe(1, h4_p, h4_s)
    (y_p,) = _post_layer(x3_p, ys, gates, 0, mod_p[1], ln_g[1, 1], ln_b[1, 1], tm=TOKEN_ROWS)
    (y_s,) = _post_layer(x3_s, ys, gates, n_p, mod_s[1], ln_g[1, 1], ln_b[1, 1], tm=ss)

    return (y_p, y_s, lat_p, kr_p, lat_s, kr_s, v_s[None])
```
